```python
import jax, jax.numpy as jnp
from jax import lax
import numpy as np

D_MODEL = 1024
BATCH = 8
SEQ = 8192
DEPTH = 2

CHUNK = 64
MEM_LEN = 256
FOX_HEADS = 8
FOX_HEAD_DIM = 64
FOX_WIDTH = FOX_HEADS * FOX_HEAD_DIM
POOL_WIDTH = D_MODEL - FOX_WIDTH
POOL_WINDOWS = (2, 4, 8, 16)
POOL_GROUPS = len(POOL_WINDOWS)
POOL_GROUP_DIM = POOL_WIDTH // POOL_GROUPS
IN_COLS = 3 * FOX_WIDTH + POOL_WIDTH + FOX_HEADS
MEM_HEADS = 4
MEM_HEAD_DIM = 128
MEM_WIDTH = MEM_HEADS * MEM_HEAD_DIM
D_FF = ((-(-8 * D_MODEL // 3) + 255) // 256) * 256
Q_BLOCK = 128
EPS = 1e-6

kernel_name = "fox_pool_hybrid_encoder"


def rmsnorm(x, g):
    xf = x.astype(jnp.float32)
    y = xf * lax.rsqrt(jnp.mean(xf * xf, axis=-1, keepdims=True) + EPS)
    return (y * g.astype(jnp.float32)).astype(x.dtype)


def forgetting_attention(q, k, v, f_logit):
    c = jnp.cumsum(jax.nn.log_sigmoid(f_logit.astype(jnp.float32)), axis=-1)
    scale = FOX_HEAD_DIM ** -0.5
    S = q.shape[2]
    outs = []
    for start in range(0, S, Q_BLOCK):
        end = start + Q_BLOCK
        qb = q[:, :, start:end]
        kb = k[:, :, :end]
        vb = v[:, :, :end]
        s = jnp.einsum('bhqd,bhkd->bhqk', qb, kb, preferred_element_type=jnp.float32) * scale
        s = s + c[:, :, start:end, None] - c[:, :, None, :end]
        qpos = start + jnp.arange(Q_BLOCK)
        kpos = jnp.arange(end)
        s = jnp.where(kpos[None, :] <= qpos[:, None], s, -jnp.inf)
        p = jax.nn.softmax(s, axis=-1)
        outs.append(jnp.einsum('bhqk,bhkd->bhqd', p.astype(vb.dtype), vb))
    return jnp.concatenate(outs, axis=2)


def multiscale_pool(p, w_pool, pool_scale):
    B, S, _ = p.shape
    pf = p.astype(jnp.float32).reshape(B, S, POOL_GROUPS, POOL_GROUP_DIM)
    cs = jnp.concatenate([jnp.zeros((B, 1, POOL_GROUPS, POOL_GROUP_DIM), jnp.float32),
                          jnp.cumsum(pf, axis=1)], axis=1)
    t = jnp.arange(S)
    outs = []
    for g, w in enumerate(POOL_WINDOWS):
        csg = cs[:, :, g]
        lower = jnp.pad(csg, ((0, 0), (w - 1, 0), (0, 0)))[:, :S]
        cnt = jnp.minimum(t + 1, w).astype(jnp.float32)[None, :, None]
        mixed = (csg[:, 1:] - lower) / cnt - pf[:, :, g]
        outs.append(jnp.einsum('bsc,cd->bsd', mixed, w_pool[g].astype(jnp.float32)))
    y = jnp.concatenate(outs, axis=-1) * pool_scale.astype(jnp.float32)
    return y.astype(p.dtype)


def head_split(t, n_heads, head_dim):
    B, S, _ = t.shape
    return t.reshape(B, S, n_heads, head_dim)


def _fwd_setup_inputs(seed: int = 0) -> dict:
    key = jax.random.key(seed)
    ks = jax.random.split(key, 24)
    f32 = jnp.float32
    D = D_MODEL

    def nrm(k, shape, fan_in):
        return jax.random.normal(k, shape, f32) * (fan_in ** -0.5)

    def gain(k, shape):
        return 1.0 + 0.05 * jax.random.normal(k, shape, f32)

    return {
        "x": jax.random.normal(ks[0], (BATCH, SEQ, D), f32),
        "mem": jax.random.normal(ks[1], (BATCH, MEM_LEN, D), f32),
        "g_mix": gain(ks[2], (DEPTH, D)),
        "w_in": nrm(ks[3], (DEPTH, D, IN_COLS), D),
        "b_forget": 3.0 + 0.5 * jax.random.normal(ks[4], (DEPTH, FOX_HEADS), f32),
        "g_q_fox": gain(ks[5], (DEPTH, FOX_HEAD_DIM)),
        "g_k_fox": gain(ks[6], (DEPTH, FOX_HEAD_DIM)),
        "w_pool": nrm(ks[7], (DEPTH, POOL_GROUPS, POOL_GROUP_DIM, POOL_GROUP_DIM), POOL_GROUP_DIM),
        "pool_scale": gain(ks[8], (DEPTH, POOL_WIDTH)),
        "w_out": nrm(ks[9], (DEPTH, D, D), D),
        "g_mem_q": gain(ks[10], (DEPTH, D)),
        "g_mem_kv": gain(ks[11], (DEPTH, D)),
        "w_mem_q": nrm(ks[12], (DEPTH, D, MEM_WIDTH), D),
        "w_mem_kv": nrm(ks[13], (DEPTH, D, 2 * MEM_WIDTH), D),
        "g_q_mem": gain(ks[14], (DEPTH, MEM_HEAD_DIM)),
        "g_k_mem": gain(ks[15], (DEPTH, MEM_HEAD_DIM)),
        "w_mem_out": nrm(ks[16], (DEPTH, MEM_WIDTH, D), MEM_WIDTH),
        "g_ffn": gain(ks[17], (DEPTH, D)),
        "w_gate_up": nrm(ks[18], (DEPTH, D, 2 * D_FF), D),
        "w_down": nrm(ks[19], (DEPTH, D_FF, D), D_FF),
    }


def _fwd_reference(x, mem, g_mix, w_in, b_forget, g_q_fox, g_k_fox, w_pool, pool_scale, w_out,
              g_mem_q, g_mem_kv, w_mem_q, w_mem_kv, g_q_mem, g_k_mem, w_mem_out,
              g_ffn, w_gate_up, w_down):
    B, S, _ = x.shape
    h = x
    for l in range(DEPTH):
        xn = rmsnorm(h, g_mix[l])
        z = jnp.einsum('bsd,dc->bsc', xn, w_in[l])
        q = z[..., :FOX_WIDTH]
        k = z[..., FOX_WIDTH:2 * FOX_WIDTH]
        v = z[..., 2 * FOX_WIDTH:3 * FOX_WIDTH]
        p_in = z[..., 3 * FOX_WIDTH:3 * FOX_WIDTH + POOL_WIDTH]
        f_logit = z[..., 3 * FOX_WIDTH + POOL_WIDTH:] + b_forget[l]
        q = rmsnorm(head_split(q, FOX_HEADS, FOX_HEAD_DIM), g_q_fox[l]).transpose(0, 2, 1, 3)
        k = rmsnorm(head_split(k, FOX_HEADS, FOX_HEAD_DIM), g_k_fox[l]).transpose(0, 2, 1, 3)
        v = head_split(v, FOX_HEADS, FOX_HEAD_DIM).transpose(0, 2, 1, 3)
        fox = forgetting_attention(q, k, v, f_logit.transpose(0, 2, 1))
        fox = fox.transpose(0, 2, 1, 3).reshape(B, S, FOX_WIDTH)
        pool = multiscale_pool(p_in, w_pool[l], pool_scale[l])
        h = h + jnp.einsum('bsc,cd->bsd', jnp.concatenate([fox, pool], axis=-1), w_out[l])

        hn = rmsnorm(h, g_mem_q[l])
        mn = rmsnorm(mem, g_mem_kv[l])
        mq = rmsnorm(head_split(jnp.einsum('bsd,dc->bsc', hn, w_mem_q[l]), MEM_HEADS, MEM_HEAD_DIM), g_q_mem[l])
        mkv = jnp.einsum('bmd,dc->bmc', mn, w_mem_kv[l])
        mk = rmsnorm(head_split(mkv[..., :MEM_WIDTH], MEM_HEADS, MEM_HEAD_DIM), g_k_mem[l])
        mv = head_split(mkv[..., MEM_WIDTH:], MEM_HEADS, MEM_HEAD_DIM)
        sc = jnp.einsum('bshd,bmhd->bhsm', mq, mk, preferred_element_type=jnp.float32) * (MEM_HEAD_DIM ** -0.5)
        pm = jax.nn.softmax(sc, axis=-1).astype(mv.dtype)
        mo = jnp.einsum('bhsm,bmhd->bshd', pm, mv).reshape(B, S, MEM_WIDTH)
        h = h + jnp.einsum('bsc,cd->bsd', mo, w_mem_out[l])

        hn = rmsnorm(h, g_ffn[l])
        gu = jnp.einsum('bsd,df->bsf', hn, w_gate_up[l])
        act = jax.nn.silu(gu[..., :D_FF]) * gu[..., D_FF:]
        h = h + jnp.einsum('bsf,fd->bsd', act, w_down[l])
    return h


import jax as _jax
import jax.numpy as _jnp

TWIN_FORMAT = 'train_step'
FWD_PARAMS = ['x', 'mem', 'g_mix', 'w_in', 'b_forget', 'g_q_fox', 'g_k_fox', 'w_pool', 'pool_scale', 'w_out', 'g_mem_q', 'g_mem_kv', 'w_mem_q', 'w_mem_kv', 'g_q_mem', 'g_k_mem', 'w_mem_out', 'g_ffn', 'w_gate_up', 'w_down']
TWIN_WEIGHTS = ['g_mix', 'w_in', 'b_forget', 'g_q_fox', 'g_k_fox', 'w_pool', 'pool_scale', 'w_out', 'g_mem_q', 'g_mem_kv', 'w_mem_q', 'w_mem_kv', 'g_q_mem', 'g_k_mem', 'w_mem_out', 'g_ffn', 'w_gate_up', 'w_down']
TWIN_DIFF_INPUT = 'x'
TWIN_INPUTS = ['x', 'mem', 'g_mix', 'w_in', 'b_forget', 'g_q_fox', 'g_k_fox', 'w_pool', 'pool_scale', 'w_out', 'g_mem_q', 'g_mem_kv', 'w_mem_q', 'w_mem_kv', 'g_q_mem', 'g_k_mem', 'w_mem_out', 'g_ffn', 'w_gate_up', 'w_down', 'loss_target', 'm_g_mix', 'm_w_in', 'm_b_forget', 'm_g_q_fox', 'm_g_k_fox', 'm_w_pool', 'm_pool_scale', 'm_w_out', 'm_g_mem_q', 'm_g_mem_kv', 'm_w_mem_q', 'm_w_mem_kv', 'm_g_q_mem', 'm_g_k_mem', 'm_w_mem_out', 'm_g_ffn', 'm_w_gate_up', 'm_w_down', 'v_g_mix', 'v_w_in', 'v_b_forget', 'v_g_q_fox', 'v_g_k_fox', 'v_w_pool', 'v_pool_scale', 'v_w_out', 'v_g_mem_q', 'v_g_mem_kv', 'v_w_mem_q', 'v_w_mem_kv', 'v_g_q_mem', 'v_g_k_mem', 'v_w_mem_out', 'v_g_ffn', 'v_w_gate_up', 'v_w_down']
TWIN_OUTPUTS = ['loss', 'grad_x', 'grad_g_mix', 'grad_w_in', 'grad_b_forget', 'grad_g_q_fox', 'grad_g_k_fox', 'grad_w_pool', 'grad_pool_scale', 'grad_w_out', 'grad_g_mem_q', 'grad_g_mem_kv', 'grad_w_mem_q', 'grad_w_mem_kv', 'grad_g_q_mem', 'grad_g_k_mem', 'grad_w_mem_out', 'grad_g_ffn', 'grad_w_gate_up', 'grad_w_down', 'delta_g_mix', 'delta_w_in', 'delta_b_forget', 'delta_g_q_fox', 'delta_g_k_fox', 'delta_w_pool', 'delta_pool_scale', 'delta_w_out', 'delta_g_mem_q', 'delta_g_mem_kv', 'delta_w_mem_q', 'delta_w_mem_kv', 'delta_g_q_mem', 'delta_g_k_mem', 'delta_w_mem_out', 'delta_g_ffn', 'delta_w_gate_up', 'delta_w_down', 'new_m_g_mix', 'new_m_w_in', 'new_m_b_forget', 'new_m_g_q_fox', 'new_m_g_k_fox', 'new_m_w_pool', 'new_m_pool_scale', 'new_m_w_out', 'new_m_g_mem_q', 'new_m_g_mem_kv', 'new_m_w_mem_q', 'new_m_w_mem_kv', 'new_m_g_q_mem', 'new_m_g_k_mem', 'new_m_w_mem_out', 'new_m_g_ffn', 'new_m_w_gate_up', 'new_m_w_down', 'new_v_g_mix', 'new_v_w_in', 'new_v_b_forget', 'new_v_g_q_fox', 'new_v_g_k_fox', 'new_v_w_pool', 'new_v_pool_scale', 'new_v_w_out', 'new_v_g_mem_q', 'new_v_g_mem_kv', 'new_v_w_mem_q', 'new_v_w_mem_kv', 'new_v_g_q_mem', 'new_v_g_k_mem', 'new_v_w_mem_out', 'new_v_g_ffn', 'new_v_w_gate_up', 'new_v_w_down']
TWIN_LEAF_KINDS = {'loss': 'loss', 'grad_x': 'grad_x', 'grad_g_mix': 'grad_w', 'grad_w_in': 'grad_w', 'grad_b_forget': 'grad_w', 'grad_g_q_fox': 'grad_w', 'grad_g_k_fox': 'grad_w', 'grad_w_pool': 'grad_w', 'grad_pool_scale': 'grad_w', 'grad_w_out': 'grad_w', 'grad_g_mem_q': 'grad_w', 'grad_g_mem_kv': 'grad_w', 'grad_w_mem_q': 'grad_w', 'grad_w_mem_kv': 'grad_w', 'grad_g_q_mem': 'grad_w', 'grad_g_k_mem': 'grad_w', 'grad_w_mem_out': 'grad_w', 'grad_g_ffn': 'grad_w', 'grad_w_gate_up': 'grad_w', 'grad_w_down': 'grad_w', 'delta_g_mix': 'delta_w', 'delta_w_in': 'delta_w', 'delta_b_forget': 'delta_w', 'delta_g_q_fox': 'delta_w', 'delta_g_k_fox': 'delta_w', 'delta_w_pool': 'delta_w', 'delta_pool_scale': 'delta_w', 'delta_w_out': 'delta_w', 'delta_g_mem_q': 'delta_w', 'delta_g_mem_kv': 'delta_w', 'delta_w_mem_q': 'delta_w', 'delta_w_mem_kv': 'delta_w', 'delta_g_q_mem': 'delta_w', 'delta_g_k_mem': 'delta_w', 'delta_w_mem_out': 'delta_w', 'delta_g_ffn': 'delta_w', 'delta_w_gate_up': 'delta_w', 'delta_w_down': 'delta_w', 'new_m_g_mix': 'new_m', 'new_m_w_in': 'new_m', 'new_m_b_forget': 'new_m', 'new_m_g_q_fox': 'new_m', 'new_m_g_k_fox': 'new_m', 'new_m_w_pool': 'new_m', 'new_m_pool_scale': 'new_m', 'new_m_w_out': 'new_m', 'new_m_g_mem_q': 'new_m', 'new_m_g_mem_kv': 'new_m', 'new_m_w_mem_q': 'new_m', 'new_m_w_mem_kv': 'new_m', 'new_m_g_q_mem': 'new_m', 'new_m_g_k_mem': 'new_m', 'new_m_w_mem_out': 'new_m', 'new_m_g_ffn': 'new_m', 'new_m_w_gate_up': 'new_m', 'new_m_w_down': 'new_m', 'new_v_g_mix': 'new_v', 'new_v_w_in': 'new_v', 'new_v_b_forget': 'new_v', 'new_v_g_q_fox': 'new_v', 'new_v_g_k_fox': 'new_v', 'new_v_w_pool': 'new_v', 'new_v_pool_scale': 'new_v', 'new_v_w_out': 'new_v', 'new_v_g_mem_q': 'new_v', 'new_v_g_mem_kv': 'new_v', 'new_v_w_mem_q': 'new_v', 'new_v_w_mem_kv': 'new_v', 'new_v_g_q_mem': 'new_v', 'new_v_g_k_mem': 'new_v', 'new_v_w_mem_out': 'new_v', 'new_v_g_ffn': 'new_v', 'new_v_w_gate_up': 'new_v', 'new_v_w_down': 'new_v'}


def _forward(args):
    return _fwd_reference(*[args[k] for k in FWD_PARAMS])


def _output_shape():
    out = _jax.eval_shape(lambda: _forward(_fwd_setup_inputs(0)))
    return out.shape, out.dtype

N_MICROBATCH = 1
ADAM_LR = 0.001
ADAM_B1 = 0.9
ADAM_B2 = 0.999
ADAM_EPS = 1e-08
ADAM_WD = 0.01
ADAM_STEP = 10
PER_EXAMPLE_BATCH_AXIS = {'x': 0, 'mem': 0, 'loss_target': 0}
SHARED_INPUTS = []
_WEIGHT_DTYPES = {'g_mix': _jnp.float32, 'w_in': _jnp.float32, 'b_forget': _jnp.float32, 'g_q_fox': _jnp.float32, 'g_k_fox': _jnp.float32, 'w_pool': _jnp.float32, 'pool_scale': _jnp.float32, 'w_out': _jnp.float32, 'g_mem_q': _jnp.float32, 'g_mem_kv': _jnp.float32, 'w_mem_q': _jnp.float32, 'w_mem_kv': _jnp.float32, 'g_q_mem': _jnp.float32, 'g_k_mem': _jnp.float32, 'w_mem_out': _jnp.float32, 'g_ffn': _jnp.float32, 'w_gate_up': _jnp.float32, 'w_down': _jnp.float32}
MOMENT_SCALE = {'g_mix': 2.772891e+01, 'w_in': 1.117086e+00, 'b_forget': 1.298659e+02, 'g_q_fox': 2.308936e+01, 'g_k_fox': 2.281044e+01, 'w_pool': 4.309891e+00, 'pool_scale': 5.106340e+01, 'w_out': 1.710665e+00, 'g_mem_q': 1.148565e-01, 'g_mem_kv': 6.447019e-01, 'w_mem_q': 1.574313e-01, 'w_mem_kv': 3.020883e-01, 'g_q_mem': 4.906012e+00, 'g_k_mem': 4.924720e+00, 'w_mem_out': 2.933668e-01, 'g_ffn': 4.947780e+01, 'w_gate_up': 3.385587e-01, 'w_down': 5.829368e-01}


def _to_microbatches(a, axis):
    t = _jnp.moveaxis(a, axis, 0)
    t = t.reshape((N_MICROBATCH, t.shape[0] // N_MICROBATCH) + t.shape[1:])
    return _jnp.moveaxis(t, 1, axis + 1)


def setup_inputs(seed: int = 0) -> dict:
    inp = _fwd_setup_inputs(seed)
    key = _jax.random.fold_in(_jax.random.key(seed), 7919)
    shape, _ = _output_shape()
    out = dict(inp)
    out["loss_target"] = _jax.random.normal(_jax.random.fold_in(key, 0), shape, _jnp.float32)
    for i, name in enumerate(TWIN_WEIGHTS):
        w = inp[name].astype(_jnp.float32)
        if MOMENT_SCALE is None:
            s = _jnp.sqrt(_jnp.mean(_jnp.square(w)) + 1e-30)
        else:
            s = MOMENT_SCALE[name]
        km, kv = _jax.random.split(_jax.random.fold_in(key, i + 1))
        out[name] = w
        out["m_" + name] = s * _jax.random.normal(km, w.shape, _jnp.float32)
        out["v_" + name] = (s * s) * _jax.random.uniform(kv, w.shape, _jnp.float32, 0.5, 1.5)
    if N_MICROBATCH > 1:
        for name, axis in PER_EXAMPLE_BATCH_AXIS.items():
            out[name] = _to_microbatches(out[name], axis)
    return {'x': out['x'], 'mem': out['mem'], 'g_mix': out['g_mix'], 'w_in': out['w_in'], 'b_forget': out['b_forget'], 'g_q_fox': out['g_q_fox'], 'g_k_fox': out['g_k_fox'], 'w_pool': out['w_pool'], 'pool_scale': out['pool_scale'], 'w_out': out['w_out'], 'g_mem_q': out['g_mem_q'], 'g_mem_kv': out['g_mem_kv'], 'w_mem_q': out['w_mem_q'], 'w_mem_kv': out['w_mem_kv'], 'g_q_mem': out['g_q_mem'], 'g_k_mem': out['g_k_mem'], 'w_mem_out': out['w_mem_out'], 'g_ffn': out['g_ffn'], 'w_gate_up': out['w_gate_up'], 'w_down': out['w_down'], 'loss_target': out['loss_target'], 'm_g_mix': out['m_g_mix'], 'm_w_in': out['m_w_in'], 'm_b_forget': out['m_b_forget'], 'm_g_q_fox': out['m_g_q_fox'], 'm_g_k_fox': out['m_g_k_fox'], 'm_w_pool': out['m_w_pool'], 'm_pool_scale': out['m_pool_scale'], 'm_w_out': out['m_w_out'], 'm_g_mem_q': out['m_g_mem_q'], 'm_g_mem_kv': out['m_g_mem_kv'], 'm_w_mem_q': out['m_w_mem_q'], 'm_w_mem_kv': out['m_w_mem_kv'], 'm_g_q_mem': out['m_g_q_mem'], 'm_g_k_mem': out['m_g_k_mem'], 'm_w_mem_out': out['m_w_mem_out'], 'm_g_ffn': out['m_g_ffn'], 'm_w_gate_up': out['m_w_gate_up'], 'm_w_down': out['m_w_down'], 'v_g_mix': out['v_g_mix'], 'v_w_in': out['v_w_in'], 'v_b_forget': out['v_b_forget'], 'v_g_q_fox': out['v_g_q_fox'], 'v_g_k_fox': out['v_g_k_fox'], 'v_w_pool': out['v_w_pool'], 'v_pool_scale': out['v_pool_scale'], 'v_w_out': out['v_w_out'], 'v_g_mem_q': out['v_g_mem_q'], 'v_g_mem_kv': out['v_g_mem_kv'], 'v_w_mem_q': out['v_w_mem_q'], 'v_w_mem_kv': out['v_w_mem_kv'], 'v_g_q_mem': out['v_g_q_mem'], 'v_g_k_mem': out['v_g_k_mem'], 'v_w_mem_out': out['v_w_mem_out'], 'v_g_ffn': out['v_g_ffn'], 'v_w_gate_up': out['v_w_gate_up'], 'v_w_down': out['v_w_down']}


def _loss(weights, diff, rest, loss_target):
    with _jax.named_scope("forward"):
        args = {**rest, TWIN_DIFF_INPUT: diff, **{k: w.astype(_WEIGHT_DTYPES[k]) for k, w in weights.items()}}
        y = _forward(args)
    with _jax.named_scope("loss_head"):
        err = _jnp.square(y.astype(_jnp.float32) - loss_target)
        return 0.5 * _jnp.sum(_jnp.mean(err, axis=-1)) if err.ndim else 0.5 * err


def _adamw(w, g, m, v):
    m = ADAM_B1 * m + (1.0 - ADAM_B1) * g
    v = ADAM_B2 * v + (1.0 - ADAM_B2) * _jnp.square(g)
    m_hat = m / (1.0 - ADAM_B1 ** ADAM_STEP)
    v_hat = v / (1.0 - ADAM_B2 ** ADAM_STEP)
    delta = -ADAM_LR * (m_hat / (_jnp.sqrt(v_hat) + ADAM_EPS) + ADAM_WD * w)
    return delta, m, v


def reference(x, mem, g_mix, w_in, b_forget, g_q_fox, g_k_fox, w_pool, pool_scale, w_out, g_mem_q, g_mem_kv, w_mem_q, w_mem_kv, g_q_mem, g_k_mem, w_mem_out, g_ffn, w_gate_up, w_down, loss_target, m_g_mix, m_w_in, m_b_forget, m_g_q_fox, m_g_k_fox, m_w_pool, m_pool_scale, m_w_out, m_g_mem_q, m_g_mem_kv, m_w_mem_q, m_w_mem_kv, m_g_q_mem, m_g_k_mem, m_w_mem_out, m_g_ffn, m_w_gate_up, m_w_down, v_g_mix, v_w_in, v_b_forget, v_g_q_fox, v_g_k_fox, v_w_pool, v_pool_scale, v_w_out, v_g_mem_q, v_g_mem_kv, v_w_mem_q, v_w_mem_kv, v_g_q_mem, v_g_k_mem, v_w_mem_out, v_g_ffn, v_w_gate_up, v_w_down):
    given = dict(x=x, mem=mem, g_mix=g_mix, w_in=w_in, b_forget=b_forget, g_q_fox=g_q_fox, g_k_fox=g_k_fox, w_pool=w_pool, pool_scale=pool_scale, w_out=w_out, g_mem_q=g_mem_q, g_mem_kv=g_mem_kv, w_mem_q=w_mem_q, w_mem_kv=w_mem_kv, g_q_mem=g_q_mem, g_k_mem=g_k_mem, w_mem_out=w_mem_out, g_ffn=g_ffn, w_gate_up=w_gate_up, w_down=w_down, loss_target=loss_target, m_g_mix=m_g_mix, m_w_in=m_w_in, m_b_forget=m_b_forget, m_g_q_fox=m_g_q_fox, m_g_k_fox=m_g_k_fox, m_w_pool=m_w_pool, m_pool_scale=m_pool_scale, m_w_out=m_w_out, m_g_mem_q=m_g_mem_q, m_g_mem_kv=m_g_mem_kv, m_w_mem_q=m_w_mem_q, m_w_mem_kv=m_w_mem_kv, m_g_q_mem=m_g_q_mem, m_g_k_mem=m_g_k_mem, m_w_mem_out=m_w_mem_out, m_g_ffn=m_g_ffn, m_w_gate_up=m_w_gate_up, m_w_down=m_w_down, v_g_mix=v_g_mix, v_w_in=v_w_in, v_b_forget=v_b_forget, v_g_q_fox=v_g_q_fox, v_g_k_fox=v_g_k_fox, v_w_pool=v_w_pool, v_pool_scale=v_pool_scale, v_w_out=v_w_out, v_g_mem_q=v_g_mem_q, v_g_mem_kv=v_g_mem_kv, v_w_mem_q=v_w_mem_q, v_w_mem_kv=v_w_mem_kv, v_g_q_mem=v_g_q_mem, v_g_k_mem=v_g_k_mem, v_w_mem_out=v_w_mem_out, v_g_ffn=v_g_ffn, v_w_gate_up=v_w_gate_up, v_w_down=v_w_down)
    weights = {n: given[n] for n in TWIN_WEIGHTS}
    shared = {n: given[n] for n in SHARED_INPUTS}
    per_example = {n: given[n] for n in ['x', 'mem']}
    grad_fn = _jax.value_and_grad(_loss, argnums=(0, 1))

    def one_microbatch(ex, loss_target):
        ex = dict(ex)
        diff = ex.pop(TWIN_DIFF_INPUT)
        return grad_fn(weights, diff, {**shared, **ex}, loss_target)

    if N_MICROBATCH == 1:
        loss, (grad_w, grad_x) = one_microbatch(per_example, given["loss_target"])
    else:
        def body(carry, xs):
            loss_sum, grad_sum = carry
            l_k, (gw_k, gx_k) = one_microbatch(xs[0], xs[1])
            with _jax.named_scope("update"):
                return (loss_sum + l_k, _jax.tree.map(_jnp.add, grad_sum, gw_k)), gx_k

        init = (_jnp.zeros((), _jnp.float32), _jax.tree.map(_jnp.zeros_like, weights))
        (loss, grad_w), grad_x = _jax.lax.scan(body, init, (per_example, given["loss_target"]))
    with _jax.named_scope("update"):
        delta_w, new_m, new_v = {}, {}, {}
        for n in TWIN_WEIGHTS:
            delta_w[n], new_m[n], new_v[n] = _adamw(weights[n], grad_w[n], given["m_" + n], given["v_" + n])
    return (loss, grad_x, *[grad_w[n] for n in TWIN_WEIGHTS], *[delta_w[n] for n in TWIN_WEIGHTS],
            *[new_m[n] for n in TWIN_WEIGHTS], *[new_v[n] for n in TWIN_WEIGHTS])
```

```python
import functools

import numpy as np
import jax
import jax.numpy as jnp
from jax import lax
from jax.experimental import pallas as pl
from jax.experimental.pallas import tpu as pltpu

F32 = jnp.float32
BF16 = jnp.bfloat16

N_DEV = 8
D_MODEL = 1024
DEPTH = 2
FOX_HEADS = 8
FOX_HEAD_DIM = 64
FOX_WIDTH = 512
POOL_WIDTH = 512
POOL_WINDOWS = (2, 4, 8, 16)
POOL_GROUP_DIM = 128
POOL_HALO = 16
IN_COLS = 2056
IN_COLS_PAD = 2176
MEM_HEADS = 4
MEM_HEAD_DIM = 128
MEM_WIDTH = 512
D_FF = 2816
EPS = 1e-6
FOX_SCALE = FOX_HEAD_DIM ** -0.5
MEM_SCALE = MEM_HEAD_DIM ** -0.5
LANES = 128

ADAM_LR = 0.001
ADAM_B1 = 0.9
ADAM_B2 = 0.999
ADAM_EPS = 1e-08
ADAM_WD = 0.01
ADAM_STEP = 10

VMEM_LIMIT = 56 * 1024 * 1024
MESH_ID = pl.DeviceIdType.MESH

WEIGHTS = ['g_mix', 'w_in', 'b_forget', 'g_q_fox', 'g_k_fox', 'w_pool', 'pool_scale', 'w_out', 'g_mem_q', 'g_mem_kv',
           'w_mem_q', 'w_mem_kv', 'g_q_mem', 'g_k_mem', 'w_mem_out', 'g_ffn', 'w_gate_up', 'w_down']
BIG = ['w_in', 'w_out', 'w_mem_q', 'w_mem_kv', 'w_mem_out', 'w_gate_up', 'w_down']
SMALL = [n for n in WEIGHTS if n not in BIG]


def _pcall(body, **kw):
    return pl.pallas_call(body, **kw)


def _params(*sem):
    return pltpu.CompilerParams(dimension_semantics=sem or None, vmem_limit_bytes=VMEM_LIMIT)


def _dot(a, b, dims=None):
    if dims is None:
        return jnp.dot(a, b, preferred_element_type=F32)
    return lax.dot_general(a, b, (dims, ((), ())), preferred_element_type=F32)


NT = ((1,), (1,))
TN = ((0,), (0,))


def _dot_exact(x, ones_bf16):
    hi = x.astype(BF16)
    r1 = x - hi.astype(F32)
    mid = r1.astype(BF16)
    lo = (r1 - mid.astype(F32)).astype(BF16)
    return _dot(hi, ones_bf16) + _dot(mid, ones_bf16) + _dot(lo, ones_bf16)


def _matmul(a, b, *, ta=False, tb=False, out_dtype=F32, res=None, tm=1024, tn=512, tk=None, name):
    m, k = (a.shape[1], a.shape[0]) if ta else a.shape
    n = b.shape[0] if tb else b.shape[1]
    assert k == (b.shape[1] if tb else b.shape[0])
    tm, tn = min(tm, m), min(tn, n)
    tk = min(tk or k, k)
    assert m % tm == 0 and n % tn == 0 and k % tk == 0, (name, m, n, k, tm, tn, tk)
    nk = k // tk
    dims = ((0 if ta else 1,), (1 if tb else 0,))

    def body(*refs):
        a_ref, b_ref = refs[0], refs[1]
        r_ref = refs[2] if res is not None else None
        o_ref = refs[3] if res is not None else refs[2]
        part = _dot(a_ref[...].astype(BF16), b_ref[...].astype(BF16), dims)

        def finish(acc):
            if r_ref is not None:
                acc = acc + r_ref[...]
            o_ref[...] = acc.astype(o_ref.dtype)

        if nk == 1:
            finish(part)
        else:
            acc_ref = refs[-1]
            kk = pl.program_id(2)

            @pl.when(kk == 0)
            def _():
                acc_ref[...] = part

            @pl.when(kk > 0)
            def _():
                acc_ref[...] += part

            @pl.when(kk == nk - 1)
            def _():
                finish(acc_ref[...])

    a_spec = pl.BlockSpec((tk, tm), lambda i, j, kk: (kk, i)) if ta else pl.BlockSpec((tm, tk), lambda i, j, kk: (i, kk))
    b_spec = pl.BlockSpec((tn, tk), lambda i, j, kk: (j, kk)) if tb else pl.BlockSpec((tk, tn), lambda i, j, kk: (kk, j))
    o_spec = pl.BlockSpec((tm, tn), lambda i, j, kk: (i, j))
    in_specs = [a_spec, b_spec] + ([o_spec] if res is not None else [])
    args = (a, b) + ((res,) if res is not None else ())
    return _pcall(
        body, name=name, grid=(m // tm, n // tn, nk), in_specs=in_specs, out_specs=o_spec,
        out_shape=jax.ShapeDtypeStruct((m, n), out_dtype),
        scratch_shapes=[pltpu.VMEM((tm, tn), F32)] if nk > 1 else [],
        compiler_params=_params("parallel", "parallel", "arbitrary"),
    )(*args)


def _rmsnorm_fwd(h, g, *, bt=512, name):
    t, d = h.shape

    def body(x_ref, g_ref, o_ref):
        x = x_ref[...]
        r = lax.rsqrt(jnp.mean(x * x, axis=-1, keepdims=True) + EPS)
        o_ref[...] = (x * r * g_ref[...]).astype(o_ref.dtype)

    return _pcall(
        body, name=name, grid=(t // bt,),
        in_specs=[pl.BlockSpec((bt, d), lambda i: (i, 0)), pl.BlockSpec((1, d), lambda i: (0, 0))],
        out_specs=pl.BlockSpec((bt, d), lambda i: (i, 0)),
        out_shape=jax.ShapeDtypeStruct((t, d), BF16), compiler_params=_params("parallel"),
    )(h, g.reshape(1, d))


def _rmsnorm_bwd(h, dy, g, dres, *, bt=512, name):
    t, d = h.shape

    def body(x_ref, dy_ref, g_ref, r_ref, dx_ref, dg_ref):
        x = x_ref[...]
        r = lax.rsqrt(jnp.mean(x * x, axis=-1, keepdims=True) + EPS)
        xhat = x * r
        dy_v = dy_ref[...].astype(F32)
        gy = dy_v * g_ref[...]
        dx = r * (gy - xhat * jnp.mean(gy * xhat, axis=-1, keepdims=True))
        dx_ref[...] = r_ref[...] + dx
        part = jnp.sum(dy_v * xhat, axis=0, keepdims=True)

        @pl.when(pl.program_id(0) == 0)
        def _():
            dg_ref[...] = part

        @pl.when(pl.program_id(0) > 0)
        def _():
            dg_ref[...] += part

    row = pl.BlockSpec((bt, d), lambda i: (i, 0))
    vec = pl.BlockSpec((1, d), lambda i: (0, 0))
    dx, dg = _pcall(
        body, name=name, grid=(t // bt,), in_specs=[row, row, vec, row], out_specs=[row, vec],
        out_shape=[jax.ShapeDtypeStruct((t, d), F32), jax.ShapeDtypeStruct((1, d), F32)],
        compiler_params=_params("arbitrary"),
    )(h, dy, g.reshape(1, d), dres)
    return dx, dg.reshape(d)


def _group_matrix(width, group):
    r = lax.broadcasted_iota(jnp.int32, (width, width), 0) // group
    c = lax.broadcasted_iota(jnp.int32, (width, width), 1) // group
    return (r == c).astype(BF16)


def _qkv_prep(z, gq, gk, *, bt=512):
    t = z.shape[0]
    w = FOX_WIDTH

    def body(q_ref, k_ref, v_ref, gq_ref, gk_ref, qo_ref, ko_ref, vo_ref):
        gm = _group_matrix(w, FOX_HEAD_DIM)
        for x_ref, g_ref, o_ref, scale in ((q_ref, gq_ref, qo_ref, FOX_SCALE), (k_ref, gk_ref, ko_ref, 1.0)):
            x = x_ref[...]
            ms = _dot_exact(x * x, gm) * (1.0 / FOX_HEAD_DIM)
            y = x * lax.rsqrt(ms + EPS) * g_ref[...]
            o_ref[...] = (y * scale).astype(BF16)
        vo_ref[...] = v_ref[...].astype(BF16)

    col = lambda c: pl.BlockSpec((bt, w), lambda i, c=c: (i, c))
    vec = pl.BlockSpec((1, w), lambda i: (0, 0))
    out = pl.BlockSpec((bt, w), lambda i: (i, 0))
    return _pcall(
        body, name="fox_qkv_prep", grid=(t // bt,), in_specs=[col(0), col(1), col(2), vec, vec], out_specs=[out, out, out],
        out_shape=[jax.ShapeDtypeStruct((t, w), BF16)] * 3, compiler_params=_params("parallel"),
    )(z, z, z, jnp.tile(gq, FOX_HEADS).reshape(1, w), jnp.tile(gk, FOX_HEADS).reshape(1, w))


def _log_sigmoid(f):
    return jnp.minimum(f, 0.0) - jnp.log1p(jnp.exp(-jnp.abs(f)))


def _forget_cumsum(ft, b, *, chunk=512):
    hh, t = ft.shape

    def body(f_ref, b_ref, c_ref):
        r = lax.broadcasted_iota(jnp.int32, (chunk, chunk), 0)
        c = lax.broadcasted_iota(jnp.int32, (chunk, chunk), 1)
        upper = (r <= c).astype(BF16)
        carry = jnp.zeros((hh, 1), F32)
        for ch in range(t // chunk):
            sl = slice(ch * chunk, (ch + 1) * chunk)
            cs = _dot_exact(_log_sigmoid(f_ref[:, sl] + b_ref[...]), upper) + carry
            c_ref[:, sl] = cs
            carry = cs[:, chunk - 1:chunk]

    return _pcall(body, name="fox_forget_cumsum", out_shape=jax.ShapeDtypeStruct((hh, t), F32),
                  compiler_params=_params())(ft, b.reshape(hh, 1))


def _forget_cumsum_bwd(dc_keys, dc_queries, ft, b, *, chunk=512):
    hh, t = ft.shape

    def body(dck_ref, dcq_ref, f_ref, b_ref, df_ref, db_ref):
        r = lax.broadcasted_iota(jnp.int32, (chunk, chunk), 0)
        c = lax.broadcasted_iota(jnp.int32, (chunk, chunk), 1)
        lower = (r >= c).astype(BF16)
        carry = jnp.zeros((hh, 1), F32)
        db = jnp.zeros((hh, 1), F32)
        for ch in reversed(range(t // chunk)):
            sl = slice(ch * chunk, (ch + 1) * chunk)
            dls = _dot_exact(dck_ref[:, sl] + dcq_ref[:, sl], lower) + carry
            carry = dls[:, 0:1]
            df = dls * jax.nn.sigmoid(-(f_ref[:, sl] + b_ref[...]))
            df_ref[:, sl] = df
            db = db + jnp.sum(df, axis=1, keepdims=True)
        db_ref[...] = db

    return _pcall(body, name="fox_forget_cumsum_bwd",
                  out_shape=[jax.ShapeDtypeStruct((hh, t), F32), jax.ShapeDtypeStruct((hh, 1), F32)],
                  compiler_params=_params())(dc_keys, dc_queries, ft, b.reshape(hh, 1))


def _lane_is_first_head():
    return lax.broadcasted_iota(jnp.int32, (1, LANES), 1) < FOX_HEAD_DIM


def _fox_fwd(qn, kn, vb, ccol, crow, *, bq=512):
    t = qn.shape[0]
    nq = t // bq
    pairs = FOX_WIDTH // LANES
    tiles = [(i, j) for i in range(nq) for j in range(i + 1)]
    it = jnp.asarray(np.array([a for a, _ in tiles], np.int32))
    jt = jnp.asarray(np.array([b for _, b in tiles], np.int32))

    def body(it_ref, jt_ref, q_ref, k_ref, v_ref, cc_ref, cr_ref, o_ref, lse_ref, m_sc, l_sc, acc_sc):
        s_id = pl.program_id(1)
        i, j = it_ref[s_id], jt_ref[s_id]
        first = _lane_is_first_head()

        @pl.when(j == 0)
        def _():
            m_sc[...] = jnp.full(m_sc.shape, -jnp.inf, F32)
            l_sc[...] = jnp.zeros(l_sc.shape, F32)
            acc_sc[...] = jnp.zeros(acc_sc.shape, F32)

        def tile(diagonal):
            q2, k2, v2 = q_ref[...], k_ref[...], v_ref[...]
            for hh in range(2):
                qh = jnp.where(first if hh == 0 else jnp.logical_not(first), q2, jnp.zeros_like(q2))
                s = _dot(qh, k2, NT) + (cc_ref[0, :, hh:hh + 1] - cr_ref[0, hh:hh + 1, :])
                if diagonal:
                    rr = lax.broadcasted_iota(jnp.int32, s.shape, 0)
                    cc = lax.broadcasted_iota(jnp.int32, s.shape, 1)
                    s = jnp.where(cc <= rr, s, -jnp.inf)
                m_prev = m_sc[hh]
                m_new = jnp.maximum(m_prev, jnp.max(s, axis=1, keepdims=True))
                alpha = jnp.exp(m_prev - m_new)
                p = jnp.exp(s - m_new)
                l_sc[hh] = alpha * l_sc[hh] + jnp.sum(p, axis=1, keepdims=True)
                acc_sc[hh] = alpha * acc_sc[hh] + _dot(p.astype(BF16), v2)
                m_sc[hh] = m_new

        @pl.when(j < i)
        def _():
            tile(False)

        @pl.when(j == i)
        def _():
            tile(True)
            o = jnp.where(first, acc_sc[0] / l_sc[0], acc_sc[1] / l_sc[1])
            o_ref[...] = o.astype(o_ref.dtype)
            for hh in range(2):
                lse_ref[0, :, hh:hh + 1] = m_sc[hh] + jnp.log(l_sc[hh])

    qspec = pl.BlockSpec((bq, LANES), lambda p, s, it, jt: (it[s], p))
    kspec = pl.BlockSpec((bq, LANES), lambda p, s, it, jt: (jt[s], p))
    colq = pl.BlockSpec((1, bq, 2), lambda p, s, it, jt: (p, it[s], 0))
    rowk = pl.BlockSpec((1, 2, bq), lambda p, s, it, jt: (p, 0, jt[s]))
    return _pcall(
        body, name="fox_attention_fwd",
        grid_spec=pltpu.PrefetchScalarGridSpec(
            num_scalar_prefetch=2, grid=(pairs, len(tiles)),
            in_specs=[qspec, kspec, kspec, colq, rowk], out_specs=[qspec, colq],
            scratch_shapes=[pltpu.VMEM((2, bq, 1), F32), pltpu.VMEM((2, bq, 1), F32), pltpu.VMEM((2, bq, LANES), F32)]),
        out_shape=[jax.ShapeDtypeStruct((t, FOX_WIDTH), BF16), jax.ShapeDtypeStruct((pairs, t, 2), F32)],
        compiler_params=_params("parallel", "arbitrary"),
    )(it, jt, qn, kn, vb, ccol, crow)


def _fox_bwd(qn, kn, knt, vb, do, lse_row, delta_row, crow, ccol, *, bq=512):
    t = qn.shape[0]
    nq = t // bq
    pairs = FOX_WIDTH // LANES
    tiles = [(i, j) for j in range(nq) for i in range(j, nq)]
    it = jnp.asarray(np.array([a for a, _ in tiles], np.int32))
    jt = jnp.asarray(np.array([b for _, b in tiles], np.int32))

    def body(it_ref, jt_ref, q_ref, k_ref, kt_ref, v_ref, do_ref, lse_ref, dl_ref, cr_ref, cc_ref,
             dqt_ref, dk_ref, dv_ref, dc_ref, dr_ref, dk_sc, dv_sc, dc_sc):
        s_id = pl.program_id(1)
        i, j = it_ref[s_id], jt_ref[s_id]
        first = _lane_is_first_head()

        @pl.when(s_id == 0)
        def _():
            dqt_ref[...] = jnp.zeros(dqt_ref.shape, F32)
            dr_ref[...] = jnp.zeros(dr_ref.shape, F32)

        @pl.when(i == j)
        def _():
            dk_sc[...] = jnp.zeros(dk_sc.shape, F32)
            dv_sc[...] = jnp.zeros(dv_sc.shape, F32)
            dc_sc[...] = jnp.zeros(dc_sc.shape, F32)

        def tile(diagonal):
            q2, k2, kt2, v2, do2 = q_ref[...], k_ref[...], kt_ref[...], v_ref[...], do_ref[...]
            dk_t, dv_t, dqt_t = [], [], []
            for hh in range(2):
                mine = first if hh == 0 else jnp.logical_not(first)
                kh = jnp.where(mine, k2, jnp.zeros_like(k2))
                vh = jnp.where(mine, v2, jnp.zeros_like(v2))
                kth = kt2[hh * FOX_HEAD_DIM:(hh + 1) * FOX_HEAD_DIM]
                st = _dot(kh, q2, NT) + (cr_ref[0, hh:hh + 1, :] - cc_ref[0, :, hh:hh + 1])
                if diagonal:
                    rr = lax.broadcasted_iota(jnp.int32, st.shape, 0)
                    cc = lax.broadcasted_iota(jnp.int32, st.shape, 1)
                    st = jnp.where(rr <= cc, st, -jnp.inf)
                pt = jnp.exp(st - lse_ref[0, hh:hh + 1, :])
                dv_t.append(_dot(pt.astype(BF16), do2))
                dpt = _dot(vh, do2, NT)
                dst = pt * (dpt - dl_ref[0, hh:hh + 1, :])
                dc_sc[hh] += jnp.sum(dst, axis=1, keepdims=True)
                dr_ref[0, i, hh:hh + 1, :] += jnp.sum(dst, axis=0, keepdims=True)
                dsb = dst.astype(BF16)
                dk_t.append(_dot(dsb, q2))
                dqt_t.append(_dot(kth, dsb))
            dk_sc[...] += jnp.where(first, dk_t[0], dk_t[1])
            dv_sc[...] += jnp.where(first, dv_t[0], dv_t[1])
            dqt_ref[0, i] += jnp.concatenate(dqt_t, axis=0)

        @pl.when(i > j)
        def _():
            tile(False)

        @pl.when(i == j)
        def _():
            tile(True)

        @pl.when(i == nq - 1)
        def _():
            dk_ref[...] = dk_sc[...]
            dv_ref[...] = dv_sc[...]
            for hh in range(2):
                dc_ref[0, :, hh:hh + 1] = -dc_sc[hh]

    qspec = pl.BlockSpec((bq, LANES), lambda p, s, it, jt: (it[s], p))
    kspec = pl.BlockSpec((bq, LANES), lambda p, s, it, jt: (jt[s], p))
    ktspec = pl.BlockSpec((LANES, bq), lambda p, s, it, jt: (p, jt[s]))
    rowq = pl.BlockSpec((1, 2, bq), lambda p, s, it, jt: (p, 0, it[s]))
    colk = pl.BlockSpec((1, bq, 2), lambda p, s, it, jt: (p, jt[s], 0))
    dqt_spec = pl.BlockSpec((1, nq, LANES, bq), lambda p, s, it, jt: (p, 0, 0, 0))
    dr_spec = pl.BlockSpec((1, nq, 2, bq), lambda p, s, it, jt: (p, 0, 0, 0))
    dqt, dk, dv, dc_keys, dc_queries = _pcall(
        body, name="fox_attention_bwd",
        grid_spec=pltpu.PrefetchScalarGridSpec(
            num_scalar_prefetch=2, grid=(pairs, len(tiles)),
            in_specs=[qspec, kspec, ktspec, kspec, qspec, rowq, rowq, rowq, colk],
            out_specs=[dqt_spec, kspec, kspec, colk, dr_spec],
            scratch_shapes=[pltpu.VMEM((bq, LANES), F32), pltpu.VMEM((bq, LANES), F32), pltpu.VMEM((2, bq, 1), F32)]),
        out_shape=[jax.ShapeDtypeStruct((pairs, nq, LANES, bq), F32), jax.ShapeDtypeStruct((t, FOX_WIDTH), F32),
                   jax.ShapeDtypeStruct((t, FOX_WIDTH), F32), jax.ShapeDtypeStruct((pairs, t, 2), F32),
                   jax.ShapeDtypeStruct((pairs, nq, 2, bq), F32)],
        compiler_params=_params("parallel", "arbitrary"),
    )(it, jt, qn, kn, knt, vb, do, lse_row, delta_row, crow, ccol)
    dq = jnp.transpose(dqt, (1, 3, 0, 2)).reshape(t, FOX_WIDTH)
    dc_keys = jnp.transpose(dc_keys, (0, 2, 1)).reshape(FOX_HEADS, t)
    dc_queries = jnp.transpose(dc_queries, (0, 2, 1, 3)).reshape(FOX_HEADS, t)
    return dq, dk, dv, dc_keys, dc_queries


def _fox_delta(dcat, fox, *, bt=512):
    t = fox.shape[0]
    w = FOX_WIDTH

    def body(do_ref, o_ref, dob_ref, dl_ref):
        dob = do_ref[...].astype(BF16)
        r = lax.broadcasted_iota(jnp.int32, (w, LANES), 0) // FOX_HEAD_DIM
        c = lax.broadcasted_iota(jnp.int32, (w, LANES), 1)
        dl_ref[...] = _dot_exact(dob.astype(F32) * o_ref[...].astype(F32), (r == c).astype(BF16))
        dob_ref[...] = dob

    blk = pl.BlockSpec((bt, w), lambda i: (i, 0))
    return _pcall(
        body, name="fox_delta", grid=(t // bt,), in_specs=[blk, blk],
        out_specs=[blk, pl.BlockSpec((bt, LANES), lambda i: (i, 0))],
        out_shape=[jax.ShapeDtypeStruct((t, w), BF16), jax.ShapeDtypeStruct((t, LANES), F32)],
        compiler_params=_params("parallel"),
    )(dcat, fox)


def _mixer_dz(z, dq, dk, dv, dpin, dfpad, gq, gk, *, bt=256):
    t = z.shape[0]
    w = FOX_WIDTH

    def body(q_ref, k_ref, dq_ref, dk_ref, dv_ref, dp_ref, df_ref, gq_ref, gk_ref, dz_ref, dgq_ref, dgk_ref):
        gm = _group_matrix(w, FOX_HEAD_DIM)
        first_step = pl.program_id(0) == 0
        for n, (x_ref, dy_ref, g_ref, dg_ref, scale) in enumerate(
                ((q_ref, dq_ref, gq_ref, dgq_ref, FOX_SCALE), (k_ref, dk_ref, gk_ref, dgk_ref, 1.0))):
            x = x_ref[...]
            r = lax.rsqrt(_dot_exact(x * x, gm) * (1.0 / FOX_HEAD_DIM) + EPS)
            xhat = x * r
            dy = dy_ref[...] * scale
            gy = dy * g_ref[...]
            dx = r * (gy - xhat * (_dot_exact(gy * xhat, gm) * (1.0 / FOX_HEAD_DIM)))
            dz_ref[:, n * w:(n + 1) * w] = dx.astype(BF16)
            part = jnp.sum(dy * xhat, axis=0, keepdims=True)

            @pl.when(first_step)
            def _():
                dg_ref[...] = part

            @pl.when(jnp.logical_not(first_step))
            def _():
                dg_ref[...] += part

        dz_ref[:, 2 * w:3 * w] = dv_ref[...].astype(BF16)
        dz_ref[:, 3 * w:4 * w] = dp_ref[...].astype(BF16)
        dz_ref[:, 4 * w:] = df_ref[...].astype(BF16)

    col = lambda c: pl.BlockSpec((bt, w), lambda i, c=c: (i, c))
    blk = pl.BlockSpec((bt, w), lambda i: (i, 0))
    vec = pl.BlockSpec((1, w), lambda i: (0, 0))
    dz, dgq, dgk = _pcall(
        body, name="mixer_dz", grid=(t // bt,),
        in_specs=[col(0), col(1), blk, blk, blk, blk, pl.BlockSpec((bt, LANES), lambda i: (i, 0)), vec, vec],
        out_specs=[pl.BlockSpec((bt, IN_COLS_PAD), lambda i: (i, 0)), vec, vec],
        out_shape=[jax.ShapeDtypeStruct((t, IN_COLS_PAD), BF16), jax.ShapeDtypeStruct((1, w), F32),
                   jax.ShapeDtypeStruct((1, w), F32)],
        compiler_params=_params("arbitrary"),
    )(z, z, dq, dk, dv, dpin, dfpad, jnp.tile(gq, FOX_HEADS).reshape(1, w), jnp.tile(gk, FOX_HEADS).reshape(1, w))
    return dz, dgq.reshape(FOX_HEADS, FOX_HEAD_DIM).sum(0), dgk.reshape(FOX_HEADS, FOX_HEAD_DIM).sum(0)


def _pool_fwd(z, wp, scale, *, bt=512):
    t = z.shape[0]
    w = POOL_WIDTH
    hb = bt // POOL_HALO

    def body(p_ref, h_ref, wp_ref, sc_ref, y_ref, mx_ref):
        i = pl.program_id(0)
        cur = p_ref[...]
        halo = jnp.where(i > 0, h_ref[...], 0.0)
        ext = jnp.concatenate([halo, cur], axis=0)
        trow = i * bt + lax.broadcasted_iota(jnp.int32, (bt, 1), 0)
        for g, win in enumerate(POOL_WINDOWS):
            sl = slice(g * LANES, (g + 1) * LANES)
            e = ext[:, sl]
            acc = e[POOL_HALO:]
            for k in range(1, win):
                acc = acc + pltpu.roll(e, k, 0)[POOL_HALO:]
            cnt = jnp.minimum(trow + 1, win).astype(F32)
            mixed = (acc / cnt - cur[:, sl]).astype(BF16)
            mx_ref[:, sl] = mixed
            y_ref[:, sl] = (_dot(mixed, wp_ref[g]) * sc_ref[:, sl]).astype(BF16)

    blk = pl.BlockSpec((bt, w), lambda i: (i, 0))
    return _pcall(
        body, name="pool_fwd", grid=(t // bt,),
        in_specs=[pl.BlockSpec((bt, w), lambda i: (i, 3)),
                  pl.BlockSpec((POOL_HALO, w), lambda i: (jnp.maximum(i * hb - 1, 0), 3)),
                  pl.BlockSpec((len(POOL_WINDOWS), LANES, LANES), lambda i: (0, 0, 0)),
                  pl.BlockSpec((1, w), lambda i: (0, 0))],
        out_specs=[blk, blk], out_shape=[jax.ShapeDtypeStruct((t, w), BF16)] * 2,
        compiler_params=_params("parallel"),
    )(z, z, wp, scale.reshape(1, w))


def _pool_bwd(dcat, mixed, wp, scale, *, bt=512):
    t = mixed.shape[0]
    w = POOL_WIDTH
    hb = bt // POOL_HALO
    nb = t // bt
    n_ext = bt + POOL_HALO

    def body(d_ref, h_ref, mx_ref, wp_ref, sc_ref, dp_ref, dwp_ref, dsc_ref):
        i = pl.program_id(0)
        cur = d_ref[...]
        nxt = jnp.where(i < nb - 1, h_ref[...], 0.0)
        ext = jnp.concatenate([cur, nxt], axis=0)
        trow = i * bt + lax.broadcasted_iota(jnp.int32, (n_ext, 1), 0)

        @pl.when(i == 0)
        def _():
            dwp_ref[...] = jnp.zeros(dwp_ref.shape, F32)
            dsc_ref[...] = jnp.zeros(dsc_ref.shape, F32)

        for g, win in enumerate(POOL_WINDOWS):
            sl = slice(g * LANES, (g + 1) * LANES)
            dy = (ext[:, sl] * sc_ref[:, sl]).astype(BF16)
            dm = _dot(dy, wp_ref[g], NT)
            mixed_g = mx_ref[:, sl]
            dsc_ref[:, sl] += jnp.sum(cur[:, sl] * _dot(mixed_g, wp_ref[g]), axis=0, keepdims=True)
            dwp_ref[g] += _dot(mixed_g, dy[:bt], TN)
            r = dm / jnp.minimum(trow + 1, win).astype(F32)
            acc = r[:bt]
            for k in range(1, win):
                acc = acc + pltpu.roll(r, n_ext - k, 0)[:bt]
            dp_ref[:, sl] = acc - dm[:bt]

    return _pcall(
        body, name="pool_bwd", grid=(nb,),
        in_specs=[pl.BlockSpec((bt, w), lambda i: (i, 1)),
                  pl.BlockSpec((POOL_HALO, w), lambda i: (jnp.minimum((i + 1) * hb, t // POOL_HALO - 1), 1)),
                  pl.BlockSpec((bt, w), lambda i: (i, 0)),
                  pl.BlockSpec((len(POOL_WINDOWS), LANES, LANES), lambda i: (0, 0, 0)),
                  pl.BlockSpec((1, w), lambda i: (0, 0))],
        out_specs=[pl.BlockSpec((bt, w), lambda i: (i, 0)),
                   pl.BlockSpec((len(POOL_WINDOWS), LANES, LANES), lambda i: (0, 0, 0)),
                   pl.BlockSpec((1, w), lambda i: (0, 0))],
        out_shape=[jax.ShapeDtypeStruct((t, w), F32), jax.ShapeDtypeStruct((len(POOL_WINDOWS), LANES, LANES), F32),
                   jax.ShapeDtypeStruct((1, w), F32)],
        compiler_params=_params("arbitrary"),
    )(dcat, dcat, mixed, wp, scale.reshape(1, w))


def _head_rms(x):
    return lax.rsqrt(jnp.mean(x * x, axis=-1, keepdims=True) + EPS)


def _mem_kv_fwd(mem, g_kv, w_kv, g_k):
    mlen, d = mem.shape

    def body(m_ref, g_ref, w_ref, gk_ref, mn_ref, mkv_ref, mk_ref, mv_ref):
        x = m_ref[...]
        mn = (x * lax.rsqrt(jnp.mean(x * x, axis=-1, keepdims=True) + EPS) * g_ref[...]).astype(BF16)
        mn_ref[...] = mn
        mkv = _dot(mn, w_ref[...])
        mkv_ref[...] = mkv
        for h in range(MEM_HEADS):
            sl = slice(h * MEM_HEAD_DIM, (h + 1) * MEM_HEAD_DIM)
            kh = mkv[:, sl]
            mk_ref[:, sl] = (kh * _head_rms(kh) * gk_ref[...]).astype(BF16)
        mv_ref[...] = mkv[:, MEM_WIDTH:].astype(BF16)

    return _pcall(
        body, name="mem_kv_fwd",
        out_shape=[jax.ShapeDtypeStruct((mlen, d), BF16), jax.ShapeDtypeStruct((mlen, 2 * MEM_WIDTH), F32),
                   jax.ShapeDtypeStruct((mlen, MEM_WIDTH), BF16), jax.ShapeDtypeStruct((mlen, MEM_WIDTH), BF16)],
        compiler_params=_params(),
    )(mem, g_kv.reshape(1, d), w_kv, g_k.reshape(1, MEM_HEAD_DIM))


def _mem_kv_bwd(dmk, dmv, mkv, mn, mem, g_kv, w_kv, g_k):
    mlen, d = mem.shape

    def body(dmk_ref, dmv_ref, mkv_ref, mn_ref, m_ref, g_ref, w_ref, gk_ref, dw_ref, dg_ref, dgk_ref, dkv_sc):
        dgk = jnp.zeros((1, MEM_HEAD_DIM), F32)
        for h in range(MEM_HEADS):
            sl = slice(h * MEM_HEAD_DIM, (h + 1) * MEM_HEAD_DIM)
            x = mkv_ref[:, sl]
            r = _head_rms(x)
            xhat = x * r
            dy = dmk_ref[:, sl]
            gy = dy * gk_ref[...]
            dkv_sc[:, sl] = (r * (gy - xhat * jnp.mean(gy * xhat, axis=-1, keepdims=True))).astype(BF16)
            dgk = dgk + jnp.sum(dy * xhat, axis=0, keepdims=True)
        dgk_ref[...] = dgk
        dkv_sc[:, MEM_WIDTH:] = dmv_ref[...].astype(BF16)
        dkv = dkv_sc[...]
        dw_ref[...] = _dot(mn_ref[...], dkv, TN)
        dmn = _dot(dkv, w_ref[...], NT)
        x = m_ref[...]
        xhat = x * lax.rsqrt(jnp.mean(x * x, axis=-1, keepdims=True) + EPS)
        dg_ref[...] = jnp.sum(dmn * xhat, axis=0, keepdims=True)

    dw, dg, dgk = _pcall(
        body, name="mem_kv_bwd",
        out_shape=[jax.ShapeDtypeStruct((d, 2 * MEM_WIDTH), F32), jax.ShapeDtypeStruct((1, d), F32),
                   jax.ShapeDtypeStruct((1, MEM_HEAD_DIM), F32)],
        scratch_shapes=[pltpu.VMEM((mlen, 2 * MEM_WIDTH), BF16)],
        compiler_params=_params(),
    )(dmk, dmv, mkv, mn, mem, g_kv.reshape(1, d), w_kv, g_k.reshape(1, MEM_HEAD_DIM))
    return dw, dg.reshape(d), dgk.reshape(MEM_HEAD_DIM)


def _cross_probs(x, g, mk_h):
    r = _head_rms(x)
    xhat = x * r
    qn = (xhat * g).astype(BF16)
    s = _dot(qn, mk_h, NT) * MEM_SCALE
    e = jnp.exp(s - jnp.max(s, axis=-1, keepdims=True))
    return r, xhat, qn, e / jnp.sum(e, axis=-1, keepdims=True)


def _cross_fwd(mq_raw, g_q, mk, mv, *, bt=512):
    t = mq_raw.shape[0]
    mlen = mk.shape[0]

    def body(x_ref, g_ref, mk_ref, mv_ref, o_ref):
        for h in range(MEM_HEADS):
            sl = slice(h * MEM_HEAD_DIM, (h + 1) * MEM_HEAD_DIM)
            _, _, _, p = _cross_probs(x_ref[:, sl], g_ref[...], mk_ref[:, sl])
            o_ref[:, sl] = _dot(p.astype(BF16), mv_ref[:, sl]).astype(BF16)

    blk = pl.BlockSpec((bt, MEM_WIDTH), lambda i: (i, 0))
    kv = pl.BlockSpec((mlen, MEM_WIDTH), lambda i: (0, 0))
    return _pcall(
        body, name="cross_attention_fwd", grid=(t // bt,),
        in_specs=[blk, pl.BlockSpec((1, MEM_HEAD_DIM), lambda i: (0, 0)), kv, kv], out_specs=blk,
        out_shape=jax.ShapeDtypeStruct((t, MEM_WIDTH), BF16), compiler_params=_params("parallel"),
    )(mq_raw, g_q.reshape(1, MEM_HEAD_DIM), mk, mv)


def _cross_bwd(mq_raw, dmo, g_q, mk, mv, *, bt=512):
    t = mq_raw.shape[0]
    mlen = mk.shape[0]

    def body(x_ref, do_ref, g_ref, mk_ref, mv_ref, dx_ref, dmk_ref, dmv_ref, dg_ref):
        @pl.when(pl.program_id(0) == 0)
        def _():
            dmk_ref[...] = jnp.zeros(dmk_ref.shape, F32)
            dmv_ref[...] = jnp.zeros(dmv_ref.shape, F32)
            dg_ref[...] = jnp.zeros(dg_ref.shape, F32)

        for h in range(MEM_HEADS):
            sl = slice(h * MEM_HEAD_DIM, (h + 1) * MEM_HEAD_DIM)
            r, xhat, qn, p = _cross_probs(x_ref[:, sl], g_ref[...], mk_ref[:, sl])
            do = do_ref[:, sl]
            dp = _dot(do, mv_ref[:, sl], NT)
            ds = (p * (dp - jnp.sum(p * dp, axis=-1, keepdims=True)) * MEM_SCALE).astype(BF16)
            dmv_ref[:, sl] += _dot(p.astype(BF16), do, TN)
            dmk_ref[:, sl] += _dot(ds, qn, TN)
            dqn = _dot(ds, mk_ref[:, sl])
            gy = dqn * g_ref[...]
            dx_ref[:, sl] = (r * (gy - xhat * jnp.mean(gy * xhat, axis=-1, keepdims=True))).astype(BF16)
            dg_ref[...] += jnp.sum(dqn * xhat, axis=0, keepdims=True)

    blk = pl.BlockSpec((bt, MEM_WIDTH), lambda i: (i, 0))
    kv = pl.BlockSpec((mlen, MEM_WIDTH), lambda i: (0, 0))
    gs = pl.BlockSpec((1, MEM_HEAD_DIM), lambda i: (0, 0))
    dx, dmk, dmv, dg = _pcall(
        body, name="cross_attention_bwd", grid=(t // bt,),
        in_specs=[blk, blk, gs, kv, kv], out_specs=[blk, kv, kv, gs],
        out_shape=[jax.ShapeDtypeStruct((t, MEM_WIDTH), BF16), jax.ShapeDtypeStruct((mlen, MEM_WIDTH), F32),
                   jax.ShapeDtypeStruct((mlen, MEM_WIDTH), F32), jax.ShapeDtypeStruct((1, MEM_HEAD_DIM), F32)],
        compiler_params=_params("arbitrary"),
    )(mq_raw, dmo, g_q.reshape(1, MEM_HEAD_DIM), mk, mv)
    return dx, dmk, dmv, dg.reshape(MEM_HEAD_DIM)


def _swiglu_fwd(gu, *, bt=256):
    t = gu.shape[0]

    def body(g_ref, u_ref, a_ref):
        g = g_ref[...]
        a_ref[...] = (g * jax.nn.sigmoid(g) * u_ref[...]).astype(BF16)

    return _pcall(
        body, name="swiglu_fwd", grid=(t // bt,),
        in_specs=[pl.BlockSpec((bt, D_FF), lambda i: (i, 0)), pl.BlockSpec((bt, D_FF), lambda i: (i, 1))],
        out_specs=pl.BlockSpec((bt, D_FF), lambda i: (i, 0)),
        out_shape=jax.ShapeDtypeStruct((t, D_FF), BF16), compiler_params=_params("parallel"),
    )(gu, gu)


def _swiglu_bwd(gu, dact, *, bt=256):
    t = gu.shape[0]

    def body(g_ref, u_ref, da_ref, dgu_ref):
        g = g_ref[...]
        da = da_ref[...]
        sg = jax.nn.sigmoid(g)
        silu = g * sg
        dgu_ref[:, :D_FF] = (da * u_ref[...] * (sg + silu * (1.0 - sg))).astype(BF16)
        dgu_ref[:, D_FF:] = (da * silu).astype(BF16)

    return _pcall(
        body, name="swiglu_bwd", grid=(t // bt,),
        in_specs=[pl.BlockSpec((bt, D_FF), lambda i: (i, 0)), pl.BlockSpec((bt, D_FF), lambda i: (i, 1)),
                  pl.BlockSpec((bt, D_FF), lambda i: (i, 0))],
        out_specs=pl.BlockSpec((bt, 2 * D_FF), lambda i: (i, 0)),
        out_shape=jax.ShapeDtypeStruct((t, 2 * D_FF), BF16), compiler_params=_params("parallel"),
    )(gu, gu, dact)


def _loss_head(y, target, *, bt=512):
    t, d = y.shape

    def body(y_ref, t_ref, dy_ref, l_ref):
        e = y_ref[...] - t_ref[...]
        dy_ref[...] = e * (1.0 / d)
        part = (0.5 / d) * jnp.sum(jnp.sum(e * e, axis=1, keepdims=True), axis=0, keepdims=True)

        @pl.when(pl.program_id(0) == 0)
        def _():
            l_ref[...] = part

        @pl.when(pl.program_id(0) > 0)
        def _():
            l_ref[...] += part

    blk = pl.BlockSpec((bt, d), lambda i: (i, 0))
    dy, loss = _pcall(
        body, name="loss_head", grid=(t // bt,), in_specs=[blk, blk],
        out_specs=[blk, pl.BlockSpec((1, 1), lambda i: (0, 0))],
        out_shape=[jax.ShapeDtypeStruct((t, d), F32), jax.ShapeDtypeStruct((1, 1), F32)],
        compiler_params=_params("arbitrary"),
    )(y, target)
    return loss, dy


def _row_tile(rows, cols, budget=1 << 19):
    best = None
    for cand in range(8, rows + 1, 8):
        if rows % cand == 0 and cand * cols <= budget:
            best = cand
    return best or rows


def _adamw(w, g, m, v, *, name):
    rows, cols = w.shape
    bt = _row_tile(rows, cols)
    c1 = 1.0 - ADAM_B1 ** ADAM_STEP
    c2 = 1.0 - ADAM_B2 ** ADAM_STEP

    def body(w_ref, g_ref, m_ref, v_ref, d_ref, nm_ref, nv_ref):
        g_v = g_ref[...]
        nm = ADAM_B1 * m_ref[...] + (1.0 - ADAM_B1) * g_v
        nv = ADAM_B2 * v_ref[...] + (1.0 - ADAM_B2) * (g_v * g_v)
        nm_ref[...] = nm
        nv_ref[...] = nv
        d_ref[...] = -ADAM_LR * ((nm / c1) / (jnp.sqrt(nv / c2) + ADAM_EPS) + ADAM_WD * w_ref[...])

    blk = pl.BlockSpec((bt, cols), lambda i: (i, 0))
    return _pcall(
        body, name=name, grid=(rows // bt,), in_specs=[blk] * 4, out_specs=[blk] * 3,
        out_shape=[jax.ShapeDtypeStruct((rows, cols), F32)] * 3, compiler_params=_params("parallel"),
    )(w, g, m, v)


def _sum_slots(x, *, name):
    n, rows, cols = x.shape
    bt = _row_tile(rows, cols, budget=1 << 17)

    def body(x_ref, o_ref):
        acc = x_ref[0].astype(F32)
        for s in range(1, n):
            acc = acc + x_ref[s].astype(F32)
        o_ref[...] = acc

    return _pcall(
        body, name=name, grid=(rows // bt,), in_specs=[pl.BlockSpec((n, bt, cols), lambda i: (0, i, 0))],
        out_specs=pl.BlockSpec((bt, cols), lambda i: (i, 0)),
        out_shape=jax.ShapeDtypeStruct((rows, cols), F32), compiler_params=_params("parallel"),
    )(x)


def _any_spec():
    return pl.BlockSpec(memory_space=pl.ANY)


def _all_gather(xs, *, name):
    n = len(xs)

    def body(*refs):
        x_refs, out_refs = refs[:n], refs[n:2 * n]
        send_sems, recv_sems, local_sems = refs[2 * n:]
        x, y, c = lax.axis_index("x"), lax.axis_index("y"), lax.axis_index("c")
        me, sibling = (x, y, c), (x, y, 1 - c)
        chips = [(1 - x, y), (x, 1 - y), (1 - x, 1 - y)]

        def slot(a, px, py, pc):
            return out_refs[a].at[4 * px + 2 * py + pc]

        def copy(a, k, block, to, src=None):
            return pltpu.make_async_remote_copy(
                src_ref=slot(a, *block) if src is None else src, dst_ref=slot(a, *block),
                send_sem=send_sems.at[a, k], recv_sem=recv_sems.at[a, k], device_id=to, device_id_type=MESH_ID)

        mine = [pltpu.make_async_copy(x_refs[a], slot(a, *me), local_sems.at[a]) for a in range(n)]
        for cp in mine:
            cp.start()
        first = []
        for j, chip in enumerate(chips):
            first += [copy(a, 1 + j, me, (*chip, c), src=x_refs[a]) for a in range(n)]
        first += [copy(a, 0, me, sibling, src=x_refs[a]) for a in range(n)]
        for cp in first:
            cp.start()
        passed = []
        for j, chip in enumerate(chips):
            for a in range(n):
                copy(a, 1 + j, (*chip, c), me).wait_recv()
                cp = copy(a, 4 + j, (*chip, c), sibling)
                cp.start()
                passed.append(cp)
        for a in range(n):
            copy(a, 0, sibling, me).wait_recv()
        for j, chip in enumerate(chips):
            for a in range(n):
                copy(a, 4 + j, (*chip, 1 - c), me).wait_recv()
        for cp in first + passed:
            cp.wait_send()
        for cp in mine:
            cp.wait()

    return _pcall(
        body, name=name, in_specs=[_any_spec()] * n, out_specs=[_any_spec()] * n,
        out_shape=[jax.ShapeDtypeStruct((N_DEV,) + x.shape, x.dtype) for x in xs],
        scratch_shapes=[pltpu.SemaphoreType.DMA((n, 7)), pltpu.SemaphoreType.DMA((n, 7)), pltpu.SemaphoreType.DMA((n,))],
    )(*xs)


def _exchange(xs, *, name):
    n = len(xs)

    def body(*refs):
        x_refs, out_refs = refs[:n], refs[n:2 * n]
        send_sems, recv_sems, local_sems = refs[2 * n:]
        x, y, c = lax.axis_index("x"), lax.axis_index("y"), lax.axis_index("c")
        me_idx = 4 * x + 2 * y + c

        def peer(k):
            px = x ^ ((k >> 2) & 1)
            py = y ^ ((k >> 1) & 1)
            pc = c ^ (k & 1)
            return (px, py, pc), 4 * px + 2 * py + pc

        def copy(a, k):
            to, to_idx = peer(k)
            return pltpu.make_async_remote_copy(
                src_ref=x_refs[a].at[to_idx], dst_ref=out_refs[a].at[me_idx],
                send_sem=send_sems.at[a, k - 1], recv_sem=recv_sems.at[a, k - 1], device_id=to, device_id_type=MESH_ID)

        def landing(a, k):
            frm, frm_idx = peer(k)
            return pltpu.make_async_remote_copy(
                src_ref=x_refs[a].at[frm_idx], dst_ref=out_refs[a].at[frm_idx],
                send_sem=send_sems.at[a, k - 1], recv_sem=recv_sems.at[a, k - 1], device_id=frm, device_id_type=MESH_ID)

        mine = [pltpu.make_async_copy(x_refs[a].at[me_idx], out_refs[a].at[me_idx], local_sems.at[a]) for a in range(n)]
        for cp in mine:
            cp.start()
        sends = [copy(a, k) for k in (2, 4, 6, 3, 5, 7, 1) for a in range(n)]
        for cp in sends:
            cp.start()
        for k in range(1, N_DEV):
            for a in range(n):
                landing(a, k).wait_recv()
        for cp in sends:
            cp.wait_send()
        for cp in mine:
            cp.wait()

    return _pcall(
        body, name=name, in_specs=[_any_spec()] * n, out_specs=[_any_spec()] * n,
        out_shape=[jax.ShapeDtypeStruct(x.shape, x.dtype) for x in xs],
        scratch_shapes=[pltpu.SemaphoreType.DMA((n, 7)), pltpu.SemaphoreType.DMA((n, 7)), pltpu.SemaphoreType.DMA((n,))],
    )(*xs)


def _full_weights(gathered, l):
    g = {k: v[:, l] for k, v in gathered.items()}
    w_in = jnp.transpose(g['w_in'], (1, 0, 2)).reshape(D_MODEL, IN_COLS)
    return {
        'w_in': jnp.pad(w_in, ((0, 0), (0, IN_COLS_PAD - IN_COLS))),
        'w_out': g['w_out'].reshape(D_MODEL, D_MODEL),
        'w_mem_q': g['w_mem_q'].reshape(D_MODEL, MEM_WIDTH),
        'w_mem_kv': g['w_mem_kv'].reshape(D_MODEL, 2 * MEM_WIDTH),
        'w_mem_out': jnp.transpose(g['w_mem_out'], (1, 0, 2)).reshape(MEM_WIDTH, D_MODEL),
        'w_gate_up': jnp.transpose(g['w_gate_up'], (1, 0, 2)).reshape(D_MODEL, 2 * D_FF),
        'w_down': g['w_down'].reshape(D_FF, D_MODEL),
    }


def _grad_blocks(name, per_layer):
    g = jnp.stack(per_layer, 0)
    if name == 'w_in':
        g = g[:, :, :IN_COLS]
    if name in ('w_in', 'w_mem_out', 'w_gate_up'):
        rows, cols = g.shape[1], g.shape[2] // N_DEV
        g = jnp.transpose(g.reshape(DEPTH, rows, N_DEV, cols), (2, 0, 1, 3))
    else:
        rows, cols = g.shape[1] // N_DEV, g.shape[2]
        g = jnp.transpose(g.reshape(DEPTH, N_DEV, rows, cols), (1, 0, 2, 3))
    return g.reshape(N_DEV, DEPTH * rows, cols).astype(BF16)


SMALL_SHAPES = {'g_mix': (DEPTH, D_MODEL), 'b_forget': (DEPTH, FOX_HEADS), 'g_q_fox': (DEPTH, FOX_HEAD_DIM),
                'g_k_fox': (DEPTH, FOX_HEAD_DIM), 'w_pool': (DEPTH, 4, POOL_GROUP_DIM, POOL_GROUP_DIM),
                'pool_scale': (DEPTH, POOL_WIDTH), 'g_mem_q': (DEPTH, D_MODEL), 'g_mem_kv': (DEPTH, D_MODEL),
                'g_q_mem': (DEPTH, MEM_HEAD_DIM), 'g_k_mem': (DEPTH, MEM_HEAD_DIM), 'g_ffn': (DEPTH, D_MODEL)}


def _small_rows(name):
    return -(-int(np.prod(SMALL_SHAPES[name])) // LANES)


SMALL_ROWS = -(-sum(_small_rows(n) for n in SMALL) // 8) * 8


def _pack_small(tree):
    parts = []
    for n in SMALL:
        flat = tree[n].reshape(-1).astype(F32)
        parts.append(jnp.pad(flat, (0, _small_rows(n) * LANES - flat.shape[0])))
    flat = jnp.concatenate(parts)
    return jnp.pad(flat, (0, SMALL_ROWS * LANES - flat.shape[0])).reshape(SMALL_ROWS, LANES)


def _unpack_small(packed):
    flat = packed.reshape(-1)
    out, at = {}, 0
    for n in SMALL:
        size = int(np.prod(SMALL_SHAPES[n]))
        out[n] = flat[at:at + size].reshape(SMALL_SHAPES[n])
        at += _small_rows(n) * LANES
    return out


def _pairs_cols(a):
    t = a.shape[0]
    return jnp.transpose(a.reshape(t, FOX_HEADS // 2, 2), (1, 0, 2))


def _pairs_rows(a):
    return a.reshape(FOX_HEADS // 2, 2, a.shape[1])


def _layer_fwd(h0, mem, p, w):
    s = {'h0': h0}
    s['xn1'] = _rmsnorm_fwd(h0, p['g_mix'], name="norm_mix_fwd")
    z = _matmul(s['xn1'], w['w_in'], tm=512, tn=IN_COLS_PAD, name="in_proj_fwd")
    s['z'] = z
    s['qn'], s['kn'], s['vb'] = _qkv_prep(z, p['g_q_fox'], p['g_k_fox'])
    s['ft'] = jnp.transpose(z[:, 4 * FOX_WIDTH:4 * FOX_WIDTH + FOX_HEADS])
    c = _forget_cumsum(s['ft'], p['b_forget'])
    s['ccol'], s['crow'] = _pairs_cols(jnp.transpose(c)), _pairs_rows(c)
    s['fox'], lse = _fox_fwd(s['qn'], s['kn'], s['vb'], s['ccol'], s['crow'])
    s['lse_row'] = jnp.transpose(lse, (0, 2, 1))
    pool, s['mixed'] = _pool_fwd(z, p['w_pool'].astype(BF16), p['pool_scale'])
    s['cat'] = jnp.concatenate([s['fox'], pool], axis=1)
    h1 = _matmul(s['cat'], w['w_out'], res=h0, name="out_proj_fwd")
    s['h1'] = h1

    s['hn2'] = _rmsnorm_fwd(h1, p['g_mem_q'], name="norm_mem_fwd")
    s['mn'], s['mkv'], s['mk'], s['mv'] = _mem_kv_fwd(mem, p['g_mem_kv'], w['w_mem_kv'], p['g_k_mem'])
    s['mq_raw'] = _matmul(s['hn2'], w['w_mem_q'], name="mem_q_fwd")
    s['mo'] = _cross_fwd(s['mq_raw'], p['g_q_mem'], s['mk'], s['mv'])
    h2 = _matmul(s['mo'], w['w_mem_out'], res=h1, name="mem_out_fwd")
    s['h2'] = h2

    s['hn3'] = _rmsnorm_fwd(h2, p['g_ffn'], name="norm_ffn_fwd")
    s['gu'] = _matmul(s['hn3'], w['w_gate_up'], name="gate_up_fwd")
    s['act'] = _swiglu_fwd(s['gu'])
    h3 = _matmul(s['act'], w['w_down'], res=h2, name="down_fwd")
    return h3, s


def _layer_bwd(dh, mem, p, w, s):
    g = {}
    g['w_down'] = _matmul(s['act'], dh, ta=True, tm=1408, tn=512, tk=1024, name="down_dw")
    dact = _matmul(dh, w['w_down'], tb=True, tn=1408, name="down_dx")
    dgu = _swiglu_bwd(s['gu'], dact)
    g['w_gate_up'] = _matmul(s['hn3'], dgu, ta=True, tm=1024, tn=512, tk=1024, name="gate_up_dw")
    dhn3 = _matmul(dgu, w['w_gate_up'], tb=True, tn=1024, tk=2816, name="gate_up_dx")
    dh, g['g_ffn'] = _rmsnorm_bwd(s['h2'], dhn3, p['g_ffn'], dh, name="norm_ffn_bwd")

    g['w_mem_out'] = _matmul(s['mo'], dh, ta=True, tm=512, tn=1024, tk=1024, name="mem_out_dw")
    dmo = _matmul(dh, w['w_mem_out'], tb=True, out_dtype=BF16, name="mem_out_dx")
    dmq, dmk, dmv, g['g_q_mem'] = _cross_bwd(s['mq_raw'], dmo, p['g_q_mem'], s['mk'], s['mv'])
    g['w_mem_kv'], g['g_mem_kv'], g['g_k_mem'] = _mem_kv_bwd(dmk, dmv, s['mkv'], s['mn'], mem, p['g_mem_kv'],
                                                               w['w_mem_kv'], p['g_k_mem'])
    g['w_mem_q'] = _matmul(s['hn2'], dmq, ta=True, tm=1024, tn=512, tk=1024, name="mem_q_dw")
    dhn2 = _matmul(dmq, w['w_mem_q'], tb=True, tn=1024, name="mem_q_dx")
    dh, g['g_mem_q'] = _rmsnorm_bwd(s['h1'], dhn2, p['g_mem_q'], dh, name="norm_mem_bwd")

    g['w_out'] = _matmul(s['cat'], dh, ta=True, tm=1024, tn=512, tk=1024, name="out_proj_dw")
    dcat = _matmul(dh, w['w_out'], tb=True, tn=1024, name="out_proj_dx")
    dpin, g['w_pool'], dscale = _pool_bwd(dcat, s['mixed'], p['w_pool'].astype(BF16), p['pool_scale'])
    g['pool_scale'] = dscale.reshape(POOL_WIDTH)
    do, delta = _fox_delta(dcat, s['fox'])
    delta_row = _pairs_rows(jnp.transpose(delta[:, :FOX_HEADS]))
    dq, dk, dv, dc_keys, dc_queries = _fox_bwd(s['qn'], s['kn'], jnp.transpose(s['kn']), s['vb'], do, s['lse_row'],
                                               delta_row, s['crow'], s['ccol'])
    dft, db = _forget_cumsum_bwd(dc_keys, dc_queries, s['ft'], p['b_forget'])
    g['b_forget'] = db.reshape(FOX_HEADS)
    dfpad = jnp.pad(jnp.transpose(dft), ((0, 0), (0, LANES - FOX_HEADS)))
    dz, g['g_q_fox'], g['g_k_fox'] = _mixer_dz(s['z'], dq, dk, dv, dpin, dfpad, p['g_q_fox'], p['g_k_fox'])
    g['w_in'] = _matmul(s['xn1'], dz, ta=True, tm=512, tn=IN_COLS_PAD, tk=1024, name="in_proj_dw")
    dxn1 = _matmul(dz, w['w_in'], tb=True, tn=1024, name="in_proj_dx")
    dh, g['g_mix'] = _rmsnorm_bwd(s['h0'], dxn1, p['g_mix'], dh, name="norm_mix_bwd")
    return dh, g


def _local_step(x2, mem2, target2, small, full):
    h = x2
    saved = []
    for l in range(DEPTH):
        h, s = _layer_fwd(h, mem2, {k: v[l] for k, v in small.items()}, full[l])
        saved.append(s)
    loss, dh = _loss_head(h, target2)
    grads = [None] * DEPTH
    for l in reversed(range(DEPTH)):
        dh, grads[l] = _layer_bwd(dh, mem2, {k: v[l] for k, v in small.items()}, full[l], saved[l])
    return loss, dh, grads


def kernel(x, mem, g_mix, w_in, b_forget, g_q_fox, g_k_fox, w_pool, pool_scale, w_out, g_mem_q, g_mem_kv, w_mem_q, w_mem_kv, g_q_mem, g_k_mem, w_mem_out, g_ffn, w_gate_up, w_down, loss_target, m_g_mix, m_w_in, m_b_forget, m_g_q_fox, m_g_k_fox, m_w_pool, m_pool_scale, m_w_out, m_g_mem_q, m_g_mem_kv, m_w_mem_q, m_w_mem_kv, m_g_q_mem, m_g_k_mem, m_w_mem_out, m_g_ffn, m_w_gate_up, m_w_down, v_g_mix, v_w_in, v_b_forget, v_g_q_fox, v_g_k_fox, v_w_pool, v_pool_scale, v_w_out, v_g_mem_q, v_g_mem_kv, v_w_mem_q, v_w_mem_kv, v_g_q_mem, v_g_k_mem, v_w_mem_out, v_g_ffn, v_w_gate_up, v_w_down):
    weights = dict(g_mix=g_mix, w_in=w_in, b_forget=b_forget, g_q_fox=g_q_fox, g_k_fox=g_k_fox, w_pool=w_pool,
                   pool_scale=pool_scale, w_out=w_out, g_mem_q=g_mem_q, g_mem_kv=g_mem_kv, w_mem_q=w_mem_q,
                   w_mem_kv=w_mem_kv, g_q_mem=g_q_mem, g_k_mem=g_k_mem, w_mem_out=w_mem_out, g_ffn=g_ffn,
                   w_gate_up=w_gate_up, w_down=w_down)
    mom_m = dict(g_mix=m_g_mix, w_in=m_w_in, b_forget=m_b_forget, g_q_fox=m_g_q_fox, g_k_fox=m_g_k_fox, w_pool=m_w_pool,
                 pool_scale=m_pool_scale, w_out=m_w_out, g_mem_q=m_g_mem_q, g_mem_kv=m_g_mem_kv, w_mem_q=m_w_mem_q,
                 w_mem_kv=m_w_mem_kv, g_q_mem=m_g_q_mem, g_k_mem=m_g_k_mem, w_mem_out=m_w_mem_out, g_ffn=m_g_ffn,
                 w_gate_up=m_w_gate_up, w_down=m_w_down)
    mom_v = dict(g_mix=v_g_mix, w_in=v_w_in, b_forget=v_b_forget, g_q_fox=v_g_q_fox, g_k_fox=v_g_k_fox, w_pool=v_w_pool,
                 pool_scale=v_pool_scale, w_out=v_w_out, g_mem_q=v_g_mem_q, g_mem_kv=v_g_mem_kv, w_mem_q=v_w_mem_q,
                 w_mem_kv=v_w_mem_kv, g_q_mem=v_g_q_mem, g_k_mem=v_g_k_mem, w_mem_out=v_w_mem_out, g_ffn=v_g_ffn,
                 w_gate_up=v_w_gate_up, w_down=v_w_down)

    gathered = _all_gather([weights[n].astype(BF16) for n in BIG], name="weights_all_gather")
    gathered = dict(zip(BIG, gathered))
    full = [_full_weights(gathered, l) for l in range(DEPTH)]
    small = {n: weights[n] for n in SMALL}

    loss_part, grad_x, grads = _local_step(x[0], mem[0], loss_target[0], small, full)
    loss = lax.psum(loss_part[0, 0], ("x", "y", "c"))

    blocks = [_grad_blocks(n, [grads[l][n] for l in range(DEPTH)]) for n in BIG]
    landed = _exchange(blocks, name="grads_exchange")
    grad = {n: _sum_slots(landed[i], name="grad_sum_" + n) for i, n in enumerate(BIG)}
    small_part = _pack_small({n: jnp.stack([grads[l][n] for l in range(DEPTH)], 0) for n in SMALL})
    (small_all,) = _all_gather([small_part], name="small_grads_all_gather")
    small_sum = _sum_slots(small_all, name="grad_sum_small")

    delta, new_m, new_v = {}, {}, {}
    for n in BIG:
        shape = weights[n].shape
        two_d = lambda a: a.reshape(shape[0] * shape[1], shape[2])
        d, nm, nv = _adamw(two_d(weights[n]), grad[n], two_d(mom_m[n]), two_d(mom_v[n]), name="adamw_" + n)
        grad[n], delta[n], new_m[n], new_v[n] = (a.reshape(shape) for a in (grad[n], d, nm, nv))
    d, nm, nv = _adamw(_pack_small(weights), small_sum, _pack_small(mom_m), _pack_small(mom_v), name="adamw_small")
    grad.update(_unpack_small(small_sum))
    delta.update(_unpack_small(d))
    new_m.update(_unpack_small(nm))
    new_v.update(_unpack_small(nv))

    return (loss, grad_x[None], *[grad[n] for n in WEIGHTS], *[delta[n] for n in WEIGHTS],
            *[new_m[n] for n in WEIGHTS], *[new_v[n] for n in WEIGHTS])
```

```python
import functools

import numpy as np
import jax
import jax.numpy as jnp
from jax import lax
from jax.experimental import pallas as pl
from jax.experimental.pallas import tpu as pltpu

F32 = jnp.float32
BF16 = jnp.bfloat16

N_DEV = 8
D_MODEL = 1024
DEPTH = 2
FOX_HEADS = 8
FOX_HEAD_DIM = 64
FOX_WIDTH = 512
POOL_WIDTH = 512
POOL_WINDOWS = (2, 4, 8, 16)
POOL_GROUP_DIM = 128
POOL_HALO = 16
IN_COLS = 2056
IN_COLS_PAD = 2176
MEM_HEADS = 4
MEM_HEAD_DIM = 128
MEM_WIDTH = 512
D_FF = 2816
EPS = 1e-6
FOX_SCALE = FOX_HEAD_DIM ** -0.5
MEM_SCALE = MEM_HEAD_DIM ** -0.5
LANES = 128

ADAM_LR = 0.001
ADAM_B1 = 0.9
ADAM_B2 = 0.999
ADAM_EPS = 1e-08
ADAM_WD = 0.01
ADAM_STEP = 10

VMEM_LIMIT = 56 * 1024 * 1024
MESH_ID = pl.DeviceIdType.MESH

WEIGHTS = ['g_mix', 'w_in', 'b_forget', 'g_q_fox', 'g_k_fox', 'w_pool', 'pool_scale', 'w_out', 'g_mem_q', 'g_mem_kv',
           'w_mem_q', 'w_mem_kv', 'g_q_mem', 'g_k_mem', 'w_mem_out', 'g_ffn', 'w_gate_up', 'w_down']
BIG = ['w_in', 'w_out', 'w_mem_q', 'w_mem_kv', 'w_mem_out', 'w_gate_up', 'w_down']
SMALL = [n for n in WEIGHTS if n not in BIG]


def _pcall(body, **kw):
    return pl.pallas_call(body, **kw)


def _params(*sem):
    return pltpu.CompilerParams(dimension_semantics=sem or None, vmem_limit_bytes=VMEM_LIMIT)


def _dot(a, b, dims=None):
    if dims is None:
        return jnp.dot(a, b, preferred_element_type=F32)
    return lax.dot_general(a, b, (dims, ((), ())), preferred_element_type=F32)


NT = ((1,), (1,))
TN = ((0,), (0,))


def _dot_exact(x, ones_bf16):
    hi = x.astype(BF16)
    r1 = x - hi.astype(F32)
    mid = r1.astype(BF16)
    lo = (r1 - mid.astype(F32)).astype(BF16)
    return _dot(hi, ones_bf16) + _dot(mid, ones_bf16) + _dot(lo, ones_bf16)


def _matmul(a, b, *, ta=False, tb=False, out_dtype=F32, res=None, tm=1024, tn=512, tk=None, name):
    m, k = (a.shape[1], a.shape[0]) if ta else a.shape
    n = b.shape[0] if tb else b.shape[1]
    assert k == (b.shape[1] if tb else b.shape[0])
    tm, tn = min(tm, m), min(tn, n)
    tk = min(tk or k, k)
    assert m % tm == 0 and n % tn == 0 and k % tk == 0, (name, m, n, k, tm, tn, tk)
    nk = k // tk
    dims = ((0 if ta else 1,), (1 if tb else 0,))

    def body(*refs):
        a_ref, b_ref = refs[0], refs[1]
        r_ref = refs[2] if res is not None else None
        o_ref = refs[3] if res is not None else refs[2]
        part = _dot(a_ref[...].astype(BF16), b_ref[...].astype(BF16), dims)

        def finish(acc):
            if r_ref is not None:
                acc = acc + r_ref[...]
            o_ref[...] = acc.astype(o_ref.dtype)

        if nk == 1:
            finish(part)
        else:
            acc_ref = refs[-1]
            kk = pl.program_id(2)

            @pl.when(kk == 0)
            def _():
                acc_ref[...] = part

            @pl.when(kk > 0)
            def _():
                acc_ref[...] += part

            @pl.when(kk == nk - 1)
            def _():
                finish(acc_ref[...])

    a_spec = pl.BlockSpec((tk, tm), lambda i, j, kk: (kk, i)) if ta else pl.BlockSpec((tm, tk), lambda i, j, kk: (i, kk))
    b_spec = pl.BlockSpec((tn, tk), lambda i, j, kk: (j, kk)) if tb else pl.BlockSpec((tk, tn), lambda i, j, kk: (kk, j))
    o_spec = pl.BlockSpec((tm, tn), lambda i, j, kk: (i, j))
    in_specs = [a_spec, b_spec] + ([o_spec] if res is not None else [])
    args = (a, b) + ((res,) if res is not None else ())
    return _pcall(
        body, name=name, grid=(m // tm, n // tn, nk), in_specs=in_specs, out_specs=o_spec,
        out_shape=jax.ShapeDtypeStruct((m, n), out_dtype),
        scratch_shapes=[pltpu.VMEM((tm, tn), F32)] if nk > 1 else [],
        compiler_params=_params("parallel", "parallel", "arbitrary"),
    )(*args)


def _rmsnorm_fwd(h, g, *, bt=512, name):
    t, d = h.shape

    def body(x_ref, g_ref, o_ref):
        x = x_ref[...]
        r = lax.rsqrt(jnp.mean(x * x, axis=-1, keepdims=True) + EPS)
        o_ref[...] = (x * r * g_ref[...]).astype(o_ref.dtype)

    return _pcall(
        body, name=name, grid=(t // bt,),
        in_specs=[pl.BlockSpec((bt, d), lambda i: (i, 0)), pl.BlockSpec((1, d), lambda i: (0, 0))],
        out_specs=pl.BlockSpec((bt, d), lambda i: (i, 0)),
        out_shape=jax.ShapeDtypeStruct((t, d), BF16), compiler_params=_params("parallel"),
    )(h, g.reshape(1, d))


def _rmsnorm_bwd(h, dy, g, dres, *, bt=512, name):
    t, d = h.shape

    def body(x_ref, dy_ref, g_ref, r_ref, dx_ref, dg_ref):
        x = x_ref[...]
        r = lax.rsqrt(jnp.mean(x * x, axis=-1, keepdims=True) + EPS)
        xhat = x * r
        dy_v = dy_ref[...].astype(F32)
        gy = dy_v * g_ref[...]
        dx = r * (gy - xhat * jnp.mean(gy * xhat, axis=-1, keepdims=True))
        dx_ref[...] = r_ref[...] + dx
        part = jnp.sum(dy_v * xhat, axis=0, keepdims=True)

        @pl.when(pl.program_id(0) == 0)
        def _():
            dg_ref[...] = part

        @pl.when(pl.program_id(0) > 0)
        def _():
            dg_ref[...] += part

    row = pl.BlockSpec((bt, d), lambda i: (i, 0))
    vec = pl.BlockSpec((1, d), lambda i: (0, 0))
    dx, dg = _pcall(
        body, name=name, grid=(t // bt,), in_specs=[row, row, vec, row], out_specs=[row, vec],
        out_shape=[jax.ShapeDtypeStruct((t, d), F32), jax.ShapeDtypeStruct((1, d), F32)],
        compiler_params=_params("arbitrary"),
    )(h, dy, g.reshape(1, d), dres)
    return dx, dg.reshape(d)


def _group_matrix(width, group):
    r = lax.broadcasted_iota(jnp.int32, (width, width), 0) // group
    c = lax.broadcasted_iota(jnp.int32, (width, width), 1) // group
    return (r == c).astype(BF16)


def _qkv_prep(z, gq, gk, *, bt=512):
    t = z.shape[0]
    w = FOX_WIDTH

    def body(q_ref, k_ref, v_ref, gq_ref, gk_ref, qo_ref, ko_ref, vo_ref):
        gm = _group_matrix(w, FOX_HEAD_DIM)
        for x_ref, g_ref, o_ref, scale in ((q_ref, gq_ref, qo_ref, FOX_SCALE), (k_ref, gk_ref, ko_ref, 1.0)):
            x = x_ref[...]
            ms = _dot_exact(x * x, gm) * (1.0 / FOX_HEAD_DIM)
            y = x * lax.rsqrt(ms + EPS) * g_ref[...]
            o_ref[...] = (y * scale).astype(BF16)
        vo_ref[...] = v_ref[...].astype(BF16)

    col = lambda c: pl.BlockSpec((bt, w), lambda i, c=c: (i, c))
    vec = pl.BlockSpec((1, w), lambda i: (0, 0))
    out = pl.BlockSpec((bt, w), lambda i: (i, 0))
    return _pcall(
        body, name="fox_qkv_prep", grid=(t // bt,), in_specs=[col(0), col(1), col(2), vec, vec], out_specs=[out, out, out],
        out_shape=[jax.ShapeDtypeStruct((t, w), BF16)] * 3, compiler_params=_params("parallel"),
    )(z, z, z, jnp.tile(gq, FOX_HEADS).reshape(1, w), jnp.tile(gk, FOX_HEADS).reshape(1, w))


def _log_sigmoid(f):
    return jnp.minimum(f, 0.0) - jnp.log1p(jnp.exp(-jnp.abs(f)))


def _forget_cumsum(ft, b, *, chunk=512):
    hh, t = ft.shape

    def body(f_ref, b_ref, c_ref):
        r = lax.broadcasted_iota(jnp.int32, (chunk, chunk), 0)
        c = lax.broadcasted_iota(jnp.int32, (chunk, chunk), 1)
        upper = (r <= c).astype(BF16)
        carry = jnp.zeros((hh, 1), F32)
        for ch in range(t // chunk):
            sl = slice(ch * chunk, (ch + 1) * chunk)
            cs = _dot_exact(_log_sigmoid(f_ref[:, sl] + b_ref[...]), upper) + carry
            c_ref[:, sl] = cs
            carry = cs[:, chunk - 1:chunk]

    return _pcall(body, name="fox_forget_cumsum", out_shape=jax.ShapeDtypeStruct((hh, t), F32),
                  compiler_params=_params())(ft, b.reshape(hh, 1))


def _forget_cumsum_bwd(dc_keys, dc_queries, ft, b, *, chunk=512):
    hh, t = ft.shape

    def body(dck_ref, dcq_ref, f_ref, b_ref, df_ref, db_ref):
        r = lax.broadcasted_iota(jnp.int32, (chunk, chunk), 0)
        c = lax.broadcasted_iota(jnp.int32, (chunk, chunk), 1)
        lower = (r >= c).astype(BF16)
        carry = jnp.zeros((hh, 1), F32)
        db = jnp.zeros((hh, 1), F32)
        for ch in reversed(range(t // chunk)):
            sl = slice(ch * chunk, (ch + 1) * chunk)
            dls = _dot_exact(dck_ref[:, sl] + dcq_ref[:, sl], lower) + carry
            carry = dls[:, 0:1]
            df = dls * jax.nn.sigmoid(-(f_ref[:, sl] + b_ref[...]))
            df_ref[:, sl] = df
            db = db + jnp.sum(df, axis=1, keepdims=True)
        db_ref[...] = db

    return _pcall(body, name="fox_forget_cumsum_bwd",
                  out_shape=[jax.ShapeDtypeStruct((hh, t), F32), jax.ShapeDtypeStruct((hh, 1), F32)],
                  compiler_params=_params())(dc_keys, dc_queries, ft, b.reshape(hh, 1))


def _lane_is_first_head():
    return lax.broadcasted_iota(jnp.int32, (1, LANES), 1) < FOX_HEAD_DIM


def _fox_fwd(qn, kn, vt, crow, ccol, *, bq=512):
    t = qn.shape[0]
    nq = t // bq
    pairs = FOX_WIDTH // LANES
    tiles = [(i, j) for i in range(nq) for j in range(i + 1)]
    it = jnp.asarray(np.array([a for a, _ in tiles], np.int32))
    jt = jnp.asarray(np.array([b for _, b in tiles], np.int32))

    def body(it_ref, jt_ref, q_ref, k_ref, vt_ref, cr_ref, cc_ref, o_ref, lse_ref, m_sc, l_sc, acc_sc):
        s_id = pl.program_id(1)
        i, j = it_ref[s_id], jt_ref[s_id]
        first = _lane_is_first_head()

        @pl.when(j == 0)
        def _():
            m_sc[...] = jnp.full(m_sc.shape, -jnp.inf, F32)
            l_sc[...] = jnp.zeros(l_sc.shape, F32)
            acc_sc[...] = jnp.zeros(acc_sc.shape, F32)

        def tile(diagonal):
            q2, k2, vt2 = q_ref[...], k_ref[...], vt_ref[...]
            for hh in range(2):
                kh = jnp.where(first if hh == 0 else jnp.logical_not(first), k2, jnp.zeros_like(k2))
                st = _dot(kh, q2, NT) + (cr_ref[0, hh:hh + 1, :] - cc_ref[0, :, hh:hh + 1])
                if diagonal:
                    rr = lax.broadcasted_iota(jnp.int32, st.shape, 0)
                    cc = lax.broadcasted_iota(jnp.int32, st.shape, 1)
                    st = jnp.where(rr <= cc, st, -jnp.inf)
                m_prev = m_sc[hh]
                m_new = jnp.maximum(m_prev, jnp.max(st, axis=0, keepdims=True))
                alpha = jnp.exp(m_prev - m_new)
                pt = jnp.exp(st - m_new)
                l_sc[hh] = alpha * l_sc[hh] + jnp.sum(pt, axis=0, keepdims=True)
                vth = vt2[hh * FOX_HEAD_DIM:(hh + 1) * FOX_HEAD_DIM]
                acc_sc[hh] = alpha * acc_sc[hh] + _dot(vth, pt.astype(BF16))
                m_sc[hh] = m_new

        @pl.when(j < i)
        def _():
            tile(False)

        @pl.when(j == i)
        def _():
            tile(True)
            ot = jnp.concatenate([acc_sc[0] / l_sc[0], acc_sc[1] / l_sc[1]], axis=0)
            o_ref[...] = jnp.transpose(ot).astype(o_ref.dtype)
            for hh in range(2):
                lse_ref[0, hh:hh + 1, :] = m_sc[hh] + jnp.log(l_sc[hh])

    qspec = pl.BlockSpec((bq, LANES), lambda p, s, it, jt: (it[s], p))
    kspec = pl.BlockSpec((bq, LANES), lambda p, s, it, jt: (jt[s], p))
    vtspec = pl.BlockSpec((LANES, bq), lambda p, s, it, jt: (p, jt[s]))
    rowq = pl.BlockSpec((1, 2, bq), lambda p, s, it, jt: (p, 0, it[s]))
    colk = pl.BlockSpec((1, bq, 2), lambda p, s, it, jt: (p, jt[s], 0))
    return _pcall(
        body, name="fox_attention_fwd",
        grid_spec=pltpu.PrefetchScalarGridSpec(
            num_scalar_prefetch=2, grid=(pairs, len(tiles)),
            in_specs=[qspec, kspec, vtspec, rowq, colk], out_specs=[qspec, rowq],
            scratch_shapes=[pltpu.VMEM((2, 1, bq), F32), pltpu.VMEM((2, 1, bq), F32),
                            pltpu.VMEM((2, FOX_HEAD_DIM, bq), F32)]),
        out_shape=[jax.ShapeDtypeStruct((t, FOX_WIDTH), BF16), jax.ShapeDtypeStruct((pairs, 2, t), F32)],
        compiler_params=_params("parallel", "arbitrary"),
    )(it, jt, qn, kn, vt, crow, ccol)


def _fox_bwd(qn, kn, knt, vb, do, lse_row, delta_row, crow, ccol, *, bq=512):
    t = qn.shape[0]
    nq = t // bq
    pairs = FOX_WIDTH // LANES
    tiles = [(i, j) for j in range(nq) for i in range(j, nq)]
    it = jnp.asarray(np.array([a for a, _ in tiles], np.int32))
    jt = jnp.asarray(np.array([b for _, b in tiles], np.int32))

    def body(it_ref, jt_ref, q_ref, k_ref, kt_ref, v_ref, do_ref, lse_ref, dl_ref, cr_ref, cc_ref,
             dqt_ref, dk_ref, dv_ref, dc_ref, dr_ref, dk_sc, dv_sc, dc_sc):
        s_id = pl.program_id(1)
        i, j = it_ref[s_id], jt_ref[s_id]
        first = _lane_is_first_head()

        @pl.when(s_id == 0)
        def _():
            dqt_ref[...] = jnp.zeros(dqt_ref.shape, F32)
            dr_ref[...] = jnp.zeros(dr_ref.shape, F32)

        @pl.when(i == j)
        def _():
            dk_sc[...] = jnp.zeros(dk_sc.shape, F32)
            dv_sc[...] = jnp.zeros(dv_sc.shape, F32)
            dc_sc[...] = jnp.zeros(dc_sc.shape, F32)

        def tile(diagonal):
            q2, k2, kt2, v2, do2 = q_ref[...], k_ref[...], kt_ref[...], v_ref[...], do_ref[...]
            dk_t, dv_t, dqt_t = [], [], []
            for hh in range(2):
                mine = first if hh == 0 else jnp.logical_not(first)
                kh = jnp.where(mine, k2, jnp.zeros_like(k2))
                vh = jnp.where(mine, v2, jnp.zeros_like(v2))
                kth = kt2[hh * FOX_HEAD_DIM:(hh + 1) * FOX_HEAD_DIM]
                st = _dot(kh, q2, NT) + (cr_ref[0, hh:hh + 1, :] - cc_ref[0, :, hh:hh + 1])
                if diagonal:
                    rr = lax.broadcasted_iota(jnp.int32, st.shape, 0)
                    cc = lax.broadcasted_iota(jnp.int32, st.shape, 1)
                    st = jnp.where(rr <= cc, st, -jnp.inf)
                pt = jnp.exp(st - lse_ref[0, hh:hh + 1, :])
                dv_t.append(_dot(pt.astype(BF16), do2))
                dpt = _dot(vh, do2, NT)
                dst = pt * (dpt - dl_ref[0, hh:hh + 1, :])
                dc_sc[hh] += jnp.sum(dst, axis=1, keepdims=True)
                dr_ref[0, i, hh:hh + 1, :] += jnp.sum(dst, axis=0, keepdims=True)
                dsb = dst.astype(BF16)
                dk_t.append(_dot(dsb, q2))
                dqt_t.append(_dot(kth, dsb))
            dk_sc[...] += jnp.where(first, dk_t[0], dk_t[1])
            dv_sc[...] += jnp.where(first, dv_t[0], dv_t[1])
            dqt_ref[0, i] += jnp.concatenate(dqt_t, axis=0)

        @pl.when(i > j)
        def _():
            tile(False)

        @pl.when(i == j)
        def _():
            tile(True)

        @pl.when(i == nq - 1)
        def _():
            dk_ref[...] = dk_sc[...]
            dv_ref[...] = dv_sc[...]
            for hh in range(2):
                dc_ref[0, :, hh:hh + 1] = -dc_sc[hh]

    qspec = pl.BlockSpec((bq, LANES), lambda p, s, it, jt: (it[s], p))
    kspec = pl.BlockSpec((bq, LANES), lambda p, s, it, jt: (jt[s], p))
    ktspec = pl.BlockSpec((LANES, bq), lambda p, s, it, jt: (p, jt[s]))
    rowq = pl.BlockSpec((1, 2, bq), lambda p, s, it, jt: (p, 0, it[s]))
    colk = pl.BlockSpec((1, bq, 2), lambda p, s, it, jt: (p, jt[s], 0))
    dqt_spec = pl.BlockSpec((1, nq, LANES, bq), lambda p, s, it, jt: (p, 0, 0, 0))
    dr_spec = pl.BlockSpec((1, nq, 2, bq), lambda p, s, it, jt: (p, 0, 0, 0))
    dqt, dk, dv, dc_keys, dc_queries = _pcall(
        body, name="fox_attention_bwd",
        grid_spec=pltpu.PrefetchScalarGridSpec(
            num_scalar_prefetch=2, grid=(pairs, len(tiles)),
            in_specs=[qspec, kspec, ktspec, kspec, qspec, rowq, rowq, rowq, colk],
            out_specs=[dqt_spec, kspec, kspec, colk, dr_spec],
            scratch_shapes=[pltpu.VMEM((bq, LANES), F32), pltpu.VMEM((bq, LANES), F32), pltpu.VMEM((2, bq, 1), F32)]),
        out_shape=[jax.ShapeDtypeStruct((pairs, nq, LANES, bq), F32), jax.ShapeDtypeStruct((t, FOX_WIDTH), F32),
                   jax.ShapeDtypeStruct((t, FOX_WIDTH), F32), jax.ShapeDtypeStruct((pairs, t, 2), F32),
                   jax.ShapeDtypeStruct((pairs, nq, 2, bq), F32)],
        compiler_params=_params("parallel", "arbitrary"),
    )(it, jt, qn, kn, knt, vb, do, lse_row, delta_row, crow, ccol)
    dq = jnp.transpose(dqt, (1, 3, 0, 2)).reshape(t, FOX_WIDTH)
    dc_keys = jnp.transpose(dc_keys, (0, 2, 1)).reshape(FOX_HEADS, t)
    dc_queries = jnp.transpose(dc_queries, (0, 2, 1, 3)).reshape(FOX_HEADS, t)
    return dq, dk, dv, dc_keys, dc_queries


def _fox_delta(dcat, fox, *, bt=512):
    t = fox.shape[0]
    w = FOX_WIDTH

    def body(do_ref, o_ref, dob_ref, dl_ref):
        dob = do_ref[...].astype(BF16)
        r = lax.broadcasted_iota(jnp.int32, (w, LANES), 0) // FOX_HEAD_DIM
        c = lax.broadcasted_iota(jnp.int32, (w, LANES), 1)
        dl_ref[...] = _dot_exact(dob.astype(F32) * o_ref[...].astype(F32), (r == c).astype(BF16))
        dob_ref[...] = dob

    blk = pl.BlockSpec((bt, w), lambda i: (i, 0))
    return _pcall(
        body, name="fox_delta", grid=(t // bt,), in_specs=[blk, blk],
        out_specs=[blk, pl.BlockSpec((bt, LANES), lambda i: (i, 0))],
        out_shape=[jax.ShapeDtypeStruct((t, w), BF16), jax.ShapeDtypeStruct((t, LANES), F32)],
        compiler_params=_params("parallel"),
    )(dcat, fox)


def _mixer_dz(z, dq, dk, dv, dpin, dfpad, gq, gk, *, bt=256):
    t = z.shape[0]
    w = FOX_WIDTH

    def body(q_ref, k_ref, dq_ref, dk_ref, dv_ref, dp_ref, df_ref, gq_ref, gk_ref, dz_ref, dgq_ref, dgk_ref):
        gm = _group_matrix(w, FOX_HEAD_DIM)
        first_step = pl.program_id(0) == 0
        for n, (x_ref, dy_ref, g_ref, dg_ref, scale) in enumerate(
                ((q_ref, dq_ref, gq_ref, dgq_ref, FOX_SCALE), (k_ref, dk_ref, gk_ref, dgk_ref, 1.0))):
            x = x_ref[...]
            r = lax.rsqrt(_dot_exact(x * x, gm) * (1.0 / FOX_HEAD_DIM) + EPS)
            xhat = x * r
            dy = dy_ref[...] * scale
            gy = dy * g_ref[...]
            dx = r * (gy - xhat * (_dot_exact(gy * xhat, gm) * (1.0 / FOX_HEAD_DIM)))
            dz_ref[:, n * w:(n + 1) * w] = dx.astype(BF16)
            part = jnp.sum(dy * xhat, axis=0, keepdims=True)

            @pl.when(first_step)
            def _():
                dg_ref[...] = part

            @pl.when(jnp.logical_not(first_step))
            def _():
                dg_ref[...] += part

        dz_ref[:, 2 * w:3 * w] = dv_ref[...].astype(BF16)
        dz_ref[:, 3 * w:4 * w] = dp_ref[...].astype(BF16)
        dz_ref[:, 4 * w:] = df_ref[...].astype(BF16)

    col = lambda c: pl.BlockSpec((bt, w), lambda i, c=c: (i, c))
    blk = pl.BlockSpec((bt, w), lambda i: (i, 0))
    vec = pl.BlockSpec((1, w), lambda i: (0, 0))
    dz, dgq, dgk = _pcall(
        body, name="mixer_dz", grid=(t // bt,),
        in_specs=[col(0), col(1), blk, blk, blk, blk, pl.BlockSpec((bt, LANES), lambda i: (i, 0)), vec, vec],
        out_specs=[pl.BlockSpec((bt, IN_COLS_PAD), lambda i: (i, 0)), vec, vec],
        out_shape=[jax.ShapeDtypeStruct((t, IN_COLS_PAD), BF16), jax.ShapeDtypeStruct((1, w), F32),
                   jax.ShapeDtypeStruct((1, w), F32)],
        compiler_params=_params("arbitrary"),
    )(z, z, dq, dk, dv, dpin, dfpad, jnp.tile(gq, FOX_HEADS).reshape(1, w), jnp.tile(gk, FOX_HEADS).reshape(1, w))
    return dz, dgq.reshape(FOX_HEADS, FOX_HEAD_DIM).sum(0), dgk.reshape(FOX_HEADS, FOX_HEAD_DIM).sum(0)


def _pool_fwd(z, wp, scale, *, bt=512):
    t = z.shape[0]
    w = POOL_WIDTH
    hb = bt // POOL_HALO

    def body(p_ref, h_ref, wp_ref, sc_ref, y_ref, mx_ref):
        i = pl.program_id(0)
        cur = p_ref[...]
        halo = jnp.where(i > 0, h_ref[...], 0.0)
        ext = jnp.concatenate([halo, cur], axis=0)
        trow = i * bt + lax.broadcasted_iota(jnp.int32, (bt, 1), 0)
        for g, win in enumerate(POOL_WINDOWS):
            sl = slice(g * LANES, (g + 1) * LANES)
            e = ext[:, sl]
            acc = e[POOL_HALO:]
            for k in range(1, win):
                acc = acc + pltpu.roll(e, k, 0)[POOL_HALO:]
            cnt = jnp.minimum(trow + 1, win).astype(F32)
            mixed = (acc / cnt - cur[:, sl]).astype(BF16)
            mx_ref[:, sl] = mixed
            y_ref[:, sl] = (_dot(mixed, wp_ref[g]) * sc_ref[:, sl]).astype(BF16)

    blk = pl.BlockSpec((bt, w), lambda i: (i, 0))
    return _pcall(
        body, name="pool_fwd", grid=(t // bt,),
        in_specs=[pl.BlockSpec((bt, w), lambda i: (i, 3)),
                  pl.BlockSpec((POOL_HALO, w), lambda i: (jnp.maximum(i * hb - 1, 0), 3)),
                  pl.BlockSpec((len(POOL_WINDOWS), LANES, LANES), lambda i: (0, 0, 0)),
                  pl.BlockSpec((1, w), lambda i: (0, 0))],
        out_specs=[blk, blk], out_shape=[jax.ShapeDtypeStruct((t, w), BF16)] * 2,
        compiler_params=_params("parallel"),
    )(z, z, wp, scale.reshape(1, w))


def _pool_bwd(dcat, mixed, wp, scale, *, bt=512):
    t = mixed.shape[0]
    w = POOL_WIDTH
    hb = bt // POOL_HALO
    nb = t // bt
    n_ext = bt + POOL_HALO

    def body(d_ref, h_ref, mx_ref, wp_ref, sc_ref, dp_ref, dwp_ref, dsc_ref):
        i = pl.program_id(0)
        cur = d_ref[...]
        nxt = jnp.where(i < nb - 1, h_ref[...], 0.0)
        ext = jnp.concatenate([cur, nxt], axis=0)
        trow = i * bt + lax.broadcasted_iota(jnp.int32, (n_ext, 1), 0)

        @pl.when(i == 0)
        def _():
            dwp_ref[...] = jnp.zeros(dwp_ref.shape, F32)
            dsc_ref[...] = jnp.zeros(dsc_ref.shape, F32)

        for g, win in enumerate(POOL_WINDOWS):
            sl = slice(g * LANES, (g + 1) * LANES)
            dy = (ext[:, sl] * sc_ref[:, sl]).astype(BF16)
            dm = _dot(dy, wp_ref[g], NT)
            mixed_g = mx_ref[:, sl]
            dsc_ref[:, sl] += jnp.sum(cur[:, sl] * _dot(mixed_g, wp_ref[g]), axis=0, keepdims=True)
            dwp_ref[g] += _dot(mixed_g, dy[:bt], TN)
            r = dm / jnp.minimum(trow + 1, win).astype(F32)
            acc = r[:bt]
            for k in range(1, win):
                acc = acc + pltpu.roll(r, n_ext - k, 0)[:bt]
            dp_ref[:, sl] = acc - dm[:bt]

    return _pcall(
        body, name="pool_bwd", grid=(nb,),
        in_specs=[pl.BlockSpec((bt, w), lambda i: (i, 1)),
                  pl.BlockSpec((POOL_HALO, w), lambda i: (jnp.minimum((i + 1) * hb, t // POOL_HALO - 1), 1)),
                  pl.BlockSpec((bt, w), lambda i: (i, 0)),
                  pl.BlockSpec((len(POOL_WINDOWS), LANES, LANES), lambda i: (0, 0, 0)),
                  pl.BlockSpec((1, w), lambda i: (0, 0))],
        out_specs=[pl.BlockSpec((bt, w), lambda i: (i, 0)),
                   pl.BlockSpec((len(POOL_WINDOWS), LANES, LANES), lambda i: (0, 0, 0)),
                   pl.BlockSpec((1, w), lambda i: (0, 0))],
        out_shape=[jax.ShapeDtypeStruct((t, w), F32), jax.ShapeDtypeStruct((len(POOL_WINDOWS), LANES, LANES), F32),
                   jax.ShapeDtypeStruct((1, w), F32)],
        compiler_params=_params("arbitrary"),
    )(dcat, dcat, mixed, wp, scale.reshape(1, w))


def _head_rms(x):
    return lax.rsqrt(jnp.mean(x * x, axis=-1, keepdims=True) + EPS)


def _mem_kv_fwd(mem, g_kv, w_kv, g_k):
    mlen, d = mem.shape

    def body(m_ref, g_ref, w_ref, gk_ref, mn_ref, mkv_ref, mk_ref, mv_ref):
        x = m_ref[...]
        mn = (x * lax.rsqrt(jnp.mean(x * x, axis=-1, keepdims=True) + EPS) * g_ref[...]).astype(BF16)
        mn_ref[...] = mn
        mkv = _dot(mn, w_ref[...])
        mkv_ref[...] = mkv
        for h in range(MEM_HEADS):
            sl = slice(h * MEM_HEAD_DIM, (h + 1) * MEM_HEAD_DIM)
            kh = mkv[:, sl]
            mk_ref[:, sl] = (kh * _head_rms(kh) * gk_ref[...]).astype(BF16)
        mv_ref[...] = mkv[:, MEM_WIDTH:].astype(BF16)

    return _pcall(
        body, name="mem_kv_fwd",
        out_shape=[jax.ShapeDtypeStruct((mlen, d), BF16), jax.ShapeDtypeStruct((mlen, 2 * MEM_WIDTH), F32),
                   jax.ShapeDtypeStruct((mlen, MEM_WIDTH), BF16), jax.ShapeDtypeStruct((mlen, MEM_WIDTH), BF16)],
        compiler_params=_params(),
    )(mem, g_kv.reshape(1, d), w_kv, g_k.reshape(1, MEM_HEAD_DIM))


def _mem_kv_bwd(dmk, dmv, mkv, mn, mem, g_kv, w_kv, g_k):
    mlen, d = mem.shape

    def body(dmk_ref, dmv_ref, mkv_ref, mn_ref, m_ref, g_ref, w_ref, gk_ref, dw_ref, dg_ref, dgk_ref, dkv_sc):
        dgk = jnp.zeros((1, MEM_HEAD_DIM), F32)
        for h in range(MEM_HEADS):
            sl = slice(h * MEM_HEAD_DIM, (h + 1) * MEM_HEAD_DIM)
            x = mkv_ref[:, sl]
            r = _head_rms(x)
            xhat = x * r
            dy = dmk_ref[:, sl]
            gy = dy * gk_ref[...]
            dkv_sc[:, sl] = (r * (gy - xhat * jnp.mean(gy * xhat, axis=-1, keepdims=True))).astype(BF16)
            dgk = dgk + jnp.sum(dy * xhat, axis=0, keepdims=True)
        dgk_ref[...] = dgk
        dkv_sc[:, MEM_WIDTH:] = dmv_ref[...].astype(BF16)
        dkv = dkv_sc[...]
        dw_ref[...] = _dot(mn_ref[...], dkv, TN)
        dmn = _dot(dkv, w_ref[...], NT)
        x = m_ref[...]
        xhat = x * lax.rsqrt(jnp.mean(x * x, axis=-1, keepdims=True) + EPS)
        dg_ref[...] = jnp.sum(dmn * xhat, axis=0, keepdims=True)

    dw, dg, dgk = _pcall(
        body, name="mem_kv_bwd",
        out_shape=[jax.ShapeDtypeStruct((d, 2 * MEM_WIDTH), F32), jax.ShapeDtypeStruct((1, d), F32),
                   jax.ShapeDtypeStruct((1, MEM_HEAD_DIM), F32)],
        scratch_shapes=[pltpu.VMEM((mlen, 2 * MEM_WIDTH), BF16)],
        compiler_params=_params(),
    )(dmk, dmv, mkv, mn, mem, g_kv.reshape(1, d), w_kv, g_k.reshape(1, MEM_HEAD_DIM))
    return dw, dg.reshape(d), dgk.reshape(MEM_HEAD_DIM)


def _cross_probs(x, g, mk_h):
    r = _head_rms(x)
    xhat = x * r
    qn = (xhat * g).astype(BF16)
    s = _dot(qn, mk_h, NT) * MEM_SCALE
    e = jnp.exp(s - jnp.max(s, axis=-1, keepdims=True))
    return r, xhat, qn, e / jnp.sum(e, axis=-1, keepdims=True)


def _cross_fwd(mq_raw, g_q, mk, mv, *, bt=512):
    t = mq_raw.shape[0]
    mlen = mk.shape[0]

    def body(x_ref, g_ref, mk_ref, mv_ref, o_ref):
        for h in range(MEM_HEADS):
            sl = slice(h * MEM_HEAD_DIM, (h + 1) * MEM_HEAD_DIM)
            _, _, _, p = _cross_probs(x_ref[:, sl], g_ref[...], mk_ref[:, sl])
            o_ref[:, sl] = _dot(p.astype(BF16), mv_ref[:, sl]).astype(BF16)

    blk = pl.BlockSpec((bt, MEM_WIDTH), lambda i: (i, 0))
    kv = pl.BlockSpec((mlen, MEM_WIDTH), lambda i: (0, 0))
    return _pcall(
        body, name="cross_attention_fwd", grid=(t // bt,),
        in_specs=[blk, pl.BlockSpec((1, MEM_HEAD_DIM), lambda i: (0, 0)), kv, kv], out_specs=blk,
        out_shape=jax.ShapeDtypeStruct((t, MEM_WIDTH), BF16), compiler_params=_params("parallel"),
    )(mq_raw, g_q.reshape(1, MEM_HEAD_DIM), mk, mv)


def _cross_bwd(mq_raw, dmo, g_q, mk, mv, *, bt=512):
    t = mq_raw.shape[0]
    mlen = mk.shape[0]

    def body(x_ref, do_ref, g_ref, mk_ref, mv_ref, dx_ref, dmk_ref, dmv_ref, dg_ref):
        @pl.when(pl.program_id(0) == 0)
        def _():
            dmk_ref[...] = jnp.zeros(dmk_ref.shape, F32)
            dmv_ref[...] = jnp.zeros(dmv_ref.shape, F32)
            dg_ref[...] = jnp.zeros(dg_ref.shape, F32)

        for h in range(MEM_HEADS):
            sl = slice(h * MEM_HEAD_DIM, (h + 1) * MEM_HEAD_DIM)
            r, xhat, qn, p = _cross_probs(x_ref[:, sl], g_ref[...], mk_ref[:, sl])
            do = do_ref[:, sl]
            dp = _dot(do, mv_ref[:, sl], NT)
            ds = (p * (dp - jnp.sum(p * dp, axis=-1, keepdims=True)) * MEM_SCALE).astype(BF16)
            dmv_ref[:, sl] += _dot(p.astype(BF16), do, TN)
            dmk_ref[:, sl] += _dot(ds, qn, TN)
            dqn = _dot(ds, mk_ref[:, sl])
            gy = dqn * g_ref[...]
            dx_ref[:, sl] = (r * (gy - xhat * jnp.mean(gy * xhat, axis=-1, keepdims=True))).astype(BF16)
            dg_ref[...] += jnp.sum(dqn * xhat, axis=0, keepdims=True)

    blk = pl.BlockSpec((bt, MEM_WIDTH), lambda i: (i, 0))
    kv = pl.BlockSpec((mlen, MEM_WIDTH), lambda i: (0, 0))
    gs = pl.BlockSpec((1, MEM_HEAD_DIM), lambda i: (0, 0))
    dx, dmk, dmv, dg = _pcall(
        body, name="cross_attention_bwd", grid=(t // bt,),
        in_specs=[blk, blk, gs, kv, kv], out_specs=[blk, kv, kv, gs],
        out_shape=[jax.ShapeDtypeStruct((t, MEM_WIDTH), BF16), jax.ShapeDtypeStruct((mlen, MEM_WIDTH), F32),
                   jax.ShapeDtypeStruct((mlen, MEM_WIDTH), F32), jax.ShapeDtypeStruct((1, MEM_HEAD_DIM), F32)],
        compiler_params=_params("arbitrary"),
    )(mq_raw, dmo, g_q.reshape(1, MEM_HEAD_DIM), mk, mv)
    return dx, dmk, dmv, dg.reshape(MEM_HEAD_DIM)


def _swiglu_fwd(gu, *, bt=256):
    t = gu.shape[0]

    def body(g_ref, u_ref, a_ref):
        g = g_ref[...]
        a_ref[...] = (g * jax.nn.sigmoid(g) * u_ref[...]).astype(BF16)

    return _pcall(
        body, name="swiglu_fwd", grid=(t // bt,),
        in_specs=[pl.BlockSpec((bt, D_FF), lambda i: (i, 0)), pl.BlockSpec((bt, D_FF), lambda i: (i, 1))],
        out_specs=pl.BlockSpec((bt, D_FF), lambda i: (i, 0)),
        out_shape=jax.ShapeDtypeStruct((t, D_FF), BF16), compiler_params=_params("parallel"),
    )(gu, gu)


def _swiglu_bwd(gu, dact, *, bt=256):
    t = gu.shape[0]

    def body(g_ref, u_ref, da_ref, dgu_ref):
        g = g_ref[...]
        da = da_ref[...]
        sg = jax.nn.sigmoid(g)
        silu = g * sg
        dgu_ref[:, :D_FF] = (da * u_ref[...] * (sg + silu * (1.0 - sg))).astype(BF16)
        dgu_ref[:, D_FF:] = (da * silu).astype(BF16)

    return _pcall(
        body, name="swiglu_bwd", grid=(t // bt,),
        in_specs=[pl.BlockSpec((bt, D_FF), lambda i: (i, 0)), pl.BlockSpec((bt, D_FF), lambda i: (i, 1)),
                  pl.BlockSpec((bt, D_FF), lambda i: (i, 0))],
        out_specs=pl.BlockSpec((bt, 2 * D_FF), lambda i: (i, 0)),
        out_shape=jax.ShapeDtypeStruct((t, 2 * D_FF), BF16), compiler_params=_params("parallel"),
    )(gu, gu, dact)


def _loss_head(y, target, *, bt=512):
    t, d = y.shape

    def body(y_ref, t_ref, dy_ref, l_ref):
        e = y_ref[...] - t_ref[...]
        dy_ref[...] = e * (1.0 / d)
        part = (0.5 / d) * jnp.sum(jnp.sum(e * e, axis=1, keepdims=True), axis=0, keepdims=True)

        @pl.when(pl.program_id(0) == 0)
        def _():
            l_ref[...] = part

        @pl.when(pl.program_id(0) > 0)
        def _():
            l_ref[...] += part

    blk = pl.BlockSpec((bt, d), lambda i: (i, 0))
    dy, loss = _pcall(
        body, name="loss_head", grid=(t // bt,), in_specs=[blk, blk],
        out_specs=[blk, pl.BlockSpec((1, 1), lambda i: (0, 0))],
        out_shape=[jax.ShapeDtypeStruct((t, d), F32), jax.ShapeDtypeStruct((1, 1), F32)],
        compiler_params=_params("arbitrary"),
    )(y, target)
    return loss, dy


def _row_tile(rows, cols, budget=1 << 19):
    best = None
    for cand in range(8, rows + 1, 8):
        if rows % cand == 0 and cand * cols <= budget:
            best = cand
    return best or rows


def _adamw(w, g, m, v, *, name):
    rows, cols = w.shape
    bt = _row_tile(rows, cols)
    c1 = 1.0 - ADAM_B1 ** ADAM_STEP
    c2 = 1.0 - ADAM_B2 ** ADAM_STEP

    def body(w_ref, g_ref, m_ref, v_ref, d_ref, nm_ref, nv_ref):
        g_v = g_ref[...]
        nm = ADAM_B1 * m_ref[...] + (1.0 - ADAM_B1) * g_v
        nv = ADAM_B2 * v_ref[...] + (1.0 - ADAM_B2) * (g_v * g_v)
        nm_ref[...] = nm
        nv_ref[...] = nv
        d_ref[...] = -ADAM_LR * ((nm / c1) / (jnp.sqrt(nv / c2) + ADAM_EPS) + ADAM_WD * w_ref[...])

    blk = pl.BlockSpec((bt, cols), lambda i: (i, 0))
    return _pcall(
        body, name=name, grid=(rows // bt,), in_specs=[blk] * 4, out_specs=[blk] * 3,
        out_shape=[jax.ShapeDtypeStruct((rows, cols), F32)] * 3, compiler_params=_params("parallel"),
    )(w, g, m, v)


def _sum_slots(x, *, name):
    n, rows, cols = x.shape
    bt = _row_tile(rows, cols, budget=1 << 17)

    def body(x_ref, o_ref):
        acc = x_ref[0].astype(F32)
        for s in range(1, n):
            acc = acc + x_ref[s].astype(F32)
        o_ref[...] = acc

    return _pcall(
        body, name=name, grid=(rows // bt,), in_specs=[pl.BlockSpec((n, bt, cols), lambda i: (0, i, 0))],
        out_specs=pl.BlockSpec((bt, cols), lambda i: (i, 0)),
        out_shape=jax.ShapeDtypeStruct((rows, cols), F32), compiler_params=_params("parallel"),
    )(x)


def _any_spec():
    return pl.BlockSpec(memory_space=pl.ANY)


def _all_gather(xs, *, name):
    n = len(xs)

    def body(*refs):
        x_refs, out_refs = refs[:n], refs[n:2 * n]
        send_sems, recv_sems, local_sems = refs[2 * n:]
        x, y, c = lax.axis_index("x"), lax.axis_index("y"), lax.axis_index("c")
        me, sibling = (x, y, c), (x, y, 1 - c)
        chips = [(1 - x, y), (x, 1 - y), (1 - x, 1 - y)]

        def slot(a, px, py, pc):
            return out_refs[a].at[4 * px + 2 * py + pc]

        def copy(a, k, block, to, src=None):
            return pltpu.make_async_remote_copy(
                src_ref=slot(a, *block) if src is None else src, dst_ref=slot(a, *block),
                send_sem=send_sems.at[a, k], recv_sem=recv_sems.at[a, k], device_id=to, device_id_type=MESH_ID)

        mine = [pltpu.make_async_copy(x_refs[a], slot(a, *me), local_sems.at[a]) for a in range(n)]
        for cp in mine:
            cp.start()
        first = []
        for j, chip in enumerate(chips):
            first += [copy(a, 1 + j, me, (*chip, c), src=x_refs[a]) for a in range(n)]
        first += [copy(a, 0, me, sibling, src=x_refs[a]) for a in range(n)]
        for cp in first:
            cp.start()
        passed = []
        for j, chip in enumerate(chips):
            for a in range(n):
                copy(a, 1 + j, (*chip, c), me).wait_recv()
                cp = copy(a, 4 + j, (*chip, c), sibling)
                cp.start()
                passed.append(cp)
        for a in range(n):
            copy(a, 0, sibling, me).wait_recv()
        for j, chip in enumerate(chips):
            for a in range(n):
                copy(a, 4 + j, (*chip, 1 - c), me).wait_recv()
        for cp in first + passed:
            cp.wait_send()
        for cp in mine:
            cp.wait()

    return _pcall(
        body, name=name, in_specs=[_any_spec()] * n, out_specs=[_any_spec()] * n,
        out_shape=[jax.ShapeDtypeStruct((N_DEV,) + x.shape, x.dtype) for x in xs],
        scratch_shapes=[pltpu.SemaphoreType.DMA((n, 7)), pltpu.SemaphoreType.DMA((n, 7)), pltpu.SemaphoreType.DMA((n,))],
    )(*xs)


def _exchange(xs, *, name):
    n = len(xs)

    def body(*refs):
        x_refs, out_refs = refs[:n], refs[n:2 * n]
        send_sems, recv_sems, local_sems = refs[2 * n:]
        x, y, c = lax.axis_index("x"), lax.axis_index("y"), lax.axis_index("c")
        me_idx = 4 * x + 2 * y + c

        def peer(k):
            px = x ^ ((k >> 2) & 1)
            py = y ^ ((k >> 1) & 1)
            pc = c ^ (k & 1)
            return (px, py, pc), 4 * px + 2 * py + pc

        def copy(a, k):
            to, to_idx = peer(k)
            return pltpu.make_async_remote_copy(
                src_ref=x_refs[a].at[to_idx], dst_ref=out_refs[a].at[me_idx],
                send_sem=send_sems.at[a, k - 1], recv_sem=recv_sems.at[a, k - 1], device_id=to, device_id_type=MESH_ID)

        def landing(a, k):
            frm, frm_idx = peer(k)
            return pltpu.make_async_remote_copy(
                src_ref=x_refs[a].at[frm_idx], dst_ref=out_refs[a].at[frm_idx],
                send_sem=send_sems.at[a, k - 1], recv_sem=recv_sems.at[a, k - 1], device_id=frm, device_id_type=MESH_ID)

        mine = [pltpu.make_async_copy(x_refs[a].at[me_idx], out_refs[a].at[me_idx], local_sems.at[a]) for a in range(n)]
        for cp in mine:
            cp.start()
        sends = [copy(a, k) for k in (2, 4, 6, 3, 5, 7, 1) for a in range(n)]
        for cp in sends:
            cp.start()
        for k in range(1, N_DEV):
            for a in range(n):
                landing(a, k).wait_recv()
        for cp in sends:
            cp.wait_send()
        for cp in mine:
            cp.wait()

    return _pcall(
        body, name=name, in_specs=[_any_spec()] * n, out_specs=[_any_spec()] * n,
        out_shape=[jax.ShapeDtypeStruct(x.shape, x.dtype) for x in xs],
        scratch_shapes=[pltpu.SemaphoreType.DMA((n, 7)), pltpu.SemaphoreType.DMA((n, 7)), pltpu.SemaphoreType.DMA((n,))],
    )(*xs)


def _full_weights(gathered, l):
    g = {k: v[:, l] for k, v in gathered.items()}
    w_in = jnp.transpose(g['w_in'], (1, 0, 2)).reshape(D_MODEL, IN_COLS)
    return {
        'w_in': jnp.pad(w_in, ((0, 0), (0, IN_COLS_PAD - IN_COLS))),
        'w_out': g['w_out'].reshape(D_MODEL, D_MODEL),
        'w_mem_q': g['w_mem_q'].reshape(D_MODEL, MEM_WIDTH),
        'w_mem_kv': g['w_mem_kv'].reshape(D_MODEL, 2 * MEM_WIDTH),
        'w_mem_out': jnp.transpose(g['w_mem_out'], (1, 0, 2)).reshape(MEM_WIDTH, D_MODEL),
        'w_gate_up': jnp.transpose(g['w_gate_up'], (1, 0, 2)).reshape(D_MODEL, 2 * D_FF),
        'w_down': g['w_down'].reshape(D_FF, D_MODEL),
    }


def _grad_blocks(name, per_layer):
    g = jnp.stack(per_layer, 0)
    if name == 'w_in':
        g = g[:, :, :IN_COLS]
    if name in ('w_in', 'w_mem_out', 'w_gate_up'):
        rows, cols = g.shape[1], g.shape[2] // N_DEV
        g = jnp.transpose(g.reshape(DEPTH, rows, N_DEV, cols), (2, 0, 1, 3))
    else:
        rows, cols = g.shape[1] // N_DEV, g.shape[2]
        g = jnp.transpose(g.reshape(DEPTH, N_DEV, rows, cols), (1, 0, 2, 3))
    return g.reshape(N_DEV, DEPTH * rows, cols).astype(BF16)


SMALL_SHAPES = {'g_mix': (DEPTH, D_MODEL), 'b_forget': (DEPTH, FOX_HEADS), 'g_q_fox': (DEPTH, FOX_HEAD_DIM),
                'g_k_fox': (DEPTH, FOX_HEAD_DIM), 'w_pool': (DEPTH, 4, POOL_GROUP_DIM, POOL_GROUP_DIM),
                'pool_scale': (DEPTH, POOL_WIDTH), 'g_mem_q': (DEPTH, D_MODEL), 'g_mem_kv': (DEPTH, D_MODEL),
                'g_q_mem': (DEPTH, MEM_HEAD_DIM), 'g_k_mem': (DEPTH, MEM_HEAD_DIM), 'g_ffn': (DEPTH, D_MODEL)}


def _small_rows(name):
    return -(-int(np.prod(SMALL_SHAPES[name])) // LANES)


SMALL_ROWS = -(-sum(_small_rows(n) for n in SMALL) // 8) * 8


def _pack_small(tree):
    parts = []
    for n in SMALL:
        flat = tree[n].reshape(-1).astype(F32)
        parts.append(jnp.pad(flat, (0, _small_rows(n) * LANES - flat.shape[0])))
    flat = jnp.concatenate(parts)
    return jnp.pad(flat, (0, SMALL_ROWS * LANES - flat.shape[0])).reshape(SMALL_ROWS, LANES)


def _unpack_small(packed):
    flat = packed.reshape(-1)
    out, at = {}, 0
    for n in SMALL:
        size = int(np.prod(SMALL_SHAPES[n]))
        out[n] = flat[at:at + size].reshape(SMALL_SHAPES[n])
        at += _small_rows(n) * LANES
    return out


def _pairs_cols(a):
    t = a.shape[0]
    return jnp.transpose(a.reshape(t, FOX_HEADS // 2, 2), (1, 0, 2))


def _pairs_rows(a):
    return a.reshape(FOX_HEADS // 2, 2, a.shape[1])


def _layer_fwd(h0, mem, p, w):
    s = {'h0': h0}
    s['xn1'] = _rmsnorm_fwd(h0, p['g_mix'], name="norm_mix_fwd")
    z = _matmul(s['xn1'], w['w_in'], tm=512, tn=IN_COLS_PAD, name="in_proj_fwd")
    s['z'] = z
    s['qn'], s['kn'], s['vb'] = _qkv_prep(z, p['g_q_fox'], p['g_k_fox'])
    s['ft'] = jnp.transpose(z[:, 4 * FOX_WIDTH:4 * FOX_WIDTH + FOX_HEADS])
    c = _forget_cumsum(s['ft'], p['b_forget'])
    s['ccol'], s['crow'] = _pairs_cols(jnp.transpose(c)), _pairs_rows(c)
    s['fox'], s['lse_row'] = _fox_fwd(s['qn'], s['kn'], jnp.transpose(s['vb']), s['crow'], s['ccol'])
    pool, s['mixed'] = _pool_fwd(z, p['w_pool'].astype(BF16), p['pool_scale'])
    s['cat'] = jnp.concatenate([s['fox'], pool], axis=1)
    h1 = _matmul(s['cat'], w['w_out'], res=h0, name="out_proj_fwd")
    s['h1'] = h1

    s['hn2'] = _rmsnorm_fwd(h1, p['g_mem_q'], name="norm_mem_fwd")
    s['mn'], s['mkv'], s['mk'], s['mv'] = _mem_kv_fwd(mem, p['g_mem_kv'], w['w_mem_kv'], p['g_k_mem'])
    s['mq_raw'] = _matmul(s['hn2'], w['w_mem_q'], name="mem_q_fwd")
    s['mo'] = _cross_fwd(s['mq_raw'], p['g_q_mem'], s['mk'], s['mv'])
    h2 = _matmul(s['mo'], w['w_mem_out'], res=h1, name="mem_out_fwd")
    s['h2'] = h2

    s['hn3'] = _rmsnorm_fwd(h2, p['g_ffn'], name="norm_ffn_fwd")
    s['gu'] = _matmul(s['hn3'], w['w_gate_up'], name="gate_up_fwd")
    s['act'] = _swiglu_fwd(s['gu'])
    h3 = _matmul(s['act'], w['w_down'], res=h2, name="down_fwd")
    return h3, s


def _layer_bwd(dh, mem, p, w, s):
    g = {}
    g['w_down'] = _matmul(s['act'], dh, ta=True, tm=1408, tn=512, tk=1024, name="down_dw")
    dact = _matmul(dh, w['w_down'], tb=True, tn=1408, name="down_dx")
    dgu = _swiglu_bwd(s['gu'], dact)
    g['w_gate_up'] = _matmul(s['hn3'], dgu, ta=True, tm=1024, tn=512, tk=1024, name="gate_up_dw")
    dhn3 = _matmul(dgu, w['w_gate_up'], tb=True, tn=1024, tk=2816, name="gate_up_dx")
    dh, g['g_ffn'] = _rmsnorm_bwd(s['h2'], dhn3, p['g_ffn'], dh, name="norm_ffn_bwd")

    g['w_mem_out'] = _matmul(s['mo'], dh, ta=True, tm=512, tn=1024, tk=1024, name="mem_out_dw")
    dmo = _matmul(dh, w['w_mem_out'], tb=True, out_dtype=BF16, name="mem_out_dx")
    dmq, dmk, dmv, g['g_q_mem'] = _cross_bwd(s['mq_raw'], dmo, p['g_q_mem'], s['mk'], s['mv'])
    g['w_mem_kv'], g['g_mem_kv'], g['g_k_mem'] = _mem_kv_bwd(dmk, dmv, s['mkv'], s['mn'], mem, p['g_mem_kv'],
                                                               w['w_mem_kv'], p['g_k_mem'])
    g['w_mem_q'] = _matmul(s['hn2'], dmq, ta=True, tm=1024, tn=512, tk=1024, name="mem_q_dw")
    dhn2 = _matmul(dmq, w['w_mem_q'], tb=True, tn=1024, name="mem_q_dx")
    dh, g['g_mem_q'] = _rmsnorm_bwd(s['h1'], dhn2, p['g_mem_q'], dh, name="norm_mem_bwd")

    g['w_out'] = _matmul(s['cat'], dh, ta=True, tm=1024, tn=512, tk=1024, name="out_proj_dw")
    dcat = _matmul(dh, w['w_out'], tb=True, tn=1024, name="out_proj_dx")
    dpin, g['w_pool'], dscale = _pool_bwd(dcat, s['mixed'], p['w_pool'].astype(BF16), p['pool_scale'])
    g['pool_scale'] = dscale.reshape(POOL_WIDTH)
    do, delta = _fox_delta(dcat, s['fox'])
    delta_row = _pairs_rows(jnp.transpose(delta[:, :FOX_HEADS]))
    dq, dk, dv, dc_keys, dc_queries = _fox_bwd(s['qn'], s['kn'], jnp.transpose(s['kn']), s['vb'], do, s['lse_row'],
                                               delta_row, s['crow'], s['ccol'])
    dft, db = _forget_cumsum_bwd(dc_keys, dc_queries, s['ft'], p['b_forget'])
    g['b_forget'] = db.reshape(FOX_HEADS)
    dfpad = jnp.pad(jnp.transpose(dft), ((0, 0), (0, LANES - FOX_HEADS)))
    dz, g['g_q_fox'], g['g_k_fox'] = _mixer_dz(s['z'], dq, dk, dv, dpin, dfpad, p['g_q_fox'], p['g_k_fox'])
    g['w_in'] = _matmul(s['xn1'], dz, ta=True, tm=512, tn=IN_COLS_PAD, tk=1024, name="in_proj_dw")
    dxn1 = _matmul(dz, w['w_in'], tb=True, tn=1024, name="in_proj_dx")
    dh, g['g_mix'] = _rmsnorm_bwd(s['h0'], dxn1, p['g_mix'], dh, name="norm_mix_bwd")
    return dh, g


def _local_step(x2, mem2, target2, small, full):
    h = x2
    saved = []
    for l in range(DEPTH):
        h, s = _layer_fwd(h, mem2, {k: v[l] for k, v in small.items()}, full[l])
        saved.append(s)
    loss, dh = _loss_head(h, target2)
    grads = [None] * DEPTH
    for l in reversed(range(DEPTH)):
        dh, grads[l] = _layer_bwd(dh, mem2, {k: v[l] for k, v in small.items()}, full[l], saved[l])
    return loss, dh, grads


def kernel(x, mem, g_mix, w_in, b_forget, g_q_fox, g_k_fox, w_pool, pool_scale, w_out, g_mem_q, g_mem_kv, w_mem_q, w_mem_kv, g_q_mem, g_k_mem, w_mem_out, g_ffn, w_gate_up, w_down, loss_target, m_g_mix, m_w_in, m_b_forget, m_g_q_fox, m_g_k_fox, m_w_pool, m_pool_scale, m_w_out, m_g_mem_q, m_g_mem_kv, m_w_mem_q, m_w_mem_kv, m_g_q_mem, m_g_k_mem, m_w_mem_out, m_g_ffn, m_w_gate_up, m_w_down, v_g_mix, v_w_in, v_b_forget, v_g_q_fox, v_g_k_fox, v_w_pool, v_pool_scale, v_w_out, v_g_mem_q, v_g_mem_kv, v_w_mem_q, v_w_mem_kv, v_g_q_mem, v_g_k_mem, v_w_mem_out, v_g_ffn, v_w_gate_up, v_w_down):
    weights = dict(g_mix=g_mix, w_in=w_in, b_forget=b_forget, g_q_fox=g_q_fox, g_k_fox=g_k_fox, w_pool=w_pool,
                   pool_scale=pool_scale, w_out=w_out, g_mem_q=g_mem_q, g_mem_kv=g_mem_kv, w_mem_q=w_mem_q,
                   w_mem_kv=w_mem_kv, g_q_mem=g_q_mem, g_k_mem=g_k_mem, w_mem_out=w_mem_out, g_ffn=g_ffn,
                   w_gate_up=w_gate_up, w_down=w_down)
    mom_m = dict(g_mix=m_g_mix, w_in=m_w_in, b_forget=m_b_forget, g_q_fox=m_g_q_fox, g_k_fox=m_g_k_fox, w_pool=m_w_pool,
                 pool_scale=m_pool_scale, w_out=m_w_out, g_mem_q=m_g_mem_q, g_mem_kv=m_g_mem_kv, w_mem_q=m_w_mem_q,
                 w_mem_kv=m_w_mem_kv, g_q_mem=m_g_q_mem, g_k_mem=m_g_k_mem, w_mem_out=m_w_mem_out, g_ffn=m_g_ffn,
                 w_gate_up=m_w_gate_up, w_down=m_w_down)
    mom_v = dict(g_mix=v_g_mix, w_in=v_w_in, b_forget=v_b_forget, g_q_fox=v_g_q_fox, g_k_fox=v_g_k_fox, w_pool=v_w_pool,
                 pool_scale=v_pool_scale, w_out=v_w_out, g_mem_q=v_g_mem_q, g_mem_kv=v_g_mem_kv, w_mem_q=v_w_mem_q,
                 w_mem_kv=v_w_mem_kv, g_q_mem=v_g_q_mem, g_k_mem=v_g_k_mem, w_mem_out=v_w_mem_out, g_ffn=v_g_ffn,
                 w_gate_up=v_w_gate_up, w_down=v_w_down)

    gathered = _all_gather([weights[n].astype(BF16) for n in BIG], name="weights_all_gather")
    gathered = dict(zip(BIG, gathered))
    full = [_full_weights(gathered, l) for l in range(DEPTH)]
    small = {n: weights[n] for n in SMALL}

    loss_part, grad_x, grads = _local_step(x[0], mem[0], loss_target[0], small, full)
    loss = lax.psum(loss_part[0, 0], ("x", "y", "c"))

    blocks = [_grad_blocks(n, [grads[l][n] for l in range(DEPTH)]) for n in BIG]
    landed = _exchange(blocks, name="grads_exchange")
    grad = {n: _sum_slots(landed[i], name="grad_sum_" + n) for i, n in enumerate(BIG)}
    small_part = _pack_small({n: jnp.stack([grads[l][n] for l in range(DEPTH)], 0) for n in SMALL})
    (small_all,) = _all_gather([small_part], name="small_grads_all_gather")
    small_sum = _sum_slots(small_all, name="grad_sum_small")

    delta, new_m, new_v = {}, {}, {}
    for n in BIG:
        shape = weights[n].shape
        two_d = lambda a: a.reshape(shape[0] * shape[1], shape[2])
        d, nm, nv = _adamw(two_d(weights[n]), grad[n], two_d(mom_m[n]), two_d(mom_v[n]), name="adamw_" + n)
        grad[n], delta[n], new_m[n], new_v[n] = (a.reshape(shape) for a in (grad[n], d, nm, nv))
    d, nm, nv = _adamw(_pack_small(weights), small_sum, _pack_small(mom_m), _pack_small(mom_v), name="adamw_small")
    grad.update(_unpack_small(small_sum))
    delta.update(_unpack_small(d))
    new_m.update(_unpack_small(nm))
    new_v.update(_unpack_small(nv))

    return (loss, grad_x[None], *[grad[n] for n in WEIGHTS], *[delta[n] for n in WEIGHTS],
            *[new_m[n] for n in WEIGHTS], *[new_v[n] for n in WEIGHTS])
```

```python
import functools

import numpy as np
import jax
import jax.numpy as jnp
from jax import lax
from jax.experimental import pallas as pl
from jax.experimental.pallas import tpu as pltpu

F32 = jnp.float32
BF16 = jnp.bfloat16

N_DEV = 8
D_MODEL = 1024
DEPTH = 2
FOX_HEADS = 8
FOX_HEAD_DIM = 64
FOX_WIDTH = 512
POOL_WIDTH = 512
POOL_WINDOWS = (2, 4, 8, 16)
POOL_GROUP_DIM = 128
POOL_HALO = 16
IN_COLS = 2056
IN_COLS_PAD = 2176
MEM_HEADS = 4
MEM_HEAD_DIM = 128
MEM_WIDTH = 512
D_FF = 2816
EPS = 1e-6
FOX_SCALE = FOX_HEAD_DIM ** -0.5
MEM_SCALE = MEM_HEAD_DIM ** -0.5
LANES = 128

ADAM_LR = 0.001
ADAM_B1 = 0.9
ADAM_B2 = 0.999
ADAM_EPS = 1e-08
ADAM_WD = 0.01
ADAM_STEP = 10

VMEM_LIMIT = 56 * 1024 * 1024
MESH_ID = pl.DeviceIdType.MESH

WEIGHTS = ['g_mix', 'w_in', 'b_forget', 'g_q_fox', 'g_k_fox', 'w_pool', 'pool_scale', 'w_out', 'g_mem_q', 'g_mem_kv',
           'w_mem_q', 'w_mem_kv', 'g_q_mem', 'g_k_mem', 'w_mem_out', 'g_ffn', 'w_gate_up', 'w_down']
BIG = ['w_in', 'w_out', 'w_mem_q', 'w_mem_kv', 'w_mem_out', 'w_gate_up', 'w_down']
SMALL = [n for n in WEIGHTS if n not in BIG]


def _pcall(body, **kw):
    return pl.pallas_call(body, **kw)


def _params(*sem):
    return pltpu.CompilerParams(dimension_semantics=sem or None, vmem_limit_bytes=VMEM_LIMIT)


def _dot(a, b, dims=None):
    if dims is None:
        return jnp.dot(a, b, preferred_element_type=F32)
    return lax.dot_general(a, b, (dims, ((), ())), preferred_element_type=F32)


NT = ((1,), (1,))
TN = ((0,), (0,))


def _dot_exact(x, ones_bf16):
    hi = x.astype(BF16)
    r1 = x - hi.astype(F32)
    mid = r1.astype(BF16)
    lo = (r1 - mid.astype(F32)).astype(BF16)
    return _dot(hi, ones_bf16) + _dot(mid, ones_bf16) + _dot(lo, ones_bf16)


def _matmul(a, b, *, ta=False, tb=False, out_dtype=F32, res=None, tm=1024, tn=512, tk=None, name):
    planes = b.shape[0] if b.ndim == 3 else None
    bshape = b.shape[-2:]
    m, k = (a.shape[1], a.shape[0]) if ta else a.shape
    n = bshape[0] if tb else bshape[1]
    assert k == (bshape[1] if tb else bshape[0])
    tm, tn = min(tm, m), min(tn, n)
    tk = min(tk or k, k)
    assert m % tm == 0 and n % tn == 0 and k % tk == 0, (name, m, n, k, tm, tn, tk)
    nk = k // tk
    dims = ((0 if ta else 1,), (1 if tb else 0,))

    def body(*refs):
        a_ref, b_ref = refs[0], refs[1]
        r_ref = refs[2] if res is not None else None
        o_ref = refs[3] if res is not None else refs[2]
        part = _dot(a_ref[...].astype(BF16), b_ref[...].astype(BF16), dims)

        def finish(acc):
            if r_ref is not None:
                acc = acc + r_ref[...]
            o_ref[...] = acc.astype(o_ref.dtype)

        if nk == 1:
            finish(part)
        else:
            acc_ref = refs[-1]
            kk = pl.program_id(3)

            @pl.when(kk == 0)
            def _():
                acc_ref[...] = part

            @pl.when(kk > 0)
            def _():
                acc_ref[...] += part

            @pl.when(kk == nk - 1)
            def _():
                finish(acc_ref[...])

    lead = (lambda p: (p,)) if planes else (lambda p: ())
    sq = (None,) if planes else ()
    a_spec = pl.BlockSpec((tk, tm), lambda p, i, j, kk: (kk, i)) if ta else pl.BlockSpec((tm, tk), lambda p, i, j, kk: (i, kk))
    b_spec = (pl.BlockSpec(sq + (tn, tk), lambda p, i, j, kk: lead(p) + (j, kk)) if tb
              else pl.BlockSpec(sq + (tk, tn), lambda p, i, j, kk: lead(p) + (kk, j)))
    o_spec = pl.BlockSpec(sq + (tm, tn), lambda p, i, j, kk: lead(p) + (i, j))
    in_specs = [a_spec, b_spec] + ([o_spec] if res is not None else [])
    args = (a, b) + ((res,) if res is not None else ())
    return _pcall(
        body, name=name, grid=(planes or 1, m // tm, n // tn, nk), in_specs=in_specs, out_specs=o_spec,
        out_shape=jax.ShapeDtypeStruct(((planes,) if planes else ()) + (m, n), out_dtype),
        scratch_shapes=[pltpu.VMEM((tm, tn), F32)] if nk > 1 else [],
        compiler_params=_params("parallel", "parallel", "parallel", "arbitrary"),
    )(*args)


def _rms(x):
    return lax.rsqrt(jnp.mean(x * x, axis=-1, keepdims=True) + EPS)


def _norm_matmul(h, g, w, *, tm=512, tn=512, name):
    t, d = h.shape
    n = w.shape[1]
    tn = min(tn, n)
    assert t % tm == 0 and n % tn == 0

    def body(h_ref, g_ref, w_ref, xn_ref, y_ref):
        @pl.when(pl.program_id(1) == 0)
        def _():
            x = h_ref[...]
            xn_ref[...] = (x * _rms(x) * g_ref[...]).astype(BF16)

        y_ref[...] = _dot(xn_ref[...], w_ref[...])

    row = pl.BlockSpec((tm, d), lambda i, j: (i, 0))
    return _pcall(
        body, name=name, grid=(t // tm, n // tn),
        in_specs=[row, pl.BlockSpec((1, d), lambda i, j: (0, 0)), pl.BlockSpec((d, tn), lambda i, j: (0, j))],
        out_specs=[row, pl.BlockSpec((tm, tn), lambda i, j: (i, j))],
        out_shape=[jax.ShapeDtypeStruct((t, d), BF16), jax.ShapeDtypeStruct((t, n), F32)],
        compiler_params=_params("parallel", "arbitrary"),
    )(h, g.reshape(1, d), w)


def _matmul_norm_bwd(a, w, h, g, dres, *, tm=512, tk=None, name):
    stacked = a.ndim == 3
    t = a.shape[-2]
    d, k = w.shape
    tk = a.shape[-1] if stacked else min(tk or k, k)
    nk = k // tk
    assert t % tm == 0 and k % tk == 0 and (not stacked or a.shape[0] == nk)

    def body(a_ref, w_ref, h_ref, g_ref, r_ref, dx_ref, dg_ref, *acc):
        i, kk = pl.program_id(0), pl.program_id(1)
        part = _dot(a_ref[...], w_ref[...], NT)

        def finish(dy):
            x = h_ref[...]
            r = _rms(x)
            xhat = x * r
            gy = dy * g_ref[...]
            dx_ref[...] = r_ref[...] + r * (gy - xhat * jnp.mean(gy * xhat, axis=-1, keepdims=True))
            dg_part = jnp.sum(dy * xhat, axis=0, keepdims=True)

            @pl.when(i == 0)
            def _():
                dg_ref[...] = dg_part

            @pl.when(i > 0)
            def _():
                dg_ref[...] += dg_part

        if nk == 1:
            finish(part)
        else:
            acc_ref = acc[0]

            @pl.when(kk == 0)
            def _():
                acc_ref[...] = part

            @pl.when(kk > 0)
            def _():
                acc_ref[...] += part

            @pl.when(kk == nk - 1)
            def _():
                finish(acc_ref[...])

    a_spec = (pl.BlockSpec((None, tm, tk), lambda i, kk: (kk, i, 0)) if stacked
              else pl.BlockSpec((tm, tk), lambda i, kk: (i, kk)))
    row = pl.BlockSpec((tm, d), lambda i, kk: (i, 0))
    vec = pl.BlockSpec((1, d), lambda i, kk: (0, 0))
    dx, dg = _pcall(
        body, name=name, grid=(t // tm, nk),
        in_specs=[a_spec, pl.BlockSpec((d, tk), lambda i, kk: (0, kk)), row, vec, row], out_specs=[row, vec],
        out_shape=[jax.ShapeDtypeStruct((t, d), F32), jax.ShapeDtypeStruct((1, d), F32)],
        scratch_shapes=[pltpu.VMEM((tm, d), F32)] if nk > 1 else [],
        compiler_params=_params("arbitrary", "arbitrary"),
    )(a, w, h, g.reshape(1, d), dres)
    return dx, dg.reshape(d)


def _norm_gate_up_swiglu(h, g, w, *, tm=512, tn=1408):
    t, d = h.shape
    nj = D_FF // tn
    assert t % tm == 0 and D_FF % tn == 0

    def body(h_ref, g_ref, wg_ref, wu_ref, hn_ref, gu_ref, act_ref):
        @pl.when(pl.program_id(1) == 0)
        def _():
            x = h_ref[...]
            hn_ref[...] = (x * _rms(x) * g_ref[...]).astype(BF16)

        hn = hn_ref[...]
        gate = _dot(hn, wg_ref[...])
        up = _dot(hn, wu_ref[...])
        gu_ref[0] = gate
        gu_ref[1] = up
        act_ref[...] = (gate * jax.nn.sigmoid(gate) * up).astype(BF16)

    row = pl.BlockSpec((tm, d), lambda i, j: (i, 0))
    return _pcall(
        body, name="gate_up_swiglu_fwd", grid=(t // tm, nj),
        in_specs=[row, pl.BlockSpec((1, d), lambda i, j: (0, 0)), pl.BlockSpec((d, tn), lambda i, j: (0, j)),
                  pl.BlockSpec((d, tn), lambda i, j: (0, nj + j))],
        out_specs=[row, pl.BlockSpec((2, tm, tn), lambda i, j: (0, i, j)), pl.BlockSpec((tm, tn), lambda i, j: (i, j))],
        out_shape=[jax.ShapeDtypeStruct((t, d), BF16), jax.ShapeDtypeStruct((2, t, D_FF), F32),
                   jax.ShapeDtypeStruct((t, D_FF), BF16)],
        compiler_params=_params("parallel", "arbitrary"),
    )(h, g.reshape(1, d), w, w)


def _down_dx_swiglu_bwd(dh, w_down, gu, *, tm=512, tn=1408):
    t, d = dh.shape
    assert t % tm == 0 and D_FF % tn == 0

    def body(dh_ref, w_ref, gu_ref, dgu_ref):
        da = _dot(dh_ref[...].astype(BF16), w_ref[...], NT)
        gate, up = gu_ref[0], gu_ref[1]
        sg = jax.nn.sigmoid(gate)
        silu = gate * sg
        dgu_ref[0] = (da * up * (sg + silu * (1.0 - sg))).astype(BF16)
        dgu_ref[1] = (da * silu).astype(BF16)

    stack = pl.BlockSpec((2, tm, tn), lambda i, j: (0, i, j))
    return _pcall(
        body, name="down_dx_swiglu_bwd", grid=(t // tm, D_FF // tn),
        in_specs=[pl.BlockSpec((tm, d), lambda i, j: (i, 0)), pl.BlockSpec((tn, d), lambda i, j: (j, 0)), stack],
        out_specs=stack, out_shape=jax.ShapeDtypeStruct((2, t, D_FF), BF16),
        compiler_params=_params("parallel", "parallel"),
    )(dh, w_down, gu)


def _group_matrix(width, group):
    r = lax.broadcasted_iota(jnp.int32, (width, width), 0) // group
    c = lax.broadcasted_iota(jnp.int32, (width, width), 1) // group
    return (r == c).astype(BF16)


def _qkv_prep(z, gq, gk, *, bt=512):
    t = z.shape[0]
    w = FOX_WIDTH

    def body(q_ref, k_ref, v_ref, gq_ref, gk_ref, qo_ref, ko_ref, vo_ref):
        gm = _group_matrix(w, FOX_HEAD_DIM)
        for x_ref, g_ref, o_ref, scale in ((q_ref, gq_ref, qo_ref, FOX_SCALE), (k_ref, gk_ref, ko_ref, 1.0)):
            x = x_ref[...]
            ms = _dot_exact(x * x, gm) * (1.0 / FOX_HEAD_DIM)
            y = x * lax.rsqrt(ms + EPS) * g_ref[...]
            o_ref[...] = (y * scale).astype(BF16)
        vo_ref[...] = v_ref[...].astype(BF16)

    col = lambda c: pl.BlockSpec((bt, w), lambda i, c=c: (i, c))
    vec = pl.BlockSpec((1, w), lambda i: (0, 0))
    out = pl.BlockSpec((bt, w), lambda i: (i, 0))
    return _pcall(
        body, name="fox_qkv_prep", grid=(t // bt,), in_specs=[col(0), col(1), col(2), vec, vec], out_specs=[out, out, out],
        out_shape=[jax.ShapeDtypeStruct((t, w), BF16)] * 3, compiler_params=_params("parallel"),
    )(z, z, z, jnp.tile(gq, FOX_HEADS).reshape(1, w), jnp.tile(gk, FOX_HEADS).reshape(1, w))


def _log_sigmoid(f):
    return jnp.minimum(f, 0.0) - jnp.log1p(jnp.exp(-jnp.abs(f)))


def _forget_cumsum(ft, b, *, chunk=512):
    hh, t = ft.shape

    def body(f_ref, b_ref, c_ref):
        r = lax.broadcasted_iota(jnp.int32, (chunk, chunk), 0)
        c = lax.broadcasted_iota(jnp.int32, (chunk, chunk), 1)
        upper = (r <= c).astype(BF16)
        carry = jnp.zeros((hh, 1), F32)
        for ch in range(t // chunk):
            sl = slice(ch * chunk, (ch + 1) * chunk)
            cs = _dot_exact(_log_sigmoid(f_ref[:, sl] + b_ref[...]), upper) + carry
            c_ref[:, sl] = cs
            carry = cs[:, chunk - 1:chunk]

    return _pcall(body, name="fox_forget_cumsum", out_shape=jax.ShapeDtypeStruct((hh, t), F32),
                  compiler_params=_params())(ft, b.reshape(hh, 1))


def _forget_cumsum_bwd(dc_keys, dc_queries, ft, b, *, chunk=512):
    hh, t = ft.shape

    def body(dck_ref, dcq_ref, f_ref, b_ref, df_ref, db_ref):
        r = lax.broadcasted_iota(jnp.int32, (chunk, chunk), 0)
        c = lax.broadcasted_iota(jnp.int32, (chunk, chunk), 1)
        lower = (r >= c).astype(BF16)
        carry = jnp.zeros((hh, 1), F32)
        db = jnp.zeros((hh, 1), F32)
        for ch in reversed(range(t // chunk)):
            sl = slice(ch * chunk, (ch + 1) * chunk)
            dls = _dot_exact(dck_ref[:, sl] + dcq_ref[:, sl], lower) + carry
            carry = dls[:, 0:1]
            df = dls * jax.nn.sigmoid(-(f_ref[:, sl] + b_ref[...]))
            df_ref[:, sl] = df
            db = db + jnp.sum(df, axis=1, keepdims=True)
        db_ref[...] = db

    return _pcall(body, name="fox_forget_cumsum_bwd",
                  out_shape=[jax.ShapeDtypeStruct((hh, t), F32), jax.ShapeDtypeStruct((hh, 1), F32)],
                  compiler_params=_params())(dc_keys, dc_queries, ft, b.reshape(hh, 1))


def _lane_is_first_head():
    return lax.broadcasted_iota(jnp.int32, (1, LANES), 1) < FOX_HEAD_DIM


def _fox_fwd(qn, kn, vt, crow, ccol, *, bq=512):
    t = qn.shape[0]
    nq = t // bq
    pairs = FOX_WIDTH // LANES
    tiles = [(i, j) for i in range(nq) for j in range(i + 1)]
    it = jnp.asarray(np.array([a for a, _ in tiles], np.int32))
    jt = jnp.asarray(np.array([b for _, b in tiles], np.int32))

    def body(it_ref, jt_ref, q_ref, k_ref, vt_ref, cr_ref, cc_ref, o_ref, lse_ref, m_sc, l_sc, acc_sc):
        s_id = pl.program_id(1)
        i, j = it_ref[s_id], jt_ref[s_id]
        first = _lane_is_first_head()

        @pl.when(j == 0)
        def _():
            m_sc[...] = jnp.full(m_sc.shape, -jnp.inf, F32)
            l_sc[...] = jnp.zeros(l_sc.shape, F32)
            acc_sc[...] = jnp.zeros(acc_sc.shape, F32)

        def tile(diagonal):
            q2, k2, vt2 = q_ref[...], k_ref[...], vt_ref[...]
            for hh in range(2):
                kh = jnp.where(first if hh == 0 else jnp.logical_not(first), k2, jnp.zeros_like(k2))
                ut = _dot(kh, q2, NT) - cc_ref[0, :, hh:hh + 1]
                if diagonal:
                    rr = lax.broadcasted_iota(jnp.int32, ut.shape, 0)
                    cc = lax.broadcasted_iota(jnp.int32, ut.shape, 1)
                    ut = jnp.where(rr <= cc, ut, -jnp.inf)
                c_t = cr_ref[0, hh:hh + 1, :]
                m_prev = m_sc[hh]
                m_new = jnp.maximum(m_prev, jnp.max(ut, axis=0, keepdims=True) + c_t)
                alpha = jnp.exp(m_prev - m_new)
                pt = jnp.exp(ut + (c_t - m_new))
                l_sc[hh] = alpha * l_sc[hh] + jnp.sum(pt, axis=0, keepdims=True)
                vth = vt2[hh * FOX_HEAD_DIM:(hh + 1) * FOX_HEAD_DIM]
                acc_sc[hh] = alpha * acc_sc[hh] + _dot(vth, pt.astype(BF16))
                m_sc[hh] = m_new

        @pl.when(j < i)
        def _():
            tile(False)

        @pl.when(j == i)
        def _():
            tile(True)
            ot = jnp.concatenate([acc_sc[0] / l_sc[0], acc_sc[1] / l_sc[1]], axis=0)
            o_ref[...] = jnp.transpose(ot).astype(o_ref.dtype)
            for hh in range(2):
                lse_ref[0, hh:hh + 1, :] = m_sc[hh] + jnp.log(l_sc[hh])

    qspec = pl.BlockSpec((bq, LANES), lambda p, s, it, jt: (it[s], p))
    kspec = pl.BlockSpec((bq, LANES), lambda p, s, it, jt: (jt[s], p))
    vtspec = pl.BlockSpec((LANES, bq), lambda p, s, it, jt: (p, jt[s]))
    rowq = pl.BlockSpec((1, 2, bq), lambda p, s, it, jt: (p, 0, it[s]))
    colk = pl.BlockSpec((1, bq, 2), lambda p, s, it, jt: (p, jt[s], 0))
    return _pcall(
        body, name="fox_attention_fwd",
        grid_spec=pltpu.PrefetchScalarGridSpec(
            num_scalar_prefetch=2, grid=(pairs, len(tiles)),
            in_specs=[qspec, kspec, vtspec, rowq, colk], out_specs=[qspec, rowq],
            scratch_shapes=[pltpu.VMEM((2, 1, bq), F32), pltpu.VMEM((2, 1, bq), F32),
                            pltpu.VMEM((2, FOX_HEAD_DIM, bq), F32)]),
        out_shape=[jax.ShapeDtypeStruct((t, FOX_WIDTH), BF16), jax.ShapeDtypeStruct((pairs, 2, t), F32)],
        compiler_params=_params("parallel", "arbitrary"),
    )(it, jt, qn, kn, vt, crow, ccol)


def _fox_bwd(qn, kn, knt, vb, do, lse_row, delta_row, crow, ccol, *, bq=512):
    t = qn.shape[0]
    nq = t // bq
    pairs = FOX_WIDTH // LANES
    tiles = [(i, j) for j in range(nq) for i in range(j, nq)]
    it = jnp.asarray(np.array([a for a, _ in tiles], np.int32))
    jt = jnp.asarray(np.array([b for _, b in tiles], np.int32))

    def body(it_ref, jt_ref, q_ref, k_ref, kt_ref, v_ref, do_ref, lse_ref, dl_ref, cr_ref, cc_ref,
             dqt_ref, dk_ref, dv_ref, dc_ref, dr_ref, dk_sc, dv_sc, dc_sc):
        s_id = pl.program_id(1)
        i, j = it_ref[s_id], jt_ref[s_id]
        first = _lane_is_first_head()

        @pl.when(s_id == 0)
        def _():
            dqt_ref[...] = jnp.zeros(dqt_ref.shape, F32)
            dr_ref[...] = jnp.zeros(dr_ref.shape, F32)

        @pl.when(i == j)
        def _():
            dk_sc[...] = jnp.zeros(dk_sc.shape, F32)
            dv_sc[...] = jnp.zeros(dv_sc.shape, F32)
            dc_sc[...] = jnp.zeros(dc_sc.shape, F32)

        def tile(diagonal):
            q2, k2, kt2, v2, do2 = q_ref[...], k_ref[...], kt_ref[...], v_ref[...], do_ref[...]
            dk_t, dv_t, dqt_t = [], [], []
            for hh in range(2):
                mine = first if hh == 0 else jnp.logical_not(first)
                kh = jnp.where(mine, k2, jnp.zeros_like(k2))
                vh = jnp.where(mine, v2, jnp.zeros_like(v2))
                kth = kt2[hh * FOX_HEAD_DIM:(hh + 1) * FOX_HEAD_DIM]
                st = _dot(kh, q2, NT) + (cr_ref[0, hh:hh + 1, :] - cc_ref[0, :, hh:hh + 1])
                if diagonal:
                    rr = lax.broadcasted_iota(jnp.int32, st.shape, 0)
                    cc = lax.broadcasted_iota(jnp.int32, st.shape, 1)
                    st = jnp.where(rr <= cc, st, -jnp.inf)
                pt = jnp.exp(st - lse_ref[0, hh:hh + 1, :])
                dv_t.append(_dot(pt.astype(BF16), do2))
                dpt = _dot(vh, do2, NT)
                dst = pt * (dpt - dl_ref[0, hh:hh + 1, :])
                dc_sc[hh] += jnp.sum(dst, axis=1, keepdims=True)
                dr_ref[0, i, hh:hh + 1, :] += jnp.sum(dst, axis=0, keepdims=True)
                dsb = dst.astype(BF16)
                dk_t.append(_dot(dsb, q2))
                dqt_t.append(_dot(kth, dsb))
            dk_sc[...] += jnp.where(first, dk_t[0], dk_t[1])
            dv_sc[...] += jnp.where(first, dv_t[0], dv_t[1])
            dqt_ref[0, i] += jnp.concatenate(dqt_t, axis=0)

        @pl.when(i > j)
        def _():
            tile(False)

        @pl.when(i == j)
        def _():
            tile(True)

        @pl.when(i == nq - 1)
        def _():
            dk_ref[...] = dk_sc[...]
            dv_ref[...] = dv_sc[...]
            for hh in range(2):
                dc_ref[0, :, hh:hh + 1] = -dc_sc[hh]

    qspec = pl.BlockSpec((bq, LANES), lambda p, s, it, jt: (it[s], p))
    kspec = pl.BlockSpec((bq, LANES), lambda p, s, it, jt: (jt[s], p))
    ktspec = pl.BlockSpec((LANES, bq), lambda p, s, it, jt: (p, jt[s]))
    rowq = pl.BlockSpec((1, 2, bq), lambda p, s, it, jt: (p, 0, it[s]))
    colk = pl.BlockSpec((1, bq, 2), lambda p, s, it, jt: (p, jt[s], 0))
    dqt_spec = pl.BlockSpec((1, nq, LANES, bq), lambda p, s, it, jt: (p, 0, 0, 0))
    dr_spec = pl.BlockSpec((1, nq, 2, bq), lambda p, s, it, jt: (p, 0, 0, 0))
    dqt, dk, dv, dc_keys, dc_queries = _pcall(
        body, name="fox_attention_bwd",
        grid_spec=pltpu.PrefetchScalarGridSpec(
            num_scalar_prefetch=2, grid=(pairs, len(tiles)),
            in_specs=[qspec, kspec, ktspec, kspec, qspec, rowq, rowq, rowq, colk],
            out_specs=[dqt_spec, kspec, kspec, colk, dr_spec],
            scratch_shapes=[pltpu.VMEM((bq, LANES), F32), pltpu.VMEM((bq, LANES), F32), pltpu.VMEM((2, bq, 1), F32)]),
        out_shape=[jax.ShapeDtypeStruct((pairs, nq, LANES, bq), F32), jax.ShapeDtypeStruct((t, FOX_WIDTH), F32),
                   jax.ShapeDtypeStruct((t, FOX_WIDTH), F32), jax.ShapeDtypeStruct((pairs, t, 2), F32),
                   jax.ShapeDtypeStruct((pairs, nq, 2, bq), F32)],
        compiler_params=_params("parallel", "arbitrary"),
    )(it, jt, qn, kn, knt, vb, do, lse_row, delta_row, crow, ccol)
    dq = jnp.transpose(dqt, (1, 3, 0, 2)).reshape(t, FOX_WIDTH)
    dc_keys = jnp.transpose(dc_keys, (0, 2, 1)).reshape(FOX_HEADS, t)
    dc_queries = jnp.transpose(dc_queries, (0, 2, 1, 3)).reshape(FOX_HEADS, t)
    return dq, dk, dv, dc_keys, dc_queries


def _fox_delta(dcat, fox, *, bt=512):
    t = fox.shape[0]
    w = FOX_WIDTH

    def body(do_ref, o_ref, dob_ref, dl_ref):
        dob = do_ref[...].astype(BF16)
        r = lax.broadcasted_iota(jnp.int32, (w, LANES), 0) // FOX_HEAD_DIM
        c = lax.broadcasted_iota(jnp.int32, (w, LANES), 1)
        dl_ref[...] = _dot_exact(dob.astype(F32) * o_ref[...].astype(F32), (r == c).astype(BF16))
        dob_ref[...] = dob

    blk = pl.BlockSpec((bt, w), lambda i: (i, 0))
    return _pcall(
        body, name="fox_delta", grid=(t // bt,), in_specs=[blk, blk],
        out_specs=[blk, pl.BlockSpec((bt, LANES), lambda i: (i, 0))],
        out_shape=[jax.ShapeDtypeStruct((t, w), BF16), jax.ShapeDtypeStruct((t, LANES), F32)],
        compiler_params=_params("parallel"),
    )(dcat, fox)


def _mixer_dz(z, dq, dk, dv, dpin, dfpad, gq, gk, *, bt=256):
    t = z.shape[0]
    w = FOX_WIDTH

    def body(q_ref, k_ref, dq_ref, dk_ref, dv_ref, dp_ref, df_ref, gq_ref, gk_ref, dz_ref, dgq_ref, dgk_ref):
        gm = _group_matrix(w, FOX_HEAD_DIM)
        first_step = pl.program_id(0) == 0
        for n, (x_ref, dy_ref, g_ref, dg_ref, scale) in enumerate(
                ((q_ref, dq_ref, gq_ref, dgq_ref, FOX_SCALE), (k_ref, dk_ref, gk_ref, dgk_ref, 1.0))):
            x = x_ref[...]
            r = lax.rsqrt(_dot_exact(x * x, gm) * (1.0 / FOX_HEAD_DIM) + EPS)
            xhat = x * r
            dy = dy_ref[...] * scale
            gy = dy * g_ref[...]
            dx = r * (gy - xhat * (_dot_exact(gy * xhat, gm) * (1.0 / FOX_HEAD_DIM)))
            dz_ref[:, n * w:(n + 1) * w] = dx.astype(BF16)
            part = jnp.sum(dy * xhat, axis=0, keepdims=True)

            @pl.when(first_step)
            def _():
                dg_ref[...] = part

            @pl.when(jnp.logical_not(first_step))
            def _():
                dg_ref[...] += part

        dz_ref[:, 2 * w:3 * w] = dv_ref[...].astype(BF16)
        dz_ref[:, 3 * w:4 * w] = dp_ref[...].astype(BF16)
        dz_ref[:, 4 * w:] = df_ref[...].astype(BF16)

    col = lambda c: pl.BlockSpec((bt, w), lambda i, c=c: (i, c))
    blk = pl.BlockSpec((bt, w), lambda i: (i, 0))
    vec = pl.BlockSpec((1, w), lambda i: (0, 0))
    dz, dgq, dgk = _pcall(
        body, name="mixer_dz", grid=(t // bt,),
        in_specs=[col(0), col(1), blk, blk, blk, blk, pl.BlockSpec((bt, LANES), lambda i: (i, 0)), vec, vec],
        out_specs=[pl.BlockSpec((bt, IN_COLS_PAD), lambda i: (i, 0)), vec, vec],
        out_shape=[jax.ShapeDtypeStruct((t, IN_COLS_PAD), BF16), jax.ShapeDtypeStruct((1, w), F32),
                   jax.ShapeDtypeStruct((1, w), F32)],
        compiler_params=_params("arbitrary"),
    )(z, z, dq, dk, dv, dpin, dfpad, jnp.tile(gq, FOX_HEADS).reshape(1, w), jnp.tile(gk, FOX_HEADS).reshape(1, w))
    return dz, dgq.reshape(FOX_HEADS, FOX_HEAD_DIM).sum(0), dgk.reshape(FOX_HEADS, FOX_HEAD_DIM).sum(0)


def _pool_fwd(z, wp, scale, *, bt=512):
    t = z.shape[0]
    w = POOL_WIDTH
    hb = bt // POOL_HALO

    def body(p_ref, h_ref, wp_ref, sc_ref, y_ref, mx_ref):
        i = pl.program_id(0)
        cur = p_ref[...]
        halo = jnp.where(i > 0, h_ref[...], 0.0)
        ext = jnp.concatenate([halo, cur], axis=0)
        trow = i * bt + lax.broadcasted_iota(jnp.int32, (bt, 1), 0)
        for g, win in enumerate(POOL_WINDOWS):
            sl = slice(g * LANES, (g + 1) * LANES)
            e = ext[:, sl]
            acc = e[POOL_HALO:]
            for k in range(1, win):
                acc = acc + pltpu.roll(e, k, 0)[POOL_HALO:]
            cnt = jnp.minimum(trow + 1, win).astype(F32)
            mixed = (acc / cnt - cur[:, sl]).astype(BF16)
            mx_ref[:, sl] = mixed
            y_ref[:, sl] = (_dot(mixed, wp_ref[g]) * sc_ref[:, sl]).astype(BF16)

    blk = pl.BlockSpec((bt, w), lambda i: (i, 0))
    return _pcall(
        body, name="pool_fwd", grid=(t // bt,),
        in_specs=[pl.BlockSpec((bt, w), lambda i: (i, 3)),
                  pl.BlockSpec((POOL_HALO, w), lambda i: (jnp.maximum(i * hb - 1, 0), 3)),
                  pl.BlockSpec((len(POOL_WINDOWS), LANES, LANES), lambda i: (0, 0, 0)),
                  pl.BlockSpec((1, w), lambda i: (0, 0))],
        out_specs=[blk, blk], out_shape=[jax.ShapeDtypeStruct((t, w), BF16)] * 2,
        compiler_params=_params("parallel"),
    )(z, z, wp, scale.reshape(1, w))


def _pool_bwd(dcat, mixed, wp, scale, *, bt=512):
    t = mixed.shape[0]
    w = POOL_WIDTH
    hb = bt // POOL_HALO
    nb = t // bt
    n_ext = bt + POOL_HALO

    def body(d_ref, h_ref, mx_ref, wp_ref, sc_ref, dp_ref, dwp_ref, dsc_ref):
        i = pl.program_id(0)
        cur = d_ref[...]
        nxt = jnp.where(i < nb - 1, h_ref[...], 0.0)
        ext = jnp.concatenate([cur, nxt], axis=0)
        trow = i * bt + lax.broadcasted_iota(jnp.int32, (n_ext, 1), 0)

        @pl.when(i == 0)
        def _():
            dwp_ref[...] = jnp.zeros(dwp_ref.shape, F32)
            dsc_ref[...] = jnp.zeros(dsc_ref.shape, F32)

        for g, win in enumerate(POOL_WINDOWS):
            sl = slice(g * LANES, (g + 1) * LANES)
            dy = (ext[:, sl] * sc_ref[:, sl]).astype(BF16)
            dm = _dot(dy, wp_ref[g], NT)
            mixed_g = mx_ref[:, sl]
            dsc_ref[:, sl] += jnp.sum(cur[:, sl] * _dot(mixed_g, wp_ref[g]), axis=0, keepdims=True)
            dwp_ref[g] += _dot(mixed_g, dy[:bt], TN)
            r = dm / jnp.minimum(trow + 1, win).astype(F32)
            acc = r[:bt]
            for k in range(1, win):
                acc = acc + pltpu.roll(r, n_ext - k, 0)[:bt]
            dp_ref[:, sl] = acc - dm[:bt]

    return _pcall(
        body, name="pool_bwd", grid=(nb,),
        in_specs=[pl.BlockSpec((bt, w), lambda i: (i, 1)),
                  pl.BlockSpec((POOL_HALO, w), lambda i: (jnp.minimum((i + 1) * hb, t // POOL_HALO - 1), 1)),
                  pl.BlockSpec((bt, w), lambda i: (i, 0)),
                  pl.BlockSpec((len(POOL_WINDOWS), LANES, LANES), lambda i: (0, 0, 0)),
                  pl.BlockSpec((1, w), lambda i: (0, 0))],
        out_specs=[pl.BlockSpec((bt, w), lambda i: (i, 0)),
                   pl.BlockSpec((len(POOL_WINDOWS), LANES, LANES), lambda i: (0, 0, 0)),
                   pl.BlockSpec((1, w), lambda i: (0, 0))],
        out_shape=[jax.ShapeDtypeStruct((t, w), F32), jax.ShapeDtypeStruct((len(POOL_WINDOWS), LANES, LANES), F32),
                   jax.ShapeDtypeStruct((1, w), F32)],
        compiler_params=_params("arbitrary"),
    )(dcat, dcat, mixed, wp, scale.reshape(1, w))


def _head_rms(x):
    return lax.rsqrt(jnp.mean(x * x, axis=-1, keepdims=True) + EPS)


def _mem_kv_fwd(mem, g_kv, w_kv, g_k):
    mlen, d = mem.shape

    def body(m_ref, g_ref, w_ref, gk_ref, mn_ref, mkv_ref, mk_ref, mv_ref):
        x = m_ref[...]
        mn = (x * lax.rsqrt(jnp.mean(x * x, axis=-1, keepdims=True) + EPS) * g_ref[...]).astype(BF16)
        mn_ref[...] = mn
        mkv = _dot(mn, w_ref[...])
        mkv_ref[...] = mkv
        for h in range(MEM_HEADS):
            sl = slice(h * MEM_HEAD_DIM, (h + 1) * MEM_HEAD_DIM)
            kh = mkv[:, sl]
            mk_ref[:, sl] = (kh * _head_rms(kh) * gk_ref[...]).astype(BF16)
        mv_ref[...] = mkv[:, MEM_WIDTH:].astype(BF16)

    return _pcall(
        body, name="mem_kv_fwd",
        out_shape=[jax.ShapeDtypeStruct((mlen, d), BF16), jax.ShapeDtypeStruct((mlen, 2 * MEM_WIDTH), F32),
                   jax.ShapeDtypeStruct((mlen, MEM_WIDTH), BF16), jax.ShapeDtypeStruct((mlen, MEM_WIDTH), BF16)],
        compiler_params=_params(),
    )(mem, g_kv.reshape(1, d), w_kv, g_k.reshape(1, MEM_HEAD_DIM))


def _mem_kv_bwd(dmk, dmv, mkv, mn, mem, g_kv, w_kv, g_k):
    mlen, d = mem.shape

    def body(dmk_ref, dmv_ref, mkv_ref, mn_ref, m_ref, g_ref, w_ref, gk_ref, dw_ref, dg_ref, dgk_ref, dkv_sc):
        dgk = jnp.zeros((1, MEM_HEAD_DIM), F32)
        for h in range(MEM_HEADS):
            sl = slice(h * MEM_HEAD_DIM, (h + 1) * MEM_HEAD_DIM)
            x = mkv_ref[:, sl]
            r = _head_rms(x)
            xhat = x * r
            dy = dmk_ref[:, sl]
            gy = dy * gk_ref[...]
            dkv_sc[:, sl] = (r * (gy - xhat * jnp.mean(gy * xhat, axis=-1, keepdims=True))).astype(BF16)
            dgk = dgk + jnp.sum(dy * xhat, axis=0, keepdims=True)
        dgk_ref[...] = dgk
        dkv_sc[:, MEM_WIDTH:] = dmv_ref[...].astype(BF16)
        dkv = dkv_sc[...]
        dw_ref[...] = _dot(mn_ref[...], dkv, TN)
        dmn = _dot(dkv, w_ref[...], NT)
        x = m_ref[...]
        xhat = x * lax.rsqrt(jnp.mean(x * x, axis=-1, keepdims=True) + EPS)
        dg_ref[...] = jnp.sum(dmn * xhat, axis=0, keepdims=True)

    dw, dg, dgk = _pcall(
        body, name="mem_kv_bwd",
        out_shape=[jax.ShapeDtypeStruct((d, 2 * MEM_WIDTH), F32), jax.ShapeDtypeStruct((1, d), F32),
                   jax.ShapeDtypeStruct((1, MEM_HEAD_DIM), F32)],
        scratch_shapes=[pltpu.VMEM((mlen, 2 * MEM_WIDTH), BF16)],
        compiler_params=_params(),
    )(dmk, dmv, mkv, mn, mem, g_kv.reshape(1, d), w_kv, g_k.reshape(1, MEM_HEAD_DIM))
    return dw, dg.reshape(d), dgk.reshape(MEM_HEAD_DIM)


def _cross_probs(x, g, mk_h):
    r = _head_rms(x)
    xhat = x * r
    qn = (xhat * g).astype(BF16)
    s = _dot(qn, mk_h, NT) * MEM_SCALE
    e = jnp.exp(s - jnp.max(s, axis=-1, keepdims=True))
    return r, xhat, qn, e / jnp.sum(e, axis=-1, keepdims=True)


def _cross_fwd(mq_raw, g_q, mk, mv, *, bt=512):
    t = mq_raw.shape[0]
    mlen = mk.shape[0]

    def body(x_ref, g_ref, mk_ref, mv_ref, o_ref):
        for h in range(MEM_HEADS):
            sl = slice(h * MEM_HEAD_DIM, (h + 1) * MEM_HEAD_DIM)
            _, _, _, p = _cross_probs(x_ref[:, sl], g_ref[...], mk_ref[:, sl])
            o_ref[:, sl] = _dot(p.astype(BF16), mv_ref[:, sl]).astype(BF16)

    blk = pl.BlockSpec((bt, MEM_WIDTH), lambda i: (i, 0))
    kv = pl.BlockSpec((mlen, MEM_WIDTH), lambda i: (0, 0))
    return _pcall(
        body, name="cross_attention_fwd", grid=(t // bt,),
        in_specs=[blk, pl.BlockSpec((1, MEM_HEAD_DIM), lambda i: (0, 0)), kv, kv], out_specs=blk,
        out_shape=jax.ShapeDtypeStruct((t, MEM_WIDTH), BF16), compiler_params=_params("parallel"),
    )(mq_raw, g_q.reshape(1, MEM_HEAD_DIM), mk, mv)


def _cross_bwd(mq_raw, dmo, g_q, mk, mv, *, bt=512):
    t = mq_raw.shape[0]
    mlen = mk.shape[0]

    def body(x_ref, do_ref, g_ref, mk_ref, mv_ref, dx_ref, dmk_ref, dmv_ref, dg_ref):
        @pl.when(pl.program_id(0) == 0)
        def _():
            dmk_ref[...] = jnp.zeros(dmk_ref.shape, F32)
            dmv_ref[...] = jnp.zeros(dmv_ref.shape, F32)
            dg_ref[...] = jnp.zeros(dg_ref.shape, F32)

        for h in range(MEM_HEADS):
            sl = slice(h * MEM_HEAD_DIM, (h + 1) * MEM_HEAD_DIM)
            r, xhat, qn, p = _cross_probs(x_ref[:, sl], g_ref[...], mk_ref[:, sl])
            do = do_ref[:, sl]
            dp = _dot(do, mv_ref[:, sl], NT)
            ds = (p * (dp - jnp.sum(p * dp, axis=-1, keepdims=True)) * MEM_SCALE).astype(BF16)
            dmv_ref[:, sl] += _dot(p.astype(BF16), do, TN)
            dmk_ref[:, sl] += _dot(ds, qn, TN)
            dqn = _dot(ds, mk_ref[:, sl])
            gy = dqn * g_ref[...]
            dx_ref[:, sl] = (r * (gy - xhat * jnp.mean(gy * xhat, axis=-1, keepdims=True))).astype(BF16)
            dg_ref[...] += jnp.sum(dqn * xhat, axis=0, keepdims=True)

    blk = pl.BlockSpec((bt, MEM_WIDTH), lambda i: (i, 0))
    kv = pl.BlockSpec((mlen, MEM_WIDTH), lambda i: (0, 0))
    gs = pl.BlockSpec((1, MEM_HEAD_DIM), lambda i: (0, 0))
    dx, dmk, dmv, dg = _pcall(
        body, name="cross_attention_bwd", grid=(t // bt,),
        in_specs=[blk, blk, gs, kv, kv], out_specs=[blk, kv, kv, gs],
        out_shape=[jax.ShapeDtypeStruct((t, MEM_WIDTH), BF16), jax.ShapeDtypeStruct((mlen, MEM_WIDTH), F32),
                   jax.ShapeDtypeStruct((mlen, MEM_WIDTH), F32), jax.ShapeDtypeStruct((1, MEM_HEAD_DIM), F32)],
        compiler_params=_params("arbitrary"),
    )(mq_raw, dmo, g_q.reshape(1, MEM_HEAD_DIM), mk, mv)
    return dx, dmk, dmv, dg.reshape(MEM_HEAD_DIM)


def _loss_head(y, target, *, bt=512):
    t, d = y.shape

    def body(y_ref, t_ref, dy_ref, l_ref):
        e = y_ref[...] - t_ref[...]
        dy_ref[...] = e * (1.0 / d)
        part = (0.5 / d) * jnp.sum(jnp.sum(e * e, axis=1, keepdims=True), axis=0, keepdims=True)

        @pl.when(pl.program_id(0) == 0)
        def _():
            l_ref[...] = part

        @pl.when(pl.program_id(0) > 0)
        def _():
            l_ref[...] += part

    blk = pl.BlockSpec((bt, d), lambda i: (i, 0))
    dy, loss = _pcall(
        body, name="loss_head", grid=(t // bt,), in_specs=[blk, blk],
        out_specs=[blk, pl.BlockSpec((1, 1), lambda i: (0, 0))],
        out_shape=[jax.ShapeDtypeStruct((t, d), F32), jax.ShapeDtypeStruct((1, 1), F32)],
        compiler_params=_params("arbitrary"),
    )(y, target)
    return loss, dy


def _row_tile(rows, cols, budget=1 << 19):
    best = None
    for cand in range(8, rows + 1, 8):
        if rows % cand == 0 and cand * cols <= budget:
            best = cand
    return best or rows


def _adamw(w, g, m, v, *, name):
    rows, cols = w.shape
    bt = _row_tile(rows, cols)
    c1 = 1.0 - ADAM_B1 ** ADAM_STEP
    c2 = 1.0 - ADAM_B2 ** ADAM_STEP

    def body(w_ref, g_ref, m_ref, v_ref, d_ref, nm_ref, nv_ref):
        g_v = g_ref[...]
        nm = ADAM_B1 * m_ref[...] + (1.0 - ADAM_B1) * g_v
        nv = ADAM_B2 * v_ref[...] + (1.0 - ADAM_B2) * (g_v * g_v)
        nm_ref[...] = nm
        nv_ref[...] = nv
        d_ref[...] = -ADAM_LR * ((nm / c1) / (jnp.sqrt(nv / c2) + ADAM_EPS) + ADAM_WD * w_ref[...])

    blk = pl.BlockSpec((bt, cols), lambda i: (i, 0))
    return _pcall(
        body, name=name, grid=(rows // bt,), in_specs=[blk] * 4, out_specs=[blk] * 3,
        out_shape=[jax.ShapeDtypeStruct((rows, cols), F32)] * 3, compiler_params=_params("parallel"),
    )(w, g, m, v)


def _sum_slots(x, *, name):
    n, rows, cols = x.shape
    bt = _row_tile(rows, cols, budget=1 << 17)

    def body(x_ref, o_ref):
        acc = x_ref[0].astype(F32)
        for s in range(1, n):
            acc = acc + x_ref[s].astype(F32)
        o_ref[...] = acc

    return _pcall(
        body, name=name, grid=(rows // bt,), in_specs=[pl.BlockSpec((n, bt, cols), lambda i: (0, i, 0))],
        out_specs=pl.BlockSpec((bt, cols), lambda i: (i, 0)),
        out_shape=jax.ShapeDtypeStruct((rows, cols), F32), compiler_params=_params("parallel"),
    )(x)


def _any_spec():
    return pl.BlockSpec(memory_space=pl.ANY)


def _all_gather(xs, *, name):
    n = len(xs)

    def body(*refs):
        x_refs, out_refs = refs[:n], refs[n:2 * n]
        send_sems, recv_sems, local_sems = refs[2 * n:]
        x, y, c = lax.axis_index("x"), lax.axis_index("y"), lax.axis_index("c")
        me, sibling = (x, y, c), (x, y, 1 - c)
        chips = [(1 - x, y), (x, 1 - y), (1 - x, 1 - y)]

        def slot(a, px, py, pc):
            return out_refs[a].at[4 * px + 2 * py + pc]

        def copy(a, k, block, to, src=None):
            return pltpu.make_async_remote_copy(
                src_ref=slot(a, *block) if src is None else src, dst_ref=slot(a, *block),
                send_sem=send_sems.at[a, k], recv_sem=recv_sems.at[a, k], device_id=to, device_id_type=MESH_ID)

        mine = [pltpu.make_async_copy(x_refs[a], slot(a, *me), local_sems.at[a]) for a in range(n)]
        for cp in mine:
            cp.start()
        first = []
        for j, chip in enumerate(chips):
            first += [copy(a, 1 + j, me, (*chip, c), src=x_refs[a]) for a in range(n)]
        first += [copy(a, 0, me, sibling, src=x_refs[a]) for a in range(n)]
        for cp in first:
            cp.start()
        passed = []
        for j, chip in enumerate(chips):
            for a in range(n):
                copy(a, 1 + j, (*chip, c), me).wait_recv()
                cp = copy(a, 4 + j, (*chip, c), sibling)
                cp.start()
                passed.append(cp)
        for a in range(n):
            copy(a, 0, sibling, me).wait_recv()
        for j, chip in enumerate(chips):
            for a in range(n):
                copy(a, 4 + j, (*chip, 1 - c), me).wait_recv()
        for cp in first + passed:
            cp.wait_send()
        for cp in mine:
            cp.wait()

    return _pcall(
        body, name=name, in_specs=[_any_spec()] * n, out_specs=[_any_spec()] * n,
        out_shape=[jax.ShapeDtypeStruct((N_DEV,) + x.shape, x.dtype) for x in xs],
        scratch_shapes=[pltpu.SemaphoreType.DMA((n, 7)), pltpu.SemaphoreType.DMA((n, 7)), pltpu.SemaphoreType.DMA((n,))],
    )(*xs)


def _exchange(xs, *, name):
    n = len(xs)

    def body(*refs):
        x_refs, out_refs = refs[:n], refs[n:2 * n]
        send_sems, recv_sems, local_sems = refs[2 * n:]
        x, y, c = lax.axis_index("x"), lax.axis_index("y"), lax.axis_index("c")
        me_idx = 4 * x + 2 * y + c

        def peer(k):
            px = x ^ ((k >> 2) & 1)
            py = y ^ ((k >> 1) & 1)
            pc = c ^ (k & 1)
            return (px, py, pc), 4 * px + 2 * py + pc

        def copy(a, k):
            to, to_idx = peer(k)
            return pltpu.make_async_remote_copy(
                src_ref=x_refs[a].at[to_idx], dst_ref=out_refs[a].at[me_idx],
                send_sem=send_sems.at[a, k - 1], recv_sem=recv_sems.at[a, k - 1], device_id=to, device_id_type=MESH_ID)

        def landing(a, k):
            frm, frm_idx = peer(k)
            return pltpu.make_async_remote_copy(
                src_ref=x_refs[a].at[frm_idx], dst_ref=out_refs[a].at[frm_idx],
                send_sem=send_sems.at[a, k - 1], recv_sem=recv_sems.at[a, k - 1], device_id=frm, device_id_type=MESH_ID)

        mine = [pltpu.make_async_copy(x_refs[a].at[me_idx], out_refs[a].at[me_idx], local_sems.at[a]) for a in range(n)]
        for cp in mine:
            cp.start()
        sends = [copy(a, k) for k in (2, 4, 6, 3, 5, 7, 1) for a in range(n)]
        for cp in sends:
            cp.start()
        for k in range(1, N_DEV):
            for a in range(n):
                landing(a, k).wait_recv()
        for cp in sends:
            cp.wait_send()
        for cp in mine:
            cp.wait()

    return _pcall(
        body, name=name, in_specs=[_any_spec()] * n, out_specs=[_any_spec()] * n,
        out_shape=[jax.ShapeDtypeStruct(x.shape, x.dtype) for x in xs],
        scratch_shapes=[pltpu.SemaphoreType.DMA((n, 7)), pltpu.SemaphoreType.DMA((n, 7)), pltpu.SemaphoreType.DMA((n,))],
    )(*xs)


def _full_weights(gathered, l):
    g = {k: v[:, l] for k, v in gathered.items()}
    w_in = jnp.transpose(g['w_in'], (1, 0, 2)).reshape(D_MODEL, IN_COLS)
    return {
        'w_in': jnp.pad(w_in, ((0, 0), (0, IN_COLS_PAD - IN_COLS))),
        'w_out': g['w_out'].reshape(D_MODEL, D_MODEL),
        'w_mem_q': g['w_mem_q'].reshape(D_MODEL, MEM_WIDTH),
        'w_mem_kv': g['w_mem_kv'].reshape(D_MODEL, 2 * MEM_WIDTH),
        'w_mem_out': jnp.transpose(g['w_mem_out'], (1, 0, 2)).reshape(MEM_WIDTH, D_MODEL),
        'w_gate_up': jnp.transpose(g['w_gate_up'], (1, 0, 2)).reshape(D_MODEL, 2 * D_FF),
        'w_down': g['w_down'].reshape(D_FF, D_MODEL),
    }


def _grad_blocks(name, per_layer):
    g = jnp.stack(per_layer, 0)
    if name == 'w_gate_up':
        rows, cols = g.shape[2], 2 * g.shape[3] // N_DEV
        g = jnp.transpose(g.reshape(DEPTH, 2, rows, N_DEV // 2, cols), (1, 3, 0, 2, 4))
        return g.reshape(N_DEV, DEPTH * rows, cols).astype(BF16)
    if name == 'w_in':
        g = g[:, :, :IN_COLS]
    if name in ('w_in', 'w_mem_out', 'w_gate_up'):
        rows, cols = g.shape[1], g.shape[2] // N_DEV
        g = jnp.transpose(g.reshape(DEPTH, rows, N_DEV, cols), (2, 0, 1, 3))
    else:
        rows, cols = g.shape[1] // N_DEV, g.shape[2]
        g = jnp.transpose(g.reshape(DEPTH, N_DEV, rows, cols), (1, 0, 2, 3))
    return g.reshape(N_DEV, DEPTH * rows, cols).astype(BF16)


SMALL_SHAPES = {'g_mix': (DEPTH, D_MODEL), 'b_forget': (DEPTH, FOX_HEADS), 'g_q_fox': (DEPTH, FOX_HEAD_DIM),
                'g_k_fox': (DEPTH, FOX_HEAD_DIM), 'w_pool': (DEPTH, 4, POOL_GROUP_DIM, POOL_GROUP_DIM),
                'pool_scale': (DEPTH, POOL_WIDTH), 'g_mem_q': (DEPTH, D_MODEL), 'g_mem_kv': (DEPTH, D_MODEL),
                'g_q_mem': (DEPTH, MEM_HEAD_DIM), 'g_k_mem': (DEPTH, MEM_HEAD_DIM), 'g_ffn': (DEPTH, D_MODEL)}


def _small_rows(name):
    return -(-int(np.prod(SMALL_SHAPES[name])) // LANES)


SMALL_ROWS = -(-sum(_small_rows(n) for n in SMALL) // 8) * 8


def _pack_small(tree):
    parts = []
    for n in SMALL:
        flat = tree[n].reshape(-1).astype(F32)
        parts.append(jnp.pad(flat, (0, _small_rows(n) * LANES - flat.shape[0])))
    flat = jnp.concatenate(parts)
    return jnp.pad(flat, (0, SMALL_ROWS * LANES - flat.shape[0])).reshape(SMALL_ROWS, LANES)


def _unpack_small(packed):
    flat = packed.reshape(-1)
    out, at = {}, 0
    for n in SMALL:
        size = int(np.prod(SMALL_SHAPES[n]))
        out[n] = flat[at:at + size].reshape(SMALL_SHAPES[n])
        at += _small_rows(n) * LANES
    return out


def _pairs_cols(a):
    t = a.shape[0]
    return jnp.transpose(a.reshape(t, FOX_HEADS // 2, 2), (1, 0, 2))


def _pairs_rows(a):
    return a.reshape(FOX_HEADS // 2, 2, a.shape[1])


def _layer_fwd(h0, mem, p, w):
    s = {'h0': h0}
    s['xn1'], z = _norm_matmul(h0, p['g_mix'], w['w_in'], tn=IN_COLS_PAD, name="norm_in_proj_fwd")
    s['z'] = z
    s['qn'], s['kn'], s['vb'] = _qkv_prep(z, p['g_q_fox'], p['g_k_fox'])
    s['ft'] = jnp.transpose(z[:, 4 * FOX_WIDTH:4 * FOX_WIDTH + FOX_HEADS])
    c = _forget_cumsum(s['ft'], p['b_forget'])
    s['ccol'], s['crow'] = _pairs_cols(jnp.transpose(c)), _pairs_rows(c)
    s['fox'], s['lse_row'] = _fox_fwd(s['qn'], s['kn'], jnp.transpose(s['vb']), s['crow'], s['ccol'])
    pool, s['mixed'] = _pool_fwd(z, p['w_pool'].astype(BF16), p['pool_scale'])
    s['cat'] = jnp.concatenate([s['fox'], pool], axis=1)
    h1 = _matmul(s['cat'], w['w_out'], res=h0, name="out_proj_fwd")
    s['h1'] = h1

    s['hn2'], s['mq_raw'] = _norm_matmul(h1, p['g_mem_q'], w['w_mem_q'], name="norm_mem_q_fwd")
    s['mn'], s['mkv'], s['mk'], s['mv'] = _mem_kv_fwd(mem, p['g_mem_kv'], w['w_mem_kv'], p['g_k_mem'])
    s['mo'] = _cross_fwd(s['mq_raw'], p['g_q_mem'], s['mk'], s['mv'])
    h2 = _matmul(s['mo'], w['w_mem_out'], res=h1, name="mem_out_fwd")
    s['h2'] = h2

    s['hn3'], s['gu'], s['act'] = _norm_gate_up_swiglu(h2, p['g_ffn'], w['w_gate_up'])
    h3 = _matmul(s['act'], w['w_down'], res=h2, name="down_fwd")
    return h3, s


def _layer_bwd(dh, mem, p, w, s):
    g = {}
    g['w_down'] = _matmul(s['act'], dh, ta=True, tm=1408, tn=512, tk=1024, name="down_dw")
    dgu = _down_dx_swiglu_bwd(dh, w['w_down'], s['gu'])
    g['w_gate_up'] = _matmul(s['hn3'], dgu, ta=True, tm=1024, tn=1408, tk=1024, name="gate_up_dw")
    dh, g['g_ffn'] = _matmul_norm_bwd(dgu, w['w_gate_up'], s['h2'], p['g_ffn'], dh, name="gate_up_dx_norm_bwd")

    g['w_mem_out'] = _matmul(s['mo'], dh, ta=True, tm=512, tn=1024, tk=1024, name="mem_out_dw")
    dmo = _matmul(dh, w['w_mem_out'], tb=True, out_dtype=BF16, name="mem_out_dx")
    dmq, dmk, dmv, g['g_q_mem'] = _cross_bwd(s['mq_raw'], dmo, p['g_q_mem'], s['mk'], s['mv'])
    g['w_mem_kv'], g['g_mem_kv'], g['g_k_mem'] = _mem_kv_bwd(dmk, dmv, s['mkv'], s['mn'], mem, p['g_mem_kv'],
                                                               w['w_mem_kv'], p['g_k_mem'])
    g['w_mem_q'] = _matmul(s['hn2'], dmq, ta=True, tm=1024, tn=512, tk=1024, name="mem_q_dw")
    dh, g['g_mem_q'] = _matmul_norm_bwd(dmq, w['w_mem_q'], s['h1'], p['g_mem_q'], dh, name="mem_q_dx_norm_bwd")

    g['w_out'] = _matmul(s['cat'], dh, ta=True, tm=1024, tn=512, tk=1024, name="out_proj_dw")
    dcat = _matmul(dh, w['w_out'], tb=True, tn=1024, name="out_proj_dx")
    dpin, g['w_pool'], dscale = _pool_bwd(dcat, s['mixed'], p['w_pool'].astype(BF16), p['pool_scale'])
    g['pool_scale'] = dscale.reshape(POOL_WIDTH)
    do, delta = _fox_delta(dcat, s['fox'])
    delta_row = _pairs_rows(jnp.transpose(delta[:, :FOX_HEADS]))
    dq, dk, dv, dc_keys, dc_queries = _fox_bwd(s['qn'], s['kn'], jnp.transpose(s['kn']), s['vb'], do, s['lse_row'],
                                               delta_row, s['crow'], s['ccol'])
    dft, db = _forget_cumsum_bwd(dc_keys, dc_queries, s['ft'], p['b_forget'])
    g['b_forget'] = db.reshape(FOX_HEADS)
    dfpad = jnp.pad(jnp.transpose(dft), ((0, 0), (0, LANES - FOX_HEADS)))
    dz, g['g_q_fox'], g['g_k_fox'] = _mixer_dz(s['z'], dq, dk, dv, dpin, dfpad, p['g_q_fox'], p['g_k_fox'])
    g['w_in'] = _matmul(s['xn1'], dz, ta=True, tm=512, tn=IN_COLS_PAD, tk=1024, name="in_proj_dw")
    dh, g['g_mix'] = _matmul_norm_bwd(dz, w['w_in'], s['h0'], p['g_mix'], dh, name="in_proj_dx_norm_bwd")
    return dh, g


def _local_step(x2, mem2, target2, small, full):
    h = x2
    saved = []
    for l in range(DEPTH):
        h, s = _layer_fwd(h, mem2, {k: v[l] for k, v in small.items()}, full[l])
        saved.append(s)
    loss, dh = _loss_head(h, target2)
    grads = [None] * DEPTH
    for l in reversed(range(DEPTH)):
        dh, grads[l] = _layer_bwd(dh, mem2, {k: v[l] for k, v in small.items()}, full[l], saved[l])
    return loss, dh, grads


def kernel(x, mem, g_mix, w_in, b_forget, g_q_fox, g_k_fox, w_pool, pool_scale, w_out, g_mem_q, g_mem_kv, w_mem_q, w_mem_kv, g_q_mem, g_k_mem, w_mem_out, g_ffn, w_gate_up, w_down, loss_target, m_g_mix, m_w_in, m_b_forget, m_g_q_fox, m_g_k_fox, m_w_pool, m_pool_scale, m_w_out, m_g_mem_q, m_g_mem_kv, m_w_mem_q, m_w_mem_kv, m_g_q_mem, m_g_k_mem, m_w_mem_out, m_g_ffn, m_w_gate_up, m_w_down, v_g_mix, v_w_in, v_b_forget, v_g_q_fox, v_g_k_fox, v_w_pool, v_pool_scale, v_w_out, v_g_mem_q, v_g_mem_kv, v_w_mem_q, v_w_mem_kv, v_g_q_mem, v_g_k_mem, v_w_mem_out, v_g_ffn, v_w_gate_up, v_w_down):
    weights = dict(g_mix=g_mix, w_in=w_in, b_forget=b_forget, g_q_fox=g_q_fox, g_k_fox=g_k_fox, w_pool=w_pool,
                   pool_scale=pool_scale, w_out=w_out, g_mem_q=g_mem_q, g_mem_kv=g_mem_kv, w_mem_q=w_mem_q,
                   w_mem_kv=w_mem_kv, g_q_mem=g_q_mem, g_k_mem=g_k_mem, w_mem_out=w_mem_out, g_ffn=g_ffn,
                   w_gate_up=w_gate_up, w_down=w_down)
    mom_m = dict(g_mix=m_g_mix, w_in=m_w_in, b_forget=m_b_forget, g_q_fox=m_g_q_fox, g_k_fox=m_g_k_fox, w_pool=m_w_pool,
                 pool_scale=m_pool_scale, w_out=m_w_out, g_mem_q=m_g_mem_q, g_mem_kv=m_g_mem_kv, w_mem_q=m_w_mem_q,
                 w_mem_kv=m_w_mem_kv, g_q_mem=m_g_q_mem, g_k_mem=m_g_k_mem, w_mem_out=m_w_mem_out, g_ffn=m_g_ffn,
                 w_gate_up=m_w_gate_up, w_down=m_w_down)
    mom_v = dict(g_mix=v_g_mix, w_in=v_w_in, b_forget=v_b_forget, g_q_fox=v_g_q_fox, g_k_fox=v_g_k_fox, w_pool=v_w_pool,
                 pool_scale=v_pool_scale, w_out=v_w_out, g_mem_q=v_g_mem_q, g_mem_kv=v_g_mem_kv, w_mem_q=v_w_mem_q,
                 w_mem_kv=v_w_mem_kv, g_q_mem=v_g_q_mem, g_k_mem=v_g_k_mem, w_mem_out=v_w_mem_out, g_ffn=v_g_ffn,
                 w_gate_up=v_w_gate_up, w_down=v_w_down)

    gathered = _all_gather([weights[n].astype(BF16) for n in BIG], name="weights_all_gather")
    gathered = dict(zip(BIG, gathered))
    full = [_full_weights(gathered, l) for l in range(DEPTH)]
    small = {n: weights[n] for n in SMALL}

    loss_part, grad_x, grads = _local_step(x[0], mem[0], loss_target[0], small, full)
    loss = lax.psum(loss_part[0, 0], ("x", "y", "c"))

    blocks = [_grad_blocks(n, [grads[l][n] for l in range(DEPTH)]) for n in BIG]
    landed = _exchange(blocks, name="grads_exchange")
    grad = {n: _sum_slots(landed[i], name="grad_sum_" + n) for i, n in enumerate(BIG)}
    small_part = _pack_small({n: jnp.stack([grads[l][n] for l in range(DEPTH)], 0) for n in SMALL})
    (small_all,) = _all_gather([small_part], name="small_grads_all_gather")
    small_sum = _sum_slots(small_all, name="grad_sum_small")

    delta, new_m, new_v = {}, {}, {}
    for n in BIG:
        shape = weights[n].shape
        two_d = lambda a: a.reshape(shape[0] * shape[1], shape[2])
        d, nm, nv = _adamw(two_d(weights[n]), grad[n], two_d(mom_m[n]), two_d(mom_v[n]), name="adamw_" + n)
        grad[n], delta[n], new_m[n], new_v[n] = (a.reshape(shape) for a in (grad[n], d, nm, nv))
    d, nm, nv = _adamw(_pack_small(weights), small_sum, _pack_small(mom_m), _pack_small(mom_v), name="adamw_small")
    grad.update(_unpack_small(small_sum))
    delta.update(_unpack_small(d))
    new_m.update(_unpack_small(nm))
    new_v.update(_unpack_small(nv))

    return (loss, grad_x[None], *[grad[n] for n in WEIGHTS], *[delta[n] for n in WEIGHTS],
            *[new_m[n] for n in WEIGHTS], *[new_v[n] for n in WEIGHTS])
```

```python
import functools

import numpy as np
import jax
import jax.numpy as jnp
from jax import lax
from jax.experimental import pallas as pl
from jax.experimental.pallas import tpu as pltpu

F32 = jnp.float32
BF16 = jnp.bfloat16

N_DEV = 8
D_MODEL = 1024
DEPTH = 2
FOX_HEADS = 8
FOX_HEAD_DIM = 64
FOX_WIDTH = 512
POOL_WIDTH = 512
POOL_WINDOWS = (2, 4, 8, 16)
POOL_GROUP_DIM = 128
POOL_HALO = 16
IN_COLS = 2056
IN_COLS_PAD = 2176
MEM_HEADS = 4
MEM_HEAD_DIM = 128
MEM_WIDTH = 512
D_FF = 2816
EPS = 1e-6
FOX_SCALE = FOX_HEAD_DIM ** -0.5
FOX_ACC_ROWS = FOX_HEAD_DIM + 16
MEM_SCALE = MEM_HEAD_DIM ** -0.5
LANES = 128

ADAM_LR = 0.001
ADAM_B1 = 0.9
ADAM_B2 = 0.999
ADAM_EPS = 1e-08
ADAM_WD = 0.01
ADAM_STEP = 10

VMEM_LIMIT = 56 * 1024 * 1024
MESH_ID = pl.DeviceIdType.MESH

WEIGHTS = ['g_mix', 'w_in', 'b_forget', 'g_q_fox', 'g_k_fox', 'w_pool', 'pool_scale', 'w_out', 'g_mem_q', 'g_mem_kv',
           'w_mem_q', 'w_mem_kv', 'g_q_mem', 'g_k_mem', 'w_mem_out', 'g_ffn', 'w_gate_up', 'w_down']
BIG = ['w_in', 'w_out', 'w_mem_q', 'w_mem_kv', 'w_mem_out', 'w_gate_up', 'w_down']
SMALL = [n for n in WEIGHTS if n not in BIG]
SENT_AT_MIXER = ['w_down', 'w_gate_up', 'w_mem_out', 'w_mem_q', 'w_mem_kv']


def _pcall(body, **kw):
    return pl.pallas_call(body, **kw)


def _params(*sem):
    return pltpu.CompilerParams(dimension_semantics=sem or None, vmem_limit_bytes=VMEM_LIMIT)


def _dot(a, b, dims=None):
    if dims is None:
        return jnp.dot(a, b, preferred_element_type=F32)
    return lax.dot_general(a, b, (dims, ((), ())), preferred_element_type=F32)


NT = ((1,), (1,))
TN = ((0,), (0,))


def _dot_exact(x, ones_bf16):
    hi = x.astype(BF16)
    r1 = x - hi.astype(F32)
    mid = r1.astype(BF16)
    lo = (r1 - mid.astype(F32)).astype(BF16)
    return _dot(hi, ones_bf16) + _dot(mid, ones_bf16) + _dot(lo, ones_bf16)


def _matmul(a, b, *, ta=False, tb=False, out_dtype=F32, res=None, after=None, tm=1024, tn=512, tk=None, name):
    planes = b.shape[0] if b.ndim == 3 else None
    bshape = b.shape[-2:]
    m, k = (a.shape[1], a.shape[0]) if ta else a.shape
    n = bshape[0] if tb else bshape[1]
    assert k == (bshape[1] if tb else bshape[0])
    tm, tn = min(tm, m), min(tn, n)
    tk = min(tk or k, k)
    assert m % tm == 0 and n % tn == 0 and k % tk == 0, (name, m, n, k, tm, tn, tk)
    nk = k // tk
    dims = ((0 if ta else 1,), (1 if tb else 0,))

    def body(*refs):
        a_ref, b_ref = refs[0], refs[1]
        r_ref = refs[2] if res is not None else None
        o_ref = refs[2 + (res is not None) + (after is not None)]
        part = _dot(a_ref[...].astype(BF16), b_ref[...].astype(BF16), dims)

        def finish(acc):
            if r_ref is not None:
                acc = acc + r_ref[...]
            o_ref[...] = acc.astype(o_ref.dtype)

        if nk == 1:
            finish(part)
        else:
            acc_ref = refs[-1]
            kk = pl.program_id(3)

            @pl.when(kk == 0)
            def _():
                acc_ref[...] = part

            @pl.when(kk > 0)
            def _():
                acc_ref[...] += part

            @pl.when(kk == nk - 1)
            def _():
                finish(acc_ref[...])

    lead = (lambda p: (p,)) if planes else (lambda p: ())
    sq = (None,) if planes else ()
    a_spec = pl.BlockSpec((tk, tm), lambda p, i, j, kk: (kk, i)) if ta else pl.BlockSpec((tm, tk), lambda p, i, j, kk: (i, kk))
    b_spec = (pl.BlockSpec(sq + (tn, tk), lambda p, i, j, kk: lead(p) + (j, kk)) if tb
              else pl.BlockSpec(sq + (tk, tn), lambda p, i, j, kk: lead(p) + (kk, j)))
    o_spec = pl.BlockSpec(sq + (tm, tn), lambda p, i, j, kk: lead(p) + (i, j))
    in_specs = ([a_spec, b_spec] + ([o_spec] if res is not None else [])
                + ([pl.BlockSpec(memory_space=pl.ANY)] if after is not None else []))
    args = (a, b) + ((res,) if res is not None else ()) + ((after,) if after is not None else ())
    return _pcall(
        body, name=name, grid=(planes or 1, m // tm, n // tn, nk), in_specs=in_specs, out_specs=o_spec,
        out_shape=jax.ShapeDtypeStruct(((planes,) if planes else ()) + (m, n), out_dtype),
        scratch_shapes=[pltpu.VMEM((tm, tn), F32)] if nk > 1 else [],
        compiler_params=_params("parallel", "parallel", "parallel", "arbitrary"),
    )(*args)


def _rms(x):
    return lax.rsqrt(jnp.mean(x * x, axis=-1, keepdims=True) + EPS)


def _norm_matmul(h, g, w, *, tm=512, tn=512, tail=0, name):
    t, d = h.shape
    n = w.shape[1]
    tn = min(tn, n)
    assert t % tm == 0 and n % tn == 0 and (not tail or tn == n)

    def body(h_ref, g_ref, w_ref, xn_ref, y_ref, *tail_ref):
        @pl.when(pl.program_id(1) == 0)
        def _():
            x = h_ref[...]
            xn_ref[...] = (x * _rms(x) * g_ref[...]).astype(BF16)

        y = _dot(xn_ref[...], w_ref[...])
        if tail:
            y_ref[...] = y[:, :n - tail]
            tail_ref[0][...] = y[:, n - tail:]
        else:
            y_ref[...] = y

    row = pl.BlockSpec((tm, d), lambda i, j: (i, 0))
    tails = ([pl.BlockSpec((tm, tail), lambda i, j: (i, 0))], [jax.ShapeDtypeStruct((t, tail), F32)]) if tail else ([], [])
    return _pcall(
        body, name=name, grid=(t // tm, n // tn),
        in_specs=[row, pl.BlockSpec((1, d), lambda i, j: (0, 0)), pl.BlockSpec((d, tn), lambda i, j: (0, j))],
        out_specs=[row, pl.BlockSpec((tm, tn - tail), lambda i, j: (i, j))] + tails[0],
        out_shape=[jax.ShapeDtypeStruct((t, d), BF16), jax.ShapeDtypeStruct((t, n - tail), F32)] + tails[1],
        compiler_params=_params("parallel", "arbitrary"),
    )(h, g.reshape(1, d), w)


def _matmul_norm_bwd(a, w, h, g, dres, *, tm=512, tk=None, name):
    stacked = a.ndim == 3
    t = a.shape[-2]
    d, k = w.shape
    tk = a.shape[-1] if stacked else min(tk or k, k)
    nk = k // tk
    assert t % tm == 0 and k % tk == 0 and (not stacked or a.shape[0] == nk)

    def body(a_ref, w_ref, h_ref, g_ref, r_ref, dx_ref, dg_ref, *acc):
        i, kk = pl.program_id(0), pl.program_id(1)
        part = _dot(a_ref[...], w_ref[...], NT)

        def finish(dy):
            x = h_ref[...]
            r = _rms(x)
            xhat = x * r
            gy = dy * g_ref[...]
            dx_ref[...] = r_ref[...] + r * (gy - xhat * jnp.mean(gy * xhat, axis=-1, keepdims=True))
            dg_part = jnp.sum(dy * xhat, axis=0, keepdims=True)

            @pl.when(i == 0)
            def _():
                dg_ref[...] = dg_part

            @pl.when(i > 0)
            def _():
                dg_ref[...] += dg_part

        if nk == 1:
            finish(part)
        else:
            acc_ref = acc[0]

            @pl.when(kk == 0)
            def _():
                acc_ref[...] = part

            @pl.when(kk > 0)
            def _():
                acc_ref[...] += part

            @pl.when(kk == nk - 1)
            def _():
                finish(acc_ref[...])

    a_spec = (pl.BlockSpec((None, tm, tk), lambda i, kk: (kk, i, 0)) if stacked
              else pl.BlockSpec((tm, tk), lambda i, kk: (i, kk)))
    row = pl.BlockSpec((tm, d), lambda i, kk: (i, 0))
    vec = pl.BlockSpec((1, d), lambda i, kk: (0, 0))
    dx, dg = _pcall(
        body, name=name, grid=(t // tm, nk),
        in_specs=[a_spec, pl.BlockSpec((d, tk), lambda i, kk: (0, kk)), row, vec, row], out_specs=[row, vec],
        out_shape=[jax.ShapeDtypeStruct((t, d), F32), jax.ShapeDtypeStruct((1, d), F32)],
        scratch_shapes=[pltpu.VMEM((tm, d), F32)] if nk > 1 else [],
        compiler_params=_params("arbitrary", "arbitrary"),
    )(a, w, h, g.reshape(1, d), dres)
    return dx, dg.reshape(d)


def _norm_gate_up_swiglu(h, g, w, *, tm=512, tn=1408):
    t, d = h.shape
    nj = D_FF // tn
    assert t % tm == 0 and D_FF % tn == 0

    def body(h_ref, g_ref, wg_ref, wu_ref, hn_ref, gu_ref, act_ref):
        @pl.when(pl.program_id(1) == 0)
        def _():
            x = h_ref[...]
            hn_ref[...] = (x * _rms(x) * g_ref[...]).astype(BF16)

        hn = hn_ref[...]
        gate = _dot(hn, wg_ref[...])
        up = _dot(hn, wu_ref[...])
        gu_ref[0] = gate
        gu_ref[1] = up
        act_ref[...] = (gate * jax.nn.sigmoid(gate) * up).astype(BF16)

    row = pl.BlockSpec((tm, d), lambda i, j: (i, 0))
    return _pcall(
        body, name="gate_up_swiglu_fwd", grid=(t // tm, nj),
        in_specs=[row, pl.BlockSpec((1, d), lambda i, j: (0, 0)), pl.BlockSpec((d, tn), lambda i, j: (0, j)),
                  pl.BlockSpec((d, tn), lambda i, j: (0, nj + j))],
        out_specs=[row, pl.BlockSpec((2, tm, tn), lambda i, j: (0, i, j)), pl.BlockSpec((tm, tn), lambda i, j: (i, j))],
        out_shape=[jax.ShapeDtypeStruct((t, d), BF16), jax.ShapeDtypeStruct((2, t, D_FF), F32),
                   jax.ShapeDtypeStruct((t, D_FF), BF16)],
        compiler_params=_params("parallel", "arbitrary"),
    )(h, g.reshape(1, d), w, w)


def _down_dx_swiglu_bwd(dh, w_down, gu, *, tm=512, tn=1408):
    t, d = dh.shape
    assert t % tm == 0 and D_FF % tn == 0

    def body(dh_ref, w_ref, gu_ref, dgu_ref):
        da = _dot(dh_ref[...].astype(BF16), w_ref[...], NT)
        gate, up = gu_ref[0], gu_ref[1]
        sg = jax.nn.sigmoid(gate)
        silu = gate * sg
        dgu_ref[0] = (da * up * (sg + silu * (1.0 - sg))).astype(BF16)
        dgu_ref[1] = (da * silu).astype(BF16)

    stack = pl.BlockSpec((2, tm, tn), lambda i, j: (0, i, j))
    return _pcall(
        body, name="down_dx_swiglu_bwd", grid=(t // tm, D_FF // tn),
        in_specs=[pl.BlockSpec((tm, d), lambda i, j: (i, 0)), pl.BlockSpec((tn, d), lambda i, j: (j, 0)), stack],
        out_specs=stack, out_shape=jax.ShapeDtypeStruct((2, t, D_FF), BF16),
        compiler_params=_params("parallel", "parallel"),
    )(dh, w_down, gu)


def _group_matrix(width, group):
    r = lax.broadcasted_iota(jnp.int32, (width, width), 0) // group
    c = lax.broadcasted_iota(jnp.int32, (width, width), 1) // group
    return (r == c).astype(BF16)


def _qkv_prep(z, gq, gk, *, bt=512):
    t = z.shape[0]
    w = FOX_WIDTH

    def body(q_ref, k_ref, v_ref, gq_ref, gk_ref, qo_ref, ko_ref, vo_ref):
        gm = _group_matrix(w, FOX_HEAD_DIM)
        for x_ref, g_ref, o_ref, scale in ((q_ref, gq_ref, qo_ref, FOX_SCALE), (k_ref, gk_ref, ko_ref, 1.0)):
            x = x_ref[...]
            ms = _dot_exact(x * x, gm) * (1.0 / FOX_HEAD_DIM)
            y = x * lax.rsqrt(ms + EPS) * g_ref[...]
            o_ref[...] = (y * scale).astype(BF16)
        vo_ref[...] = v_ref[...].astype(BF16)

    col = lambda c: pl.BlockSpec((bt, w), lambda i, c=c: (i, c))
    vec = pl.BlockSpec((1, w), lambda i: (0, 0))
    out = pl.BlockSpec((bt, w), lambda i: (i, 0))
    return _pcall(
        body, name="fox_qkv_prep", grid=(t // bt,), in_specs=[col(0), col(1), col(2), vec, vec], out_specs=[out, out, out],
        out_shape=[jax.ShapeDtypeStruct((t, w), BF16)] * 3, compiler_params=_params("parallel"),
    )(z, z, z, jnp.tile(gq, FOX_HEADS).reshape(1, w), jnp.tile(gk, FOX_HEADS).reshape(1, w))


def _log_sigmoid(f):
    return jnp.minimum(f, 0.0) - jnp.log1p(jnp.exp(-jnp.abs(f)))


def _forget_cumsum(ft, b, *, chunk=512):
    hh, t = ft.shape

    def body(f_ref, b_ref, c_ref):
        r = lax.broadcasted_iota(jnp.int32, (chunk, chunk), 0)
        c = lax.broadcasted_iota(jnp.int32, (chunk, chunk), 1)
        upper = (r <= c).astype(BF16)
        carry = jnp.zeros((hh, 1), F32)
        for ch in range(t // chunk):
            sl = slice(ch * chunk, (ch + 1) * chunk)
            cs = _dot_exact(_log_sigmoid(f_ref[:, sl] + b_ref[...]), upper) + carry
            c_ref[:, sl] = cs
            carry = cs[:, chunk - 1:chunk]

    return _pcall(body, name="fox_forget_cumsum", out_shape=jax.ShapeDtypeStruct((hh, t), F32),
                  compiler_params=_params())(ft, b.reshape(hh, 1))


def _forget_cumsum_bwd(dc_keys, dc_queries, ft, b, *, chunk=512):
    hh, t = ft.shape

    def body(dck_ref, dcq_ref, f_ref, b_ref, df_ref, db_ref):
        r = lax.broadcasted_iota(jnp.int32, (chunk, chunk), 0)
        c = lax.broadcasted_iota(jnp.int32, (chunk, chunk), 1)
        lower = (r >= c).astype(BF16)
        carry = jnp.zeros((hh, 1), F32)
        db = jnp.zeros((hh, 1), F32)
        for ch in reversed(range(t // chunk)):
            sl = slice(ch * chunk, (ch + 1) * chunk)
            dls = _dot_exact(dck_ref[:, sl] + dcq_ref[:, sl], lower) + carry
            carry = dls[:, 0:1]
            df = dls * jax.nn.sigmoid(-(f_ref[:, sl] + b_ref[...]))
            df_ref[:, sl] = df
            db = db + jnp.sum(df, axis=1, keepdims=True)
        db_ref[...] = db

    return _pcall(body, name="fox_forget_cumsum_bwd",
                  out_shape=[jax.ShapeDtypeStruct((hh, t), F32), jax.ShapeDtypeStruct((hh, 1), F32)],
                  compiler_params=_params())(dc_keys, dc_queries, ft, b.reshape(hh, 1))


def _lane_is_first_head():
    return lax.broadcasted_iota(jnp.int32, (1, LANES), 1) < FOX_HEAD_DIM


def _fox_fwd(qnt, kn, vt, crow, ccol, *, bq=512):
    t = kn.shape[0]
    nq = t // bq
    pairs = FOX_WIDTH // LANES
    tiles = [(i, j) for i in range(nq) for j in range(i, -1, -1)]
    it = jnp.asarray(np.array([a for a, _ in tiles], np.int32))
    jt = jnp.asarray(np.array([b for _, b in tiles], np.int32))

    def body(it_ref, jt_ref, qt_ref, k_ref, vt_ref, cr_ref, cc_ref, o_ref, lse_ref, m_sc, acc_sc):
        s_id = pl.program_id(1)
        i, j = it_ref[s_id], jt_ref[s_id]
        first = _lane_is_first_head()

        @pl.when(j == i)
        def _():
            m_sc[...] = jnp.full(m_sc.shape, -jnp.inf, F32)
            acc_sc[...] = jnp.zeros(acc_sc.shape, F32)

        def scores():
            k2, qt2 = k_ref[...], qt_ref[...]
            return [_dot(jnp.where(first if hh == 0 else jnp.logical_not(first), k2, jnp.zeros_like(k2)), qt2)
                    for hh in range(2)]

        def pv(hh, pt_bf16):
            v_ones = jnp.concatenate([vt_ref[hh * FOX_HEAD_DIM:(hh + 1) * FOX_HEAD_DIM, :],
                                      jnp.ones((FOX_ACC_ROWS - FOX_HEAD_DIM, bq), BF16)], axis=0)
            return _dot(v_ones, pt_bf16)

        def tile(diagonal):
            sc = scores()
            for hh in range(2):
                ut = sc[hh] - cc_ref[0, :, hh:hh + 1]
                if diagonal:
                    rr = lax.broadcasted_iota(jnp.int32, ut.shape, 0)
                    cc = lax.broadcasted_iota(jnp.int32, ut.shape, 1)
                    ut = jnp.where(rr <= cc, ut, -jnp.inf)
                c_t = cr_ref[0, hh:hh + 1, :]
                m_prev = m_sc[hh]
                m_new = jnp.maximum(m_prev, jnp.max(ut, axis=0, keepdims=True) + c_t)
                acc_sc[hh] = jnp.exp(m_prev - m_new) * acc_sc[hh] + pv(hh, jnp.exp(ut + (c_t - m_new)).astype(BF16))
                m_sc[hh] = m_new

        @pl.when(j == i)
        def _():
            tile(True)

        @pl.when(j < i)
        def _():
            tile(False)

        @pl.when(j == 0)
        def _():
            sums = [acc_sc[hh, FOX_HEAD_DIM:FOX_HEAD_DIM + 1, :] for hh in range(2)]
            ot = jnp.concatenate([acc_sc[hh, :FOX_HEAD_DIM, :] / sums[hh] for hh in range(2)], axis=0)
            o_ref[...] = jnp.transpose(ot).astype(o_ref.dtype)
            for hh in range(2):
                lse_ref[0, hh:hh + 1, :] = m_sc[hh] + jnp.log(sums[hh])

    qspec = pl.BlockSpec((bq, LANES), lambda p, s, it, jt: (it[s], p))
    kspec = pl.BlockSpec((bq, LANES), lambda p, s, it, jt: (jt[s], p))
    vtspec = pl.BlockSpec((LANES, bq), lambda p, s, it, jt: (p, jt[s]))
    qtspec = pl.BlockSpec((LANES, bq), lambda p, s, it, jt: (p, it[s]))
    rowq = pl.BlockSpec((1, 2, bq), lambda p, s, it, jt: (p, 0, it[s]))
    colk = pl.BlockSpec((1, bq, 2), lambda p, s, it, jt: (p, jt[s], 0))
    return _pcall(
        body, name="fox_attention_fwd",
        grid_spec=pltpu.PrefetchScalarGridSpec(
            num_scalar_prefetch=2, grid=(pairs, len(tiles)),
            in_specs=[qtspec, kspec, vtspec, rowq, colk], out_specs=[qspec, rowq],
            scratch_shapes=[pltpu.VMEM((2, 1, bq), F32), pltpu.VMEM((2, FOX_ACC_ROWS, bq), F32)]),
        out_shape=[jax.ShapeDtypeStruct((t, FOX_WIDTH), BF16), jax.ShapeDtypeStruct((pairs, 2, t), F32)],
        compiler_params=_params("parallel", "arbitrary"),
    )(it, jt, qnt, kn, vt, crow, ccol)


def _fox_bwd(qn, kn, knt, vb, do, lse_row, delta_row, crow, ccol, *, bq=512):
    t = qn.shape[0]
    nq = t // bq
    pairs = FOX_WIDTH // LANES
    tiles = [(i, j) for j in range(nq) for i in range(j, nq)]
    it = jnp.asarray(np.array([a for a, _ in tiles], np.int32))
    jt = jnp.asarray(np.array([b for _, b in tiles], np.int32))

    def body(it_ref, jt_ref, q_ref, k_ref, kt_ref, v_ref, do_ref, lse_ref, dl_ref, cr_ref, cc_ref,
             dqt_ref, dk_ref, dv_ref, dc_ref, dr_ref, dk_sc, dv_sc, dc_sc):
        s_id = pl.program_id(1)
        i, j = it_ref[s_id], jt_ref[s_id]
        first = _lane_is_first_head()

        @pl.when(s_id == 0)
        def _():
            dqt_ref[...] = jnp.zeros(dqt_ref.shape, F32)
            dr_ref[...] = jnp.zeros(dr_ref.shape, F32)

        @pl.when(i == j)
        def _():
            dk_sc[...] = jnp.zeros(dk_sc.shape, F32)
            dv_sc[...] = jnp.zeros(dv_sc.shape, F32)
            dc_sc[...] = jnp.zeros(dc_sc.shape, F32)

        def tile(diagonal):
            q2, k2, kt2, v2, do2 = q_ref[...], k_ref[...], kt_ref[...], v_ref[...], do_ref[...]
            dk_t, dv_t, dqt_t = [], [], []
            for hh in range(2):
                mine = first if hh == 0 else jnp.logical_not(first)
                kh = jnp.where(mine, k2, jnp.zeros_like(k2))
                vh = jnp.where(mine, v2, jnp.zeros_like(v2))
                kth = kt2[hh * FOX_HEAD_DIM:(hh + 1) * FOX_HEAD_DIM]
                st = _dot(kh, q2, NT) + (cr_ref[0, hh:hh + 1, :] - cc_ref[0, :, hh:hh + 1])
                if diagonal:
                    rr = lax.broadcasted_iota(jnp.int32, st.shape, 0)
                    cc = lax.broadcasted_iota(jnp.int32, st.shape, 1)
                    st = jnp.where(rr <= cc, st, -jnp.inf)
                pt = jnp.exp(st - lse_ref[0, hh:hh + 1, :])
                dv_t.append(_dot(pt.astype(BF16), do2))
                dpt = _dot(vh, do2, NT)
                dst = pt * (dpt - dl_ref[0, hh:hh + 1, :])
                dc_sc[hh] += jnp.sum(dst, axis=1, keepdims=True)
                dr_ref[0, i, hh:hh + 1, :] += jnp.sum(dst, axis=0, keepdims=True)
                dsb = dst.astype(BF16)
                dk_t.append(_dot(dsb, q2))
                dqt_t.append(_dot(kth, dsb))
            dk_sc[...] += jnp.where(first, dk_t[0], dk_t[1])
            dv_sc[...] += jnp.where(first, dv_t[0], dv_t[1])
            dqt_ref[0, i] += jnp.concatenate(dqt_t, axis=0)

        @pl.when(i > j)
        def _():
            tile(False)

        @pl.when(i == j)
        def _():
            tile(True)

        @pl.when(i == nq - 1)
        def _():
            dk_ref[...] = dk_sc[...]
            dv_ref[...] = dv_sc[...]
            for hh in range(2):
                dc_ref[0, :, hh:hh + 1] = -dc_sc[hh]

    qspec = pl.BlockSpec((bq, LANES), lambda p, s, it, jt: (it[s], p))
    kspec = pl.BlockSpec((bq, LANES), lambda p, s, it, jt: (jt[s], p))
    ktspec = pl.BlockSpec((LANES, bq), lambda p, s, it, jt: (p, jt[s]))
    rowq = pl.BlockSpec((1, 2, bq), lambda p, s, it, jt: (p, 0, it[s]))
    colk = pl.BlockSpec((1, bq, 2), lambda p, s, it, jt: (p, jt[s], 0))
    dqt_spec = pl.BlockSpec((1, nq, LANES, bq), lambda p, s, it, jt: (p, 0, 0, 0))
    dr_spec = pl.BlockSpec((1, nq, 2, bq), lambda p, s, it, jt: (p, 0, 0, 0))
    dqt, dk, dv, dc_keys, dc_queries = _pcall(
        body, name="fox_attention_bwd",
        grid_spec=pltpu.PrefetchScalarGridSpec(
            num_scalar_prefetch=2, grid=(pairs, len(tiles)),
            in_specs=[qspec, kspec, ktspec, kspec, qspec, rowq, rowq, rowq, colk],
            out_specs=[dqt_spec, kspec, kspec, colk, dr_spec],
            scratch_shapes=[pltpu.VMEM((bq, LANES), F32), pltpu.VMEM((bq, LANES), F32), pltpu.VMEM((2, bq, 1), F32)]),
        out_shape=[jax.ShapeDtypeStruct((pairs, nq, LANES, bq), F32), jax.ShapeDtypeStruct((t, FOX_WIDTH), F32),
                   jax.ShapeDtypeStruct((t, FOX_WIDTH), F32), jax.ShapeDtypeStruct((pairs, t, 2), F32),
                   jax.ShapeDtypeStruct((pairs, nq, 2, bq), F32)],
        compiler_params=_params("parallel", "arbitrary"),
    )(it, jt, qn, kn, knt, vb, do, lse_row, delta_row, crow, ccol)
    dq = jnp.transpose(dqt, (1, 3, 0, 2)).reshape(t, FOX_WIDTH)
    dc_keys = jnp.transpose(dc_keys, (0, 2, 1)).reshape(FOX_HEADS, t)
    dc_queries = jnp.transpose(dc_queries, (0, 2, 1, 3)).reshape(FOX_HEADS, t)
    return dq, dk, dv, dc_keys, dc_queries


def _fox_delta(dcat, fox, *, bt=512):
    t = fox.shape[0]
    w = FOX_WIDTH

    def body(do_ref, o_ref, dob_ref, dl_ref):
        dob = do_ref[...].astype(BF16)
        r = lax.broadcasted_iota(jnp.int32, (w, LANES), 0) // FOX_HEAD_DIM
        c = lax.broadcasted_iota(jnp.int32, (w, LANES), 1)
        dl_ref[...] = _dot_exact(dob.astype(F32) * o_ref[...].astype(F32), (r == c).astype(BF16))
        dob_ref[...] = dob

    blk = pl.BlockSpec((bt, w), lambda i: (i, 0))
    return _pcall(
        body, name="fox_delta", grid=(t // bt,), in_specs=[blk, blk],
        out_specs=[blk, pl.BlockSpec((bt, LANES), lambda i: (i, 0))],
        out_shape=[jax.ShapeDtypeStruct((t, w), BF16), jax.ShapeDtypeStruct((t, LANES), F32)],
        compiler_params=_params("parallel"),
    )(dcat, fox)


def _mixer_dz(z, dq, dk, dv, dpin, dfpad, gq, gk, *, bt=256):
    t = z.shape[0]
    w = FOX_WIDTH

    def body(q_ref, k_ref, dq_ref, dk_ref, dv_ref, dp_ref, df_ref, gq_ref, gk_ref, dz_ref, dgq_ref, dgk_ref):
        gm = _group_matrix(w, FOX_HEAD_DIM)
        first_step = pl.program_id(0) == 0
        for n, (x_ref, dy_ref, g_ref, dg_ref, scale) in enumerate(
                ((q_ref, dq_ref, gq_ref, dgq_ref, FOX_SCALE), (k_ref, dk_ref, gk_ref, dgk_ref, 1.0))):
            x = x_ref[...]
            r = lax.rsqrt(_dot_exact(x * x, gm) * (1.0 / FOX_HEAD_DIM) + EPS)
            xhat = x * r
            dy = dy_ref[...] * scale
            gy = dy * g_ref[...]
            dx = r * (gy - xhat * (_dot_exact(gy * xhat, gm) * (1.0 / FOX_HEAD_DIM)))
            dz_ref[:, n * w:(n + 1) * w] = dx.astype(BF16)
            part = jnp.sum(dy * xhat, axis=0, keepdims=True)

            @pl.when(first_step)
            def _():
                dg_ref[...] = part

            @pl.when(jnp.logical_not(first_step))
            def _():
                dg_ref[...] += part

        dz_ref[:, 2 * w:3 * w] = dv_ref[...].astype(BF16)
        dz_ref[:, 3 * w:4 * w] = dp_ref[...].astype(BF16)
        dz_ref[:, 4 * w:] = df_ref[...].astype(BF16)

    col = lambda c: pl.BlockSpec((bt, w), lambda i, c=c: (i, c))
    blk = pl.BlockSpec((bt, w), lambda i: (i, 0))
    vec = pl.BlockSpec((1, w), lambda i: (0, 0))
    dz, dgq, dgk = _pcall(
        body, name="mixer_dz", grid=(t // bt,),
        in_specs=[col(0), col(1), blk, blk, blk, blk, pl.BlockSpec((bt, LANES), lambda i: (i, 0)), vec, vec],
        out_specs=[pl.BlockSpec((bt, IN_COLS_PAD), lambda i: (i, 0)), vec, vec],
        out_shape=[jax.ShapeDtypeStruct((t, IN_COLS_PAD), BF16), jax.ShapeDtypeStruct((1, w), F32),
                   jax.ShapeDtypeStruct((1, w), F32)],
        compiler_params=_params("arbitrary"),
    )(z, z, dq, dk, dv, dpin, dfpad, jnp.tile(gq, FOX_HEADS).reshape(1, w), jnp.tile(gk, FOX_HEADS).reshape(1, w))
    return dz, dgq.reshape(FOX_HEADS, FOX_HEAD_DIM).sum(0), dgk.reshape(FOX_HEADS, FOX_HEAD_DIM).sum(0)


def _pool_fwd(z, wp, scale, *, bt=512):
    t = z.shape[0]
    w = POOL_WIDTH
    hb = bt // POOL_HALO

    def body(p_ref, h_ref, wp_ref, sc_ref, y_ref, mx_ref):
        i = pl.program_id(0)
        cur = p_ref[...]
        halo = jnp.where(i > 0, h_ref[...], 0.0)
        ext = jnp.concatenate([halo, cur], axis=0)
        trow = i * bt + lax.broadcasted_iota(jnp.int32, (bt, 1), 0)
        for g, win in enumerate(POOL_WINDOWS):
            sl = slice(g * LANES, (g + 1) * LANES)
            e = ext[:, sl]
            acc = e[POOL_HALO:]
            for k in range(1, win):
                acc = acc + pltpu.roll(e, k, 0)[POOL_HALO:]
            cnt = jnp.minimum(trow + 1, win).astype(F32)
            mixed = (acc / cnt - cur[:, sl]).astype(BF16)
            mx_ref[:, sl] = mixed
            y_ref[:, sl] = (_dot(mixed, wp_ref[g]) * sc_ref[:, sl]).astype(BF16)

    blk = pl.BlockSpec((bt, w), lambda i: (i, 0))
    return _pcall(
        body, name="pool_fwd", grid=(t // bt,),
        in_specs=[pl.BlockSpec((bt, w), lambda i: (i, 3)),
                  pl.BlockSpec((POOL_HALO, w), lambda i: (jnp.maximum(i * hb - 1, 0), 3)),
                  pl.BlockSpec((len(POOL_WINDOWS), LANES, LANES), lambda i: (0, 0, 0)),
                  pl.BlockSpec((1, w), lambda i: (0, 0))],
        out_specs=[blk, blk], out_shape=[jax.ShapeDtypeStruct((t, w), BF16)] * 2,
        compiler_params=_params("parallel"),
    )(z, z, wp, scale.reshape(1, w))


def _pool_bwd(dcat, mixed, wp, scale, *, bt=512):
    t = mixed.shape[0]
    w = POOL_WIDTH
    hb = bt // POOL_HALO
    nb = t // bt
    n_ext = bt + POOL_HALO

    def body(d_ref, h_ref, mx_ref, wp_ref, sc_ref, dp_ref, dwp_ref, dsc_ref):
        i = pl.program_id(0)
        cur = d_ref[...]
        nxt = jnp.where(i < nb - 1, h_ref[...], 0.0)
        ext = jnp.concatenate([cur, nxt], axis=0)
        trow = i * bt + lax.broadcasted_iota(jnp.int32, (n_ext, 1), 0)

        @pl.when(i == 0)
        def _():
            dwp_ref[...] = jnp.zeros(dwp_ref.shape, F32)
            dsc_ref[...] = jnp.zeros(dsc_ref.shape, F32)

        for g, win in enumerate(POOL_WINDOWS):
            sl = slice(g * LANES, (g + 1) * LANES)
            dy = (ext[:, sl] * sc_ref[:, sl]).astype(BF16)
            dm = _dot(dy, wp_ref[g], NT)
            mixed_g = mx_ref[:, sl]
            dsc_ref[:, sl] += jnp.sum(cur[:, sl] * _dot(mixed_g, wp_ref[g]), axis=0, keepdims=True)
            dwp_ref[g] += _dot(mixed_g, dy[:bt], TN)
            r = dm / jnp.minimum(trow + 1, win).astype(F32)
            acc = r[:bt]
            for k in range(1, win):
                acc = acc + pltpu.roll(r, n_ext - k, 0)[:bt]
            dp_ref[:, sl] = acc - dm[:bt]

    return _pcall(
        body, name="pool_bwd", grid=(nb,),
        in_specs=[pl.BlockSpec((bt, w), lambda i: (i, 1)),
                  pl.BlockSpec((POOL_HALO, w), lambda i: (jnp.minimum((i + 1) * hb, t // POOL_HALO - 1), 1)),
                  pl.BlockSpec((bt, w), lambda i: (i, 0)),
                  pl.BlockSpec((len(POOL_WINDOWS), LANES, LANES), lambda i: (0, 0, 0)),
                  pl.BlockSpec((1, w), lambda i: (0, 0))],
        out_specs=[pl.BlockSpec((bt, w), lambda i: (i, 0)),
                   pl.BlockSpec((len(POOL_WINDOWS), LANES, LANES), lambda i: (0, 0, 0)),
                   pl.BlockSpec((1, w), lambda i: (0, 0))],
        out_shape=[jax.ShapeDtypeStruct((t, w), F32), jax.ShapeDtypeStruct((len(POOL_WINDOWS), LANES, LANES), F32),
                   jax.ShapeDtypeStruct((1, w), F32)],
        compiler_params=_params("arbitrary"),
    )(dcat, dcat, mixed, wp, scale.reshape(1, w))


def _head_rms(x):
    return lax.rsqrt(jnp.mean(x * x, axis=-1, keepdims=True) + EPS)


def _mem_kv_fwd(mem, g_kv, w_kv, g_k):
    mlen, d = mem.shape

    def body(m_ref, g_ref, w_ref, gk_ref, mn_ref, mkv_ref, mk_ref, mv_ref):
        x = m_ref[...]
        mn = (x * lax.rsqrt(jnp.mean(x * x, axis=-1, keepdims=True) + EPS) * g_ref[...]).astype(BF16)
        mn_ref[...] = mn
        mkv = _dot(mn, w_ref[...])
        mkv_ref[...] = mkv
        for h in range(MEM_HEADS):
            sl = slice(h * MEM_HEAD_DIM, (h + 1) * MEM_HEAD_DIM)
            kh = mkv[:, sl]
            mk_ref[:, sl] = (kh * _head_rms(kh) * gk_ref[...]).astype(BF16)
        mv_ref[...] = mkv[:, MEM_WIDTH:].astype(BF16)

    return _pcall(
        body, name="mem_kv_fwd",
        out_shape=[jax.ShapeDtypeStruct((mlen, d), BF16), jax.ShapeDtypeStruct((mlen, 2 * MEM_WIDTH), F32),
                   jax.ShapeDtypeStruct((mlen, MEM_WIDTH), BF16), jax.ShapeDtypeStruct((mlen, MEM_WIDTH), BF16)],
        compiler_params=_params(),
    )(mem, g_kv.reshape(1, d), w_kv, g_k.reshape(1, MEM_HEAD_DIM))


def _mem_kv_bwd(dmk, dmv, mkv, mn, mem, g_kv, w_kv, g_k):
    mlen, d = mem.shape

    def body(dmk_ref, dmv_ref, mkv_ref, mn_ref, m_ref, g_ref, w_ref, gk_ref, dw_ref, dg_ref, dgk_ref, dkv_sc):
        dgk = jnp.zeros((1, MEM_HEAD_DIM), F32)
        for h in range(MEM_HEADS):
            sl = slice(h * MEM_HEAD_DIM, (h + 1) * MEM_HEAD_DIM)
            x = mkv_ref[:, sl]
            r = _head_rms(x)
            xhat = x * r
            dy = dmk_ref[:, sl]
            gy = dy * gk_ref[...]
            dkv_sc[:, sl] = (r * (gy - xhat * jnp.mean(gy * xhat, axis=-1, keepdims=True))).astype(BF16)
            dgk = dgk + jnp.sum(dy * xhat, axis=0, keepdims=True)
        dgk_ref[...] = dgk
        dkv_sc[:, MEM_WIDTH:] = dmv_ref[...].astype(BF16)
        dkv = dkv_sc[...]
        dw_ref[...] = _dot(mn_ref[...], dkv, TN)
        dmn = _dot(dkv, w_ref[...], NT)
        x = m_ref[...]
        xhat = x * lax.rsqrt(jnp.mean(x * x, axis=-1, keepdims=True) + EPS)
        dg_ref[...] = jnp.sum(dmn * xhat, axis=0, keepdims=True)

    dw, dg, dgk = _pcall(
        body, name="mem_kv_bwd",
        out_shape=[jax.ShapeDtypeStruct((d, 2 * MEM_WIDTH), F32), jax.ShapeDtypeStruct((1, d), F32),
                   jax.ShapeDtypeStruct((1, MEM_HEAD_DIM), F32)],
        scratch_shapes=[pltpu.VMEM((mlen, 2 * MEM_WIDTH), BF16)],
        compiler_params=_params(),
    )(dmk, dmv, mkv, mn, mem, g_kv.reshape(1, d), w_kv, g_k.reshape(1, MEM_HEAD_DIM))
    return dw, dg.reshape(d), dgk.reshape(MEM_HEAD_DIM)


def _cross_probs(x, g, mk_h):
    r = _head_rms(x)
    xhat = x * r
    qn = (xhat * g).astype(BF16)
    s = _dot(qn, mk_h, NT) * MEM_SCALE
    e = jnp.exp(s - jnp.max(s, axis=-1, keepdims=True))
    return r, xhat, qn, e / jnp.sum(e, axis=-1, keepdims=True)


def _cross_fwd(mq_raw, g_q, mk, mv, *, bt=512):
    t = mq_raw.shape[0]
    mlen = mk.shape[0]

    def body(x_ref, g_ref, mk_ref, mv_ref, o_ref):
        for h in range(MEM_HEADS):
            sl = slice(h * MEM_HEAD_DIM, (h + 1) * MEM_HEAD_DIM)
            _, _, _, p = _cross_probs(x_ref[:, sl], g_ref[...], mk_ref[:, sl])
            o_ref[:, sl] = _dot(p.astype(BF16), mv_ref[:, sl]).astype(BF16)

    blk = pl.BlockSpec((bt, MEM_WIDTH), lambda i: (i, 0))
    kv = pl.BlockSpec((mlen, MEM_WIDTH), lambda i: (0, 0))
    return _pcall(
        body, name="cross_attention_fwd", grid=(t // bt,),
        in_specs=[blk, pl.BlockSpec((1, MEM_HEAD_DIM), lambda i: (0, 0)), kv, kv], out_specs=blk,
        out_shape=jax.ShapeDtypeStruct((t, MEM_WIDTH), BF16), compiler_params=_params("parallel"),
    )(mq_raw, g_q.reshape(1, MEM_HEAD_DIM), mk, mv)


def _cross_bwd(mq_raw, dmo, g_q, mk, mv, *, bt=512):
    t = mq_raw.shape[0]
    mlen = mk.shape[0]

    def body(x_ref, do_ref, g_ref, mk_ref, mv_ref, dx_ref, dmk_ref, dmv_ref, dg_ref):
        @pl.when(pl.program_id(0) == 0)
        def _():
            dmk_ref[...] = jnp.zeros(dmk_ref.shape, F32)
            dmv_ref[...] = jnp.zeros(dmv_ref.shape, F32)
            dg_ref[...] = jnp.zeros(dg_ref.shape, F32)

        for h in range(MEM_HEADS):
            sl = slice(h * MEM_HEAD_DIM, (h + 1) * MEM_HEAD_DIM)
            r, xhat, qn, p = _cross_probs(x_ref[:, sl], g_ref[...], mk_ref[:, sl])
            do = do_ref[:, sl]
            dp = _dot(do, mv_ref[:, sl], NT)
            ds = (p * (dp - jnp.sum(p * dp, axis=-1, keepdims=True)) * MEM_SCALE).astype(BF16)
            dmv_ref[:, sl] += _dot(p.astype(BF16), do, TN)
            dmk_ref[:, sl] += _dot(ds, qn, TN)
            dqn = _dot(ds, mk_ref[:, sl])
            gy = dqn * g_ref[...]
            dx_ref[:, sl] = (r * (gy - xhat * jnp.mean(gy * xhat, axis=-1, keepdims=True))).astype(BF16)
            dg_ref[...] += jnp.sum(dqn * xhat, axis=0, keepdims=True)

    blk = pl.BlockSpec((bt, MEM_WIDTH), lambda i: (i, 0))
    kv = pl.BlockSpec((mlen, MEM_WIDTH), lambda i: (0, 0))
    gs = pl.BlockSpec((1, MEM_HEAD_DIM), lambda i: (0, 0))
    dx, dmk, dmv, dg = _pcall(
        body, name="cross_attention_bwd", grid=(t // bt,),
        in_specs=[blk, blk, gs, kv, kv], out_specs=[blk, kv, kv, gs],
        out_shape=[jax.ShapeDtypeStruct((t, MEM_WIDTH), BF16), jax.ShapeDtypeStruct((mlen, MEM_WIDTH), F32),
                   jax.ShapeDtypeStruct((mlen, MEM_WIDTH), F32), jax.ShapeDtypeStruct((1, MEM_HEAD_DIM), F32)],
        compiler_params=_params("arbitrary"),
    )(mq_raw, dmo, g_q.reshape(1, MEM_HEAD_DIM), mk, mv)
    return dx, dmk, dmv, dg.reshape(MEM_HEAD_DIM)


def _loss_head(y, target, *, bt=512):
    t, d = y.shape

    def body(y_ref, t_ref, dy_ref, l_ref):
        e = y_ref[...] - t_ref[...]
        dy_ref[...] = e * (1.0 / d)
        part = (0.5 / d) * jnp.sum(jnp.sum(e * e, axis=1, keepdims=True), axis=0, keepdims=True)

        @pl.when(pl.program_id(0) == 0)
        def _():
            l_ref[...] = part

        @pl.when(pl.program_id(0) > 0)
        def _():
            l_ref[...] += part

    blk = pl.BlockSpec((bt, d), lambda i: (i, 0))
    dy, loss = _pcall(
        body, name="loss_head", grid=(t // bt,), in_specs=[blk, blk],
        out_specs=[blk, pl.BlockSpec((1, 1), lambda i: (0, 0))],
        out_shape=[jax.ShapeDtypeStruct((t, d), F32), jax.ShapeDtypeStruct((1, 1), F32)],
        compiler_params=_params("arbitrary"),
    )(y, target)
    return loss, dy


def _row_tile(rows, cols, budget=1 << 19):
    best = None
    for cand in range(8, rows + 1, 8):
        if rows % cand == 0 and cand * cols <= budget:
            best = cand
    return best or rows


def _adamw(w, g, m, v, *, name):
    rows, cols = w.shape
    bt = _row_tile(rows, cols)
    c1 = 1.0 - ADAM_B1 ** ADAM_STEP
    c2 = 1.0 - ADAM_B2 ** ADAM_STEP

    def body(w_ref, g_ref, m_ref, v_ref, d_ref, nm_ref, nv_ref):
        g_v = g_ref[...]
        nm = ADAM_B1 * m_ref[...] + (1.0 - ADAM_B1) * g_v
        nv = ADAM_B2 * v_ref[...] + (1.0 - ADAM_B2) * (g_v * g_v)
        nm_ref[...] = nm
        nv_ref[...] = nv
        d_ref[...] = -ADAM_LR * ((nm / c1) / (jnp.sqrt(nv / c2) + ADAM_EPS) + ADAM_WD * w_ref[...])

    blk = pl.BlockSpec((bt, cols), lambda i: (i, 0))
    return _pcall(
        body, name=name, grid=(rows // bt,), in_specs=[blk] * 4, out_specs=[blk] * 3,
        out_shape=[jax.ShapeDtypeStruct((rows, cols), F32)] * 3, compiler_params=_params("parallel"),
    )(w, g, m, v)


def _sum_slots(x, *, name):
    n, rows, cols = x.shape
    bt = _row_tile(rows, cols, budget=1 << 17)

    def body(x_ref, o_ref):
        acc = x_ref[0].astype(F32)
        for s in range(1, n):
            acc = acc + x_ref[s].astype(F32)
        o_ref[...] = acc

    return _pcall(
        body, name=name, grid=(rows // bt,), in_specs=[pl.BlockSpec((n, bt, cols), lambda i: (0, i, 0))],
        out_specs=pl.BlockSpec((bt, cols), lambda i: (i, 0)),
        out_shape=jax.ShapeDtypeStruct((rows, cols), F32), compiler_params=_params("parallel"),
    )(x)


def _any_spec():
    return pl.BlockSpec(memory_space=pl.ANY)


def _all_gather(xs, *, name):
    n = len(xs)

    def body(*refs):
        x_refs, out_refs = refs[:n], refs[n:2 * n]
        send_sems, recv_sems, local_sems = refs[2 * n:]
        x, y, c = lax.axis_index("x"), lax.axis_index("y"), lax.axis_index("c")
        me, sibling = (x, y, c), (x, y, 1 - c)
        chips = [(1 - x, y), (x, 1 - y), (1 - x, 1 - y)]

        def slot(a, px, py, pc):
            return out_refs[a].at[4 * px + 2 * py + pc]

        def copy(a, k, block, to, src=None):
            return pltpu.make_async_remote_copy(
                src_ref=slot(a, *block) if src is None else src, dst_ref=slot(a, *block),
                send_sem=send_sems.at[a, k], recv_sem=recv_sems.at[a, k], device_id=to, device_id_type=MESH_ID)

        mine = [pltpu.make_async_copy(x_refs[a], slot(a, *me), local_sems.at[a]) for a in range(n)]
        for cp in mine:
            cp.start()
        first = []
        for j, chip in enumerate(chips):
            first += [copy(a, 1 + j, me, (*chip, c), src=x_refs[a]) for a in range(n)]
        first += [copy(a, 0, me, sibling, src=x_refs[a]) for a in range(n)]
        for cp in first:
            cp.start()
        passed = []
        for j, chip in enumerate(chips):
            for a in range(n):
                copy(a, 1 + j, (*chip, c), me).wait_recv()
                cp = copy(a, 4 + j, (*chip, c), sibling)
                cp.start()
                passed.append(cp)
        for a in range(n):
            copy(a, 0, sibling, me).wait_recv()
        for j, chip in enumerate(chips):
            for a in range(n):
                copy(a, 4 + j, (*chip, 1 - c), me).wait_recv()
        for cp in first + passed:
            cp.wait_send()
        for cp in mine:
            cp.wait()

    return _pcall(
        body, name=name, in_specs=[_any_spec()] * n, out_specs=[_any_spec()] * n,
        out_shape=[jax.ShapeDtypeStruct((N_DEV,) + x.shape, x.dtype) for x in xs],
        scratch_shapes=[pltpu.SemaphoreType.DMA((n, 7)), pltpu.SemaphoreType.DMA((n, 7)), pltpu.SemaphoreType.DMA((n,))],
    )(*xs)


def _exchange(xs, *, name):
    n = len(xs)

    def body(*refs):
        x_refs, out_refs = refs[:n], refs[n:2 * n]
        send_sems, recv_sems, local_sems = refs[2 * n:]
        x, y, c = lax.axis_index("x"), lax.axis_index("y"), lax.axis_index("c")
        me_idx = 4 * x + 2 * y + c

        def peer(k):
            px = x ^ ((k >> 2) & 1)
            py = y ^ ((k >> 1) & 1)
            pc = c ^ (k & 1)
            return (px, py, pc), 4 * px + 2 * py + pc

        def copy(a, k):
            to, to_idx = peer(k)
            return pltpu.make_async_remote_copy(
                src_ref=x_refs[a].at[to_idx], dst_ref=out_refs[a].at[me_idx],
                send_sem=send_sems.at[a, k - 1], recv_sem=recv_sems.at[a, k - 1], device_id=to, device_id_type=MESH_ID)

        def landing(a, k):
            frm, frm_idx = peer(k)
            return pltpu.make_async_remote_copy(
                src_ref=x_refs[a].at[frm_idx], dst_ref=out_refs[a].at[frm_idx],
                send_sem=send_sems.at[a, k - 1], recv_sem=recv_sems.at[a, k - 1], device_id=frm, device_id_type=MESH_ID)

        mine = [pltpu.make_async_copy(x_refs[a].at[me_idx], out_refs[a].at[me_idx], local_sems.at[a]) for a in range(n)]
        for cp in mine:
            cp.start()
        sends = [copy(a, k) for k in (2, 4, 6, 3, 5, 7, 1) for a in range(n)]
        for cp in sends:
            cp.start()
        for k in range(1, N_DEV):
            for a in range(n):
                landing(a, k).wait_recv()
        for cp in sends:
            cp.wait_send()
        for cp in mine:
            cp.wait()

    return _pcall(
        body, name=name, in_specs=[_any_spec()] * n, out_specs=[_any_spec()] * n,
        out_shape=[jax.ShapeDtypeStruct(x.shape, x.dtype) for x in xs],
        scratch_shapes=[pltpu.SemaphoreType.DMA((n, 7)), pltpu.SemaphoreType.DMA((n, 7)), pltpu.SemaphoreType.DMA((n,))],
    )(*xs)


def _mesh_peer(k):
    px = lax.axis_index("x") ^ ((k >> 2) & 1)
    py = lax.axis_index("y") ^ ((k >> 1) & 1)
    pc = lax.axis_index("c") ^ (k & 1)
    return (px, py, pc), 4 * px + 2 * py + pc


def _my_index():
    return 4 * lax.axis_index("x") + 2 * lax.axis_index("y") + lax.axis_index("c")


def _landing_zones(xs, blocks):
    me = _my_index()
    lands = []
    for x in xs:
        own = lax.dynamic_index_in_dim(x, me, 0, keepdims=True) if blocks else x[None]
        zone = lax.empty((N_DEV,) + own.shape[1:], x.dtype)
        lands.append(lax.dynamic_update_slice(zone, own, (me,) + (0,) * (own.ndim - 1)))
    return lands


def _send_start(xs, lands, *, blocks, name):
    n = len(xs)
    peers = N_DEV - 1

    def body(*refs):
        x_refs, land_refs = refs[:n], refs[n:2 * n]
        send_sems, recv_sems = refs[2 * n:2 * n + peers], refs[2 * n + peers:2 * n + 2 * peers]
        token = refs[-1]
        me_idx = _my_index()
        for k in (2, 4, 6, 3, 5, 7, 1):
            to, to_idx = _mesh_peer(k)
            for a in range(n):
                pltpu.make_async_remote_copy(
                    src_ref=x_refs[a].at[to_idx] if blocks else x_refs[a], dst_ref=land_refs[a].at[me_idx],
                    send_sem=send_sems[k - 1], recv_sem=recv_sems[k - 1], device_id=to, device_id_type=MESH_ID).start()
        token[...] = jnp.zeros(token.shape, token.dtype)

    hbm = pl.BlockSpec(memory_space=pltpu.HBM)
    sem = pl.BlockSpec(memory_space=pltpu.SEMAPHORE)
    both = list(xs) + list(lands)
    out = _pcall(
        body, name=name,
        out_shape=(*[pltpu.SemaphoreType.DMA(())] * (2 * peers), *[pltpu.HBM(a.shape, a.dtype) for a in both],
                   jax.ShapeDtypeStruct((8, LANES), F32)),
        in_specs=[hbm] * (2 * n),
        out_specs=(*[sem] * (2 * peers), *[hbm] * (2 * n), pl.BlockSpec(memory_space=pltpu.VMEM)),
        input_output_aliases={i: 2 * peers + i for i in range(2 * n)},
        compiler_params=pltpu.CompilerParams(has_side_effects=pltpu.SideEffectType.DATAFLOW_SIDE_EFFECTING),
    )(*[pltpu.with_memory_space_constraint(a, pltpu.HBM) for a in both])
    return dict(sems=out[:2 * peers], xs=out[2 * peers:2 * peers + n], lands=out[2 * peers + n:2 * peers + 2 * n],
                token=out[-1], blocks=blocks)


def _send_wait(started, after, *, name):
    n = len(started['xs'])
    blocks = started['blocks']
    peers = N_DEV - 1

    def body(*refs):
        x_refs, land_refs = refs[:n], refs[n:2 * n]
        send_sems, recv_sems = refs[2 * n:2 * n + peers], refs[2 * n + peers:2 * n + 2 * peers]
        for k in range(1, N_DEV):
            frm, frm_idx = _mesh_peer(k)
            for a in range(n):
                copy = pltpu.make_async_remote_copy(
                    src_ref=x_refs[a].at[frm_idx] if blocks else x_refs[a], dst_ref=land_refs[a].at[frm_idx],
                    send_sem=send_sems[k - 1], recv_sem=recv_sems[k - 1], device_id=frm, device_id_type=MESH_ID)
                copy.wait_send()
                copy.wait_recv()

    hbm = pl.BlockSpec(memory_space=pltpu.HBM)
    sem = pl.BlockSpec(memory_space=pltpu.SEMAPHORE)
    both = list(started['xs']) + list(started['lands'])
    out = _pcall(
        body, name=name, out_shape=[pltpu.HBM(a.shape, a.dtype) for a in both],
        in_specs=[hbm] * (2 * n) + [sem] * (2 * peers) + [pl.BlockSpec(memory_space=pl.ANY)], out_specs=[hbm] * (2 * n),
        input_output_aliases={i: i for i in range(2 * n)},
        compiler_params=pltpu.CompilerParams(has_side_effects=pltpu.SideEffectType.DATAFLOW_SIDE_EFFECTING),
    )(*both, *started['sems'], after)
    return out[n:]


COLUMN_SHARDED = ('w_in', 'w_mem_out', 'w_gate_up')


def _full_weight(name, gathered, l):
    g = gathered[:, l]
    if name in COLUMN_SHARDED:
        g = jnp.transpose(g, (1, 0, 2))
        g = g.reshape(g.shape[0], -1)
    else:
        g = g.reshape(-1, g.shape[-1])
    return jnp.pad(g, ((0, 0), (0, IN_COLS_PAD - IN_COLS))) if name == 'w_in' else g


def _grad_blocks(name, g):
    if name == 'w_gate_up':
        rows, cols = g.shape[1], 2 * g.shape[2] // N_DEV
        g = jnp.transpose(g.reshape(2, rows, N_DEV // 2, cols), (0, 2, 1, 3))
        return g.reshape(N_DEV, rows, cols).astype(BF16)
    if name == 'w_in':
        g = g[:, :IN_COLS]
    if name in COLUMN_SHARDED:
        rows, cols = g.shape[0], g.shape[1] // N_DEV
        g = jnp.transpose(g.reshape(rows, N_DEV, cols), (1, 0, 2))
    else:
        rows, cols = g.shape[0] // N_DEV, g.shape[1]
        g = g.reshape(N_DEV, rows, cols)
    return g.astype(BF16)


SMALL_SHAPES = {'g_mix': (DEPTH, D_MODEL), 'b_forget': (DEPTH, FOX_HEADS), 'g_q_fox': (DEPTH, FOX_HEAD_DIM),
                'g_k_fox': (DEPTH, FOX_HEAD_DIM), 'w_pool': (DEPTH, 4, POOL_GROUP_DIM, POOL_GROUP_DIM),
                'pool_scale': (DEPTH, POOL_WIDTH), 'g_mem_q': (DEPTH, D_MODEL), 'g_mem_kv': (DEPTH, D_MODEL),
                'g_q_mem': (DEPTH, MEM_HEAD_DIM), 'g_k_mem': (DEPTH, MEM_HEAD_DIM), 'g_ffn': (DEPTH, D_MODEL)}


def _small_rows(name):
    return -(-int(np.prod(SMALL_SHAPES[name])) // LANES)


SMALL_ROWS = -(-sum(_small_rows(n) for n in SMALL) // 8) * 8


def _pack_small(tree):
    parts = []
    for n in SMALL:
        flat = tree[n].reshape(-1).astype(F32)
        parts.append(jnp.pad(flat, (0, _small_rows(n) * LANES - flat.shape[0])))
    flat = jnp.concatenate(parts)
    return jnp.pad(flat, (0, SMALL_ROWS * LANES - flat.shape[0])).reshape(SMALL_ROWS, LANES)


def _unpack_small(packed):
    flat = packed.reshape(-1)
    out, at = {}, 0
    for n in SMALL:
        size = int(np.prod(SMALL_SHAPES[n]))
        out[n] = flat[at:at + size].reshape(SMALL_SHAPES[n])
        at += _small_rows(n) * LANES
    return out


def _pairs_cols(a):
    t = a.shape[0]
    return jnp.transpose(a.reshape(t, FOX_HEADS // 2, 2), (1, 0, 2))


def _pairs_rows(a):
    return a.reshape(FOX_HEADS // 2, 2, a.shape[1])


def _layer_fwd(h0, mem, p, w_in, other_weights):
    s = {'h0': h0}
    s['xn1'], z, f = _norm_matmul(h0, p['g_mix'], w_in, tn=IN_COLS_PAD, tail=LANES, name="norm_in_proj_fwd")
    s['z'] = z
    s['qn'], s['kn'], s['vb'] = _qkv_prep(z, p['g_q_fox'], p['g_k_fox'])
    s['ft'] = jnp.transpose(f[:, :FOX_HEADS])
    c = _forget_cumsum(s['ft'], p['b_forget'])
    s['ccol'], s['crow'] = _pairs_cols(jnp.transpose(c)), _pairs_rows(c)
    s['qnt'] = jnp.transpose(s['qn'])
    s['fox'], s['lse_row'] = _fox_fwd(s['qnt'], s['kn'], jnp.transpose(s['vb']), s['crow'], s['ccol'])
    pool, s['mixed'] = _pool_fwd(z, p['w_pool'].astype(BF16), p['pool_scale'])
    s['cat'] = jnp.concatenate([s['fox'], pool], axis=1)
    w = dict(other_weights(s['lse_row']), w_in=w_in)
    h1 = _matmul(s['cat'], w['w_out'], res=h0, name="out_proj_fwd")
    s['h1'] = h1

    s['hn2'], s['mq_raw'] = _norm_matmul(h1, p['g_mem_q'], w['w_mem_q'], name="norm_mem_q_fwd")
    s['mn'], s['mkv'], s['mk'], s['mv'] = _mem_kv_fwd(mem, p['g_mem_kv'], w['w_mem_kv'], p['g_k_mem'])
    s['mo'] = _cross_fwd(s['mq_raw'], p['g_q_mem'], s['mk'], s['mv'])
    h2 = _matmul(s['mo'], w['w_mem_out'], res=h1, name="mem_out_fwd")
    s['h2'] = h2

    s['hn3'], s['gu'], s['act'] = _norm_gate_up_swiglu(h2, p['g_ffn'], w['w_gate_up'])
    h3 = _matmul(s['act'], w['w_down'], res=h2, name="down_fwd")
    return h3, s, w


def _layer_bwd(dh, mem, p, w, s, after=None, at_mixer=None):
    g = {}
    g['w_down'] = _matmul(s['act'], dh, ta=True, after=after, tm=1408, tn=512, tk=1024, name="down_dw")
    dgu = _down_dx_swiglu_bwd(dh, w['w_down'], s['gu'])
    g['w_gate_up'] = _matmul(s['hn3'], dgu, ta=True, tm=1024, tn=1408, tk=1024, name="gate_up_dw")
    dh, g['g_ffn'] = _matmul_norm_bwd(dgu, w['w_gate_up'], s['h2'], p['g_ffn'], dh, name="gate_up_dx_norm_bwd")

    g['w_mem_out'] = _matmul(s['mo'], dh, ta=True, tm=512, tn=1024, tk=1024, name="mem_out_dw")
    dmo = _matmul(dh, w['w_mem_out'], tb=True, out_dtype=BF16, name="mem_out_dx")
    dmq, dmk, dmv, g['g_q_mem'] = _cross_bwd(s['mq_raw'], dmo, p['g_q_mem'], s['mk'], s['mv'])
    g['w_mem_kv'], g['g_mem_kv'], g['g_k_mem'] = _mem_kv_bwd(dmk, dmv, s['mkv'], s['mn'], mem, p['g_mem_kv'],
                                                               w['w_mem_kv'], p['g_k_mem'])
    g['w_mem_q'] = _matmul(s['hn2'], dmq, ta=True, tm=1024, tn=512, tk=1024, name="mem_q_dw")
    dh, g['g_mem_q'] = _matmul_norm_bwd(dmq, w['w_mem_q'], s['h1'], p['g_mem_q'], dh, name="mem_q_dx_norm_bwd")

    after = at_mixer(g) if at_mixer is not None else None
    g['w_out'] = _matmul(s['cat'], dh, ta=True, after=after, tm=1024, tn=512, tk=1024, name="out_proj_dw")
    dcat = _matmul(dh, w['w_out'], tb=True, tn=1024, name="out_proj_dx")
    dpin, g['w_pool'], dscale = _pool_bwd(dcat, s['mixed'], p['w_pool'].astype(BF16), p['pool_scale'])
    g['pool_scale'] = dscale.reshape(POOL_WIDTH)
    do, delta = _fox_delta(dcat, s['fox'])
    delta_row = _pairs_rows(jnp.transpose(delta[:, :FOX_HEADS]))
    dq, dk, dv, dc_keys, dc_queries = _fox_bwd(s['qn'], s['kn'], jnp.transpose(s['kn']), s['vb'], do, s['lse_row'],
                                               delta_row, s['crow'], s['ccol'])
    dft, db = _forget_cumsum_bwd(dc_keys, dc_queries, s['ft'], p['b_forget'])
    g['b_forget'] = db.reshape(FOX_HEADS)
    dfpad = jnp.pad(jnp.transpose(dft), ((0, 0), (0, LANES - FOX_HEADS)))
    dz, g['g_q_fox'], g['g_k_fox'] = _mixer_dz(s['z'], dq, dk, dv, dpin, dfpad, p['g_q_fox'], p['g_k_fox'])
    g['w_in'] = _matmul(s['xn1'], dz, ta=True, tm=512, tn=IN_COLS_PAD, tk=1024, name="in_proj_dw")
    dh, g['g_mix'] = _matmul_norm_bwd(dz, w['w_in'], s['h0'], p['g_mix'], dh, name="in_proj_dx_norm_bwd")
    return dh, g


def _local_step(x2, mem2, target2, small, w_in, other_weights, send_grads):
    h = x2
    saved, full = [], []
    for l in range(DEPTH):
        h, s, w = _layer_fwd(h, mem2, {k: v[l] for k, v in small.items()}, w_in[l], other_weights(l))
        saved.append(s)
        full.append(w)
    loss, dh = _loss_head(h, target2)
    after = None
    grads = [None] * DEPTH
    for l in reversed(range(1, DEPTH)):
        dh, grads[l] = _layer_bwd(dh, mem2, {k: v[l] for k, v in small.items()}, full[l], saved[l], after=after)
        after = send_grads(l, BIG, grads[l])
    dh, grads[0] = _layer_bwd(dh, mem2, {k: v[0] for k, v in small.items()}, full[0], saved[0], after=after,
                              at_mixer=lambda g: send_grads(0, SENT_AT_MIXER, g))
    return loss, dh, grads


def kernel(x, mem, g_mix, w_in, b_forget, g_q_fox, g_k_fox, w_pool, pool_scale, w_out, g_mem_q, g_mem_kv, w_mem_q, w_mem_kv, g_q_mem, g_k_mem, w_mem_out, g_ffn, w_gate_up, w_down, loss_target, m_g_mix, m_w_in, m_b_forget, m_g_q_fox, m_g_k_fox, m_w_pool, m_pool_scale, m_w_out, m_g_mem_q, m_g_mem_kv, m_w_mem_q, m_w_mem_kv, m_g_q_mem, m_g_k_mem, m_w_mem_out, m_g_ffn, m_w_gate_up, m_w_down, v_g_mix, v_w_in, v_b_forget, v_g_q_fox, v_g_k_fox, v_w_pool, v_pool_scale, v_w_out, v_g_mem_q, v_g_mem_kv, v_w_mem_q, v_w_mem_kv, v_g_q_mem, v_g_k_mem, v_w_mem_out, v_g_ffn, v_w_gate_up, v_w_down):
    weights = dict(g_mix=g_mix, w_in=w_in, b_forget=b_forget, g_q_fox=g_q_fox, g_k_fox=g_k_fox, w_pool=w_pool,
                   pool_scale=pool_scale, w_out=w_out, g_mem_q=g_mem_q, g_mem_kv=g_mem_kv, w_mem_q=w_mem_q,
                   w_mem_kv=w_mem_kv, g_q_mem=g_q_mem, g_k_mem=g_k_mem, w_mem_out=w_mem_out, g_ffn=g_ffn,
                   w_gate_up=w_gate_up, w_down=w_down)
    mom_m = dict(g_mix=m_g_mix, w_in=m_w_in, b_forget=m_b_forget, g_q_fox=m_g_q_fox, g_k_fox=m_g_k_fox, w_pool=m_w_pool,
                 pool_scale=m_pool_scale, w_out=m_w_out, g_mem_q=m_g_mem_q, g_mem_kv=m_g_mem_kv, w_mem_q=m_w_mem_q,
                 w_mem_kv=m_w_mem_kv, g_q_mem=m_g_q_mem, g_k_mem=m_g_k_mem, w_mem_out=m_w_mem_out, g_ffn=m_g_ffn,
                 w_gate_up=m_w_gate_up, w_down=m_w_down)
    mom_v = dict(g_mix=v_g_mix, w_in=v_w_in, b_forget=v_b_forget, g_q_fox=v_g_q_fox, g_k_fox=v_g_k_fox, w_pool=v_w_pool,
                 pool_scale=v_pool_scale, w_out=v_w_out, g_mem_q=v_g_mem_q, g_mem_kv=v_g_mem_kv, w_mem_q=v_w_mem_q,
                 w_mem_kv=v_w_mem_kv, g_q_mem=v_g_q_mem, g_k_mem=v_g_k_mem, w_mem_out=v_w_mem_out, g_ffn=v_g_ffn,
                 w_gate_up=v_w_gate_up, w_down=v_w_down)

    shards = {n: weights[n].astype(BF16) for n in BIG}
    (w_in_all,) = _all_gather([shards['w_in']], name="w_in_all_gather")
    later = [n for n in BIG if n != 'w_in']
    weights_sent = _send_start([shards[n] for n in later], _landing_zones([shards[n] for n in later], False),
                               blocks=False, name="weights_send_start")
    small = {n: weights[n] for n in SMALL}
    small['g_mix'] = small['g_mix'] + weights_sent['token'][0, 0]
    gathered = {}

    def other_weights(l):
        def get(after):
            if not gathered:
                gathered.update(zip(later, _send_wait(weights_sent, after, name="weights_send_wait")))
            return {n: _full_weight(n, gathered[n], l) for n in later}
        return get

    grads_sent = []

    def send_grads(l, names, g):
        blocks = [_grad_blocks(n, g[n]) for n in names]
        sent = _send_start(blocks, _landing_zones(blocks, True), blocks=True, name=f"grads{l}_send_start")
        grads_sent.append((l, names, sent))
        return sent['token']

    w_in_full = [_full_weight('w_in', w_in_all, l) for l in range(DEPTH)]
    loss_part, grad_x, grads = _local_step(x[0], mem[0], loss_target[0], small, w_in_full, other_weights, send_grads)
    loss = lax.psum(loss_part[0, 0], ("x", "y", "c"))

    last = [n for n in BIG if n not in SENT_AT_MIXER]
    landed = {(0, n): a for n, a in zip(last, _exchange([_grad_blocks(n, grads[0][n]) for n in last],
                                                         name="grads0_exchange"))}
    for l, names, sent in grads_sent:
        landed.update({(l, n): a for n, a in zip(names, _send_wait(sent, grad_x, name=f"grads{l}_send_wait"))})
    grad = {n: jnp.concatenate([_sum_slots(landed[l, n], name="grad_sum_" + n) for l in range(DEPTH)], 0) for n in BIG}
    small_part = _pack_small({n: jnp.stack([grads[l][n] for l in range(DEPTH)], 0) for n in SMALL})
    (small_all,) = _all_gather([small_part], name="small_grads_all_gather")
    small_sum = _sum_slots(small_all, name="grad_sum_small")

    delta, new_m, new_v = {}, {}, {}
    for n in BIG:
        shape = weights[n].shape
        two_d = lambda a: a.reshape(shape[0] * shape[1], shape[2])
        d, nm, nv = _adamw(two_d(weights[n]), grad[n], two_d(mom_m[n]), two_d(mom_v[n]), name="adamw_" + n)
        grad[n], delta[n], new_m[n], new_v[n] = (a.reshape(shape) for a in (grad[n], d, nm, nv))
    d, nm, nv = _adamw(_pack_small(weights), small_sum, _pack_small(mom_m), _pack_small(mom_v), name="adamw_small")
    grad.update(_unpack_small(small_sum))
    delta.update(_unpack_small(d))
    new_m.update(_unpack_small(nm))
    new_v.update(_unpack_small(nv))

    return (loss, grad_x[None], *[grad[n] for n in WEIGHTS], *[delta[n] for n in WEIGHTS],
            *[new_m[n] for n in WEIGHTS], *[new_v[n] for n in WEIGHTS])
```

```python
import functools

import numpy as np
import jax
import jax.numpy as jnp
from jax import lax
from jax.experimental import pallas as pl
from jax.experimental.pallas import tpu as pltpu

F32 = jnp.float32
BF16 = jnp.bfloat16

N_DEV = 8
D_MODEL = 1024
DEPTH = 2
FOX_HEADS = 8
FOX_HEAD_DIM = 64
FOX_WIDTH = 512
POOL_WIDTH = 512
POOL_WINDOWS = (2, 4, 8, 16)
POOL_GROUP_DIM = 128
POOL_HALO = 16
IN_COLS = 2056
IN_COLS_PAD = 2176
MEM_HEADS = 4
MEM_HEAD_DIM = 128
MEM_WIDTH = 512
D_FF = 2816
EPS = 1e-6
FOX_SCALE = FOX_HEAD_DIM ** -0.5
FOX_ACC_ROWS = FOX_HEAD_DIM + 16
MEM_SCALE = MEM_HEAD_DIM ** -0.5
LANES = 128

ADAM_LR = 0.001
ADAM_B1 = 0.9
ADAM_B2 = 0.999
ADAM_EPS = 1e-08
ADAM_WD = 0.01
ADAM_STEP = 10

VMEM_LIMIT = 56 * 1024 * 1024
MESH_ID = pl.DeviceIdType.MESH

WEIGHTS = ['g_mix', 'w_in', 'b_forget', 'g_q_fox', 'g_k_fox', 'w_pool', 'pool_scale', 'w_out', 'g_mem_q', 'g_mem_kv',
           'w_mem_q', 'w_mem_kv', 'g_q_mem', 'g_k_mem', 'w_mem_out', 'g_ffn', 'w_gate_up', 'w_down']
BIG = ['w_in', 'w_out', 'w_mem_q', 'w_mem_kv', 'w_mem_out', 'w_gate_up', 'w_down']
SMALL = [n for n in WEIGHTS if n not in BIG]
SENT_AT_MIXER = ['w_down', 'w_gate_up', 'w_mem_out', 'w_mem_q', 'w_mem_kv', 'w_out']


def _pcall(body, **kw):
    return pl.pallas_call(body, **kw)


def _params(*sem):
    return pltpu.CompilerParams(dimension_semantics=sem or None, vmem_limit_bytes=VMEM_LIMIT)


def _dot(a, b, dims=None):
    if dims is None:
        return jnp.dot(a, b, preferred_element_type=F32)
    return lax.dot_general(a, b, (dims, ((), ())), preferred_element_type=F32)


NT = ((1,), (1,))
TN = ((0,), (0,))


def _dot_exact(x, ones_bf16):
    hi = x.astype(BF16)
    r1 = x - hi.astype(F32)
    mid = r1.astype(BF16)
    lo = (r1 - mid.astype(F32)).astype(BF16)
    return _dot(hi, ones_bf16) + _dot(mid, ones_bf16) + _dot(lo, ones_bf16)


def _matmul(a, b, *, ta=False, tb=False, out_dtype=F32, res=None, after=None, tm=1024, tn=512, tk=None, name):
    planes = b.shape[0] if b.ndim == 3 else None
    bshape = b.shape[-2:]
    m, k = (a.shape[1], a.shape[0]) if ta else a.shape
    n = bshape[0] if tb else bshape[1]
    assert k == (bshape[1] if tb else bshape[0])
    tm, tn = min(tm, m), min(tn, n)
    tk = min(tk or k, k)
    assert m % tm == 0 and n % tn == 0 and k % tk == 0, (name, m, n, k, tm, tn, tk)
    nk = k // tk
    dims = ((0 if ta else 1,), (1 if tb else 0,))

    def body(*refs):
        a_ref, b_ref = refs[0], refs[1]
        r_ref = refs[2] if res is not None else None
        o_ref = refs[2 + (res is not None) + (after is not None)]
        part = _dot(a_ref[...].astype(BF16), b_ref[...].astype(BF16), dims)

        def finish(acc):
            if r_ref is not None:
                acc = acc + r_ref[...]
            o_ref[...] = acc.astype(o_ref.dtype)

        if nk == 1:
            finish(part)
        else:
            acc_ref = refs[-1]
            kk = pl.program_id(3)

            @pl.when(kk == 0)
            def _():
                acc_ref[...] = part

            @pl.when(kk > 0)
            def _():
                acc_ref[...] += part

            @pl.when(kk == nk - 1)
            def _():
                finish(acc_ref[...])

    lead = (lambda p: (p,)) if planes else (lambda p: ())
    sq = (None,) if planes else ()
    a_spec = pl.BlockSpec((tk, tm), lambda p, i, j, kk: (kk, i)) if ta else pl.BlockSpec((tm, tk), lambda p, i, j, kk: (i, kk))
    b_spec = (pl.BlockSpec(sq + (tn, tk), lambda p, i, j, kk: lead(p) + (j, kk)) if tb
              else pl.BlockSpec(sq + (tk, tn), lambda p, i, j, kk: lead(p) + (kk, j)))
    o_spec = pl.BlockSpec(sq + (tm, tn), lambda p, i, j, kk: lead(p) + (i, j))
    in_specs = ([a_spec, b_spec] + ([o_spec] if res is not None else [])
                + ([pl.BlockSpec(memory_space=pl.ANY)] if after is not None else []))
    args = (a, b) + ((res,) if res is not None else ()) + ((after,) if after is not None else ())
    return _pcall(
        body, name=name, grid=(planes or 1, m // tm, n // tn, nk), in_specs=in_specs, out_specs=o_spec,
        out_shape=jax.ShapeDtypeStruct(((planes,) if planes else ()) + (m, n), out_dtype),
        scratch_shapes=[pltpu.VMEM((tm, tn), F32)] if nk > 1 else [],
        compiler_params=_params("parallel", "parallel", "parallel", "arbitrary"),
    )(*args)


def _rms(x):
    return lax.rsqrt(jnp.mean(x * x, axis=-1, keepdims=True) + EPS)


def _norm_matmul(h, g, w, *, tm=512, tn=512, tail=0, name):
    t, d = h.shape
    n = w.shape[1]
    tn = min(tn, n)
    assert t % tm == 0 and n % tn == 0 and (not tail or tn == n)

    def body(h_ref, g_ref, w_ref, xn_ref, y_ref, *tail_ref):
        @pl.when(pl.program_id(1) == 0)
        def _():
            x = h_ref[...]
            xn_ref[...] = (x * _rms(x) * g_ref[...]).astype(BF16)

        y = _dot(xn_ref[...], w_ref[...])
        if tail:
            y_ref[...] = y[:, :n - tail]
            tail_ref[0][...] = y[:, n - tail:]
        else:
            y_ref[...] = y

    row = pl.BlockSpec((tm, d), lambda i, j: (i, 0))
    tails = ([pl.BlockSpec((tm, tail), lambda i, j: (i, 0))], [jax.ShapeDtypeStruct((t, tail), F32)]) if tail else ([], [])
    return _pcall(
        body, name=name, grid=(t // tm, n // tn),
        in_specs=[row, pl.BlockSpec((1, d), lambda i, j: (0, 0)), pl.BlockSpec((d, tn), lambda i, j: (0, j))],
        out_specs=[row, pl.BlockSpec((tm, tn - tail), lambda i, j: (i, j))] + tails[0],
        out_shape=[jax.ShapeDtypeStruct((t, d), BF16), jax.ShapeDtypeStruct((t, n - tail), F32)] + tails[1],
        compiler_params=_params("parallel", "arbitrary"),
    )(h, g.reshape(1, d), w)


def _matmul_norm_bwd(a, w, h, g, dres, *, tm=512, tk=None, name):
    stacked = a.ndim == 3
    t = a.shape[-2]
    d, k = w.shape
    tk = a.shape[-1] if stacked else min(tk or k, k)
    nk = k // tk
    assert t % tm == 0 and k % tk == 0 and (not stacked or a.shape[0] == nk)

    def body(a_ref, w_ref, h_ref, g_ref, r_ref, dx_ref, dg_ref, *acc):
        i, kk = pl.program_id(0), pl.program_id(1)
        part = _dot(a_ref[...], w_ref[...], NT)

        def finish(dy):
            x = h_ref[...]
            r = _rms(x)
            xhat = x * r
            gy = dy * g_ref[...]
            dx_ref[...] = r_ref[...] + r * (gy - xhat * jnp.mean(gy * xhat, axis=-1, keepdims=True))
            dg_part = jnp.sum(dy * xhat, axis=0, keepdims=True)

            @pl.when(i == 0)
            def _():
                dg_ref[...] = dg_part

            @pl.when(i > 0)
            def _():
                dg_ref[...] += dg_part

        if nk == 1:
            finish(part)
        else:
            acc_ref = acc[0]

            @pl.when(kk == 0)
            def _():
                acc_ref[...] = part

            @pl.when(kk > 0)
            def _():
                acc_ref[...] += part

            @pl.when(kk == nk - 1)
            def _():
                finish(acc_ref[...])

    a_spec = (pl.BlockSpec((None, tm, tk), lambda i, kk: (kk, i, 0)) if stacked
              else pl.BlockSpec((tm, tk), lambda i, kk: (i, kk)))
    row = pl.BlockSpec((tm, d), lambda i, kk: (i, 0))
    vec = pl.BlockSpec((1, d), lambda i, kk: (0, 0))
    dx, dg = _pcall(
        body, name=name, grid=(t // tm, nk),
        in_specs=[a_spec, pl.BlockSpec((d, tk), lambda i, kk: (0, kk)), row, vec, row], out_specs=[row, vec],
        out_shape=[jax.ShapeDtypeStruct((t, d), F32), jax.ShapeDtypeStruct((1, d), F32)],
        scratch_shapes=[pltpu.VMEM((tm, d), F32)] if nk > 1 else [],
        compiler_params=_params("arbitrary", "arbitrary"),
    )(a, w, h, g.reshape(1, d), dres)
    return dx, dg.reshape(d)


def _norm_gate_up_swiglu(h, g, w, *, tm=512, tn=1408):
    t, d = h.shape
    nj = D_FF // tn
    assert t % tm == 0 and D_FF % tn == 0

    def body(h_ref, g_ref, wg_ref, wu_ref, hn_ref, gu_ref, act_ref):
        @pl.when(pl.program_id(1) == 0)
        def _():
            x = h_ref[...]
            hn_ref[...] = (x * _rms(x) * g_ref[...]).astype(BF16)

        hn = hn_ref[...]
        gate = _dot(hn, wg_ref[...])
        up = _dot(hn, wu_ref[...])
        gu_ref[0] = gate
        gu_ref[1] = up
        act_ref[...] = (gate * jax.nn.sigmoid(gate) * up).astype(BF16)

    row = pl.BlockSpec((tm, d), lambda i, j: (i, 0))
    return _pcall(
        body, name="gate_up_swiglu_fwd", grid=(t // tm, nj),
        in_specs=[row, pl.BlockSpec((1, d), lambda i, j: (0, 0)), pl.BlockSpec((d, tn), lambda i, j: (0, j)),
                  pl.BlockSpec((d, tn), lambda i, j: (0, nj + j))],
        out_specs=[row, pl.BlockSpec((2, tm, tn), lambda i, j: (0, i, j)), pl.BlockSpec((tm, tn), lambda i, j: (i, j))],
        out_shape=[jax.ShapeDtypeStruct((t, d), BF16), jax.ShapeDtypeStruct((2, t, D_FF), F32),
                   jax.ShapeDtypeStruct((t, D_FF), BF16)],
        compiler_params=_params("parallel", "arbitrary"),
    )(h, g.reshape(1, d), w, w)


def _down_dx_swiglu_bwd(dh, w_down, gu, *, after=None, tm=512, tn=1408):
    t, d = dh.shape
    assert t % tm == 0 and D_FF % tn == 0

    def body(dh_ref, w_ref, gu_ref, *rest):
        dgu_ref = rest[-1]
        da = _dot(dh_ref[...].astype(BF16), w_ref[...], NT)
        gate, up = gu_ref[0], gu_ref[1]
        sg = jax.nn.sigmoid(gate)
        silu = gate * sg
        dgu_ref[0] = (da * up * (sg + silu * (1.0 - sg))).astype(BF16)
        dgu_ref[1] = (da * silu).astype(BF16)

    stack = pl.BlockSpec((2, tm, tn), lambda i, j: (0, i, j))
    return _pcall(
        body, name="down_dx_swiglu_bwd", grid=(t // tm, D_FF // tn),
        in_specs=[pl.BlockSpec((tm, d), lambda i, j: (i, 0)), pl.BlockSpec((tn, d), lambda i, j: (j, 0)), stack]
        + ([pl.BlockSpec(memory_space=pl.ANY)] if after is not None else []),
        out_specs=stack, out_shape=jax.ShapeDtypeStruct((2, t, D_FF), BF16),
        compiler_params=_params("parallel", "parallel"),
    )(dh, w_down, gu, *((after,) if after is not None else ()))


def _group_matrix(width, group):
    r = lax.broadcasted_iota(jnp.int32, (width, width), 0) // group
    c = lax.broadcasted_iota(jnp.int32, (width, width), 1) // group
    return (r == c).astype(BF16)


def _qkv_prep(z, gq, gk, *, bt=512):
    t = z.shape[0]
    w = FOX_WIDTH

    def body(q_ref, k_ref, v_ref, gq_ref, gk_ref, qo_ref, ko_ref, vo_ref):
        gm = _group_matrix(w, FOX_HEAD_DIM)
        for x_ref, g_ref, o_ref, scale in ((q_ref, gq_ref, qo_ref, FOX_SCALE), (k_ref, gk_ref, ko_ref, 1.0)):
            x = x_ref[...]
            ms = _dot_exact(x * x, gm) * (1.0 / FOX_HEAD_DIM)
            y = x * lax.rsqrt(ms + EPS) * g_ref[...]
            o_ref[...] = (y * scale).astype(BF16)
        vo_ref[...] = v_ref[...].astype(BF16)

    col = lambda c: pl.BlockSpec((bt, w), lambda i, c=c: (i, c))
    vec = pl.BlockSpec((1, w), lambda i: (0, 0))
    out = pl.BlockSpec((bt, w), lambda i: (i, 0))
    return _pcall(
        body, name="fox_qkv_prep", grid=(t // bt,), in_specs=[col(0), col(1), col(2), vec, vec], out_specs=[out, out, out],
        out_shape=[jax.ShapeDtypeStruct((t, w), BF16)] * 3, compiler_params=_params("parallel"),
    )(z, z, z, jnp.tile(gq, FOX_HEADS).reshape(1, w), jnp.tile(gk, FOX_HEADS).reshape(1, w))


def _log_sigmoid(f):
    return jnp.minimum(f, 0.0) - jnp.log1p(jnp.exp(-jnp.abs(f)))


def _forget_cumsum(ft, b, *, chunk=512):
    hh, t = ft.shape

    def body(f_ref, b_ref, c_ref):
        r = lax.broadcasted_iota(jnp.int32, (chunk, chunk), 0)
        c = lax.broadcasted_iota(jnp.int32, (chunk, chunk), 1)
        upper = (r <= c).astype(BF16)
        carry = jnp.zeros((hh, 1), F32)
        for ch in range(t // chunk):
            sl = slice(ch * chunk, (ch + 1) * chunk)
            cs = _dot_exact(_log_sigmoid(f_ref[:, sl] + b_ref[...]), upper) + carry
            c_ref[:, sl] = cs
            carry = cs[:, chunk - 1:chunk]

    return _pcall(body, name="fox_forget_cumsum", out_shape=jax.ShapeDtypeStruct((hh, t), F32),
                  compiler_params=_params())(ft, b.reshape(hh, 1))


def _forget_cumsum_bwd(dc_keys, dc_queries, ft, b, *, chunk=512):
    hh, t = ft.shape

    def body(dck_ref, dcq_ref, f_ref, b_ref, df_ref, db_ref):
        r = lax.broadcasted_iota(jnp.int32, (chunk, chunk), 0)
        c = lax.broadcasted_iota(jnp.int32, (chunk, chunk), 1)
        lower = (r >= c).astype(BF16)
        carry = jnp.zeros((hh, 1), F32)
        db = jnp.zeros((hh, 1), F32)
        for ch in reversed(range(t // chunk)):
            sl = slice(ch * chunk, (ch + 1) * chunk)
            dls = _dot_exact(dck_ref[:, sl] + dcq_ref[:, sl], lower) + carry
            carry = dls[:, 0:1]
            df = dls * jax.nn.sigmoid(-(f_ref[:, sl] + b_ref[...]))
            df_ref[:, sl] = df
            db = db + jnp.sum(df, axis=1, keepdims=True)
        db_ref[...] = db

    return _pcall(body, name="fox_forget_cumsum_bwd",
                  out_shape=[jax.ShapeDtypeStruct((hh, t), F32), jax.ShapeDtypeStruct((hh, 1), F32)],
                  compiler_params=_params())(dc_keys, dc_queries, ft, b.reshape(hh, 1))


def _lane_is_first_head():
    return lax.broadcasted_iota(jnp.int32, (1, LANES), 1) < FOX_HEAD_DIM


def _fox_fwd(qnt, kn, vt, crow, ccol, *, bq=512):
    t = kn.shape[0]
    nq = t // bq
    pairs = FOX_WIDTH // LANES
    tiles = [(i, j) for i in range(nq) for j in range(i, -1, -1)]
    it = jnp.asarray(np.array([a for a, _ in tiles], np.int32))
    jt = jnp.asarray(np.array([b for _, b in tiles], np.int32))

    def body(it_ref, jt_ref, qt_ref, k_ref, vt_ref, cr_ref, cc_ref, o_ref, lse_ref, m_sc, acc_sc):
        s_id = pl.program_id(1)
        i, j = it_ref[s_id], jt_ref[s_id]
        first = _lane_is_first_head()

        @pl.when(j == i)
        def _():
            m_sc[...] = jnp.full(m_sc.shape, -jnp.inf, F32)
            acc_sc[...] = jnp.zeros(acc_sc.shape, F32)

        def scores():
            k2, qt2 = k_ref[...], qt_ref[...]
            return [_dot(jnp.where(first if hh == 0 else jnp.logical_not(first), k2, jnp.zeros_like(k2)), qt2)
                    for hh in range(2)]

        def pv(hh, pt_bf16):
            v_ones = jnp.concatenate([vt_ref[hh * FOX_HEAD_DIM:(hh + 1) * FOX_HEAD_DIM, :],
                                      jnp.ones((FOX_ACC_ROWS - FOX_HEAD_DIM, bq), BF16)], axis=0)
            return _dot(v_ones, pt_bf16)

        def tile(diagonal):
            sc = scores()
            for hh in range(2):
                ut = sc[hh] - cc_ref[0, :, hh:hh + 1]
                if diagonal:
                    rr = lax.broadcasted_iota(jnp.int32, ut.shape, 0)
                    cc = lax.broadcasted_iota(jnp.int32, ut.shape, 1)
                    ut = jnp.where(rr <= cc, ut, -jnp.inf)
                c_t = cr_ref[0, hh:hh + 1, :]
                m_prev = m_sc[hh]
                m_new = jnp.maximum(m_prev, jnp.max(ut, axis=0, keepdims=True) + c_t)
                acc_sc[hh] = jnp.exp(m_prev - m_new) * acc_sc[hh] + pv(hh, jnp.exp(ut + (c_t - m_new)).astype(BF16))
                m_sc[hh] = m_new

        @pl.when(j == i)
        def _():
            tile(True)

        @pl.when(j < i)
        def _():
            tile(False)

        @pl.when(j == 0)
        def _():
            sums = [acc_sc[hh, FOX_HEAD_DIM:FOX_HEAD_DIM + 1, :] for hh in range(2)]
            ot = jnp.concatenate([acc_sc[hh, :FOX_HEAD_DIM, :] / sums[hh] for hh in range(2)], axis=0)
            o_ref[...] = jnp.transpose(ot).astype(o_ref.dtype)
            for hh in range(2):
                lse_ref[0, hh:hh + 1, :] = m_sc[hh] + jnp.log(sums[hh])

    qspec = pl.BlockSpec((bq, LANES), lambda p, s, it, jt: (it[s], p))
    kspec = pl.BlockSpec((bq, LANES), lambda p, s, it, jt: (jt[s], p))
    vtspec = pl.BlockSpec((LANES, bq), lambda p, s, it, jt: (p, jt[s]))
    qtspec = pl.BlockSpec((LANES, bq), lambda p, s, it, jt: (p, it[s]))
    rowq = pl.BlockSpec((1, 2, bq), lambda p, s, it, jt: (p, 0, it[s]))
    colk = pl.BlockSpec((1, bq, 2), lambda p, s, it, jt: (p, jt[s], 0))
    return _pcall(
        body, name="fox_attention_fwd",
        grid_spec=pltpu.PrefetchScalarGridSpec(
            num_scalar_prefetch=2, grid=(pairs, len(tiles)),
            in_specs=[qtspec, kspec, vtspec, rowq, colk], out_specs=[qspec, rowq],
            scratch_shapes=[pltpu.VMEM((2, 1, bq), F32), pltpu.VMEM((2, FOX_ACC_ROWS, bq), F32)]),
        out_shape=[jax.ShapeDtypeStruct((t, FOX_WIDTH), BF16), jax.ShapeDtypeStruct((pairs, 2, t), F32)],
        compiler_params=_params("parallel", "arbitrary"),
    )(it, jt, qnt, kn, vt, crow, ccol)


def _fox_bwd(qn, kn, knt, vb, do, lse_row, delta_row, crow, ccol, *, bq=512):
    t = qn.shape[0]
    nq = t // bq
    pairs = FOX_WIDTH // LANES
    tiles = [(i, j) for j in range(nq) for i in range(j, nq)]
    it = jnp.asarray(np.array([a for a, _ in tiles], np.int32))
    jt = jnp.asarray(np.array([b for _, b in tiles], np.int32))

    def body(it_ref, jt_ref, q_ref, k_ref, kt_ref, v_ref, do_ref, lse_ref, dl_ref, cr_ref, cc_ref,
             dqt_ref, dk_ref, dv_ref, dc_ref, dr_ref, dk_sc, dv_sc, dc_sc):
        s_id = pl.program_id(1)
        i, j = it_ref[s_id], jt_ref[s_id]
        first = _lane_is_first_head()

        @pl.when(s_id == 0)
        def _():
            dqt_ref[...] = jnp.zeros(dqt_ref.shape, F32)
            dr_ref[...] = jnp.zeros(dr_ref.shape, F32)

        @pl.when(i == j)
        def _():
            dk_sc[...] = jnp.zeros(dk_sc.shape, F32)
            dv_sc[...] = jnp.zeros(dv_sc.shape, F32)
            dc_sc[...] = jnp.zeros(dc_sc.shape, F32)

        def tile(diagonal):
            q2, k2, kt2, v2, do2 = q_ref[...], k_ref[...], kt_ref[...], v_ref[...], do_ref[...]
            dk_t, dv_t, dqt_t = [], [], []
            for hh in range(2):
                mine = first if hh == 0 else jnp.logical_not(first)
                kh = jnp.where(mine, k2, jnp.zeros_like(k2))
                vh = jnp.where(mine, v2, jnp.zeros_like(v2))
                kth = kt2[hh * FOX_HEAD_DIM:(hh + 1) * FOX_HEAD_DIM]
                st = _dot(kh, q2, NT) + (cr_ref[0, hh:hh + 1, :] - cc_ref[0, :, hh:hh + 1])
                if diagonal:
                    rr = lax.broadcasted_iota(jnp.int32, st.shape, 0)
                    cc = lax.broadcasted_iota(jnp.int32, st.shape, 1)
                    st = jnp.where(rr <= cc, st, -jnp.inf)
                pt = jnp.exp(st - lse_ref[0, hh:hh + 1, :])
                dv_t.append(_dot(pt.astype(BF16), do2))
                dpt = _dot(vh, do2, NT)
                dst = pt * (dpt - dl_ref[0, hh:hh + 1, :])
                dc_sc[hh] += jnp.sum(dst, axis=1, keepdims=True)
                dr_ref[0, i, hh:hh + 1, :] += jnp.sum(dst, axis=0, keepdims=True)
                dsb = dst.astype(BF16)
                dk_t.append(_dot(dsb, q2))
                dqt_t.append(_dot(kth, dsb))
            dk_sc[...] += jnp.where(first, dk_t[0], dk_t[1])
            dv_sc[...] += jnp.where(first, dv_t[0], dv_t[1])
            dqt_ref[0, i] += jnp.concatenate(dqt_t, axis=0)

        @pl.when(i > j)
        def _():
            tile(False)

        @pl.when(i == j)
        def _():
            tile(True)

        @pl.when(i == nq - 1)
        def _():
            dk_ref[...] = dk_sc[...]
            dv_ref[...] = dv_sc[...]
            for hh in range(2):
                dc_ref[0, :, hh:hh + 1] = -dc_sc[hh]

    qspec = pl.BlockSpec((bq, LANES), lambda p, s, it, jt: (it[s], p))
    kspec = pl.BlockSpec((bq, LANES), lambda p, s, it, jt: (jt[s], p))
    ktspec = pl.BlockSpec((LANES, bq), lambda p, s, it, jt: (p, jt[s]))
    rowq = pl.BlockSpec((1, 2, bq), lambda p, s, it, jt: (p, 0, it[s]))
    colk = pl.BlockSpec((1, bq, 2), lambda p, s, it, jt: (p, jt[s], 0))
    dqt_spec = pl.BlockSpec((1, nq, LANES, bq), lambda p, s, it, jt: (p, 0, 0, 0))
    dr_spec = pl.BlockSpec((1, nq, 2, bq), lambda p, s, it, jt: (p, 0, 0, 0))
    dqt, dk, dv, dc_keys, dc_queries = _pcall(
        body, name="fox_attention_bwd",
        grid_spec=pltpu.PrefetchScalarGridSpec(
            num_scalar_prefetch=2, grid=(pairs, len(tiles)),
            in_specs=[qspec, kspec, ktspec, kspec, qspec, rowq, rowq, rowq, colk],
            out_specs=[dqt_spec, kspec, kspec, colk, dr_spec],
            scratch_shapes=[pltpu.VMEM((bq, LANES), F32), pltpu.VMEM((bq, LANES), F32), pltpu.VMEM((2, bq, 1), F32)]),
        out_shape=[jax.ShapeDtypeStruct((pairs, nq, LANES, bq), F32), jax.ShapeDtypeStruct((t, FOX_WIDTH), F32),
                   jax.ShapeDtypeStruct((t, FOX_WIDTH), F32), jax.ShapeDtypeStruct((pairs, t, 2), F32),
                   jax.ShapeDtypeStruct((pairs, nq, 2, bq), F32)],
        compiler_params=_params("parallel", "arbitrary"),
    )(it, jt, qn, kn, knt, vb, do, lse_row, delta_row, crow, ccol)
    dq = jnp.transpose(dqt, (1, 3, 0, 2)).reshape(t, FOX_WIDTH)
    dc_keys = jnp.transpose(dc_keys, (0, 2, 1)).reshape(FOX_HEADS, t)
    dc_queries = jnp.transpose(dc_queries, (0, 2, 1, 3)).reshape(FOX_HEADS, t)
    return dq, dk, dv, dc_keys, dc_queries


def _fox_delta(dcat, fox, *, bt=512):
    t = fox.shape[0]
    w = FOX_WIDTH

    def body(do_ref, o_ref, dob_ref, dl_ref):
        dob = do_ref[...].astype(BF16)
        r = lax.broadcasted_iota(jnp.int32, (w, LANES), 0) // FOX_HEAD_DIM
        c = lax.broadcasted_iota(jnp.int32, (w, LANES), 1)
        dl_ref[...] = _dot_exact(dob.astype(F32) * o_ref[...].astype(F32), (r == c).astype(BF16))
        dob_ref[...] = dob

    blk = pl.BlockSpec((bt, w), lambda i: (i, 0))
    return _pcall(
        body, name="fox_delta", grid=(t // bt,), in_specs=[blk, blk],
        out_specs=[blk, pl.BlockSpec((bt, LANES), lambda i: (i, 0))],
        out_shape=[jax.ShapeDtypeStruct((t, w), BF16), jax.ShapeDtypeStruct((t, LANES), F32)],
        compiler_params=_params("parallel"),
    )(dcat, fox)


def _mixer_dz(z, dq, dk, dv, dpin, dfpad, gq, gk, *, bt=256):
    t = z.shape[0]
    w = FOX_WIDTH

    def body(q_ref, k_ref, dq_ref, dk_ref, dv_ref, dp_ref, df_ref, gq_ref, gk_ref, dz_ref, dgq_ref, dgk_ref):
        gm = _group_matrix(w, FOX_HEAD_DIM)
        first_step = pl.program_id(0) == 0
        for n, (x_ref, dy_ref, g_ref, dg_ref, scale) in enumerate(
                ((q_ref, dq_ref, gq_ref, dgq_ref, FOX_SCALE), (k_ref, dk_ref, gk_ref, dgk_ref, 1.0))):
            x = x_ref[...]
            r = lax.rsqrt(_dot_exact(x * x, gm) * (1.0 / FOX_HEAD_DIM) + EPS)
            xhat = x * r
            dy = dy_ref[...] * scale
            gy = dy * g_ref[...]
            dx = r * (gy - xhat * (_dot_exact(gy * xhat, gm) * (1.0 / FOX_HEAD_DIM)))
            dz_ref[:, n * w:(n + 1) * w] = dx.astype(BF16)
            part = jnp.sum(dy * xhat, axis=0, keepdims=True)

            @pl.when(first_step)
            def _():
                dg_ref[...] = part

            @pl.when(jnp.logical_not(first_step))
            def _():
                dg_ref[...] += part

        dz_ref[:, 2 * w:3 * w] = dv_ref[...].astype(BF16)
        dz_ref[:, 3 * w:4 * w] = dp_ref[...].astype(BF16)
        dz_ref[:, 4 * w:] = df_ref[...].astype(BF16)

    col = lambda c: pl.BlockSpec((bt, w), lambda i, c=c: (i, c))
    blk = pl.BlockSpec((bt, w), lambda i: (i, 0))
    vec = pl.BlockSpec((1, w), lambda i: (0, 0))
    dz, dgq, dgk = _pcall(
        body, name="mixer_dz", grid=(t // bt,),
        in_specs=[col(0), col(1), blk, blk, blk, blk, pl.BlockSpec((bt, LANES), lambda i: (i, 0)), vec, vec],
        out_specs=[pl.BlockSpec((bt, IN_COLS_PAD), lambda i: (i, 0)), vec, vec],
        out_shape=[jax.ShapeDtypeStruct((t, IN_COLS_PAD), BF16), jax.ShapeDtypeStruct((1, w), F32),
                   jax.ShapeDtypeStruct((1, w), F32)],
        compiler_params=_params("arbitrary"),
    )(z, z, dq, dk, dv, dpin, dfpad, jnp.tile(gq, FOX_HEADS).reshape(1, w), jnp.tile(gk, FOX_HEADS).reshape(1, w))
    return dz, dgq.reshape(FOX_HEADS, FOX_HEAD_DIM).sum(0), dgk.reshape(FOX_HEADS, FOX_HEAD_DIM).sum(0)


def _pool_fwd(z, wp, scale, *, bt=512):
    t = z.shape[0]
    w = POOL_WIDTH
    hb = bt // POOL_HALO

    def body(p_ref, h_ref, wp_ref, sc_ref, y_ref, mx_ref):
        i = pl.program_id(0)
        cur = p_ref[...]
        halo = jnp.where(i > 0, h_ref[...], 0.0)
        ext = jnp.concatenate([halo, cur], axis=0)
        trow = i * bt + lax.broadcasted_iota(jnp.int32, (bt, 1), 0)
        for g, win in enumerate(POOL_WINDOWS):
            sl = slice(g * LANES, (g + 1) * LANES)
            e = ext[:, sl]
            acc = e[POOL_HALO:]
            for k in range(1, win):
                acc = acc + pltpu.roll(e, k, 0)[POOL_HALO:]
            cnt = jnp.minimum(trow + 1, win).astype(F32)
            mixed = (acc / cnt - cur[:, sl]).astype(BF16)
            mx_ref[:, sl] = mixed
            y_ref[:, sl] = (_dot(mixed, wp_ref[g]) * sc_ref[:, sl]).astype(BF16)

    blk = pl.BlockSpec((bt, w), lambda i: (i, 0))
    return _pcall(
        body, name="pool_fwd", grid=(t // bt,),
        in_specs=[pl.BlockSpec((bt, w), lambda i: (i, 3)),
                  pl.BlockSpec((POOL_HALO, w), lambda i: (jnp.maximum(i * hb - 1, 0), 3)),
                  pl.BlockSpec((len(POOL_WINDOWS), LANES, LANES), lambda i: (0, 0, 0)),
                  pl.BlockSpec((1, w), lambda i: (0, 0))],
        out_specs=[blk, blk], out_shape=[jax.ShapeDtypeStruct((t, w), BF16)] * 2,
        compiler_params=_params("parallel"),
    )(z, z, wp, scale.reshape(1, w))


def _pool_bwd(dcat, mixed, wp, scale, *, bt=512):
    t = mixed.shape[0]
    w = POOL_WIDTH
    hb = bt // POOL_HALO
    nb = t // bt
    n_ext = bt + POOL_HALO

    def body(d_ref, h_ref, mx_ref, wp_ref, sc_ref, dp_ref, dwp_ref, dsc_ref):
        i = pl.program_id(0)
        cur = d_ref[...]
        nxt = jnp.where(i < nb - 1, h_ref[...], 0.0)
        ext = jnp.concatenate([cur, nxt], axis=0)
        trow = i * bt + lax.broadcasted_iota(jnp.int32, (n_ext, 1), 0)

        @pl.when(i == 0)
        def _():
            dwp_ref[...] = jnp.zeros(dwp_ref.shape, F32)
            dsc_ref[...] = jnp.zeros(dsc_ref.shape, F32)

        for g, win in enumerate(POOL_WINDOWS):
            sl = slice(g * LANES, (g + 1) * LANES)
            dy = (ext[:, sl] * sc_ref[:, sl]).astype(BF16)
            dm = _dot(dy, wp_ref[g], NT)
            mixed_g = mx_ref[:, sl]
            dsc_ref[:, sl] += jnp.sum(cur[:, sl] * _dot(mixed_g, wp_ref[g]), axis=0, keepdims=True)
            dwp_ref[g] += _dot(mixed_g, dy[:bt], TN)
            r = dm / jnp.minimum(trow + 1, win).astype(F32)
            acc = r[:bt]
            for k in range(1, win):
                acc = acc + pltpu.roll(r, n_ext - k, 0)[:bt]
            dp_ref[:, sl] = acc - dm[:bt]

    return _pcall(
        body, name="pool_bwd", grid=(nb,),
        in_specs=[pl.BlockSpec((bt, w), lambda i: (i, 1)),
                  pl.BlockSpec((POOL_HALO, w), lambda i: (jnp.minimum((i + 1) * hb, t // POOL_HALO - 1), 1)),
                  pl.BlockSpec((bt, w), lambda i: (i, 0)),
                  pl.BlockSpec((len(POOL_WINDOWS), LANES, LANES), lambda i: (0, 0, 0)),
                  pl.BlockSpec((1, w), lambda i: (0, 0))],
        out_specs=[pl.BlockSpec((bt, w), lambda i: (i, 0)),
                   pl.BlockSpec((len(POOL_WINDOWS), LANES, LANES), lambda i: (0, 0, 0)),
                   pl.BlockSpec((1, w), lambda i: (0, 0))],
        out_shape=[jax.ShapeDtypeStruct((t, w), F32), jax.ShapeDtypeStruct((len(POOL_WINDOWS), LANES, LANES), F32),
                   jax.ShapeDtypeStruct((1, w), F32)],
        compiler_params=_params("arbitrary"),
    )(dcat, dcat, mixed, wp, scale.reshape(1, w))


def _head_rms(x):
    return lax.rsqrt(jnp.mean(x * x, axis=-1, keepdims=True) + EPS)


def _mem_kv_fwd(mem, g_kv, w_kv, g_k):
    mlen, d = mem.shape

    def body(m_ref, g_ref, w_ref, gk_ref, mn_ref, mkv_ref, mk_ref, mv_ref):
        x = m_ref[...]
        mn = (x * lax.rsqrt(jnp.mean(x * x, axis=-1, keepdims=True) + EPS) * g_ref[...]).astype(BF16)
        mn_ref[...] = mn
        mkv = _dot(mn, w_ref[...])
        mkv_ref[...] = mkv
        for h in range(MEM_HEADS):
            sl = slice(h * MEM_HEAD_DIM, (h + 1) * MEM_HEAD_DIM)
            kh = mkv[:, sl]
            mk_ref[:, sl] = (kh * _head_rms(kh) * gk_ref[...]).astype(BF16)
        mv_ref[...] = mkv[:, MEM_WIDTH:].astype(BF16)

    return _pcall(
        body, name="mem_kv_fwd",
        out_shape=[jax.ShapeDtypeStruct((mlen, d), BF16), jax.ShapeDtypeStruct((mlen, 2 * MEM_WIDTH), F32),
                   jax.ShapeDtypeStruct((mlen, MEM_WIDTH), BF16), jax.ShapeDtypeStruct((mlen, MEM_WIDTH), BF16)],
        compiler_params=_params(),
    )(mem, g_kv.reshape(1, d), w_kv, g_k.reshape(1, MEM_HEAD_DIM))


def _mem_kv_bwd(dmk, dmv, mkv, mn, mem, g_kv, w_kv, g_k):
    mlen, d = mem.shape

    def body(dmk_ref, dmv_ref, mkv_ref, mn_ref, m_ref, g_ref, w_ref, gk_ref, dw_ref, dg_ref, dgk_ref, dkv_sc):
        dgk = jnp.zeros((1, MEM_HEAD_DIM), F32)
        for h in range(MEM_HEADS):
            sl = slice(h * MEM_HEAD_DIM, (h + 1) * MEM_HEAD_DIM)
            x = mkv_ref[:, sl]
            r = _head_rms(x)
            xhat = x * r
            dy = dmk_ref[:, sl]
            gy = dy * gk_ref[...]
            dkv_sc[:, sl] = (r * (gy - xhat * jnp.mean(gy * xhat, axis=-1, keepdims=True))).astype(BF16)
            dgk = dgk + jnp.sum(dy * xhat, axis=0, keepdims=True)
        dgk_ref[...] = dgk
        dkv_sc[:, MEM_WIDTH:] = dmv_ref[...].astype(BF16)
        dkv = dkv_sc[...]
        dw_ref[...] = _dot(mn_ref[...], dkv, TN)
        dmn = _dot(dkv, w_ref[...], NT)
        x = m_ref[...]
        xhat = x * lax.rsqrt(jnp.mean(x * x, axis=-1, keepdims=True) + EPS)
        dg_ref[...] = jnp.sum(dmn * xhat, axis=0, keepdims=True)

    dw, dg, dgk = _pcall(
        body, name="mem_kv_bwd",
        out_shape=[jax.ShapeDtypeStruct((d, 2 * MEM_WIDTH), F32), jax.ShapeDtypeStruct((1, d), F32),
                   jax.ShapeDtypeStruct((1, MEM_HEAD_DIM), F32)],
        scratch_shapes=[pltpu.VMEM((mlen, 2 * MEM_WIDTH), BF16)],
        compiler_params=_params(),
    )(dmk, dmv, mkv, mn, mem, g_kv.reshape(1, d), w_kv, g_k.reshape(1, MEM_HEAD_DIM))
    return dw, dg.reshape(d), dgk.reshape(MEM_HEAD_DIM)


def _cross_probs(x, g, mk_h):
    r = _head_rms(x)
    xhat = x * r
    qn = (xhat * g).astype(BF16)
    s = _dot(qn, mk_h, NT) * MEM_SCALE
    e = jnp.exp(s - jnp.max(s, axis=-1, keepdims=True))
    return r, xhat, qn, e / jnp.sum(e, axis=-1, keepdims=True)


def _cross_fwd(mq_raw, g_q, mk, mv, *, bt=512):
    t = mq_raw.shape[0]
    mlen = mk.shape[0]

    def body(x_ref, g_ref, mk_ref, mv_ref, o_ref):
        for h in range(MEM_HEADS):
            sl = slice(h * MEM_HEAD_DIM, (h + 1) * MEM_HEAD_DIM)
            _, _, _, p = _cross_probs(x_ref[:, sl], g_ref[...], mk_ref[:, sl])
            o_ref[:, sl] = _dot(p.astype(BF16), mv_ref[:, sl]).astype(BF16)

    blk = pl.BlockSpec((bt, MEM_WIDTH), lambda i: (i, 0))
    kv = pl.BlockSpec((mlen, MEM_WIDTH), lambda i: (0, 0))
    return _pcall(
        body, name="cross_attention_fwd", grid=(t // bt,),
        in_specs=[blk, pl.BlockSpec((1, MEM_HEAD_DIM), lambda i: (0, 0)), kv, kv], out_specs=blk,
        out_shape=jax.ShapeDtypeStruct((t, MEM_WIDTH), BF16), compiler_params=_params("parallel"),
    )(mq_raw, g_q.reshape(1, MEM_HEAD_DIM), mk, mv)


def _cross_bwd(mq_raw, dmo, g_q, mk, mv, *, bt=512):
    t = mq_raw.shape[0]
    mlen = mk.shape[0]

    def body(x_ref, do_ref, g_ref, mk_ref, mv_ref, dx_ref, dmk_ref, dmv_ref, dg_ref):
        @pl.when(pl.program_id(0) == 0)
        def _():
            dmk_ref[...] = jnp.zeros(dmk_ref.shape, F32)
            dmv_ref[...] = jnp.zeros(dmv_ref.shape, F32)
            dg_ref[...] = jnp.zeros(dg_ref.shape, F32)

        for h in range(MEM_HEADS):
            sl = slice(h * MEM_HEAD_DIM, (h + 1) * MEM_HEAD_DIM)
            r, xhat, qn, p = _cross_probs(x_ref[:, sl], g_ref[...], mk_ref[:, sl])
            do = do_ref[:, sl]
            dp = _dot(do, mv_ref[:, sl], NT)
            ds = (p * (dp - jnp.sum(p * dp, axis=-1, keepdims=True)) * MEM_SCALE).astype(BF16)
            dmv_ref[:, sl] += _dot(p.astype(BF16), do, TN)
            dmk_ref[:, sl] += _dot(ds, qn, TN)
            dqn = _dot(ds, mk_ref[:, sl])
            gy = dqn * g_ref[...]
            dx_ref[:, sl] = (r * (gy - xhat * jnp.mean(gy * xhat, axis=-1, keepdims=True))).astype(BF16)
            dg_ref[...] += jnp.sum(dqn * xhat, axis=0, keepdims=True)

    blk = pl.BlockSpec((bt, MEM_WIDTH), lambda i: (i, 0))
    kv = pl.BlockSpec((mlen, MEM_WIDTH), lambda i: (0, 0))
    gs = pl.BlockSpec((1, MEM_HEAD_DIM), lambda i: (0, 0))
    dx, dmk, dmv, dg = _pcall(
        body, name="cross_attention_bwd", grid=(t // bt,),
        in_specs=[blk, blk, gs, kv, kv], out_specs=[blk, kv, kv, gs],
        out_shape=[jax.ShapeDtypeStruct((t, MEM_WIDTH), BF16), jax.ShapeDtypeStruct((mlen, MEM_WIDTH), F32),
                   jax.ShapeDtypeStruct((mlen, MEM_WIDTH), F32), jax.ShapeDtypeStruct((1, MEM_HEAD_DIM), F32)],
        compiler_params=_params("arbitrary"),
    )(mq_raw, dmo, g_q.reshape(1, MEM_HEAD_DIM), mk, mv)
    return dx, dmk, dmv, dg.reshape(MEM_HEAD_DIM)


def _loss_head(y, target, *, bt=512):
    t, d = y.shape

    def body(y_ref, t_ref, dy_ref, l_ref):
        e = y_ref[...] - t_ref[...]
        dy_ref[...] = e * (1.0 / d)
        part = (0.5 / d) * jnp.sum(jnp.sum(e * e, axis=1, keepdims=True), axis=0, keepdims=True)

        @pl.when(pl.program_id(0) == 0)
        def _():
            l_ref[...] = part

        @pl.when(pl.program_id(0) > 0)
        def _():
            l_ref[...] += part

    blk = pl.BlockSpec((bt, d), lambda i: (i, 0))
    dy, loss = _pcall(
        body, name="loss_head", grid=(t // bt,), in_specs=[blk, blk],
        out_specs=[blk, pl.BlockSpec((1, 1), lambda i: (0, 0))],
        out_shape=[jax.ShapeDtypeStruct((t, d), F32), jax.ShapeDtypeStruct((1, 1), F32)],
        compiler_params=_params("arbitrary"),
    )(y, target)
    return loss, dy


def _row_tile(rows, cols, budget=1 << 19):
    best = None
    for cand in range(8, rows + 1, 8):
        if rows % cand == 0 and cand * cols <= budget:
            best = cand
    return best or rows


def _adamw(w, g, m, v, *, name):
    rows, cols = w.shape
    bt = _row_tile(rows, cols)
    c1 = 1.0 - ADAM_B1 ** ADAM_STEP
    c2 = 1.0 - ADAM_B2 ** ADAM_STEP

    def body(w_ref, g_ref, m_ref, v_ref, d_ref, nm_ref, nv_ref):
        g_v = g_ref[...]
        nm = ADAM_B1 * m_ref[...] + (1.0 - ADAM_B1) * g_v
        nv = ADAM_B2 * v_ref[...] + (1.0 - ADAM_B2) * (g_v * g_v)
        nm_ref[...] = nm
        nv_ref[...] = nv
        d_ref[...] = -ADAM_LR * ((nm / c1) / (jnp.sqrt(nv / c2) + ADAM_EPS) + ADAM_WD * w_ref[...])

    blk = pl.BlockSpec((bt, cols), lambda i: (i, 0))
    return _pcall(
        body, name=name, grid=(rows // bt,), in_specs=[blk] * 4, out_specs=[blk] * 3,
        out_shape=[jax.ShapeDtypeStruct((rows, cols), F32)] * 3, compiler_params=_params("parallel"),
    )(w, g, m, v)


def _sum_slots(x, *, name):
    n, rows, cols = x.shape
    bt = _row_tile(rows, cols, budget=1 << 17)

    def body(x_ref, o_ref):
        acc = x_ref[0].astype(F32)
        for s in range(1, n):
            acc = acc + x_ref[s].astype(F32)
        o_ref[...] = acc

    return _pcall(
        body, name=name, grid=(rows // bt,), in_specs=[pl.BlockSpec((n, bt, cols), lambda i: (0, i, 0))],
        out_specs=pl.BlockSpec((bt, cols), lambda i: (i, 0)),
        out_shape=jax.ShapeDtypeStruct((rows, cols), F32), compiler_params=_params("parallel"),
    )(x)


def _any_spec():
    return pl.BlockSpec(memory_space=pl.ANY)


def _all_gather(xs, *, name):
    n = len(xs)

    def body(*refs):
        x_refs, out_refs = refs[:n], refs[n:2 * n]
        send_sems, recv_sems, local_sems = refs[2 * n:]
        x, y, c = lax.axis_index("x"), lax.axis_index("y"), lax.axis_index("c")
        me, sibling = (x, y, c), (x, y, 1 - c)
        chips = [(1 - x, y), (x, 1 - y), (1 - x, 1 - y)]

        def slot(a, px, py, pc):
            return out_refs[a].at[4 * px + 2 * py + pc]

        def copy(a, k, block, to, src=None):
            return pltpu.make_async_remote_copy(
                src_ref=slot(a, *block) if src is None else src, dst_ref=slot(a, *block),
                send_sem=send_sems.at[a, k], recv_sem=recv_sems.at[a, k], device_id=to, device_id_type=MESH_ID)

        mine = [pltpu.make_async_copy(x_refs[a], slot(a, *me), local_sems.at[a]) for a in range(n)]
        for cp in mine:
            cp.start()
        first = []
        for j, chip in enumerate(chips):
            first += [copy(a, 1 + j, me, (*chip, c), src=x_refs[a]) for a in range(n)]
        first += [copy(a, 0, me, sibling, src=x_refs[a]) for a in range(n)]
        for cp in first:
            cp.start()
        passed = []
        for j, chip in enumerate(chips):
            for a in range(n):
                copy(a, 1 + j, (*chip, c), me).wait_recv()
                cp = copy(a, 4 + j, (*chip, c), sibling)
                cp.start()
                passed.append(cp)
        for a in range(n):
            copy(a, 0, sibling, me).wait_recv()
        for j, chip in enumerate(chips):
            for a in range(n):
                copy(a, 4 + j, (*chip, 1 - c), me).wait_recv()
        for cp in first + passed:
            cp.wait_send()
        for cp in mine:
            cp.wait()

    return _pcall(
        body, name=name, in_specs=[_any_spec()] * n, out_specs=[_any_spec()] * n,
        out_shape=[jax.ShapeDtypeStruct((N_DEV,) + x.shape, x.dtype) for x in xs],
        scratch_shapes=[pltpu.SemaphoreType.DMA((n, 7)), pltpu.SemaphoreType.DMA((n, 7)), pltpu.SemaphoreType.DMA((n,))],
    )(*xs)


def _exchange(xs, *, name):
    n = len(xs)

    def body(*refs):
        x_refs, out_refs = refs[:n], refs[n:2 * n]
        send_sems, recv_sems, local_sems = refs[2 * n:]
        x, y, c = lax.axis_index("x"), lax.axis_index("y"), lax.axis_index("c")
        me_idx = 4 * x + 2 * y + c

        def peer(k):
            px = x ^ ((k >> 2) & 1)
            py = y ^ ((k >> 1) & 1)
            pc = c ^ (k & 1)
            return (px, py, pc), 4 * px + 2 * py + pc

        def copy(a, k):
            to, to_idx = peer(k)
            return pltpu.make_async_remote_copy(
                src_ref=x_refs[a].at[to_idx], dst_ref=out_refs[a].at[me_idx],
                send_sem=send_sems.at[a, k - 1], recv_sem=recv_sems.at[a, k - 1], device_id=to, device_id_type=MESH_ID)

        def landing(a, k):
            frm, frm_idx = peer(k)
            return pltpu.make_async_remote_copy(
                src_ref=x_refs[a].at[frm_idx], dst_ref=out_refs[a].at[frm_idx],
                send_sem=send_sems.at[a, k - 1], recv_sem=recv_sems.at[a, k - 1], device_id=frm, device_id_type=MESH_ID)

        mine = [pltpu.make_async_copy(x_refs[a].at[me_idx], out_refs[a].at[me_idx], local_sems.at[a]) for a in range(n)]
        for cp in mine:
            cp.start()
        sends = [copy(a, k) for k in (2, 4, 6, 3, 5, 7, 1) for a in range(n)]
        for cp in sends:
            cp.start()
        for k in range(1, N_DEV):
            for a in range(n):
                landing(a, k).wait_recv()
        for cp in sends:
            cp.wait_send()
        for cp in mine:
            cp.wait()

    return _pcall(
        body, name=name, in_specs=[_any_spec()] * n, out_specs=[_any_spec()] * n,
        out_shape=[jax.ShapeDtypeStruct(x.shape, x.dtype) for x in xs],
        scratch_shapes=[pltpu.SemaphoreType.DMA((n, 7)), pltpu.SemaphoreType.DMA((n, 7)), pltpu.SemaphoreType.DMA((n,))],
    )(*xs)


def _mesh_peer(k):
    px = lax.axis_index("x") ^ ((k >> 2) & 1)
    py = lax.axis_index("y") ^ ((k >> 1) & 1)
    pc = lax.axis_index("c") ^ (k & 1)
    return (px, py, pc), 4 * px + 2 * py + pc


def _my_index():
    return 4 * lax.axis_index("x") + 2 * lax.axis_index("y") + lax.axis_index("c")


def _landing_zones(xs, blocks):
    me = _my_index()
    lands = []
    for x in xs:
        own = lax.dynamic_index_in_dim(x, me, 0, keepdims=True) if blocks else x[None]
        zone = lax.empty((N_DEV,) + own.shape[1:], x.dtype)
        lands.append(lax.dynamic_update_slice(zone, own, (me,) + (0,) * (own.ndim - 1)))
    return lands


def _send_start(xs, lands, *, blocks, name, after=None):
    n = len(xs)
    peers = N_DEV - 1
    first_out = 2 * n + (after is not None)

    def body(*refs):
        x_refs, land_refs = refs[:n], refs[n:2 * n]
        send_sems, recv_sems = refs[first_out:first_out + peers], refs[first_out + peers:first_out + 2 * peers]
        token = refs[-1]
        me_idx = _my_index()
        for k in (2, 4, 6, 3, 5, 7, 1):
            to, to_idx = _mesh_peer(k)
            for a in range(n):
                pltpu.make_async_remote_copy(
                    src_ref=x_refs[a].at[to_idx] if blocks else x_refs[a], dst_ref=land_refs[a].at[me_idx],
                    send_sem=send_sems[k - 1], recv_sem=recv_sems[k - 1], device_id=to, device_id_type=MESH_ID).start()
        token[...] = jnp.zeros(token.shape, token.dtype)

    hbm = pl.BlockSpec(memory_space=pltpu.HBM)
    sem = pl.BlockSpec(memory_space=pltpu.SEMAPHORE)
    both = list(xs) + list(lands)
    out = _pcall(
        body, name=name,
        out_shape=(*[pltpu.SemaphoreType.DMA(())] * (2 * peers), *[pltpu.HBM(a.shape, a.dtype) for a in both],
                   jax.ShapeDtypeStruct((8, LANES), F32)),
        in_specs=[hbm] * (2 * n) + ([pl.BlockSpec(memory_space=pl.ANY)] if after is not None else []),
        out_specs=(*[sem] * (2 * peers), *[hbm] * (2 * n), pl.BlockSpec(memory_space=pltpu.VMEM)),
        input_output_aliases={i: 2 * peers + i for i in range(2 * n)},
        compiler_params=pltpu.CompilerParams(has_side_effects=pltpu.SideEffectType.DATAFLOW_SIDE_EFFECTING),
    )(*[pltpu.with_memory_space_constraint(a, pltpu.HBM) for a in both], *((after,) if after is not None else ()))
    return dict(sems=out[:2 * peers], xs=out[2 * peers:2 * peers + n], lands=out[2 * peers + n:2 * peers + 2 * n],
                token=out[-1], blocks=blocks)


def _send_wait(started, after, *, name):
    n = len(started['xs'])
    blocks = started['blocks']
    peers = N_DEV - 1

    def body(*refs):
        x_refs, land_refs = refs[:n], refs[n:2 * n]
        send_sems, recv_sems = refs[2 * n:2 * n + peers], refs[2 * n + peers:2 * n + 2 * peers]
        for k in range(1, N_DEV):
            frm, frm_idx = _mesh_peer(k)
            for a in range(n):
                copy = pltpu.make_async_remote_copy(
                    src_ref=x_refs[a].at[frm_idx] if blocks else x_refs[a], dst_ref=land_refs[a].at[frm_idx],
                    send_sem=send_sems[k - 1], recv_sem=recv_sems[k - 1], device_id=frm, device_id_type=MESH_ID)
                copy.wait_send()
                copy.wait_recv()

    hbm = pl.BlockSpec(memory_space=pltpu.HBM)
    sem = pl.BlockSpec(memory_space=pltpu.SEMAPHORE)
    both = list(started['xs']) + list(started['lands'])
    out = _pcall(
        body, name=name, out_shape=[pltpu.HBM(a.shape, a.dtype) for a in both],
        in_specs=[hbm] * (2 * n) + [sem] * (2 * peers) + [pl.BlockSpec(memory_space=pl.ANY)], out_specs=[hbm] * (2 * n),
        input_output_aliases={i: i for i in range(2 * n)},
        compiler_params=pltpu.CompilerParams(has_side_effects=pltpu.SideEffectType.DATAFLOW_SIDE_EFFECTING),
    )(*both, *started['sems'], after)
    return out[n:]


COLUMN_SHARDED = ('w_in', 'w_mem_out', 'w_gate_up')


def _full_weight(name, gathered, l):
    g = gathered[:, l]
    if name in COLUMN_SHARDED:
        g = jnp.transpose(g, (1, 0, 2))
        g = g.reshape(g.shape[0], -1)
    else:
        g = g.reshape(-1, g.shape[-1])
    return jnp.pad(g, ((0, 0), (0, IN_COLS_PAD - IN_COLS))) if name == 'w_in' else g


def _grad_blocks(name, g):
    if name == 'w_gate_up':
        rows, cols = g.shape[1], 2 * g.shape[2] // N_DEV
        g = jnp.transpose(g.reshape(2, rows, N_DEV // 2, cols), (0, 2, 1, 3))
        return g.reshape(N_DEV, rows, cols).astype(BF16)
    if name == 'w_in':
        g = g[:, :IN_COLS]
    if name in COLUMN_SHARDED:
        rows, cols = g.shape[0], g.shape[1] // N_DEV
        g = jnp.transpose(g.reshape(rows, N_DEV, cols), (1, 0, 2))
    else:
        rows, cols = g.shape[0] // N_DEV, g.shape[1]
        g = g.reshape(N_DEV, rows, cols)
    return g.astype(BF16)


SMALL_SHAPES = {'g_mix': (DEPTH, D_MODEL), 'b_forget': (DEPTH, FOX_HEADS), 'g_q_fox': (DEPTH, FOX_HEAD_DIM),
                'g_k_fox': (DEPTH, FOX_HEAD_DIM), 'w_pool': (DEPTH, 4, POOL_GROUP_DIM, POOL_GROUP_DIM),
                'pool_scale': (DEPTH, POOL_WIDTH), 'g_mem_q': (DEPTH, D_MODEL), 'g_mem_kv': (DEPTH, D_MODEL),
                'g_q_mem': (DEPTH, MEM_HEAD_DIM), 'g_k_mem': (DEPTH, MEM_HEAD_DIM), 'g_ffn': (DEPTH, D_MODEL)}


def _small_rows(name):
    return -(-int(np.prod(SMALL_SHAPES[name])) // LANES)


SMALL_ROWS = -(-sum(_small_rows(n) for n in SMALL) // 8) * 8


def _pack_small(tree):
    parts = []
    for n in SMALL:
        flat = tree[n].reshape(-1).astype(F32)
        parts.append(jnp.pad(flat, (0, _small_rows(n) * LANES - flat.shape[0])))
    flat = jnp.concatenate(parts)
    return jnp.pad(flat, (0, SMALL_ROWS * LANES - flat.shape[0])).reshape(SMALL_ROWS, LANES)


def _unpack_small(packed):
    flat = packed.reshape(-1)
    out, at = {}, 0
    for n in SMALL:
        size = int(np.prod(SMALL_SHAPES[n]))
        out[n] = flat[at:at + size].reshape(SMALL_SHAPES[n])
        at += _small_rows(n) * LANES
    return out


def _pairs_cols(a):
    t = a.shape[0]
    return jnp.transpose(a.reshape(t, FOX_HEADS // 2, 2), (1, 0, 2))


def _pairs_rows(a):
    return a.reshape(FOX_HEADS // 2, 2, a.shape[1])


def _layer_fwd(h0, mem, p, w_in, other_weights):
    s = {'h0': h0}
    s['xn1'], z, f = _norm_matmul(h0, p['g_mix'], w_in, tn=IN_COLS_PAD, tail=LANES, name="norm_in_proj_fwd")
    s['z'] = z
    s['qn'], s['kn'], s['vb'] = _qkv_prep(z, p['g_q_fox'], p['g_k_fox'])
    s['ft'] = jnp.transpose(f[:, :FOX_HEADS])
    c = _forget_cumsum(s['ft'], p['b_forget'])
    s['ccol'], s['crow'] = _pairs_cols(jnp.transpose(c)), _pairs_rows(c)
    s['qnt'] = jnp.transpose(s['qn'])
    s['fox'], s['lse_row'] = _fox_fwd(s['qnt'], s['kn'], jnp.transpose(s['vb']), s['crow'], s['ccol'])
    pool, s['mixed'] = _pool_fwd(z, p['w_pool'].astype(BF16), p['pool_scale'])
    s['cat'] = jnp.concatenate([s['fox'], pool], axis=1)
    w = dict(other_weights(s['lse_row']), w_in=w_in)
    h1 = _matmul(s['cat'], w['w_out'], res=h0, name="out_proj_fwd")
    s['h1'] = h1

    s['hn2'], s['mq_raw'] = _norm_matmul(h1, p['g_mem_q'], w['w_mem_q'], name="norm_mem_q_fwd")
    s['mn'], s['mkv'], s['mk'], s['mv'] = _mem_kv_fwd(mem, p['g_mem_kv'], w['w_mem_kv'], p['g_k_mem'])
    s['mo'] = _cross_fwd(s['mq_raw'], p['g_q_mem'], s['mk'], s['mv'])
    h2 = _matmul(s['mo'], w['w_mem_out'], res=h1, name="mem_out_fwd")
    s['h2'] = h2

    s['hn3'], s['gu'], s['act'] = _norm_gate_up_swiglu(h2, p['g_ffn'], w['w_gate_up'])
    h3 = _matmul(s['act'], w['w_down'], res=h2, name="down_fwd")
    return h3, s, w


def _layer_bwd(dh, mem, p, w, s, after=None, at_mixer=None):
    g = {}
    g['w_down'] = _matmul(s['act'], dh, ta=True, tm=1408, tn=512, tk=1024, name="down_dw")
    dgu = _down_dx_swiglu_bwd(dh, w['w_down'], s['gu'], after=after)
    g['w_gate_up'] = _matmul(s['hn3'], dgu, ta=True, tm=1024, tn=1408, tk=1024, name="gate_up_dw")
    dh, g['g_ffn'] = _matmul_norm_bwd(dgu, w['w_gate_up'], s['h2'], p['g_ffn'], dh, name="gate_up_dx_norm_bwd")

    g['w_mem_out'] = _matmul(s['mo'], dh, ta=True, tm=512, tn=1024, tk=1024, name="mem_out_dw")
    dmo = _matmul(dh, w['w_mem_out'], tb=True, out_dtype=BF16, name="mem_out_dx")
    dmq, dmk, dmv, g['g_q_mem'] = _cross_bwd(s['mq_raw'], dmo, p['g_q_mem'], s['mk'], s['mv'])
    g['w_mem_kv'], g['g_mem_kv'], g['g_k_mem'] = _mem_kv_bwd(dmk, dmv, s['mkv'], s['mn'], mem, p['g_mem_kv'],
                                                               w['w_mem_kv'], p['g_k_mem'])
    g['w_mem_q'] = _matmul(s['hn2'], dmq, ta=True, tm=1024, tn=512, tk=1024, name="mem_q_dw")
    dh, g['g_mem_q'] = _matmul_norm_bwd(dmq, w['w_mem_q'], s['h1'], p['g_mem_q'], dh, name="mem_q_dx_norm_bwd")

    g['w_out'] = _matmul(s['cat'], dh, ta=True, tm=1024, tn=512, tk=1024, name="out_proj_dw")
    after = at_mixer(g) if at_mixer is not None else None
    dcat = _matmul(dh, w['w_out'], tb=True, after=after, tn=1024, name="out_proj_dx")
    dpin, g['w_pool'], dscale = _pool_bwd(dcat, s['mixed'], p['w_pool'].astype(BF16), p['pool_scale'])
    g['pool_scale'] = dscale.reshape(POOL_WIDTH)
    do, delta = _fox_delta(dcat, s['fox'])
    delta_row = _pairs_rows(jnp.transpose(delta[:, :FOX_HEADS]))
    dq, dk, dv, dc_keys, dc_queries = _fox_bwd(s['qn'], s['kn'], jnp.transpose(s['kn']), s['vb'], do, s['lse_row'],
                                               delta_row, s['crow'], s['ccol'])
    dft, db = _forget_cumsum_bwd(dc_keys, dc_queries, s['ft'], p['b_forget'])
    g['b_forget'] = db.reshape(FOX_HEADS)
    dfpad = jnp.pad(jnp.transpose(dft), ((0, 0), (0, LANES - FOX_HEADS)))
    dz, g['g_q_fox'], g['g_k_fox'] = _mixer_dz(s['z'], dq, dk, dv, dpin, dfpad, p['g_q_fox'], p['g_k_fox'])
    g['w_in'] = _matmul(s['xn1'], dz, ta=True, tm=512, tn=IN_COLS_PAD, tk=1024, name="in_proj_dw")
    dh, g['g_mix'] = _matmul_norm_bwd(dz, w['w_in'], s['h0'], p['g_mix'], dh, name="in_proj_dx_norm_bwd")
    return dh, g


def _local_step(x2, mem2, target2, small, w_in, other_weights, send_grads):
    h = x2
    saved, full = [], []
    for l in range(DEPTH):
        h, s, w = _layer_fwd(h, mem2, {k: v[l] for k, v in small.items()}, w_in[l], other_weights(l))
        saved.append(s)
        full.append(w)
    loss, dh = _loss_head(h, target2)
    after = None
    grads = [None] * DEPTH
    for l in reversed(range(1, DEPTH)):
        dh, grads[l] = _layer_bwd(dh, mem2, {k: v[l] for k, v in small.items()}, full[l], saved[l], after=after)
        after = send_grads(l, BIG, grads[l])
    dh, grads[0] = _layer_bwd(dh, mem2, {k: v[0] for k, v in small.items()}, full[0], saved[0], after=after,
                              at_mixer=lambda g: send_grads(0, SENT_AT_MIXER, g))
    return loss, dh, grads


def kernel(x, mem, g_mix, w_in, b_forget, g_q_fox, g_k_fox, w_pool, pool_scale, w_out, g_mem_q, g_mem_kv, w_mem_q, w_mem_kv, g_q_mem, g_k_mem, w_mem_out, g_ffn, w_gate_up, w_down, loss_target, m_g_mix, m_w_in, m_b_forget, m_g_q_fox, m_g_k_fox, m_w_pool, m_pool_scale, m_w_out, m_g_mem_q, m_g_mem_kv, m_w_mem_q, m_w_mem_kv, m_g_q_mem, m_g_k_mem, m_w_mem_out, m_g_ffn, m_w_gate_up, m_w_down, v_g_mix, v_w_in, v_b_forget, v_g_q_fox, v_g_k_fox, v_w_pool, v_pool_scale, v_w_out, v_g_mem_q, v_g_mem_kv, v_w_mem_q, v_w_mem_kv, v_g_q_mem, v_g_k_mem, v_w_mem_out, v_g_ffn, v_w_gate_up, v_w_down):
    weights = dict(g_mix=g_mix, w_in=w_in, b_forget=b_forget, g_q_fox=g_q_fox, g_k_fox=g_k_fox, w_pool=w_pool,
                   pool_scale=pool_scale, w_out=w_out, g_mem_q=g_mem_q, g_mem_kv=g_mem_kv, w_mem_q=w_mem_q,
                   w_mem_kv=w_mem_kv, g_q_mem=g_q_mem, g_k_mem=g_k_mem, w_mem_out=w_mem_out, g_ffn=g_ffn,
                   w_gate_up=w_gate_up, w_down=w_down)
    mom_m = dict(g_mix=m_g_mix, w_in=m_w_in, b_forget=m_b_forget, g_q_fox=m_g_q_fox, g_k_fox=m_g_k_fox, w_pool=m_w_pool,
                 pool_scale=m_pool_scale, w_out=m_w_out, g_mem_q=m_g_mem_q, g_mem_kv=m_g_mem_kv, w_mem_q=m_w_mem_q,
                 w_mem_kv=m_w_mem_kv, g_q_mem=m_g_q_mem, g_k_mem=m_g_k_mem, w_mem_out=m_w_mem_out, g_ffn=m_g_ffn,
                 w_gate_up=m_w_gate_up, w_down=m_w_down)
    mom_v = dict(g_mix=v_g_mix, w_in=v_w_in, b_forget=v_b_forget, g_q_fox=v_g_q_fox, g_k_fox=v_g_k_fox, w_pool=v_w_pool,
                 pool_scale=v_pool_scale, w_out=v_w_out, g_mem_q=v_g_mem_q, g_mem_kv=v_g_mem_kv, w_mem_q=v_w_mem_q,
                 w_mem_kv=v_w_mem_kv, g_q_mem=v_g_q_mem, g_k_mem=v_g_k_mem, w_mem_out=v_w_mem_out, g_ffn=v_g_ffn,
                 w_gate_up=v_w_gate_up, w_down=v_w_down)

    shards = {n: weights[n].astype(BF16) for n in BIG}
    (w_in_all,) = _all_gather([shards['w_in']], name="w_in_all_gather")
    later = [n for n in BIG if n != 'w_in']
    weights_sent = _send_start([shards[n] for n in later], _landing_zones([shards[n] for n in later], False),
                               blocks=False, name="weights_send_start", after=w_in_all)
    small = {n: weights[n] for n in SMALL}
    small['g_mix'] = small['g_mix'] + weights_sent['token'][0, 0]
    gathered = {}

    def other_weights(l):
        def get(after):
            if not gathered:
                gathered.update(zip(later, _send_wait(weights_sent, after, name="weights_send_wait")))
            return {n: _full_weight(n, gathered[n], l) for n in later}
        return get

    grads_sent = []

    def send_grads(l, names, g):
        blocks = [_grad_blocks(n, g[n]) for n in names]
        sent = _send_start(blocks, _landing_zones(blocks, True), blocks=True, name=f"grads{l}_send_start")
        grads_sent.append((l, names, sent))
        return sent['token']

    w_in_full = [_full_weight('w_in', w_in_all, l) for l in range(DEPTH)]
    loss_part, grad_x, grads = _local_step(x[0], mem[0], loss_target[0], small, w_in_full, other_weights, send_grads)
    loss = lax.psum(loss_part[0, 0], ("x", "y", "c"))

    last = [n for n in BIG if n not in SENT_AT_MIXER]
    landed = {(0, n): a for n, a in zip(last, _exchange([_grad_blocks(n, grads[0][n]) for n in last],
                                                         name="grads0_exchange"))}
    for l, names, sent in grads_sent:
        landed.update({(l, n): a for n, a in zip(names, _send_wait(sent, grad_x, name=f"grads{l}_send_wait"))})
    grad = {n: jnp.concatenate([_sum_slots(landed[l, n], name="grad_sum_" + n) for l in range(DEPTH)], 0) for n in BIG}
    small_part = _pack_small({n: jnp.stack([grads[l][n] for l in range(DEPTH)], 0) for n in SMALL})
    (small_all,) = _all_gather([small_part], name="small_grads_all_gather")
    small_sum = _sum_slots(small_all, name="grad_sum_small")

    delta, new_m, new_v = {}, {}, {}
    for n in BIG:
        shape = weights[n].shape
        two_d = lambda a: a.reshape(shape[0] * shape[1], shape[2])
        d, nm, nv = _adamw(two_d(weights[n]), grad[n], two_d(mom_m[n]), two_d(mom_v[n]), name="adamw_" + n)
        grad[n], delta[n], new_m[n], new_v[n] = (a.reshape(shape) for a in (grad[n], d, nm, nv))
    d, nm, nv = _adamw(_pack_small(weights), small_sum, _pack_small(mom_m), _pack_small(mom_v), name="adamw_small")
    grad.update(_unpack_small(small_sum))
    delta.update(_unpack_small(d))
    new_m.update(_unpack_small(nm))
    new_v.update(_unpack_small(nv))

    return (loss, grad_x[None], *[grad[n] for n in WEIGHTS], *[delta[n] for n in WEIGHTS],
            *[new_m[n] for n in WEIGHTS], *[new_v[n] for n in WEIGHTS])
```

```python
import functools

import numpy as np
import jax
import jax.numpy as jnp
from jax import lax
from jax.experimental import pallas as pl
from jax.experimental.pallas import tpu as pltpu

F32 = jnp.float32
BF16 = jnp.bfloat16

N_DEV = 8
D_MODEL = 1024
DEPTH = 2
FOX_HEADS = 8
FOX_HEAD_DIM = 64
FOX_WIDTH = 512
POOL_WIDTH = 512
POOL_WINDOWS = (2, 4, 8, 16)
POOL_GROUP_DIM = 128
POOL_HALO = 16
IN_COLS = 2056
IN_COLS_PAD = 2176
MEM_HEADS = 4
MEM_HEAD_DIM = 128
MEM_WIDTH = 512
D_FF = 2816
EPS = 1e-6
FOX_SCALE = FOX_HEAD_DIM ** -0.5
LOG2E = 1.4426950408889634
LN2 = 0.6931471805599453
FOX_ACC_ROWS = FOX_HEAD_DIM + 16
MEM_SCALE = MEM_HEAD_DIM ** -0.5
LANES = 128

ADAM_LR = 0.001
ADAM_B1 = 0.9
ADAM_B2 = 0.999
ADAM_EPS = 1e-08
ADAM_WD = 0.01
ADAM_STEP = 10

VMEM_LIMIT = 56 * 1024 * 1024
MESH_ID = pl.DeviceIdType.MESH

WEIGHTS = ['g_mix', 'w_in', 'b_forget', 'g_q_fox', 'g_k_fox', 'w_pool', 'pool_scale', 'w_out', 'g_mem_q', 'g_mem_kv',
           'w_mem_q', 'w_mem_kv', 'g_q_mem', 'g_k_mem', 'w_mem_out', 'g_ffn', 'w_gate_up', 'w_down']
BIG = ['w_in', 'w_out', 'w_mem_q', 'w_mem_kv', 'w_mem_out', 'w_gate_up', 'w_down']
SMALL = [n for n in WEIGHTS if n not in BIG]
SENT_AT_MIXER = ['w_down', 'w_gate_up', 'w_mem_out', 'w_mem_q', 'w_mem_kv', 'w_out']


def _pcall(body, **kw):
    return pl.pallas_call(body, **kw)


def _params(*sem):
    return pltpu.CompilerParams(dimension_semantics=sem or None, vmem_limit_bytes=VMEM_LIMIT)


def _dot(a, b, dims=None):
    if dims is None:
        return jnp.dot(a, b, preferred_element_type=F32)
    return lax.dot_general(a, b, (dims, ((), ())), preferred_element_type=F32)


NT = ((1,), (1,))
TN = ((0,), (0,))


def _dot_exact(x, ones_bf16):
    hi = x.astype(BF16)
    r1 = x - hi.astype(F32)
    mid = r1.astype(BF16)
    lo = (r1 - mid.astype(F32)).astype(BF16)
    return _dot(hi, ones_bf16) + _dot(mid, ones_bf16) + _dot(lo, ones_bf16)


def _matmul(a, b, *, ta=False, tb=False, out_dtype=F32, res=None, after=None, tm=1024, tn=512, tk=None, name):
    planes = b.shape[0] if b.ndim == 3 else None
    bshape = b.shape[-2:]
    m, k = (a.shape[1], a.shape[0]) if ta else a.shape
    n = bshape[0] if tb else bshape[1]
    assert k == (bshape[1] if tb else bshape[0])
    tm, tn = min(tm, m), min(tn, n)
    tk = min(tk or k, k)
    assert m % tm == 0 and n % tn == 0 and k % tk == 0, (name, m, n, k, tm, tn, tk)
    nk = k // tk
    dims = ((0 if ta else 1,), (1 if tb else 0,))

    def body(*refs):
        a_ref, b_ref = refs[0], refs[1]
        r_ref = refs[2] if res is not None else None
        o_ref = refs[2 + (res is not None) + (after is not None)]
        part = _dot(a_ref[...].astype(BF16), b_ref[...].astype(BF16), dims)

        def finish(acc):
            if r_ref is not None:
                acc = acc + r_ref[...]
            o_ref[...] = acc.astype(o_ref.dtype)

        if nk == 1:
            finish(part)
        else:
            acc_ref = refs[-1]
            kk = pl.program_id(3)

            @pl.when(kk == 0)
            def _():
                acc_ref[...] = part

            @pl.when(kk > 0)
            def _():
                acc_ref[...] += part

            @pl.when(kk == nk - 1)
            def _():
                finish(acc_ref[...])

    lead = (lambda p: (p,)) if planes else (lambda p: ())
    sq = (None,) if planes else ()
    a_spec = pl.BlockSpec((tk, tm), lambda p, i, j, kk: (kk, i)) if ta else pl.BlockSpec((tm, tk), lambda p, i, j, kk: (i, kk))
    b_spec = (pl.BlockSpec(sq + (tn, tk), lambda p, i, j, kk: lead(p) + (j, kk)) if tb
              else pl.BlockSpec(sq + (tk, tn), lambda p, i, j, kk: lead(p) + (kk, j)))
    o_spec = pl.BlockSpec(sq + (tm, tn), lambda p, i, j, kk: lead(p) + (i, j))
    in_specs = ([a_spec, b_spec] + ([o_spec] if res is not None else [])
                + ([pl.BlockSpec(memory_space=pl.ANY)] if after is not None else []))
    args = (a, b) + ((res,) if res is not None else ()) + ((after,) if after is not None else ())
    return _pcall(
        body, name=name, grid=(planes or 1, m // tm, n // tn, nk), in_specs=in_specs, out_specs=o_spec,
        out_shape=jax.ShapeDtypeStruct(((planes,) if planes else ()) + (m, n), out_dtype),
        scratch_shapes=[pltpu.VMEM((tm, tn), F32)] if nk > 1 else [],
        compiler_params=_params("parallel", "parallel", "parallel", "arbitrary"),
    )(*args)


def _rms(x):
    return lax.rsqrt(jnp.mean(x * x, axis=-1, keepdims=True) + EPS)


def _norm_matmul(h, g, w, *, tm=512, tn=512, tail=0, name):
    t, d = h.shape
    n = w.shape[1]
    tn = min(tn, n)
    assert t % tm == 0 and n % tn == 0 and (not tail or tn == n)

    def body(h_ref, g_ref, w_ref, xn_ref, y_ref, *tail_ref):
        @pl.when(pl.program_id(1) == 0)
        def _():
            x = h_ref[...]
            xn_ref[...] = (x * _rms(x) * g_ref[...]).astype(BF16)

        y = _dot(xn_ref[...], w_ref[...])
        if tail:
            y_ref[...] = y[:, :n - tail]
            tail_ref[0][...] = y[:, n - tail:]
        else:
            y_ref[...] = y

    row = pl.BlockSpec((tm, d), lambda i, j: (i, 0))
    tails = ([pl.BlockSpec((tm, tail), lambda i, j: (i, 0))], [jax.ShapeDtypeStruct((t, tail), F32)]) if tail else ([], [])
    return _pcall(
        body, name=name, grid=(t // tm, n // tn),
        in_specs=[row, pl.BlockSpec((1, d), lambda i, j: (0, 0)), pl.BlockSpec((d, tn), lambda i, j: (0, j))],
        out_specs=[row, pl.BlockSpec((tm, tn - tail), lambda i, j: (i, j))] + tails[0],
        out_shape=[jax.ShapeDtypeStruct((t, d), BF16), jax.ShapeDtypeStruct((t, n - tail), F32)] + tails[1],
        compiler_params=_params("parallel", "arbitrary"),
    )(h, g.reshape(1, d), w)


def _matmul_norm_bwd(a, w, h, g, dres, *, tm=512, tk=None, name):
    stacked = a.ndim == 3
    t = a.shape[-2]
    d, k = w.shape
    tk = a.shape[-1] if stacked else min(tk or k, k)
    nk = k // tk
    assert t % tm == 0 and k % tk == 0 and (not stacked or a.shape[0] == nk)

    def body(a_ref, w_ref, h_ref, g_ref, r_ref, dx_ref, dg_ref, *acc):
        i, kk = pl.program_id(0), pl.program_id(1)
        part = _dot(a_ref[...], w_ref[...], NT)

        def finish(dy):
            x = h_ref[...]
            r = _rms(x)
            xhat = x * r
            gy = dy * g_ref[...]
            dx_ref[...] = r_ref[...] + r * (gy - xhat * jnp.mean(gy * xhat, axis=-1, keepdims=True))
            dg_part = jnp.sum(dy * xhat, axis=0, keepdims=True)

            @pl.when(i == 0)
            def _():
                dg_ref[...] = dg_part

            @pl.when(i > 0)
            def _():
                dg_ref[...] += dg_part

        if nk == 1:
            finish(part)
        else:
            acc_ref = acc[0]

            @pl.when(kk == 0)
            def _():
                acc_ref[...] = part

            @pl.when(kk > 0)
            def _():
                acc_ref[...] += part

            @pl.when(kk == nk - 1)
            def _():
                finish(acc_ref[...])

    a_spec = (pl.BlockSpec((None, tm, tk), lambda i, kk: (kk, i, 0)) if stacked
              else pl.BlockSpec((tm, tk), lambda i, kk: (i, kk)))
    row = pl.BlockSpec((tm, d), lambda i, kk: (i, 0))
    vec = pl.BlockSpec((1, d), lambda i, kk: (0, 0))
    dx, dg = _pcall(
        body, name=name, grid=(t // tm, nk),
        in_specs=[a_spec, pl.BlockSpec((d, tk), lambda i, kk: (0, kk)), row, vec, row], out_specs=[row, vec],
        out_shape=[jax.ShapeDtypeStruct((t, d), F32), jax.ShapeDtypeStruct((1, d), F32)],
        scratch_shapes=[pltpu.VMEM((tm, d), F32)] if nk > 1 else [],
        compiler_params=_params("arbitrary", "arbitrary"),
    )(a, w, h, g.reshape(1, d), dres)
    return dx, dg.reshape(d)


def _norm_gate_up_swiglu(h, g, w, *, tm=512, tn=1408):
    t, d = h.shape
    nj = D_FF // tn
    assert t % tm == 0 and D_FF % tn == 0

    def body(h_ref, g_ref, wg_ref, wu_ref, hn_ref, gu_ref, act_ref):
        @pl.when(pl.program_id(1) == 0)
        def _():
            x = h_ref[...]
            hn_ref[...] = (x * _rms(x) * g_ref[...]).astype(BF16)

        hn = hn_ref[...]
        gate = _dot(hn, wg_ref[...])
        up = _dot(hn, wu_ref[...])
        gu_ref[0] = gate
        gu_ref[1] = up
        act_ref[...] = (gate * jax.nn.sigmoid(gate) * up).astype(BF16)

    row = pl.BlockSpec((tm, d), lambda i, j: (i, 0))
    return _pcall(
        body, name="gate_up_swiglu_fwd", grid=(t // tm, nj),
        in_specs=[row, pl.BlockSpec((1, d), lambda i, j: (0, 0)), pl.BlockSpec((d, tn), lambda i, j: (0, j)),
                  pl.BlockSpec((d, tn), lambda i, j: (0, nj + j))],
        out_specs=[row, pl.BlockSpec((2, tm, tn), lambda i, j: (0, i, j)), pl.BlockSpec((tm, tn), lambda i, j: (i, j))],
        out_shape=[jax.ShapeDtypeStruct((t, d), BF16), jax.ShapeDtypeStruct((2, t, D_FF), F32),
                   jax.ShapeDtypeStruct((t, D_FF), BF16)],
        compiler_params=_params("parallel", "arbitrary"),
    )(h, g.reshape(1, d), w, w)


def _down_dx_swiglu_bwd(dh, w_down, gu, *, after=None, tm=512, tn=1408):
    t, d = dh.shape
    assert t % tm == 0 and D_FF % tn == 0

    def body(dh_ref, w_ref, gu_ref, *rest):
        dgu_ref = rest[-1]
        da = _dot(dh_ref[...].astype(BF16), w_ref[...], NT)
        gate, up = gu_ref[0], gu_ref[1]
        sg = jax.nn.sigmoid(gate)
        silu = gate * sg
        dgu_ref[0] = (da * up * (sg + silu * (1.0 - sg))).astype(BF16)
        dgu_ref[1] = (da * silu).astype(BF16)

    stack = pl.BlockSpec((2, tm, tn), lambda i, j: (0, i, j))
    return _pcall(
        body, name="down_dx_swiglu_bwd", grid=(t // tm, D_FF // tn),
        in_specs=[pl.BlockSpec((tm, d), lambda i, j: (i, 0)), pl.BlockSpec((tn, d), lambda i, j: (j, 0)), stack]
        + ([pl.BlockSpec(memory_space=pl.ANY)] if after is not None else []),
        out_specs=stack, out_shape=jax.ShapeDtypeStruct((2, t, D_FF), BF16),
        compiler_params=_params("parallel", "parallel"),
    )(dh, w_down, gu, *((after,) if after is not None else ()))


def _group_matrix(width, group):
    r = lax.broadcasted_iota(jnp.int32, (width, width), 0) // group
    c = lax.broadcasted_iota(jnp.int32, (width, width), 1) // group
    return (r == c).astype(BF16)


def _qkv_prep(z, gq, gk, *, bt=512):
    t = z.shape[0]
    w = FOX_WIDTH

    def body(q_ref, k_ref, v_ref, gq_ref, gk_ref, qo_ref, ko_ref, vo_ref, q2_ref):
        gm = _group_matrix(w, FOX_HEAD_DIM)
        for x_ref, g_ref, o_ref, scale in ((q_ref, gq_ref, qo_ref, FOX_SCALE), (k_ref, gk_ref, ko_ref, 1.0)):
            x = x_ref[...]
            ms = _dot_exact(x * x, gm) * (1.0 / FOX_HEAD_DIM)
            y = x * lax.rsqrt(ms + EPS) * g_ref[...]
            o_ref[...] = (y * scale).astype(BF16)
            if o_ref is qo_ref:
                q2_ref[...] = (y * (scale * LOG2E)).astype(BF16)
        vo_ref[...] = v_ref[...].astype(BF16)

    col = lambda c: pl.BlockSpec((bt, w), lambda i, c=c: (i, c))
    vec = pl.BlockSpec((1, w), lambda i: (0, 0))
    out = pl.BlockSpec((bt, w), lambda i: (i, 0))
    return _pcall(
        body, name="fox_qkv_prep", grid=(t // bt,), in_specs=[col(0), col(1), col(2), vec, vec], out_specs=[out] * 4,
        out_shape=[jax.ShapeDtypeStruct((t, w), BF16)] * 4, compiler_params=_params("parallel"),
    )(z, z, z, jnp.tile(gq, FOX_HEADS).reshape(1, w), jnp.tile(gk, FOX_HEADS).reshape(1, w))


def _log_sigmoid(f):
    return jnp.minimum(f, 0.0) - jnp.log1p(jnp.exp(-jnp.abs(f)))


def _forget_cumsum(ft, b, *, chunk=512):
    hh, t = ft.shape

    def body(f_ref, b_ref, c_ref):
        r = lax.broadcasted_iota(jnp.int32, (chunk, chunk), 0)
        c = lax.broadcasted_iota(jnp.int32, (chunk, chunk), 1)
        upper = (r <= c).astype(BF16)
        carry = jnp.zeros((hh, 1), F32)
        for ch in range(t // chunk):
            sl = slice(ch * chunk, (ch + 1) * chunk)
            cs = _dot_exact(_log_sigmoid(f_ref[:, sl] + b_ref[...]), upper) + carry
            c_ref[:, sl] = cs
            carry = cs[:, chunk - 1:chunk]

    return _pcall(body, name="fox_forget_cumsum", out_shape=jax.ShapeDtypeStruct((hh, t), F32),
                  compiler_params=_params())(ft, b.reshape(hh, 1))


def _forget_cumsum_bwd(dc_keys, dc_queries, ft, b, *, chunk=512):
    hh, t = ft.shape

    def body(dck_ref, dcq_ref, f_ref, b_ref, df_ref, db_ref):
        r = lax.broadcasted_iota(jnp.int32, (chunk, chunk), 0)
        c = lax.broadcasted_iota(jnp.int32, (chunk, chunk), 1)
        lower = (r >= c).astype(BF16)
        carry = jnp.zeros((hh, 1), F32)
        db = jnp.zeros((hh, 1), F32)
        for ch in reversed(range(t // chunk)):
            sl = slice(ch * chunk, (ch + 1) * chunk)
            dls = _dot_exact(dck_ref[:, sl] + dcq_ref[:, sl], lower) + carry
            carry = dls[:, 0:1]
            df = dls * jax.nn.sigmoid(-(f_ref[:, sl] + b_ref[...]))
            df_ref[:, sl] = df
            db = db + jnp.sum(df, axis=1, keepdims=True)
        db_ref[...] = db

    return _pcall(body, name="fox_forget_cumsum_bwd",
                  out_shape=[jax.ShapeDtypeStruct((hh, t), F32), jax.ShapeDtypeStruct((hh, 1), F32)],
                  compiler_params=_params())(dc_keys, dc_queries, ft, b.reshape(hh, 1))


def _lane_is_first_head():
    return lax.broadcasted_iota(jnp.int32, (1, LANES), 1) < FOX_HEAD_DIM


def _fox_fwd(q2t, kn, vt, crow2, ccol2, *, bq=512, bk=1024):
    t = kn.shape[0]
    nq = t // bq
    pairs = FOX_WIDTH // LANES
    assert t % bk == 0
    tiles = [(i, j) for i in range(nq) for j in range(i * bq // bk, -1, -1)]
    it = jnp.asarray(np.array([a for a, _ in tiles], np.int32))
    jt = jnp.asarray(np.array([b for _, b in tiles], np.int32))

    def body(it_ref, jt_ref, qt_ref, k_ref, vt_ref, cr_ref, cc_ref, o_ref, lse_ref, m_sc, acc_sc):
        s_id = pl.program_id(1)
        i, j = it_ref[s_id], jt_ref[s_id]
        first = _lane_is_first_head()
        holds_diagonal = j == (i * bq) // bk

        @pl.when(holds_diagonal)
        def _():
            m_sc[...] = jnp.full(m_sc.shape, -jnp.inf, F32)
            acc_sc[...] = jnp.zeros(acc_sc.shape, F32)

        def scores():
            k2, qt2 = k_ref[...], qt_ref[...]
            return [_dot(jnp.where(first if hh == 0 else jnp.logical_not(first), k2, jnp.zeros_like(k2)), qt2)
                    for hh in range(2)]

        def pv(hh, pt_bf16):
            v_ones = jnp.concatenate([vt_ref[hh * FOX_HEAD_DIM:(hh + 1) * FOX_HEAD_DIM, :],
                                      jnp.ones((FOX_ACC_ROWS - FOX_HEAD_DIM, bk), BF16)], axis=0)
            return _dot(v_ones, pt_bf16)

        def tile(diagonal):
            sc = scores()
            for hh in range(2):
                ut = sc[hh] - cc_ref[0, :, hh:hh + 1]
                if diagonal:
                    key = j * bk + lax.broadcasted_iota(jnp.int32, ut.shape, 0)
                    query = i * bq + lax.broadcasted_iota(jnp.int32, ut.shape, 1)
                    ut = jnp.where(key <= query, ut, -jnp.inf)
                c_t = cr_ref[0, hh:hh + 1, :]
                m_prev = m_sc[hh]
                m_new = jnp.maximum(m_prev, jnp.max(ut, axis=0, keepdims=True) + c_t)
                acc_sc[hh] = jnp.exp2(m_prev - m_new) * acc_sc[hh] + pv(hh, jnp.exp2(ut + (c_t - m_new)).astype(BF16))
                m_sc[hh] = m_new

        @pl.when(holds_diagonal)
        def _():
            tile(True)

        @pl.when(jnp.logical_not(holds_diagonal))
        def _():
            tile(False)

        @pl.when(j == 0)
        def _():
            sums = [acc_sc[hh, FOX_HEAD_DIM:FOX_HEAD_DIM + 1, :] for hh in range(2)]
            ot = jnp.concatenate([acc_sc[hh, :FOX_HEAD_DIM, :] / sums[hh] for hh in range(2)], axis=0)
            o_ref[...] = jnp.transpose(ot).astype(o_ref.dtype)
            for hh in range(2):
                lse_ref[0, hh:hh + 1, :] = m_sc[hh] * LN2 + jnp.log(sums[hh])

    qspec = pl.BlockSpec((bq, LANES), lambda p, s, it, jt: (it[s], p))
    kspec = pl.BlockSpec((bk, LANES), lambda p, s, it, jt: (jt[s], p))
    vtspec = pl.BlockSpec((LANES, bk), lambda p, s, it, jt: (p, jt[s]))
    qtspec = pl.BlockSpec((LANES, bq), lambda p, s, it, jt: (p, it[s]))
    rowq = pl.BlockSpec((1, 2, bq), lambda p, s, it, jt: (p, 0, it[s]))
    colk = pl.BlockSpec((1, bk, 2), lambda p, s, it, jt: (p, jt[s], 0))
    return _pcall(
        body, name="fox_attention_fwd",
        grid_spec=pltpu.PrefetchScalarGridSpec(
            num_scalar_prefetch=2, grid=(pairs, len(tiles)),
            in_specs=[qtspec, kspec, vtspec, rowq, colk], out_specs=[qspec, rowq],
            scratch_shapes=[pltpu.VMEM((2, 1, bq), F32), pltpu.VMEM((2, FOX_ACC_ROWS, bq), F32)]),
        out_shape=[jax.ShapeDtypeStruct((t, FOX_WIDTH), BF16), jax.ShapeDtypeStruct((pairs, 2, t), F32)],
        compiler_params=_params("parallel", "arbitrary"),
    )(it, jt, q2t, kn, vt, crow2, ccol2)


def _fox_bwd(qn, kn, knt, vb, do, lse_row, delta_row, crow, ccol, *, bq=512):
    t = qn.shape[0]
    nq = t // bq
    pairs = FOX_WIDTH // LANES
    tiles = [(i, j) for j in range(nq) for i in range(j, nq)]
    it = jnp.asarray(np.array([a for a, _ in tiles], np.int32))
    jt = jnp.asarray(np.array([b for _, b in tiles], np.int32))

    def body(it_ref, jt_ref, q_ref, k_ref, kt_ref, v_ref, do_ref, lse_ref, dl_ref, cr_ref, cc_ref,
             dqt_ref, dk_ref, dv_ref, dc_ref, dr_ref, dk_sc, dv_sc, dc_sc):
        s_id = pl.program_id(1)
        i, j = it_ref[s_id], jt_ref[s_id]
        first = _lane_is_first_head()

        @pl.when(s_id == 0)
        def _():
            dqt_ref[...] = jnp.zeros(dqt_ref.shape, F32)
            dr_ref[...] = jnp.zeros(dr_ref.shape, F32)

        @pl.when(i == j)
        def _():
            dk_sc[...] = jnp.zeros(dk_sc.shape, F32)
            dv_sc[...] = jnp.zeros(dv_sc.shape, F32)
            dc_sc[...] = jnp.zeros(dc_sc.shape, F32)

        def tile(diagonal):
            q2, k2, kt2, v2, do2 = q_ref[...], k_ref[...], kt_ref[...], v_ref[...], do_ref[...]
            dk_t, dv_t, dqt_t = [], [], []
            for hh in range(2):
                mine = first if hh == 0 else jnp.logical_not(first)
                kh = jnp.where(mine, k2, jnp.zeros_like(k2))
                vh = jnp.where(mine, v2, jnp.zeros_like(v2))
                kth = kt2[hh * FOX_HEAD_DIM:(hh + 1) * FOX_HEAD_DIM]
                st = _dot(kh, q2, NT) + (cr_ref[0, hh:hh + 1, :] - cc_ref[0, :, hh:hh + 1])
                if diagonal:
                    rr = lax.broadcasted_iota(jnp.int32, st.shape, 0)
                    cc = lax.broadcasted_iota(jnp.int32, st.shape, 1)
                    st = jnp.where(rr <= cc, st, -jnp.inf)
                pt = jnp.exp(st - lse_ref[0, hh:hh + 1, :])
                dv_t.append(_dot(pt.astype(BF16), do2))
                dpt = _dot(vh, do2, NT)
                dst = pt * (dpt - dl_ref[0, hh:hh + 1, :])
                dc_sc[hh] += jnp.sum(dst, axis=1, keepdims=True)
                dr_ref[0, i, hh:hh + 1, :] += jnp.sum(dst, axis=0, keepdims=True)
                dsb = dst.astype(BF16)
                dk_t.append(_dot(dsb, q2))
                dqt_t.append(_dot(kth, dsb))
            dk_sc[...] += jnp.where(first, dk_t[0], dk_t[1])
            dv_sc[...] += jnp.where(first, dv_t[0], dv_t[1])
            dqt_ref[0, i] += jnp.concatenate(dqt_t, axis=0)

        @pl.when(i > j)
        def _():
            tile(False)

        @pl.when(i == j)
        def _():
            tile(True)

        @pl.when(i == nq - 1)
        def _():
            dk_ref[...] = dk_sc[...]
            dv_ref[...] = dv_sc[...]
            for hh in range(2):
                dc_ref[0, :, hh:hh + 1] = -dc_sc[hh]

    qspec = pl.BlockSpec((bq, LANES), lambda p, s, it, jt: (it[s], p))
    kspec = pl.BlockSpec((bq, LANES), lambda p, s, it, jt: (jt[s], p))
    ktspec = pl.BlockSpec((LANES, bq), lambda p, s, it, jt: (p, jt[s]))
    rowq = pl.BlockSpec((1, 2, bq), lambda p, s, it, jt: (p, 0, it[s]))
    colk = pl.BlockSpec((1, bq, 2), lambda p, s, it, jt: (p, jt[s], 0))
    dqt_spec = pl.BlockSpec((1, nq, LANES, bq), lambda p, s, it, jt: (p, 0, 0, 0))
    dr_spec = pl.BlockSpec((1, nq, 2, bq), lambda p, s, it, jt: (p, 0, 0, 0))
    dqt, dk, dv, dc_keys, dc_queries = _pcall(
        body, name="fox_attention_bwd",
        grid_spec=pltpu.PrefetchScalarGridSpec(
            num_scalar_prefetch=2, grid=(pairs, len(tiles)),
            in_specs=[qspec, kspec, ktspec, kspec, qspec, rowq, rowq, rowq, colk],
            out_specs=[dqt_spec, kspec, kspec, colk, dr_spec],
            scratch_shapes=[pltpu.VMEM((bq, LANES), F32), pltpu.VMEM((bq, LANES), F32), pltpu.VMEM((2, bq, 1), F32)]),
        out_shape=[jax.ShapeDtypeStruct((pairs, nq, LANES, bq), F32), jax.ShapeDtypeStruct((t, FOX_WIDTH), F32),
                   jax.ShapeDtypeStruct((t, FOX_WIDTH), F32), jax.ShapeDtypeStruct((pairs, t, 2), F32),
                   jax.ShapeDtypeStruct((pairs, nq, 2, bq), F32)],
        compiler_params=_params("parallel", "arbitrary"),
    )(it, jt, qn, kn, knt, vb, do, lse_row, delta_row, crow, ccol)
    dq = jnp.transpose(dqt, (1, 3, 0, 2)).reshape(t, FOX_WIDTH)
    dc_keys = jnp.transpose(dc_keys, (0, 2, 1)).reshape(FOX_HEADS, t)
    dc_queries = jnp.transpose(dc_queries, (0, 2, 1, 3)).reshape(FOX_HEADS, t)
    return dq, dk, dv, dc_keys, dc_queries


def _fox_delta(dcat, fox, *, bt=512):
    t = fox.shape[0]
    w = FOX_WIDTH

    def body(do_ref, o_ref, dob_ref, dl_ref):
        dob = do_ref[...].astype(BF16)
        r = lax.broadcasted_iota(jnp.int32, (w, LANES), 0) // FOX_HEAD_DIM
        c = lax.broadcasted_iota(jnp.int32, (w, LANES), 1)
        dl_ref[...] = _dot_exact(dob.astype(F32) * o_ref[...].astype(F32), (r == c).astype(BF16))
        dob_ref[...] = dob

    blk = pl.BlockSpec((bt, w), lambda i: (i, 0))
    return _pcall(
        body, name="fox_delta", grid=(t // bt,), in_specs=[blk, blk],
        out_specs=[blk, pl.BlockSpec((bt, LANES), lambda i: (i, 0))],
        out_shape=[jax.ShapeDtypeStruct((t, w), BF16), jax.ShapeDtypeStruct((t, LANES), F32)],
        compiler_params=_params("parallel"),
    )(dcat, fox)


def _mixer_dz(z, dq, dk, dv, dpin, dfpad, gq, gk, *, bt=256):
    t = z.shape[0]
    w = FOX_WIDTH

    def body(q_ref, k_ref, dq_ref, dk_ref, dv_ref, dp_ref, df_ref, gq_ref, gk_ref, dz_ref, dgq_ref, dgk_ref):
        gm = _group_matrix(w, FOX_HEAD_DIM)
        first_step = pl.program_id(0) == 0
        for n, (x_ref, dy_ref, g_ref, dg_ref, scale) in enumerate(
                ((q_ref, dq_ref, gq_ref, dgq_ref, FOX_SCALE), (k_ref, dk_ref, gk_ref, dgk_ref, 1.0))):
            x = x_ref[...]
            r = lax.rsqrt(_dot_exact(x * x, gm) * (1.0 / FOX_HEAD_DIM) + EPS)
            xhat = x * r
            dy = dy_ref[...] * scale
            gy = dy * g_ref[...]
            dx = r * (gy - xhat * (_dot_exact(gy * xhat, gm) * (1.0 / FOX_HEAD_DIM)))
            dz_ref[:, n * w:(n + 1) * w] = dx.astype(BF16)
            part = jnp.sum(dy * xhat, axis=0, keepdims=True)

            @pl.when(first_step)
            def _():
                dg_ref[...] = part

            @pl.when(jnp.logical_not(first_step))
            def _():
                dg_ref[...] += part

        dz_ref[:, 2 * w:3 * w] = dv_ref[...].astype(BF16)
        dz_ref[:, 3 * w:4 * w] = dp_ref[...].astype(BF16)
        dz_ref[:, 4 * w:] = df_ref[...].astype(BF16)

    col = lambda c: pl.BlockSpec((bt, w), lambda i, c=c: (i, c))
    blk = pl.BlockSpec((bt, w), lambda i: (i, 0))
    vec = pl.BlockSpec((1, w), lambda i: (0, 0))
    dz, dgq, dgk = _pcall(
        body, name="mixer_dz", grid=(t // bt,),
        in_specs=[col(0), col(1), blk, blk, blk, blk, pl.BlockSpec((bt, LANES), lambda i: (i, 0)), vec, vec],
        out_specs=[pl.BlockSpec((bt, IN_COLS_PAD), lambda i: (i, 0)), vec, vec],
        out_shape=[jax.ShapeDtypeStruct((t, IN_COLS_PAD), BF16), jax.ShapeDtypeStruct((1, w), F32),
                   jax.ShapeDtypeStruct((1, w), F32)],
        compiler_params=_params("arbitrary"),
    )(z, z, dq, dk, dv, dpin, dfpad, jnp.tile(gq, FOX_HEADS).reshape(1, w), jnp.tile(gk, FOX_HEADS).reshape(1, w))
    return dz, dgq.reshape(FOX_HEADS, FOX_HEAD_DIM).sum(0), dgk.reshape(FOX_HEADS, FOX_HEAD_DIM).sum(0)


def _pool_fwd(z, wp, scale, *, bt=512):
    t = z.shape[0]
    w = POOL_WIDTH
    hb = bt // POOL_HALO

    def body(p_ref, h_ref, wp_ref, sc_ref, y_ref, mx_ref):
        i = pl.program_id(0)
        cur = p_ref[...]
        halo = jnp.where(i > 0, h_ref[...], 0.0)
        ext = jnp.concatenate([halo, cur], axis=0)
        trow = i * bt + lax.broadcasted_iota(jnp.int32, (bt, 1), 0)
        for g, win in enumerate(POOL_WINDOWS):
            sl = slice(g * LANES, (g + 1) * LANES)
            e = ext[:, sl]
            acc = e[POOL_HALO:]
            for k in range(1, win):
                acc = acc + pltpu.roll(e, k, 0)[POOL_HALO:]
            cnt = jnp.minimum(trow + 1, win).astype(F32)
            mixed = (acc / cnt - cur[:, sl]).astype(BF16)
            mx_ref[:, sl] = mixed
            y_ref[:, sl] = (_dot(mixed, wp_ref[g]) * sc_ref[:, sl]).astype(BF16)

    blk = pl.BlockSpec((bt, w), lambda i: (i, 0))
    return _pcall(
        body, name="pool_fwd", grid=(t // bt,),
        in_specs=[pl.BlockSpec((bt, w), lambda i: (i, 3)),
                  pl.BlockSpec((POOL_HALO, w), lambda i: (jnp.maximum(i * hb - 1, 0), 3)),
                  pl.BlockSpec((len(POOL_WINDOWS), LANES, LANES), lambda i: (0, 0, 0)),
                  pl.BlockSpec((1, w), lambda i: (0, 0))],
        out_specs=[blk, blk], out_shape=[jax.ShapeDtypeStruct((t, w), BF16)] * 2,
        compiler_params=_params("parallel"),
    )(z, z, wp, scale.reshape(1, w))


def _pool_bwd(dcat, mixed, wp, scale, *, bt=512):
    t = mixed.shape[0]
    w = POOL_WIDTH
    hb = bt // POOL_HALO
    nb = t // bt
    n_ext = bt + POOL_HALO

    def body(d_ref, h_ref, mx_ref, wp_ref, sc_ref, dp_ref, dwp_ref, dsc_ref):
        i = pl.program_id(0)
        cur = d_ref[...]
        nxt = jnp.where(i < nb - 1, h_ref[...], 0.0)
        ext = jnp.concatenate([cur, nxt], axis=0)
        trow = i * bt + lax.broadcasted_iota(jnp.int32, (n_ext, 1), 0)

        @pl.when(i == 0)
        def _():
            dwp_ref[...] = jnp.zeros(dwp_ref.shape, F32)
            dsc_ref[...] = jnp.zeros(dsc_ref.shape, F32)

        for g, win in enumerate(POOL_WINDOWS):
            sl = slice(g * LANES, (g + 1) * LANES)
            dy = (ext[:, sl] * sc_ref[:, sl]).astype(BF16)
            dm = _dot(dy, wp_ref[g], NT)
            mixed_g = mx_ref[:, sl]
            dsc_ref[:, sl] += jnp.sum(cur[:, sl] * _dot(mixed_g, wp_ref[g]), axis=0, keepdims=True)
            dwp_ref[g] += _dot(mixed_g, dy[:bt], TN)
            r = dm / jnp.minimum(trow + 1, win).astype(F32)
            acc = r[:bt]
            for k in range(1, win):
                acc = acc + pltpu.roll(r, n_ext - k, 0)[:bt]
            dp_ref[:, sl] = acc - dm[:bt]

    return _pcall(
        body, name="pool_bwd", grid=(nb,),
        in_specs=[pl.BlockSpec((bt, w), lambda i: (i, 1)),
                  pl.BlockSpec((POOL_HALO, w), lambda i: (jnp.minimum((i + 1) * hb, t // POOL_HALO - 1), 1)),
                  pl.BlockSpec((bt, w), lambda i: (i, 0)),
                  pl.BlockSpec((len(POOL_WINDOWS), LANES, LANES), lambda i: (0, 0, 0)),
                  pl.BlockSpec((1, w), lambda i: (0, 0))],
        out_specs=[pl.BlockSpec((bt, w), lambda i: (i, 0)),
                   pl.BlockSpec((len(POOL_WINDOWS), LANES, LANES), lambda i: (0, 0, 0)),
                   pl.BlockSpec((1, w), lambda i: (0, 0))],
        out_shape=[jax.ShapeDtypeStruct((t, w), F32), jax.ShapeDtypeStruct((len(POOL_WINDOWS), LANES, LANES), F32),
                   jax.ShapeDtypeStruct((1, w), F32)],
        compiler_params=_params("arbitrary"),
    )(dcat, dcat, mixed, wp, scale.reshape(1, w))


def _head_rms(x):
    return lax.rsqrt(jnp.mean(x * x, axis=-1, keepdims=True) + EPS)


def _mem_kv_fwd(mem, g_kv, w_kv, g_k):
    mlen, d = mem.shape

    def body(m_ref, g_ref, w_ref, gk_ref, mn_ref, mkv_ref, mk_ref, mv_ref):
        x = m_ref[...]
        mn = (x * lax.rsqrt(jnp.mean(x * x, axis=-1, keepdims=True) + EPS) * g_ref[...]).astype(BF16)
        mn_ref[...] = mn
        mkv = _dot(mn, w_ref[...])
        mkv_ref[...] = mkv
        for h in range(MEM_HEADS):
            sl = slice(h * MEM_HEAD_DIM, (h + 1) * MEM_HEAD_DIM)
            kh = mkv[:, sl]
            mk_ref[:, sl] = (kh * _head_rms(kh) * gk_ref[...]).astype(BF16)
        mv_ref[...] = mkv[:, MEM_WIDTH:].astype(BF16)

    return _pcall(
        body, name="mem_kv_fwd",
        out_shape=[jax.ShapeDtypeStruct((mlen, d), BF16), jax.ShapeDtypeStruct((mlen, 2 * MEM_WIDTH), F32),
                   jax.ShapeDtypeStruct((mlen, MEM_WIDTH), BF16), jax.ShapeDtypeStruct((mlen, MEM_WIDTH), BF16)],
        compiler_params=_params(),
    )(mem, g_kv.reshape(1, d), w_kv, g_k.reshape(1, MEM_HEAD_DIM))


def _mem_kv_bwd(dmk, dmv, mkv, mn, mem, g_kv, w_kv, g_k):
    mlen, d = mem.shape

    def body(dmk_ref, dmv_ref, mkv_ref, mn_ref, m_ref, g_ref, w_ref, gk_ref, dw_ref, dg_ref, dgk_ref, dkv_sc):
        dgk = jnp.zeros((1, MEM_HEAD_DIM), F32)
        for h in range(MEM_HEADS):
            sl = slice(h * MEM_HEAD_DIM, (h + 1) * MEM_HEAD_DIM)
            x = mkv_ref[:, sl]
            r = _head_rms(x)
            xhat = x * r
            dy = dmk_ref[:, sl]
            gy = dy * gk_ref[...]
            dkv_sc[:, sl] = (r * (gy - xhat * jnp.mean(gy * xhat, axis=-1, keepdims=True))).astype(BF16)
            dgk = dgk + jnp.sum(dy * xhat, axis=0, keepdims=True)
        dgk_ref[...] = dgk
        dkv_sc[:, MEM_WIDTH:] = dmv_ref[...].astype(BF16)
        dkv = dkv_sc[...]
        dw_ref[...] = _dot(mn_ref[...], dkv, TN).astype(BF16)
        dmn = _dot(dkv, w_ref[...], NT)
        x = m_ref[...]
        xhat = x * lax.rsqrt(jnp.mean(x * x, axis=-1, keepdims=True) + EPS)
        dg_ref[...] = jnp.sum(dmn * xhat, axis=0, keepdims=True)

    dw, dg, dgk = _pcall(
        body, name="mem_kv_bwd",
        out_shape=[jax.ShapeDtypeStruct((d, 2 * MEM_WIDTH), BF16), jax.ShapeDtypeStruct((1, d), F32),
                   jax.ShapeDtypeStruct((1, MEM_HEAD_DIM), F32)],
        scratch_shapes=[pltpu.VMEM((mlen, 2 * MEM_WIDTH), BF16)],
        compiler_params=_params(),
    )(dmk, dmv, mkv, mn, mem, g_kv.reshape(1, d), w_kv, g_k.reshape(1, MEM_HEAD_DIM))
    return dw, dg.reshape(d), dgk.reshape(MEM_HEAD_DIM)


def _cross_probs(x, g, mk_h):
    r = _head_rms(x)
    xhat = x * r
    qn = (xhat * g).astype(BF16)
    s = _dot(qn, mk_h, NT) * MEM_SCALE
    e = jnp.exp(s - jnp.max(s, axis=-1, keepdims=True))
    return r, xhat, qn, e / jnp.sum(e, axis=-1, keepdims=True)


def _cross_fwd(mq_raw, g_q, mk, mv, *, bt=512):
    t = mq_raw.shape[0]
    mlen = mk.shape[0]

    def body(x_ref, g_ref, mk_ref, mv_ref, o_ref):
        for h in range(MEM_HEADS):
            sl = slice(h * MEM_HEAD_DIM, (h + 1) * MEM_HEAD_DIM)
            _, _, _, p = _cross_probs(x_ref[:, sl], g_ref[...], mk_ref[:, sl])
            o_ref[:, sl] = _dot(p.astype(BF16), mv_ref[:, sl]).astype(BF16)

    blk = pl.BlockSpec((bt, MEM_WIDTH), lambda i: (i, 0))
    kv = pl.BlockSpec((mlen, MEM_WIDTH), lambda i: (0, 0))
    return _pcall(
        body, name="cross_attention_fwd", grid=(t // bt,),
        in_specs=[blk, pl.BlockSpec((1, MEM_HEAD_DIM), lambda i: (0, 0)), kv, kv], out_specs=blk,
        out_shape=jax.ShapeDtypeStruct((t, MEM_WIDTH), BF16), compiler_params=_params("parallel"),
    )(mq_raw, g_q.reshape(1, MEM_HEAD_DIM), mk, mv)


def _cross_bwd(mq_raw, dmo, g_q, mk, mv, *, bt=512):
    t = mq_raw.shape[0]
    mlen = mk.shape[0]

    def body(x_ref, do_ref, g_ref, mk_ref, mv_ref, dx_ref, dmk_ref, dmv_ref, dg_ref):
        @pl.when(pl.program_id(0) == 0)
        def _():
            dmk_ref[...] = jnp.zeros(dmk_ref.shape, F32)
            dmv_ref[...] = jnp.zeros(dmv_ref.shape, F32)
            dg_ref[...] = jnp.zeros(dg_ref.shape, F32)

        for h in range(MEM_HEADS):
            sl = slice(h * MEM_HEAD_DIM, (h + 1) * MEM_HEAD_DIM)
            r, xhat, qn, p = _cross_probs(x_ref[:, sl], g_ref[...], mk_ref[:, sl])
            do = do_ref[:, sl]
            dp = _dot(do, mv_ref[:, sl], NT)
            ds = (p * (dp - jnp.sum(p * dp, axis=-1, keepdims=True)) * MEM_SCALE).astype(BF16)
            dmv_ref[:, sl] += _dot(p.astype(BF16), do, TN)
            dmk_ref[:, sl] += _dot(ds, qn, TN)
            dqn = _dot(ds, mk_ref[:, sl])
            gy = dqn * g_ref[...]
            dx_ref[:, sl] = (r * (gy - xhat * jnp.mean(gy * xhat, axis=-1, keepdims=True))).astype(BF16)
            dg_ref[...] += jnp.sum(dqn * xhat, axis=0, keepdims=True)

    blk = pl.BlockSpec((bt, MEM_WIDTH), lambda i: (i, 0))
    kv = pl.BlockSpec((mlen, MEM_WIDTH), lambda i: (0, 0))
    gs = pl.BlockSpec((1, MEM_HEAD_DIM), lambda i: (0, 0))
    dx, dmk, dmv, dg = _pcall(
        body, name="cross_attention_bwd", grid=(t // bt,),
        in_specs=[blk, blk, gs, kv, kv], out_specs=[blk, kv, kv, gs],
        out_shape=[jax.ShapeDtypeStruct((t, MEM_WIDTH), BF16), jax.ShapeDtypeStruct((mlen, MEM_WIDTH), F32),
                   jax.ShapeDtypeStruct((mlen, MEM_WIDTH), F32), jax.ShapeDtypeStruct((1, MEM_HEAD_DIM), F32)],
        compiler_params=_params("arbitrary"),
    )(mq_raw, dmo, g_q.reshape(1, MEM_HEAD_DIM), mk, mv)
    return dx, dmk, dmv, dg.reshape(MEM_HEAD_DIM)


def _loss_head(y, target, *, bt=512):
    t, d = y.shape

    def body(y_ref, t_ref, dy_ref, l_ref):
        e = y_ref[...] - t_ref[...]
        dy_ref[...] = e * (1.0 / d)
        part = (0.5 / d) * jnp.sum(jnp.sum(e * e, axis=1, keepdims=True), axis=0, keepdims=True)

        @pl.when(pl.program_id(0) == 0)
        def _():
            l_ref[...] = part

        @pl.when(pl.program_id(0) > 0)
        def _():
            l_ref[...] += part

    blk = pl.BlockSpec((bt, d), lambda i: (i, 0))
    dy, loss = _pcall(
        body, name="loss_head", grid=(t // bt,), in_specs=[blk, blk],
        out_specs=[blk, pl.BlockSpec((1, 1), lambda i: (0, 0))],
        out_shape=[jax.ShapeDtypeStruct((t, d), F32), jax.ShapeDtypeStruct((1, 1), F32)],
        compiler_params=_params("arbitrary"),
    )(y, target)
    return loss, dy


def _row_tile(rows, cols, budget=1 << 19):
    best = None
    for cand in range(8, rows + 1, 8):
        if rows % cand == 0 and cand * cols <= budget:
            best = cand
    return best or rows


def _adamw(w, g, m, v, *, name):
    rows, cols = w.shape
    bt = _row_tile(rows, cols)
    c1 = 1.0 - ADAM_B1 ** ADAM_STEP
    c2 = 1.0 - ADAM_B2 ** ADAM_STEP

    def body(w_ref, g_ref, m_ref, v_ref, d_ref, nm_ref, nv_ref):
        g_v = g_ref[...]
        nm = ADAM_B1 * m_ref[...] + (1.0 - ADAM_B1) * g_v
        nv = ADAM_B2 * v_ref[...] + (1.0 - ADAM_B2) * (g_v * g_v)
        nm_ref[...] = nm
        nv_ref[...] = nv
        d_ref[...] = -ADAM_LR * ((nm / c1) / (jnp.sqrt(nv / c2) + ADAM_EPS) + ADAM_WD * w_ref[...])

    blk = pl.BlockSpec((bt, cols), lambda i: (i, 0))
    return _pcall(
        body, name=name, grid=(rows // bt,), in_specs=[blk] * 4, out_specs=[blk] * 3,
        out_shape=[jax.ShapeDtypeStruct((rows, cols), F32)] * 3, compiler_params=_params("parallel"),
    )(w, g, m, v)


def _sum_slots(x, *, name):
    n, rows, cols = x.shape
    bt = _row_tile(rows, cols, budget=1 << 17)

    def body(x_ref, o_ref):
        acc = x_ref[0].astype(F32)
        for s in range(1, n):
            acc = acc + x_ref[s].astype(F32)
        o_ref[...] = acc

    return _pcall(
        body, name=name, grid=(rows // bt,), in_specs=[pl.BlockSpec((n, bt, cols), lambda i: (0, i, 0))],
        out_specs=pl.BlockSpec((bt, cols), lambda i: (i, 0)),
        out_shape=jax.ShapeDtypeStruct((rows, cols), F32), compiler_params=_params("parallel"),
    )(x)


def _any_spec():
    return pl.BlockSpec(memory_space=pl.ANY)


def _all_gather(xs, *, name):
    n = len(xs)

    def body(*refs):
        x_refs, out_refs = refs[:n], refs[n:2 * n]
        send_sems, recv_sems, local_sems = refs[2 * n:]
        x, y, c = lax.axis_index("x"), lax.axis_index("y"), lax.axis_index("c")
        me, sibling = (x, y, c), (x, y, 1 - c)
        chips = [(1 - x, y), (x, 1 - y), (1 - x, 1 - y)]

        def slot(a, px, py, pc):
            return out_refs[a].at[4 * px + 2 * py + pc]

        def copy(a, k, block, to, src=None):
            return pltpu.make_async_remote_copy(
                src_ref=slot(a, *block) if src is None else src, dst_ref=slot(a, *block),
                send_sem=send_sems.at[a, k], recv_sem=recv_sems.at[a, k], device_id=to, device_id_type=MESH_ID)

        mine = [pltpu.make_async_copy(x_refs[a], slot(a, *me), local_sems.at[a]) for a in range(n)]
        for cp in mine:
            cp.start()
        first = []
        for j, chip in enumerate(chips):
            first += [copy(a, 1 + j, me, (*chip, c), src=x_refs[a]) for a in range(n)]
        first += [copy(a, 0, me, sibling, src=x_refs[a]) for a in range(n)]
        for cp in first:
            cp.start()
        passed = []
        for j, chip in enumerate(chips):
            for a in range(n):
                copy(a, 1 + j, (*chip, c), me).wait_recv()
                cp = copy(a, 4 + j, (*chip, c), sibling)
                cp.start()
                passed.append(cp)
        for a in range(n):
            copy(a, 0, sibling, me).wait_recv()
        for j, chip in enumerate(chips):
            for a in range(n):
                copy(a, 4 + j, (*chip, 1 - c), me).wait_recv()
        for cp in first + passed:
            cp.wait_send()
        for cp in mine:
            cp.wait()

    return _pcall(
        body, name=name, in_specs=[_any_spec()] * n, out_specs=[_any_spec()] * n,
        out_shape=[jax.ShapeDtypeStruct((N_DEV,) + x.shape, x.dtype) for x in xs],
        scratch_shapes=[pltpu.SemaphoreType.DMA((n, 7)), pltpu.SemaphoreType.DMA((n, 7)), pltpu.SemaphoreType.DMA((n,))],
    )(*xs)


def _exchange(xs, *, name):
    n = len(xs)

    def body(*refs):
        x_refs, out_refs = refs[:n], refs[n:2 * n]
        send_sems, recv_sems, local_sems = refs[2 * n:]
        x, y, c = lax.axis_index("x"), lax.axis_index("y"), lax.axis_index("c")
        me_idx = 4 * x + 2 * y + c

        def peer(k):
            px = x ^ ((k >> 2) & 1)
            py = y ^ ((k >> 1) & 1)
            pc = c ^ (k & 1)
            return (px, py, pc), 4 * px + 2 * py + pc

        def copy(a, k):
            to, to_idx = peer(k)
            return pltpu.make_async_remote_copy(
                src_ref=x_refs[a].at[to_idx], dst_ref=out_refs[a].at[me_idx],
                send_sem=send_sems.at[a, k - 1], recv_sem=recv_sems.at[a, k - 1], device_id=to, device_id_type=MESH_ID)

        def landing(a, k):
            frm, frm_idx = peer(k)
            return pltpu.make_async_remote_copy(
                src_ref=x_refs[a].at[frm_idx], dst_ref=out_refs[a].at[frm_idx],
                send_sem=send_sems.at[a, k - 1], recv_sem=recv_sems.at[a, k - 1], device_id=frm, device_id_type=MESH_ID)

        mine = [pltpu.make_async_copy(x_refs[a].at[me_idx], out_refs[a].at[me_idx], local_sems.at[a]) for a in range(n)]
        for cp in mine:
            cp.start()
        sends = [copy(a, k) for k in (2, 4, 6, 3, 5, 7, 1) for a in range(n)]
        for cp in sends:
            cp.start()
        for k in range(1, N_DEV):
            for a in range(n):
                landing(a, k).wait_recv()
        for cp in sends:
            cp.wait_send()
        for cp in mine:
            cp.wait()

    return _pcall(
        body, name=name, in_specs=[_any_spec()] * n, out_specs=[_any_spec()] * n,
        out_shape=[jax.ShapeDtypeStruct(x.shape, x.dtype) for x in xs],
        scratch_shapes=[pltpu.SemaphoreType.DMA((n, 7)), pltpu.SemaphoreType.DMA((n, 7)), pltpu.SemaphoreType.DMA((n,))],
    )(*xs)


def _mesh_peer(k):
    px = lax.axis_index("x") ^ ((k >> 2) & 1)
    py = lax.axis_index("y") ^ ((k >> 1) & 1)
    pc = lax.axis_index("c") ^ (k & 1)
    return (px, py, pc), 4 * px + 2 * py + pc


def _my_index():
    return 4 * lax.axis_index("x") + 2 * lax.axis_index("y") + lax.axis_index("c")


def _landing_zones(xs, blocks):
    me = _my_index()
    lands = []
    for x in xs:
        own = lax.dynamic_index_in_dim(x, me, 0, keepdims=True) if blocks else x[None]
        zone = lax.empty((N_DEV,) + own.shape[1:], x.dtype)
        lands.append(lax.dynamic_update_slice(zone, own, (me,) + (0,) * (own.ndim - 1)))
    return lands


def _send_start(xs, lands, *, blocks, name, after=None):
    n = len(xs)
    peers = N_DEV - 1
    first_out = 2 * n + (after is not None)

    def body(*refs):
        x_refs, land_refs = refs[:n], refs[n:2 * n]
        send_sems, recv_sems = refs[first_out:first_out + peers], refs[first_out + peers:first_out + 2 * peers]
        token = refs[-1]
        me_idx = _my_index()
        for k in (2, 4, 6, 3, 5, 7, 1):
            to, to_idx = _mesh_peer(k)
            for a in range(n):
                pltpu.make_async_remote_copy(
                    src_ref=x_refs[a].at[to_idx] if blocks else x_refs[a], dst_ref=land_refs[a].at[me_idx],
                    send_sem=send_sems[k - 1], recv_sem=recv_sems[k - 1], device_id=to, device_id_type=MESH_ID).start()
        token[...] = jnp.zeros(token.shape, token.dtype)

    hbm = pl.BlockSpec(memory_space=pltpu.HBM)
    sem = pl.BlockSpec(memory_space=pltpu.SEMAPHORE)
    both = list(xs) + list(lands)
    out = _pcall(
        body, name=name,
        out_shape=(*[pltpu.SemaphoreType.DMA(())] * (2 * peers), *[pltpu.HBM(a.shape, a.dtype) for a in both],
                   jax.ShapeDtypeStruct((8, LANES), F32)),
        in_specs=[hbm] * (2 * n) + ([pl.BlockSpec(memory_space=pl.ANY)] if after is not None else []),
        out_specs=(*[sem] * (2 * peers), *[hbm] * (2 * n), pl.BlockSpec(memory_space=pltpu.VMEM)),
        input_output_aliases={i: 2 * peers + i for i in range(2 * n)},
        compiler_params=pltpu.CompilerParams(has_side_effects=pltpu.SideEffectType.DATAFLOW_SIDE_EFFECTING),
    )(*[pltpu.with_memory_space_constraint(a, pltpu.HBM) for a in both], *((after,) if after is not None else ()))
    return dict(sems=out[:2 * peers], xs=out[2 * peers:2 * peers + n], lands=out[2 * peers + n:2 * peers + 2 * n],
                token=out[-1], blocks=blocks)


def _send_wait(started, after, *, name):
    n = len(started['xs'])
    blocks = started['blocks']
    peers = N_DEV - 1

    def body(*refs):
        x_refs, land_refs = refs[:n], refs[n:2 * n]
        send_sems, recv_sems = refs[2 * n:2 * n + peers], refs[2 * n + peers:2 * n + 2 * peers]
        for k in range(1, N_DEV):
            frm, frm_idx = _mesh_peer(k)
            for a in range(n):
                copy = pltpu.make_async_remote_copy(
                    src_ref=x_refs[a].at[frm_idx] if blocks else x_refs[a], dst_ref=land_refs[a].at[frm_idx],
                    send_sem=send_sems[k - 1], recv_sem=recv_sems[k - 1], device_id=frm, device_id_type=MESH_ID)
                copy.wait_send()
                copy.wait_recv()

    hbm = pl.BlockSpec(memory_space=pltpu.HBM)
    sem = pl.BlockSpec(memory_space=pltpu.SEMAPHORE)
    both = list(started['xs']) + list(started['lands'])
    out = _pcall(
        body, name=name, out_shape=[pltpu.HBM(a.shape, a.dtype) for a in both],
        in_specs=[hbm] * (2 * n) + [sem] * (2 * peers) + [pl.BlockSpec(memory_space=pl.ANY)], out_specs=[hbm] * (2 * n),
        input_output_aliases={i: i for i in range(2 * n)},
        compiler_params=pltpu.CompilerParams(has_side_effects=pltpu.SideEffectType.DATAFLOW_SIDE_EFFECTING),
    )(*both, *started['sems'], after)
    return out[n:]


COLUMN_SHARDED = ('w_in', 'w_mem_out', 'w_gate_up')


def _full_weight(name, g):
    if name in COLUMN_SHARDED:
        g = jnp.transpose(g, (1, 0, 2))
        g = g.reshape(g.shape[0], -1)
    else:
        g = g.reshape(-1, g.shape[-1])
    return jnp.pad(g, ((0, 0), (0, IN_COLS_PAD - IN_COLS))) if name == 'w_in' else g


def _grad_blocks(name, g):
    if name == 'w_gate_up':
        rows, cols = g.shape[1], 2 * g.shape[2] // N_DEV
        g = jnp.transpose(g.reshape(2, rows, N_DEV // 2, cols), (0, 2, 1, 3))
        return g.reshape(N_DEV, rows, cols).astype(BF16)
    if name == 'w_in':
        g = g[:, :IN_COLS]
    if name in COLUMN_SHARDED:
        rows, cols = g.shape[0], g.shape[1] // N_DEV
        g = jnp.transpose(g.reshape(rows, N_DEV, cols), (1, 0, 2))
    else:
        rows, cols = g.shape[0] // N_DEV, g.shape[1]
        g = g.reshape(N_DEV, rows, cols)
    return g.astype(BF16)


SMALL_SHAPES = {'g_mix': (DEPTH, D_MODEL), 'b_forget': (DEPTH, FOX_HEADS), 'g_q_fox': (DEPTH, FOX_HEAD_DIM),
                'g_k_fox': (DEPTH, FOX_HEAD_DIM), 'w_pool': (DEPTH, 4, POOL_GROUP_DIM, POOL_GROUP_DIM),
                'pool_scale': (DEPTH, POOL_WIDTH), 'g_mem_q': (DEPTH, D_MODEL), 'g_mem_kv': (DEPTH, D_MODEL),
                'g_q_mem': (DEPTH, MEM_HEAD_DIM), 'g_k_mem': (DEPTH, MEM_HEAD_DIM), 'g_ffn': (DEPTH, D_MODEL)}


def _small_rows(name):
    return -(-int(np.prod(SMALL_SHAPES[name])) // LANES)


SMALL_ROWS = -(-sum(_small_rows(n) for n in SMALL) // 8) * 8


def _pack_small(tree):
    parts = []
    for n in SMALL:
        flat = tree[n].reshape(-1).astype(F32)
        parts.append(jnp.pad(flat, (0, _small_rows(n) * LANES - flat.shape[0])))
    flat = jnp.concatenate(parts)
    return jnp.pad(flat, (0, SMALL_ROWS * LANES - flat.shape[0])).reshape(SMALL_ROWS, LANES)


def _unpack_small(packed):
    flat = packed.reshape(-1)
    out, at = {}, 0
    for n in SMALL:
        size = int(np.prod(SMALL_SHAPES[n]))
        out[n] = flat[at:at + size].reshape(SMALL_SHAPES[n])
        at += _small_rows(n) * LANES
    return out


def _pairs_cols(a):
    t = a.shape[0]
    return jnp.transpose(a.reshape(t, FOX_HEADS // 2, 2), (1, 0, 2))


def _pairs_rows(a):
    return a.reshape(FOX_HEADS // 2, 2, a.shape[1])


def _layer_fwd(h0, mem, p, w_in, other_weights):
    s = {'h0': h0}
    s['xn1'], z, f = _norm_matmul(h0, p['g_mix'], w_in, tn=IN_COLS_PAD, tail=LANES, name="norm_in_proj_fwd")
    s['z'] = z
    s['qn'], s['kn'], s['vb'], q2 = _qkv_prep(z, p['g_q_fox'], p['g_k_fox'])
    s['ft'] = jnp.transpose(f[:, :FOX_HEADS])
    c = _forget_cumsum(s['ft'], p['b_forget'])
    s['ccol'], s['crow'] = _pairs_cols(jnp.transpose(c)), _pairs_rows(c)
    s['fox'], s['lse_row'] = _fox_fwd(jnp.transpose(q2), s['kn'], jnp.transpose(s['vb']), s['crow'] * LOG2E,
                                      s['ccol'] * LOG2E)
    pool, s['mixed'] = _pool_fwd(z, p['w_pool'].astype(BF16), p['pool_scale'])
    s['cat'] = jnp.concatenate([s['fox'], pool], axis=1)
    w = dict(other_weights(s['lse_row']), w_in=w_in)
    h1 = _matmul(s['cat'], w['w_out'], res=h0, name="out_proj_fwd")
    s['h1'] = h1

    s['hn2'], s['mq_raw'] = _norm_matmul(h1, p['g_mem_q'], w['w_mem_q'], name="norm_mem_q_fwd")
    s['mn'], s['mkv'], s['mk'], s['mv'] = _mem_kv_fwd(mem, p['g_mem_kv'], w['w_mem_kv'], p['g_k_mem'])
    s['mo'] = _cross_fwd(s['mq_raw'], p['g_q_mem'], s['mk'], s['mv'])
    h2 = _matmul(s['mo'], w['w_mem_out'], res=h1, name="mem_out_fwd")
    s['h2'] = h2

    s['hn3'], s['gu'], s['act'] = _norm_gate_up_swiglu(h2, p['g_ffn'], w['w_gate_up'])
    h3 = _matmul(s['act'], w['w_down'], res=h2, name="down_fwd")
    return h3, s, w


def _layer_bwd(dh, mem, p, w, s, after=None, at_mixer=None):
    g = {}
    g['w_down'] = _matmul(s['act'], dh, ta=True, out_dtype=BF16, tm=1408, tn=512, tk=1024, name="down_dw")
    dgu = _down_dx_swiglu_bwd(dh, w['w_down'], s['gu'], after=after)
    g['w_gate_up'] = _matmul(s['hn3'], dgu, ta=True, out_dtype=BF16, tm=1024, tn=1408, tk=1024, name="gate_up_dw")
    dh, g['g_ffn'] = _matmul_norm_bwd(dgu, w['w_gate_up'], s['h2'], p['g_ffn'], dh, name="gate_up_dx_norm_bwd")

    g['w_mem_out'] = _matmul(s['mo'], dh, ta=True, out_dtype=BF16, tm=512, tn=1024, tk=1024, name="mem_out_dw")
    dmo = _matmul(dh, w['w_mem_out'], tb=True, out_dtype=BF16, name="mem_out_dx")
    dmq, dmk, dmv, g['g_q_mem'] = _cross_bwd(s['mq_raw'], dmo, p['g_q_mem'], s['mk'], s['mv'])
    g['w_mem_kv'], g['g_mem_kv'], g['g_k_mem'] = _mem_kv_bwd(dmk, dmv, s['mkv'], s['mn'], mem, p['g_mem_kv'],
                                                               w['w_mem_kv'], p['g_k_mem'])
    g['w_mem_q'] = _matmul(s['hn2'], dmq, ta=True, out_dtype=BF16, tm=1024, tn=512, tk=1024, name="mem_q_dw")
    dh, g['g_mem_q'] = _matmul_norm_bwd(dmq, w['w_mem_q'], s['h1'], p['g_mem_q'], dh, name="mem_q_dx_norm_bwd")

    g['w_out'] = _matmul(s['cat'], dh, ta=True, out_dtype=BF16, tm=1024, tn=512, tk=1024, name="out_proj_dw")
    after = at_mixer(g) if at_mixer is not None else None
    dcat = _matmul(dh, w['w_out'], tb=True, after=after, tn=1024, name="out_proj_dx")
    dpin, g['w_pool'], dscale = _pool_bwd(dcat, s['mixed'], p['w_pool'].astype(BF16), p['pool_scale'])
    g['pool_scale'] = dscale.reshape(POOL_WIDTH)
    do, delta = _fox_delta(dcat, s['fox'])
    delta_row = _pairs_rows(jnp.transpose(delta[:, :FOX_HEADS]))
    dq, dk, dv, dc_keys, dc_queries = _fox_bwd(s['qn'], s['kn'], jnp.transpose(s['kn']), s['vb'], do, s['lse_row'],
                                               delta_row, s['crow'], s['ccol'])
    dft, db = _forget_cumsum_bwd(dc_keys, dc_queries, s['ft'], p['b_forget'])
    g['b_forget'] = db.reshape(FOX_HEADS)
    dfpad = jnp.pad(jnp.transpose(dft), ((0, 0), (0, LANES - FOX_HEADS)))
    dz, g['g_q_fox'], g['g_k_fox'] = _mixer_dz(s['z'], dq, dk, dv, dpin, dfpad, p['g_q_fox'], p['g_k_fox'])
    g['w_in'] = _matmul(s['xn1'], dz, ta=True, out_dtype=BF16, tm=512, tn=IN_COLS_PAD, tk=1024, name="in_proj_dw")
    dh, g['g_mix'] = _matmul_norm_bwd(dz, w['w_in'], s['h0'], p['g_mix'], dh, name="in_proj_dx_norm_bwd")
    return dh, g


def _local_step(x2, mem2, target2, small, w_in, other_weights, send_grads):
    h = x2
    saved, full = [], []
    for l in range(DEPTH):
        h, s, w = _layer_fwd(h, mem2, {k: v[l] for k, v in small.items()}, w_in[l], other_weights(l))
        saved.append(s)
        full.append(w)
    loss, dh = _loss_head(h, target2)
    after = None
    grads = [None] * DEPTH
    for l in reversed(range(1, DEPTH)):
        dh, grads[l] = _layer_bwd(dh, mem2, {k: v[l] for k, v in small.items()}, full[l], saved[l], after=after)
        after = send_grads(l, BIG, grads[l])
    dh, grads[0] = _layer_bwd(dh, mem2, {k: v[0] for k, v in small.items()}, full[0], saved[0], after=after,
                              at_mixer=lambda g: send_grads(0, SENT_AT_MIXER, g))
    return loss, dh, grads


def kernel(x, mem, g_mix, w_in, b_forget, g_q_fox, g_k_fox, w_pool, pool_scale, w_out, g_mem_q, g_mem_kv, w_mem_q, w_mem_kv, g_q_mem, g_k_mem, w_mem_out, g_ffn, w_gate_up, w_down, loss_target, m_g_mix, m_w_in, m_b_forget, m_g_q_fox, m_g_k_fox, m_w_pool, m_pool_scale, m_w_out, m_g_mem_q, m_g_mem_kv, m_w_mem_q, m_w_mem_kv, m_g_q_mem, m_g_k_mem, m_w_mem_out, m_g_ffn, m_w_gate_up, m_w_down, v_g_mix, v_w_in, v_b_forget, v_g_q_fox, v_g_k_fox, v_w_pool, v_pool_scale, v_w_out, v_g_mem_q, v_g_mem_kv, v_w_mem_q, v_w_mem_kv, v_g_q_mem, v_g_k_mem, v_w_mem_out, v_g_ffn, v_w_gate_up, v_w_down):
    weights = dict(g_mix=g_mix, w_in=w_in, b_forget=b_forget, g_q_fox=g_q_fox, g_k_fox=g_k_fox, w_pool=w_pool,
                   pool_scale=pool_scale, w_out=w_out, g_mem_q=g_mem_q, g_mem_kv=g_mem_kv, w_mem_q=w_mem_q,
                   w_mem_kv=w_mem_kv, g_q_mem=g_q_mem, g_k_mem=g_k_mem, w_mem_out=w_mem_out, g_ffn=g_ffn,
                   w_gate_up=w_gate_up, w_down=w_down)
    mom_m = dict(g_mix=m_g_mix, w_in=m_w_in, b_forget=m_b_forget, g_q_fox=m_g_q_fox, g_k_fox=m_g_k_fox, w_pool=m_w_pool,
                 pool_scale=m_pool_scale, w_out=m_w_out, g_mem_q=m_g_mem_q, g_mem_kv=m_g_mem_kv, w_mem_q=m_w_mem_q,
                 w_mem_kv=m_w_mem_kv, g_q_mem=m_g_q_mem, g_k_mem=m_g_k_mem, w_mem_out=m_w_mem_out, g_ffn=m_g_ffn,
                 w_gate_up=m_w_gate_up, w_down=m_w_down)
    mom_v = dict(g_mix=v_g_mix, w_in=v_w_in, b_forget=v_b_forget, g_q_fox=v_g_q_fox, g_k_fox=v_g_k_fox, w_pool=v_w_pool,
                 pool_scale=v_pool_scale, w_out=v_w_out, g_mem_q=v_g_mem_q, g_mem_kv=v_g_mem_kv, w_mem_q=v_w_mem_q,
                 w_mem_kv=v_w_mem_kv, g_q_mem=v_g_q_mem, g_k_mem=v_g_k_mem, w_mem_out=v_w_mem_out, g_ffn=v_g_ffn,
                 w_gate_up=v_w_gate_up, w_down=v_w_down)

    (w_in_all,) = _all_gather([weights['w_in'].astype(BF16)], name="w_in_all_gather")
    later = [(n, l) for l in range(DEPTH) for n in BIG if n != 'w_in']
    shards = [weights[n][l].astype(BF16) for n, l in later]
    weights_sent = _send_start(shards, _landing_zones(shards, False), blocks=False, name="weights_send_start",
                               after=w_in_all)
    small = {n: weights[n] for n in SMALL}
    small['g_mix'] = small['g_mix'] + weights_sent['token'][0, 0]
    gathered = {}

    def other_weights(l):
        def get(after):
            if not gathered:
                gathered.update(zip(later, _send_wait(weights_sent, after, name="weights_send_wait")))
            return {n: _full_weight(n, gathered[n, l]) for n in BIG if n != 'w_in'}
        return get

    grads_sent = []

    def send_grads(l, names, g):
        blocks = [_grad_blocks(n, g[n]) for n in names]
        sent = _send_start(blocks, _landing_zones(blocks, True), blocks=True, name=f"grads{l}_send_start")
        grads_sent.append((l, names, sent))
        return sent['token']

    w_in_full = [_full_weight('w_in', w_in_all[:, l]) for l in range(DEPTH)]
    loss_part, grad_x, grads = _local_step(x[0], mem[0], loss_target[0], small, w_in_full, other_weights, send_grads)
    loss = lax.psum(loss_part[0, 0], ("x", "y", "c"))

    last = [n for n in BIG if n not in SENT_AT_MIXER]
    landed = {(0, n): a for n, a in zip(last, _exchange([_grad_blocks(n, grads[0][n]) for n in last],
                                                         name="grads0_exchange"))}
    for l, names, sent in grads_sent:
        landed.update({(l, n): a for n, a in zip(names, _send_wait(sent, grad_x, name=f"grads{l}_send_wait"))})
    grad = {n: jnp.concatenate([_sum_slots(landed[l, n], name="grad_sum_" + n) for l in range(DEPTH)], 0) for n in BIG}
    small_part = _pack_small({n: jnp.stack([grads[l][n] for l in range(DEPTH)], 0) for n in SMALL})
    (small_all,) = _all_gather([small_part], name="small_grads_all_gather")
    small_sum = _sum_slots(small_all, name="grad_sum_small")

    delta, new_m, new_v = {}, {}, {}
    for n in BIG:
        shape = weights[n].shape
        two_d = lambda a: a.reshape(shape[0] * shape[1], shape[2])
        d, nm, nv = _adamw(two_d(weights[n]), grad[n], two_d(mom_m[n]), two_d(mom_v[n]), name="adamw_" + n)
        grad[n], delta[n], new_m[n], new_v[n] = (a.reshape(shape) for a in (grad[n], d, nm, nv))
    d, nm, nv = _adamw(_pack_small(weights), small_sum, _pack_small(mom_m), _pack_small(mom_v), name="adamw_small")
    grad.update(_unpack_small(small_sum))
    delta.update(_unpack_small(d))
    new_m.update(_unpack_small(nm))
    new_v.update(_unpack_small(nv))

    return (loss, grad_x[None], *[grad[n] for n in WEIGHTS], *[delta[n] for n in WEIGHTS],
            *[new_m[n] for n in WEIGHTS], *[new_v[n] for n in WEIGHTS])
```

```python
import functools

import numpy as np
import jax
import jax.numpy as jnp
from jax import lax
from jax.experimental import pallas as pl
from jax.experimental.pallas import tpu as pltpu

F32 = jnp.float32
BF16 = jnp.bfloat16

N_DEV = 8
D_MODEL = 1024
DEPTH = 2
FOX_HEADS = 8
FOX_HEAD_DIM = 64
FOX_WIDTH = 512
POOL_WIDTH = 512
POOL_WINDOWS = (2, 4, 8, 16)
POOL_GROUP_DIM = 128
POOL_HALO = 16
IN_COLS = 2056
IN_COLS_PAD = 2176
MEM_HEADS = 4
MEM_HEAD_DIM = 128
MEM_WIDTH = 512
D_FF = 2816
EPS = 1e-6
FOX_SCALE = FOX_HEAD_DIM ** -0.5
LOG2E = 1.4426950408889634
LN2 = 0.6931471805599453
FOX_ACC_ROWS = FOX_HEAD_DIM + 16
MEM_SCALE = MEM_HEAD_DIM ** -0.5
LANES = 128

ADAM_LR = 0.001
ADAM_B1 = 0.9
ADAM_B2 = 0.999
ADAM_EPS = 1e-08
ADAM_WD = 0.01
ADAM_STEP = 10

VMEM_LIMIT = 56 * 1024 * 1024
MESH_ID = pl.DeviceIdType.MESH

WEIGHTS = ['g_mix', 'w_in', 'b_forget', 'g_q_fox', 'g_k_fox', 'w_pool', 'pool_scale', 'w_out', 'g_mem_q', 'g_mem_kv',
           'w_mem_q', 'w_mem_kv', 'g_q_mem', 'g_k_mem', 'w_mem_out', 'g_ffn', 'w_gate_up', 'w_down']
BIG = ['w_in', 'w_out', 'w_mem_q', 'w_mem_kv', 'w_mem_out', 'w_gate_up', 'w_down']
SMALL = [n for n in WEIGHTS if n not in BIG]
SENT_AT_MIXER = ['w_down', 'w_gate_up', 'w_mem_out', 'w_mem_q', 'w_mem_kv', 'w_out']


def _pcall(body, **kw):
    return pl.pallas_call(body, **kw)


def _params(*sem):
    return pltpu.CompilerParams(dimension_semantics=sem or None, vmem_limit_bytes=VMEM_LIMIT)


def _dot(a, b, dims=None):
    if dims is None:
        return jnp.dot(a, b, preferred_element_type=F32)
    return lax.dot_general(a, b, (dims, ((), ())), preferred_element_type=F32)


NT = ((1,), (1,))
TN = ((0,), (0,))


def _dot_exact(x, ones_bf16, terms=3):
    hi = x.astype(BF16)
    r1 = x - hi.astype(F32)
    mid = r1.astype(BF16)
    if terms == 2:
        return _dot(hi, ones_bf16) + _dot(mid, ones_bf16)
    lo = (r1 - mid.astype(F32)).astype(BF16)
    return _dot(hi, ones_bf16) + _dot(mid, ones_bf16) + _dot(lo, ones_bf16)


def _matmul(a, b, *, ta=False, tb=False, out_dtype=F32, res=None, after=None, tm=1024, tn=512, tk=None, name):
    planes = b.shape[0] if b.ndim == 3 else None
    bshape = b.shape[-2:]
    m, k = (a.shape[1], a.shape[0]) if ta else a.shape
    n = bshape[0] if tb else bshape[1]
    assert k == (bshape[1] if tb else bshape[0])
    tm, tn = min(tm, m), min(tn, n)
    tk = min(tk or k, k)
    assert m % tm == 0 and n % tn == 0 and k % tk == 0, (name, m, n, k, tm, tn, tk)
    nk = k // tk
    dims = ((0 if ta else 1,), (1 if tb else 0,))

    def body(*refs):
        a_ref, b_ref = refs[0], refs[1]
        r_ref = refs[2] if res is not None else None
        o_ref = refs[2 + (res is not None) + (after is not None)]
        part = _dot(a_ref[...].astype(BF16), b_ref[...].astype(BF16), dims)

        def finish(acc):
            if r_ref is not None:
                acc = acc + r_ref[...]
            o_ref[...] = acc.astype(o_ref.dtype)

        if nk == 1:
            finish(part)
        else:
            acc_ref = refs[-1]
            kk = pl.program_id(3)

            @pl.when(kk == 0)
            def _():
                acc_ref[...] = part

            @pl.when(kk > 0)
            def _():
                acc_ref[...] += part

            @pl.when(kk == nk - 1)
            def _():
                finish(acc_ref[...])

    lead = (lambda p: (p,)) if planes else (lambda p: ())
    sq = (None,) if planes else ()
    a_spec = pl.BlockSpec((tk, tm), lambda p, i, j, kk: (kk, i)) if ta else pl.BlockSpec((tm, tk), lambda p, i, j, kk: (i, kk))
    b_spec = (pl.BlockSpec(sq + (tn, tk), lambda p, i, j, kk: lead(p) + (j, kk)) if tb
              else pl.BlockSpec(sq + (tk, tn), lambda p, i, j, kk: lead(p) + (kk, j)))
    o_spec = pl.BlockSpec(sq + (tm, tn), lambda p, i, j, kk: lead(p) + (i, j))
    in_specs = ([a_spec, b_spec] + ([o_spec] if res is not None else [])
                + ([pl.BlockSpec(memory_space=pl.ANY)] if after is not None else []))
    args = (a, b) + ((res,) if res is not None else ()) + ((after,) if after is not None else ())
    return _pcall(
        body, name=name, grid=(planes or 1, m // tm, n // tn, nk), in_specs=in_specs, out_specs=o_spec,
        out_shape=jax.ShapeDtypeStruct(((planes,) if planes else ()) + (m, n), out_dtype),
        scratch_shapes=[pltpu.VMEM((tm, tn), F32)] if nk > 1 else [],
        compiler_params=_params("parallel", "parallel", "parallel", "arbitrary"),
    )(*args)


def _rms(x):
    return lax.rsqrt(jnp.mean(x * x, axis=-1, keepdims=True) + EPS)


def _norm_matmul(h, g, w, *, tm=512, tn=512, tail=0, name):
    t, d = h.shape
    n = w.shape[1]
    tn = min(tn, n)
    assert t % tm == 0 and n % tn == 0 and (not tail or tn == n)

    def body(h_ref, g_ref, w_ref, xn_ref, y_ref, *tail_ref):
        @pl.when(pl.program_id(1) == 0)
        def _():
            x = h_ref[...]
            xn_ref[...] = (x * _rms(x) * g_ref[...]).astype(BF16)

        y = _dot(xn_ref[...], w_ref[...])
        if tail:
            y_ref[...] = y[:, :n - tail]
            tail_ref[0][...] = y[:, n - tail:]
        else:
            y_ref[...] = y

    row = pl.BlockSpec((tm, d), lambda i, j: (i, 0))
    tails = ([pl.BlockSpec((tm, tail), lambda i, j: (i, 0))], [jax.ShapeDtypeStruct((t, tail), F32)]) if tail else ([], [])
    return _pcall(
        body, name=name, grid=(t // tm, n // tn),
        in_specs=[row, pl.BlockSpec((1, d), lambda i, j: (0, 0)), pl.BlockSpec((d, tn), lambda i, j: (0, j))],
        out_specs=[row, pl.BlockSpec((tm, tn - tail), lambda i, j: (i, j))] + tails[0],
        out_shape=[jax.ShapeDtypeStruct((t, d), BF16), jax.ShapeDtypeStruct((t, n - tail), F32)] + tails[1],
        compiler_params=_params("parallel", "arbitrary"),
    )(h, g.reshape(1, d), w)


def _matmul_norm_bwd(a, w, h, g, dres, *, tm=512, tk=None, name):
    stacked = a.ndim == 3
    t = a.shape[-2]
    d, k = w.shape
    tk = a.shape[-1] if stacked else min(tk or k, k)
    nk = k // tk
    assert t % tm == 0 and k % tk == 0 and (not stacked or a.shape[0] == nk)

    def body(a_ref, w_ref, h_ref, g_ref, r_ref, dx_ref, dg_ref, *acc):
        i, kk = pl.program_id(0), pl.program_id(1)
        part = _dot(a_ref[...], w_ref[...], NT)

        def finish(dy):
            x = h_ref[...]
            r = _rms(x)
            xhat = x * r
            gy = dy * g_ref[...]
            dx_ref[...] = r_ref[...] + r * (gy - xhat * jnp.mean(gy * xhat, axis=-1, keepdims=True))
            dg_part = jnp.sum(dy * xhat, axis=0, keepdims=True)

            @pl.when(i == 0)
            def _():
                dg_ref[...] = dg_part

            @pl.when(i > 0)
            def _():
                dg_ref[...] += dg_part

        if nk == 1:
            finish(part)
        else:
            acc_ref = acc[0]

            @pl.when(kk == 0)
            def _():
                acc_ref[...] = part

            @pl.when(kk > 0)
            def _():
                acc_ref[...] += part

            @pl.when(kk == nk - 1)
            def _():
                finish(acc_ref[...])

    a_spec = (pl.BlockSpec((None, tm, tk), lambda i, kk: (kk, i, 0)) if stacked
              else pl.BlockSpec((tm, tk), lambda i, kk: (i, kk)))
    row = pl.BlockSpec((tm, d), lambda i, kk: (i, 0))
    vec = pl.BlockSpec((1, d), lambda i, kk: (0, 0))
    dx, dg = _pcall(
        body, name=name, grid=(t // tm, nk),
        in_specs=[a_spec, pl.BlockSpec((d, tk), lambda i, kk: (0, kk)), row, vec, row], out_specs=[row, vec],
        out_shape=[jax.ShapeDtypeStruct((t, d), F32), jax.ShapeDtypeStruct((1, d), F32)],
        scratch_shapes=[pltpu.VMEM((tm, d), F32)] if nk > 1 else [],
        compiler_params=_params("arbitrary", "arbitrary"),
    )(a, w, h, g.reshape(1, d), dres)
    return dx, dg.reshape(d)


def _norm_gate_up_swiglu(h, g, w, *, tm=512, tn=1408):
    t, d = h.shape
    nj = D_FF // tn
    assert t % tm == 0 and D_FF % tn == 0

    def body(h_ref, g_ref, wg_ref, wu_ref, hn_ref, gu_ref, act_ref):
        @pl.when(pl.program_id(1) == 0)
        def _():
            x = h_ref[...]
            hn_ref[...] = (x * _rms(x) * g_ref[...]).astype(BF16)

        hn = hn_ref[...]
        gate = _dot(hn, wg_ref[...])
        up = _dot(hn, wu_ref[...])
        gu_ref[0] = gate
        gu_ref[1] = up
        act_ref[...] = (gate * jax.nn.sigmoid(gate) * up).astype(BF16)

    row = pl.BlockSpec((tm, d), lambda i, j: (i, 0))
    return _pcall(
        body, name="gate_up_swiglu_fwd", grid=(t // tm, nj),
        in_specs=[row, pl.BlockSpec((1, d), lambda i, j: (0, 0)), pl.BlockSpec((d, tn), lambda i, j: (0, j)),
                  pl.BlockSpec((d, tn), lambda i, j: (0, nj + j))],
        out_specs=[row, pl.BlockSpec((2, tm, tn), lambda i, j: (0, i, j)), pl.BlockSpec((tm, tn), lambda i, j: (i, j))],
        out_shape=[jax.ShapeDtypeStruct((t, d), BF16), jax.ShapeDtypeStruct((2, t, D_FF), F32),
                   jax.ShapeDtypeStruct((t, D_FF), BF16)],
        compiler_params=_params("parallel", "arbitrary"),
    )(h, g.reshape(1, d), w, w)


def _down_dx_swiglu_bwd(dh, w_down, gu, *, after=None, tm=512, tn=1408):
    t, d = dh.shape
    assert t % tm == 0 and D_FF % tn == 0

    def body(dh_ref, w_ref, gu_ref, *rest):
        dgu_ref = rest[-1]
        da = _dot(dh_ref[...].astype(BF16), w_ref[...], NT)
        gate, up = gu_ref[0], gu_ref[1]
        sg = jax.nn.sigmoid(gate)
        silu = gate * sg
        dgu_ref[0] = (da * up * (sg + silu * (1.0 - sg))).astype(BF16)
        dgu_ref[1] = (da * silu).astype(BF16)

    stack = pl.BlockSpec((2, tm, tn), lambda i, j: (0, i, j))
    return _pcall(
        body, name="down_dx_swiglu_bwd", grid=(t // tm, D_FF // tn),
        in_specs=[pl.BlockSpec((tm, d), lambda i, j: (i, 0)), pl.BlockSpec((tn, d), lambda i, j: (j, 0)), stack]
        + ([pl.BlockSpec(memory_space=pl.ANY)] if after is not None else []),
        out_specs=stack, out_shape=jax.ShapeDtypeStruct((2, t, D_FF), BF16),
        compiler_params=_params("parallel", "parallel"),
    )(dh, w_down, gu, *((after,) if after is not None else ()))


def _group_matrix(width, group):
    r = lax.broadcasted_iota(jnp.int32, (width, width), 0) // group
    c = lax.broadcasted_iota(jnp.int32, (width, width), 1) // group
    return (r == c).astype(BF16)


def _qkv_prep(z, gq, gk, *, bt=512):
    t = z.shape[0]
    w = FOX_WIDTH

    def body(q_ref, k_ref, v_ref, gq_ref, gk_ref, qo_ref, ko_ref, vo_ref, q2_ref):
        gm = _group_matrix(w, FOX_HEAD_DIM)
        for x_ref, g_ref, o_ref, scale in ((q_ref, gq_ref, qo_ref, FOX_SCALE), (k_ref, gk_ref, ko_ref, 1.0)):
            x = x_ref[...]
            ms = _dot_exact(x * x, gm, terms=2) * (1.0 / FOX_HEAD_DIM)
            y = x * lax.rsqrt(ms + EPS) * g_ref[...]
            o_ref[...] = (y * scale).astype(BF16)
            if o_ref is qo_ref:
                q2_ref[...] = (y * (scale * LOG2E)).astype(BF16)
        vo_ref[...] = v_ref[...].astype(BF16)

    col = lambda c: pl.BlockSpec((bt, w), lambda i, c=c: (i, c))
    vec = pl.BlockSpec((1, w), lambda i: (0, 0))
    out = pl.BlockSpec((bt, w), lambda i: (i, 0))
    return _pcall(
        body, name="fox_qkv_prep", grid=(t // bt,), in_specs=[col(0), col(1), col(2), vec, vec], out_specs=[out] * 4,
        out_shape=[jax.ShapeDtypeStruct((t, w), BF16)] * 4, compiler_params=_params("parallel"),
    )(z, z, z, jnp.tile(gq, FOX_HEADS).reshape(1, w), jnp.tile(gk, FOX_HEADS).reshape(1, w))


def _log_sigmoid(f):
    return jnp.minimum(f, 0.0) - jnp.log1p(jnp.exp(-jnp.abs(f)))


def _forget_cumsum(ft, b, *, chunk=512):
    hh, t = ft.shape

    def body(f_ref, b_ref, c_ref):
        r = lax.broadcasted_iota(jnp.int32, (chunk, chunk), 0)
        c = lax.broadcasted_iota(jnp.int32, (chunk, chunk), 1)
        upper = (r <= c).astype(BF16)
        carry = jnp.zeros((hh, 1), F32)
        for ch in range(t // chunk):
            sl = slice(ch * chunk, (ch + 1) * chunk)
            cs = _dot_exact(_log_sigmoid(f_ref[:, sl] + b_ref[...]), upper) + carry
            c_ref[:, sl] = cs
            carry = cs[:, chunk - 1:chunk]

    return _pcall(body, name="fox_forget_cumsum", out_shape=jax.ShapeDtypeStruct((hh, t), F32),
                  compiler_params=_params())(ft, b.reshape(hh, 1))


def _forget_cumsum_bwd(dc_keys, dc_queries, ft, b, *, chunk=512):
    hh, t = ft.shape

    def body(dck_ref, dcq_ref, f_ref, b_ref, df_ref, db_ref):
        r = lax.broadcasted_iota(jnp.int32, (chunk, chunk), 0)
        c = lax.broadcasted_iota(jnp.int32, (chunk, chunk), 1)
        lower = (r >= c).astype(BF16)
        carry = jnp.zeros((hh, 1), F32)
        db = jnp.zeros((hh, 1), F32)
        for ch in reversed(range(t // chunk)):
            sl = slice(ch * chunk, (ch + 1) * chunk)
            dls = _dot_exact(dck_ref[:, sl] + dcq_ref[:, sl], lower) + carry
            carry = dls[:, 0:1]
            df = dls * jax.nn.sigmoid(-(f_ref[:, sl] + b_ref[...]))
            df_ref[:, sl] = df
            db = db + jnp.sum(df, axis=1, keepdims=True)
        db_ref[...] = db

    return _pcall(body, name="fox_forget_cumsum_bwd",
                  out_shape=[jax.ShapeDtypeStruct((hh, t), F32), jax.ShapeDtypeStruct((hh, 1), F32)],
                  compiler_params=_params())(dc_keys, dc_queries, ft, b.reshape(hh, 1))


def _lane_is_first_head():
    return lax.broadcasted_iota(jnp.int32, (1, LANES), 1) < FOX_HEAD_DIM


def _fox_fwd(q2t, kn, vt, crow2, ccol2, *, bq=512, bk=1024):
    t = kn.shape[0]
    nq = t // bq
    pairs = FOX_WIDTH // LANES
    assert t % bk == 0
    tiles = [(i, j) for i in range(nq) for j in range(i * bq // bk, -1, -1)]
    it = jnp.asarray(np.array([a for a, _ in tiles], np.int32))
    jt = jnp.asarray(np.array([b for _, b in tiles], np.int32))

    def body(it_ref, jt_ref, qt_ref, k_ref, vt_ref, cr_ref, cc_ref, o_ref, lse_ref, m_sc, acc_sc):
        s_id = pl.program_id(1)
        i, j = it_ref[s_id], jt_ref[s_id]
        first = _lane_is_first_head()
        holds_diagonal = j == (i * bq) // bk

        @pl.when(holds_diagonal)
        def _():
            m_sc[...] = jnp.full(m_sc.shape, -jnp.inf, F32)
            acc_sc[...] = jnp.zeros(acc_sc.shape, F32)

        def scores():
            k2, qt2 = k_ref[...], qt_ref[...]
            return [_dot(jnp.where(first if hh == 0 else jnp.logical_not(first), k2, jnp.zeros_like(k2)), qt2)
                    for hh in range(2)]

        def pv(hh, pt_bf16):
            v_ones = jnp.concatenate([vt_ref[hh * FOX_HEAD_DIM:(hh + 1) * FOX_HEAD_DIM, :],
                                      jnp.ones((FOX_ACC_ROWS - FOX_HEAD_DIM, bk), BF16)], axis=0)
            return _dot(v_ones, pt_bf16)

        def tile(diagonal):
            sc = scores()
            for hh in range(2):
                ut = sc[hh] - cc_ref[0, :, hh:hh + 1]
                if diagonal:
                    key = j * bk + lax.broadcasted_iota(jnp.int32, ut.shape, 0)
                    query = i * bq + lax.broadcasted_iota(jnp.int32, ut.shape, 1)
                    ut = jnp.where(key <= query, ut, -jnp.inf)
                c_t = cr_ref[0, hh:hh + 1, :]
                m_prev = m_sc[hh]
                m_new = jnp.maximum(m_prev, jnp.max(ut, axis=0, keepdims=True) + c_t)
                acc_sc[hh] = jnp.exp2(m_prev - m_new) * acc_sc[hh] + pv(hh, jnp.exp2(ut + (c_t - m_new)).astype(BF16))
                m_sc[hh] = m_new

        @pl.when(holds_diagonal)
        def _():
            tile(True)

        @pl.when(jnp.logical_not(holds_diagonal))
        def _():
            tile(False)

        @pl.when(j == 0)
        def _():
            sums = [acc_sc[hh, FOX_HEAD_DIM:FOX_HEAD_DIM + 1, :] for hh in range(2)]
            ot = jnp.concatenate([acc_sc[hh, :FOX_HEAD_DIM, :] / sums[hh] for hh in range(2)], axis=0)
            o_ref[...] = jnp.transpose(ot).astype(o_ref.dtype)
            for hh in range(2):
                lse_ref[0, hh:hh + 1, :] = m_sc[hh] * LN2 + jnp.log(sums[hh])

    qspec = pl.BlockSpec((bq, LANES), lambda p, s, it, jt: (it[s], p))
    kspec = pl.BlockSpec((bk, LANES), lambda p, s, it, jt: (jt[s], p))
    vtspec = pl.BlockSpec((LANES, bk), lambda p, s, it, jt: (p, jt[s]))
    qtspec = pl.BlockSpec((LANES, bq), lambda p, s, it, jt: (p, it[s]))
    rowq = pl.BlockSpec((1, 2, bq), lambda p, s, it, jt: (p, 0, it[s]))
    colk = pl.BlockSpec((1, bk, 2), lambda p, s, it, jt: (p, jt[s], 0))
    return _pcall(
        body, name="fox_attention_fwd",
        grid_spec=pltpu.PrefetchScalarGridSpec(
            num_scalar_prefetch=2, grid=(pairs, len(tiles)),
            in_specs=[qtspec, kspec, vtspec, rowq, colk], out_specs=[qspec, rowq],
            scratch_shapes=[pltpu.VMEM((2, 1, bq), F32), pltpu.VMEM((2, FOX_ACC_ROWS, bq), F32)]),
        out_shape=[jax.ShapeDtypeStruct((t, FOX_WIDTH), BF16), jax.ShapeDtypeStruct((pairs, 2, t), F32)],
        compiler_params=_params("parallel", "arbitrary"),
    )(it, jt, q2t, kn, vt, crow2, ccol2)


def _fox_bwd(qn, qnt, kn, knt, vb, do, dot, lse_row, delta_row, crow, ccol, *, bq=512, bk=512):
    t = qn.shape[0]
    nq, nk = t // bq, t // bk
    pairs = FOX_WIDTH // LANES
    tiles = [(i, j) for j in range(nk) for i in range(j * bk // bq, nq)]
    it = jnp.asarray(np.array([a for a, _ in tiles], np.int32))
    jt = jnp.asarray(np.array([b for _, b in tiles], np.int32))

    def body(it_ref, jt_ref, q_ref, qt_ref, k_ref, kt_ref, v_ref, do_ref, dot_ref, lse_ref, dl_ref, cr_ref, cc_ref,
             dqt_ref, dk_ref, dv_ref, dc_ref, dr_ref, dk_sc, dv_sc, dc_sc):
        s_id = pl.program_id(1)
        i, j = it_ref[s_id], jt_ref[s_id]
        first = _lane_is_first_head()
        holds_diagonal = i == (j * bk) // bq

        @pl.when(s_id == 0)
        def _():
            dqt_ref[...] = jnp.zeros(dqt_ref.shape, F32)
            dr_ref[...] = jnp.zeros(dr_ref.shape, F32)

        @pl.when(holds_diagonal)
        def _():
            dk_sc[...] = jnp.zeros(dk_sc.shape, F32)
            dv_sc[...] = jnp.zeros(dv_sc.shape, F32)
            dc_sc[...] = jnp.zeros(dc_sc.shape, F32)

        def tile(diagonal):
            q2, qt2, k2, kt2, v2, do2, dot2 = (q_ref[...], qt_ref[...], k_ref[...], kt_ref[...], v_ref[...], do_ref[...],
                                               dot_ref[...])
            dk_t, dv_t, dqt_t = [], [], []
            for hh in range(2):
                mine = first if hh == 0 else jnp.logical_not(first)
                st = (_dot(jnp.where(mine, k2, jnp.zeros_like(k2)), qt2)
                      + (cr_ref[0, hh:hh + 1, :] - cc_ref[0, :, hh:hh + 1]))
                if diagonal:
                    key = j * bk + lax.broadcasted_iota(jnp.int32, st.shape, 0)
                    query = i * bq + lax.broadcasted_iota(jnp.int32, st.shape, 1)
                    st = jnp.where(key <= query, st, -jnp.inf)
                pt = jnp.exp(st - lse_ref[0, hh:hh + 1, :])
                dv_t.append(_dot(pt.astype(BF16), do2))
                dpt = _dot(jnp.where(mine, v2, jnp.zeros_like(v2)), dot2)
                dst = pt * (dpt - dl_ref[0, hh:hh + 1, :])
                dc_sc[hh] += jnp.sum(dst, axis=1, keepdims=True)
                dr_ref[0, i, hh:hh + 1, :] += jnp.sum(dst, axis=0, keepdims=True)
                dsb = dst.astype(BF16)
                dk_t.append(_dot(dsb, q2))
                dqt_t.append(_dot(kt2[hh * FOX_HEAD_DIM:(hh + 1) * FOX_HEAD_DIM], dsb))
            dk_sc[...] += jnp.where(first, dk_t[0], dk_t[1])
            dv_sc[...] += jnp.where(first, dv_t[0], dv_t[1])
            dqt_ref[0, i] += jnp.concatenate(dqt_t, axis=0)

        @pl.when(jnp.logical_not(holds_diagonal))
        def _():
            tile(False)

        @pl.when(holds_diagonal)
        def _():
            tile(True)

        @pl.when(i == nq - 1)
        def _():
            dk_ref[...] = dk_sc[...]
            dv_ref[...] = dv_sc[...]
            for hh in range(2):
                dc_ref[0, :, hh:hh + 1] = -dc_sc[hh]

    qspec = pl.BlockSpec((bq, LANES), lambda p, s, it, jt: (it[s], p))
    qtspec = pl.BlockSpec((LANES, bq), lambda p, s, it, jt: (p, it[s]))
    kspec = pl.BlockSpec((bk, LANES), lambda p, s, it, jt: (jt[s], p))
    ktspec = pl.BlockSpec((LANES, bk), lambda p, s, it, jt: (p, jt[s]))
    rowq = pl.BlockSpec((1, 2, bq), lambda p, s, it, jt: (p, 0, it[s]))
    colk = pl.BlockSpec((1, bk, 2), lambda p, s, it, jt: (p, jt[s], 0))
    dqt_spec = pl.BlockSpec((1, nq, LANES, bq), lambda p, s, it, jt: (p, 0, 0, 0))
    dr_spec = pl.BlockSpec((1, nq, 2, bq), lambda p, s, it, jt: (p, 0, 0, 0))
    dqt, dk, dv, dc_keys, dc_queries = _pcall(
        body, name="fox_attention_bwd",
        grid_spec=pltpu.PrefetchScalarGridSpec(
            num_scalar_prefetch=2, grid=(pairs, len(tiles)),
            in_specs=[qspec, qtspec, kspec, ktspec, kspec, qspec, qtspec, rowq, rowq, rowq, colk],
            out_specs=[dqt_spec, kspec, kspec, colk, dr_spec],
            scratch_shapes=[pltpu.VMEM((bk, LANES), F32), pltpu.VMEM((bk, LANES), F32), pltpu.VMEM((2, bk, 1), F32)]),
        out_shape=[jax.ShapeDtypeStruct((pairs, nq, LANES, bq), F32), jax.ShapeDtypeStruct((t, FOX_WIDTH), F32),
                   jax.ShapeDtypeStruct((t, FOX_WIDTH), F32), jax.ShapeDtypeStruct((pairs, t, 2), F32),
                   jax.ShapeDtypeStruct((pairs, nq, 2, bq), F32)],
        compiler_params=_params("parallel", "arbitrary"),
    )(it, jt, qn, qnt, kn, knt, vb, do, dot, lse_row, delta_row, crow, ccol)
    dq = jnp.transpose(dqt, (1, 3, 0, 2)).reshape(t, FOX_WIDTH)
    dc_keys = jnp.transpose(dc_keys, (0, 2, 1)).reshape(FOX_HEADS, t)
    dc_queries = jnp.transpose(dc_queries, (0, 2, 1, 3)).reshape(FOX_HEADS, t)
    return dq, dk, dv, dc_keys, dc_queries


def _fox_delta(dcat, fox, *, bt=512):
    t = fox.shape[0]
    w = FOX_WIDTH

    def body(do_ref, o_ref, dob_ref, dl_ref):
        dob = do_ref[...].astype(BF16)
        r = lax.broadcasted_iota(jnp.int32, (w, LANES), 0) // FOX_HEAD_DIM
        c = lax.broadcasted_iota(jnp.int32, (w, LANES), 1)
        dl_ref[...] = _dot_exact(dob.astype(F32) * o_ref[...].astype(F32), (r == c).astype(BF16))
        dob_ref[...] = dob

    blk = pl.BlockSpec((bt, w), lambda i: (i, 0))
    return _pcall(
        body, name="fox_delta", grid=(t // bt,), in_specs=[blk, blk],
        out_specs=[blk, pl.BlockSpec((bt, LANES), lambda i: (i, 0))],
        out_shape=[jax.ShapeDtypeStruct((t, w), BF16), jax.ShapeDtypeStruct((t, LANES), F32)],
        compiler_params=_params("parallel"),
    )(dcat, fox)


def _mixer_dz(z, dq, dk, dv, dpin, dfpad, gq, gk, *, bt=256):
    t = z.shape[0]
    w = FOX_WIDTH

    def body(q_ref, k_ref, dq_ref, dk_ref, dv_ref, dp_ref, df_ref, gq_ref, gk_ref, dz_ref, dgq_ref, dgk_ref):
        gm = _group_matrix(w, FOX_HEAD_DIM)
        first_step = pl.program_id(0) == 0
        for n, (x_ref, dy_ref, g_ref, dg_ref, scale) in enumerate(
                ((q_ref, dq_ref, gq_ref, dgq_ref, FOX_SCALE), (k_ref, dk_ref, gk_ref, dgk_ref, 1.0))):
            x = x_ref[...]
            r = lax.rsqrt(_dot_exact(x * x, gm, terms=2) * (1.0 / FOX_HEAD_DIM) + EPS)
            xhat = x * r
            dy = dy_ref[...] * scale
            gy = dy * g_ref[...]
            dx = r * (gy - xhat * (_dot_exact(gy * xhat, gm, terms=2) * (1.0 / FOX_HEAD_DIM)))
            dz_ref[:, n * w:(n + 1) * w] = dx.astype(BF16)
            part = jnp.sum(dy * xhat, axis=0, keepdims=True)

            @pl.when(first_step)
            def _():
                dg_ref[...] = part

            @pl.when(jnp.logical_not(first_step))
            def _():
                dg_ref[...] += part

        dz_ref[:, 2 * w:3 * w] = dv_ref[...].astype(BF16)
        dz_ref[:, 3 * w:4 * w] = dp_ref[...].astype(BF16)
        dz_ref[:, 4 * w:] = df_ref[...].astype(BF16)

    col = lambda c: pl.BlockSpec((bt, w), lambda i, c=c: (i, c))
    blk = pl.BlockSpec((bt, w), lambda i: (i, 0))
    vec = pl.BlockSpec((1, w), lambda i: (0, 0))
    dz, dgq, dgk = _pcall(
        body, name="mixer_dz", grid=(t // bt,),
        in_specs=[col(0), col(1), blk, blk, blk, blk, pl.BlockSpec((bt, LANES), lambda i: (i, 0)), vec, vec],
        out_specs=[pl.BlockSpec((bt, IN_COLS_PAD), lambda i: (i, 0)), vec, vec],
        out_shape=[jax.ShapeDtypeStruct((t, IN_COLS_PAD), BF16), jax.ShapeDtypeStruct((1, w), F32),
                   jax.ShapeDtypeStruct((1, w), F32)],
        compiler_params=_params("arbitrary"),
    )(z, z, dq, dk, dv, dpin, dfpad, jnp.tile(gq, FOX_HEADS).reshape(1, w), jnp.tile(gk, FOX_HEADS).reshape(1, w))
    return dz, dgq.reshape(FOX_HEADS, FOX_HEAD_DIM).sum(0), dgk.reshape(FOX_HEADS, FOX_HEAD_DIM).sum(0)


def _pool_fwd(z, wp, scale, *, bt=512):
    t = z.shape[0]
    w = POOL_WIDTH
    hb = bt // POOL_HALO

    def body(p_ref, h_ref, wp_ref, sc_ref, y_ref, mx_ref):
        i = pl.program_id(0)
        cur = p_ref[...]
        halo = jnp.where(i > 0, h_ref[...], 0.0)
        ext = jnp.concatenate([halo, cur], axis=0)
        trow = i * bt + lax.broadcasted_iota(jnp.int32, (bt, 1), 0)
        for g, win in enumerate(POOL_WINDOWS):
            sl = slice(g * LANES, (g + 1) * LANES)
            e = ext[:, sl]
            acc = e[POOL_HALO:]
            for k in range(1, win):
                acc = acc + pltpu.roll(e, k, 0)[POOL_HALO:]
            cnt = jnp.minimum(trow + 1, win).astype(F32)
            mixed = (acc / cnt - cur[:, sl]).astype(BF16)
            mx_ref[:, sl] = mixed
            y_ref[:, sl] = (_dot(mixed, wp_ref[g]) * sc_ref[:, sl]).astype(BF16)

    blk = pl.BlockSpec((bt, w), lambda i: (i, 0))
    return _pcall(
        body, name="pool_fwd", grid=(t // bt,),
        in_specs=[pl.BlockSpec((bt, w), lambda i: (i, 3)),
                  pl.BlockSpec((POOL_HALO, w), lambda i: (jnp.maximum(i * hb - 1, 0), 3)),
                  pl.BlockSpec((len(POOL_WINDOWS), LANES, LANES), lambda i: (0, 0, 0)),
                  pl.BlockSpec((1, w), lambda i: (0, 0))],
        out_specs=[blk, blk], out_shape=[jax.ShapeDtypeStruct((t, w), BF16)] * 2,
        compiler_params=_params("parallel"),
    )(z, z, wp, scale.reshape(1, w))


def _pool_bwd(dcat, mixed, wp, scale, *, bt=512):
    t = mixed.shape[0]
    w = POOL_WIDTH
    hb = bt // POOL_HALO
    nb = t // bt
    n_ext = bt + POOL_HALO

    def body(d_ref, h_ref, mx_ref, wp_ref, sc_ref, dp_ref, dwp_ref, dsc_ref):
        i = pl.program_id(0)
        cur = d_ref[...]
        nxt = jnp.where(i < nb - 1, h_ref[...], 0.0)
        ext = jnp.concatenate([cur, nxt], axis=0)
        trow = i * bt + lax.broadcasted_iota(jnp.int32, (n_ext, 1), 0)

        @pl.when(i == 0)
        def _():
            dwp_ref[...] = jnp.zeros(dwp_ref.shape, F32)
            dsc_ref[...] = jnp.zeros(dsc_ref.shape, F32)

        for g, win in enumerate(POOL_WINDOWS):
            sl = slice(g * LANES, (g + 1) * LANES)
            dy = (ext[:, sl] * sc_ref[:, sl]).astype(BF16)
            dm = _dot(dy, wp_ref[g], NT)
            mixed_g = mx_ref[:, sl]
            dsc_ref[:, sl] += jnp.sum(cur[:, sl] * _dot(mixed_g, wp_ref[g]), axis=0, keepdims=True)
            dwp_ref[g] += _dot(mixed_g, dy[:bt], TN)
            r = dm / jnp.minimum(trow + 1, win).astype(F32)
            acc = r[:bt]
            for k in range(1, win):
                acc = acc + pltpu.roll(r, n_ext - k, 0)[:bt]
            dp_ref[:, sl] = acc - dm[:bt]

    return _pcall(
        body, name="pool_bwd", grid=(nb,),
        in_specs=[pl.BlockSpec((bt, w), lambda i: (i, 1)),
                  pl.BlockSpec((POOL_HALO, w), lambda i: (jnp.minimum((i + 1) * hb, t // POOL_HALO - 1), 1)),
                  pl.BlockSpec((bt, w), lambda i: (i, 0)),
                  pl.BlockSpec((len(POOL_WINDOWS), LANES, LANES), lambda i: (0, 0, 0)),
                  pl.BlockSpec((1, w), lambda i: (0, 0))],
        out_specs=[pl.BlockSpec((bt, w), lambda i: (i, 0)),
                   pl.BlockSpec((len(POOL_WINDOWS), LANES, LANES), lambda i: (0, 0, 0)),
                   pl.BlockSpec((1, w), lambda i: (0, 0))],
        out_shape=[jax.ShapeDtypeStruct((t, w), F32), jax.ShapeDtypeStruct((len(POOL_WINDOWS), LANES, LANES), F32),
                   jax.ShapeDtypeStruct((1, w), F32)],
        compiler_params=_params("arbitrary"),
    )(dcat, dcat, mixed, wp, scale.reshape(1, w))


def _head_rms(x):
    return lax.rsqrt(jnp.mean(x * x, axis=-1, keepdims=True) + EPS)


def _mem_kv_fwd(mem, g_kv, w_kv, g_k):
    mlen, d = mem.shape

    def body(m_ref, g_ref, w_ref, gk_ref, mn_ref, mkv_ref, mk_ref, mv_ref):
        x = m_ref[...]
        mn = (x * lax.rsqrt(jnp.mean(x * x, axis=-1, keepdims=True) + EPS) * g_ref[...]).astype(BF16)
        mn_ref[...] = mn
        mkv = _dot(mn, w_ref[...])
        mkv_ref[...] = mkv
        for h in range(MEM_HEADS):
            sl = slice(h * MEM_HEAD_DIM, (h + 1) * MEM_HEAD_DIM)
            kh = mkv[:, sl]
            mk_ref[:, sl] = (kh * _head_rms(kh) * gk_ref[...]).astype(BF16)
        mv_ref[...] = mkv[:, MEM_WIDTH:].astype(BF16)

    return _pcall(
        body, name="mem_kv_fwd",
        out_shape=[jax.ShapeDtypeStruct((mlen, d), BF16), jax.ShapeDtypeStruct((mlen, 2 * MEM_WIDTH), F32),
                   jax.ShapeDtypeStruct((mlen, MEM_WIDTH), BF16), jax.ShapeDtypeStruct((mlen, MEM_WIDTH), BF16)],
        compiler_params=_params(),
    )(mem, g_kv.reshape(1, d), w_kv, g_k.reshape(1, MEM_HEAD_DIM))


def _mem_kv_bwd(dmk, dmv, mkv, mn, mem, g_kv, w_kv, g_k):
    mlen, d = mem.shape

    def body(dmk_ref, dmv_ref, mkv_ref, mn_ref, m_ref, g_ref, w_ref, gk_ref, dw_ref, dg_ref, dgk_ref, dkv_sc):
        dgk = jnp.zeros((1, MEM_HEAD_DIM), F32)
        for h in range(MEM_HEADS):
            sl = slice(h * MEM_HEAD_DIM, (h + 1) * MEM_HEAD_DIM)
            x = mkv_ref[:, sl]
            r = _head_rms(x)
            xhat = x * r
            dy = dmk_ref[:, sl]
            gy = dy * gk_ref[...]
            dkv_sc[:, sl] = (r * (gy - xhat * jnp.mean(gy * xhat, axis=-1, keepdims=True))).astype(BF16)
            dgk = dgk + jnp.sum(dy * xhat, axis=0, keepdims=True)
        dgk_ref[...] = dgk
        dkv_sc[:, MEM_WIDTH:] = dmv_ref[...].astype(BF16)
        dkv = dkv_sc[...]
        dw_ref[...] = _dot(mn_ref[...], dkv, TN).astype(BF16)
        dmn = _dot(dkv, w_ref[...], NT)
        x = m_ref[...]
        xhat = x * lax.rsqrt(jnp.mean(x * x, axis=-1, keepdims=True) + EPS)
        dg_ref[...] = jnp.sum(dmn * xhat, axis=0, keepdims=True)

    dw, dg, dgk = _pcall(
        body, name="mem_kv_bwd",
        out_shape=[jax.ShapeDtypeStruct((d, 2 * MEM_WIDTH), BF16), jax.ShapeDtypeStruct((1, d), F32),
                   jax.ShapeDtypeStruct((1, MEM_HEAD_DIM), F32)],
        scratch_shapes=[pltpu.VMEM((mlen, 2 * MEM_WIDTH), BF16)],
        compiler_params=_params(),
    )(dmk, dmv, mkv, mn, mem, g_kv.reshape(1, d), w_kv, g_k.reshape(1, MEM_HEAD_DIM))
    return dw, dg.reshape(d), dgk.reshape(MEM_HEAD_DIM)


def _cross_probs(x, g, mk_h):
    r = _head_rms(x)
    xhat = x * r
    qn = (xhat * g).astype(BF16)
    s = _dot(qn, mk_h, NT) * MEM_SCALE
    e = jnp.exp(s - jnp.max(s, axis=-1, keepdims=True))
    return r, xhat, qn, e / jnp.sum(e, axis=-1, keepdims=True)


def _cross_fwd(mq_raw, g_q, mk, mv, *, bt=512):
    t = mq_raw.shape[0]
    mlen = mk.shape[0]

    def body(x_ref, g_ref, mk_ref, mv_ref, o_ref):
        for h in range(MEM_HEADS):
            sl = slice(h * MEM_HEAD_DIM, (h + 1) * MEM_HEAD_DIM)
            _, _, _, p = _cross_probs(x_ref[:, sl], g_ref[...], mk_ref[:, sl])
            o_ref[:, sl] = _dot(p.astype(BF16), mv_ref[:, sl]).astype(BF16)

    blk = pl.BlockSpec((bt, MEM_WIDTH), lambda i: (i, 0))
    kv = pl.BlockSpec((mlen, MEM_WIDTH), lambda i: (0, 0))
    return _pcall(
        body, name="cross_attention_fwd", grid=(t // bt,),
        in_specs=[blk, pl.BlockSpec((1, MEM_HEAD_DIM), lambda i: (0, 0)), kv, kv], out_specs=blk,
        out_shape=jax.ShapeDtypeStruct((t, MEM_WIDTH), BF16), compiler_params=_params("parallel"),
    )(mq_raw, g_q.reshape(1, MEM_HEAD_DIM), mk, mv)


def _cross_bwd(mq_raw, dmo, g_q, mk, mv, *, bt=512):
    t = mq_raw.shape[0]
    mlen = mk.shape[0]

    def body(x_ref, do_ref, g_ref, mk_ref, mv_ref, dx_ref, dmk_ref, dmv_ref, dg_ref):
        @pl.when(pl.program_id(0) == 0)
        def _():
            dmk_ref[...] = jnp.zeros(dmk_ref.shape, F32)
            dmv_ref[...] = jnp.zeros(dmv_ref.shape, F32)
            dg_ref[...] = jnp.zeros(dg_ref.shape, F32)

        for h in range(MEM_HEADS):
            sl = slice(h * MEM_HEAD_DIM, (h + 1) * MEM_HEAD_DIM)
            r, xhat, qn, p = _cross_probs(x_ref[:, sl], g_ref[...], mk_ref[:, sl])
            do = do_ref[:, sl]
            dp = _dot(do, mv_ref[:, sl], NT)
            ds = (p * (dp - jnp.sum(p * dp, axis=-1, keepdims=True)) * MEM_SCALE).astype(BF16)
            dmv_ref[:, sl] += _dot(p.astype(BF16), do, TN)
            dmk_ref[:, sl] += _dot(ds, qn, TN)
            dqn = _dot(ds, mk_ref[:, sl])
            gy = dqn * g_ref[...]
            dx_ref[:, sl] = (r * (gy - xhat * jnp.mean(gy * xhat, axis=-1, keepdims=True))).astype(BF16)
            dg_ref[...] += jnp.sum(dqn * xhat, axis=0, keepdims=True)

    blk = pl.BlockSpec((bt, MEM_WIDTH), lambda i: (i, 0))
    kv = pl.BlockSpec((mlen, MEM_WIDTH), lambda i: (0, 0))
    gs = pl.BlockSpec((1, MEM_HEAD_DIM), lambda i: (0, 0))
    dx, dmk, dmv, dg = _pcall(
        body, name="cross_attention_bwd", grid=(t // bt,),
        in_specs=[blk, blk, gs, kv, kv], out_specs=[blk, kv, kv, gs],
        out_shape=[jax.ShapeDtypeStruct((t, MEM_WIDTH), BF16), jax.ShapeDtypeStruct((mlen, MEM_WIDTH), F32),
                   jax.ShapeDtypeStruct((mlen, MEM_WIDTH), F32), jax.ShapeDtypeStruct((1, MEM_HEAD_DIM), F32)],
        compiler_params=_params("arbitrary"),
    )(mq_raw, dmo, g_q.reshape(1, MEM_HEAD_DIM), mk, mv)
    return dx, dmk, dmv, dg.reshape(MEM_HEAD_DIM)


def _loss_head(y, target, *, bt=512):
    t, d = y.shape

    def body(y_ref, t_ref, dy_ref, l_ref):
        e = y_ref[...] - t_ref[...]
        dy_ref[...] = e * (1.0 / d)
        part = (0.5 / d) * jnp.sum(jnp.sum(e * e, axis=1, keepdims=True), axis=0, keepdims=True)

        @pl.when(pl.program_id(0) == 0)
        def _():
            l_ref[...] = part

        @pl.when(pl.program_id(0) > 0)
        def _():
            l_ref[...] += part

    blk = pl.BlockSpec((bt, d), lambda i: (i, 0))
    dy, loss = _pcall(
        body, name="loss_head", grid=(t // bt,), in_specs=[blk, blk],
        out_specs=[blk, pl.BlockSpec((1, 1), lambda i: (0, 0))],
        out_shape=[jax.ShapeDtypeStruct((t, d), F32), jax.ShapeDtypeStruct((1, 1), F32)],
        compiler_params=_params("arbitrary"),
    )(y, target)
    return loss, dy


def _row_tile(rows, cols, budget=1 << 19):
    best = None
    for cand in range(8, rows + 1, 8):
        if rows % cand == 0 and cand * cols <= budget:
            best = cand
    return best or rows


def _adamw(w, g, m, v, *, name):
    rows, cols = w.shape
    bt = _row_tile(rows, cols)
    c1 = 1.0 - ADAM_B1 ** ADAM_STEP
    c2 = 1.0 - ADAM_B2 ** ADAM_STEP

    def body(w_ref, g_ref, m_ref, v_ref, d_ref, nm_ref, nv_ref):
        g_v = g_ref[...]
        nm = ADAM_B1 * m_ref[...] + (1.0 - ADAM_B1) * g_v
        nv = ADAM_B2 * v_ref[...] + (1.0 - ADAM_B2) * (g_v * g_v)
        nm_ref[...] = nm
        nv_ref[...] = nv
        d_ref[...] = -ADAM_LR * ((nm / c1) / (jnp.sqrt(nv / c2) + ADAM_EPS) + ADAM_WD * w_ref[...])

    blk = pl.BlockSpec((bt, cols), lambda i: (i, 0))
    return _pcall(
        body, name=name, grid=(rows // bt,), in_specs=[blk] * 4, out_specs=[blk] * 3,
        out_shape=[jax.ShapeDtypeStruct((rows, cols), F32)] * 3, compiler_params=_params("parallel"),
    )(w, g, m, v)


def _sum_slots(x, *, name, after=None):
    n, rows, cols = x.shape
    bt = _row_tile(rows, cols, budget=1 << 17)

    def body(x_ref, *rest):
        o_ref = rest[-1]
        acc = x_ref[0].astype(F32)
        for s in range(1, n):
            acc = acc + x_ref[s].astype(F32)
        o_ref[...] = acc

    return _pcall(
        body, name=name, grid=(rows // bt,),
        in_specs=[pl.BlockSpec((n, bt, cols), lambda i: (0, i, 0))]
        + ([pl.BlockSpec(memory_space=pl.ANY)] if after is not None else []),
        out_specs=pl.BlockSpec((bt, cols), lambda i: (i, 0)),
        out_shape=jax.ShapeDtypeStruct((rows, cols), F32), compiler_params=_params("parallel"),
    )(x, *((after,) if after is not None else ()))


def _any_spec():
    return pl.BlockSpec(memory_space=pl.ANY)


def _all_gather(xs, *, name):
    n = len(xs)

    def body(*refs):
        x_refs, out_refs = refs[:n], refs[n:2 * n]
        send_sems, recv_sems, local_sems = refs[2 * n:]
        x, y, c = lax.axis_index("x"), lax.axis_index("y"), lax.axis_index("c")
        me, sibling = (x, y, c), (x, y, 1 - c)
        chips = [(1 - x, y), (x, 1 - y), (1 - x, 1 - y)]

        def slot(a, px, py, pc):
            return out_refs[a].at[4 * px + 2 * py + pc]

        def copy(a, k, block, to, src=None):
            return pltpu.make_async_remote_copy(
                src_ref=slot(a, *block) if src is None else src, dst_ref=slot(a, *block),
                send_sem=send_sems.at[a, k], recv_sem=recv_sems.at[a, k], device_id=to, device_id_type=MESH_ID)

        mine = [pltpu.make_async_copy(x_refs[a], slot(a, *me), local_sems.at[a]) for a in range(n)]
        for cp in mine:
            cp.start()
        first = []
        for j, chip in enumerate(chips):
            first += [copy(a, 1 + j, me, (*chip, c), src=x_refs[a]) for a in range(n)]
        first += [copy(a, 0, me, sibling, src=x_refs[a]) for a in range(n)]
        for cp in first:
            cp.start()
        passed = []
        for j, chip in enumerate(chips):
            for a in range(n):
                copy(a, 1 + j, (*chip, c), me).wait_recv()
                cp = copy(a, 4 + j, (*chip, c), sibling)
                cp.start()
                passed.append(cp)
        for a in range(n):
            copy(a, 0, sibling, me).wait_recv()
        for j, chip in enumerate(chips):
            for a in range(n):
                copy(a, 4 + j, (*chip, 1 - c), me).wait_recv()
        for cp in first + passed:
            cp.wait_send()
        for cp in mine:
            cp.wait()

    return _pcall(
        body, name=name, in_specs=[_any_spec()] * n, out_specs=[_any_spec()] * n,
        out_shape=[jax.ShapeDtypeStruct((N_DEV,) + x.shape, x.dtype) for x in xs],
        scratch_shapes=[pltpu.SemaphoreType.DMA((n, 7)), pltpu.SemaphoreType.DMA((n, 7)), pltpu.SemaphoreType.DMA((n,))],
    )(*xs)


def _mesh_peer(k):
    px = lax.axis_index("x") ^ ((k >> 2) & 1)
    py = lax.axis_index("y") ^ ((k >> 1) & 1)
    pc = lax.axis_index("c") ^ (k & 1)
    return (px, py, pc), 4 * px + 2 * py + pc


def _my_index():
    return 4 * lax.axis_index("x") + 2 * lax.axis_index("y") + lax.axis_index("c")


def _landing_zones(xs, blocks):
    me = _my_index()
    lands = []
    for x in xs:
        own = lax.dynamic_index_in_dim(x, me, 0, keepdims=True) if blocks else x[None]
        zone = lax.empty((N_DEV,) + own.shape[1:], x.dtype)
        lands.append(lax.dynamic_update_slice(zone, own, (me,) + (0,) * (own.ndim - 1)))
    return lands


def _send_start(xs, lands, *, blocks, name, after=None):
    n = len(xs)
    peers = N_DEV - 1
    first_out = 2 * n + (after is not None)

    def body(*refs):
        x_refs, land_refs = refs[:n], refs[n:2 * n]
        send_sems, recv_sems = refs[first_out:first_out + peers], refs[first_out + peers:first_out + 2 * peers]
        token = refs[-1]
        me_idx = _my_index()
        for k in (2, 4, 6, 3, 5, 7, 1):
            to, to_idx = _mesh_peer(k)
            for a in range(n):
                pltpu.make_async_remote_copy(
                    src_ref=x_refs[a].at[to_idx] if blocks else x_refs[a], dst_ref=land_refs[a].at[me_idx],
                    send_sem=send_sems[k - 1], recv_sem=recv_sems[k - 1], device_id=to, device_id_type=MESH_ID).start()
        token[...] = jnp.zeros(token.shape, token.dtype)

    hbm = pl.BlockSpec(memory_space=pltpu.HBM)
    sem = pl.BlockSpec(memory_space=pltpu.SEMAPHORE)
    both = list(xs) + list(lands)
    out = _pcall(
        body, name=name,
        out_shape=(*[pltpu.SemaphoreType.DMA(())] * (2 * peers), *[pltpu.HBM(a.shape, a.dtype) for a in both],
                   jax.ShapeDtypeStruct((8, LANES), F32)),
        in_specs=[hbm] * (2 * n) + ([pl.BlockSpec(memory_space=pl.ANY)] if after is not None else []),
        out_specs=(*[sem] * (2 * peers), *[hbm] * (2 * n), pl.BlockSpec(memory_space=pltpu.VMEM)),
        input_output_aliases={i: 2 * peers + i for i in range(2 * n)},
        compiler_params=pltpu.CompilerParams(has_side_effects=pltpu.SideEffectType.DATAFLOW_SIDE_EFFECTING),
    )(*[pltpu.with_memory_space_constraint(a, pltpu.HBM) for a in both], *((after,) if after is not None else ()))
    return dict(sems=out[:2 * peers], xs=out[2 * peers:2 * peers + n], lands=out[2 * peers + n:2 * peers + 2 * n],
                token=out[-1], blocks=blocks)


def _send_wait(started, after, *, name):
    n = len(started['xs'])
    blocks = started['blocks']
    peers = N_DEV - 1

    def body(*refs):
        x_refs, land_refs = refs[:n], refs[n:2 * n]
        send_sems, recv_sems = refs[2 * n:2 * n + peers], refs[2 * n + peers:2 * n + 2 * peers]
        for k in range(1, N_DEV):
            frm, frm_idx = _mesh_peer(k)
            for a in range(n):
                copy = pltpu.make_async_remote_copy(
                    src_ref=x_refs[a].at[frm_idx] if blocks else x_refs[a], dst_ref=land_refs[a].at[frm_idx],
                    send_sem=send_sems[k - 1], recv_sem=recv_sems[k - 1], device_id=frm, device_id_type=MESH_ID)
                copy.wait_send()
                copy.wait_recv()

    hbm = pl.BlockSpec(memory_space=pltpu.HBM)
    sem = pl.BlockSpec(memory_space=pltpu.SEMAPHORE)
    both = list(started['xs']) + list(started['lands'])
    out = _pcall(
        body, name=name, out_shape=[pltpu.HBM(a.shape, a.dtype) for a in both],
        in_specs=[hbm] * (2 * n) + [sem] * (2 * peers) + [pl.BlockSpec(memory_space=pl.ANY)], out_specs=[hbm] * (2 * n),
        input_output_aliases={i: i for i in range(2 * n)},
        compiler_params=pltpu.CompilerParams(has_side_effects=pltpu.SideEffectType.DATAFLOW_SIDE_EFFECTING),
    )(*both, *started['sems'], after)
    return out[n:]


COLUMN_SHARDED = ('w_in', 'w_mem_out', 'w_gate_up')


def _full_weight(name, g):
    if name in COLUMN_SHARDED:
        g = jnp.transpose(g, (1, 0, 2))
        g = g.reshape(g.shape[0], -1)
    else:
        g = g.reshape(-1, g.shape[-1])
    return jnp.pad(g, ((0, 0), (0, IN_COLS_PAD - IN_COLS))) if name == 'w_in' else g


def _grad_blocks(name, g):
    if name == 'w_gate_up':
        rows, cols = g.shape[1], 2 * g.shape[2] // N_DEV
        g = jnp.transpose(g.reshape(2, rows, N_DEV // 2, cols), (0, 2, 1, 3))
        return g.reshape(N_DEV, rows, cols).astype(BF16)
    if name == 'w_in':
        g = g[:, :IN_COLS]
    if name in COLUMN_SHARDED:
        rows, cols = g.shape[0], g.shape[1] // N_DEV
        g = jnp.transpose(g.reshape(rows, N_DEV, cols), (1, 0, 2))
    else:
        rows, cols = g.shape[0] // N_DEV, g.shape[1]
        g = g.reshape(N_DEV, rows, cols)
    return g.astype(BF16)


SMALL_SHAPES = {'g_mix': (DEPTH, D_MODEL), 'b_forget': (DEPTH, FOX_HEADS), 'g_q_fox': (DEPTH, FOX_HEAD_DIM),
                'g_k_fox': (DEPTH, FOX_HEAD_DIM), 'w_pool': (DEPTH, 4, POOL_GROUP_DIM, POOL_GROUP_DIM),
                'pool_scale': (DEPTH, POOL_WIDTH), 'g_mem_q': (DEPTH, D_MODEL), 'g_mem_kv': (DEPTH, D_MODEL),
                'g_q_mem': (DEPTH, MEM_HEAD_DIM), 'g_k_mem': (DEPTH, MEM_HEAD_DIM), 'g_ffn': (DEPTH, D_MODEL)}


def _small_rows(name):
    return -(-int(np.prod(SMALL_SHAPES[name])) // LANES)


SMALL_ROWS = -(-sum(_small_rows(n) for n in SMALL) // 8) * 8


def _pack_small(tree):
    parts = []
    for n in SMALL:
        flat = tree[n].reshape(-1).astype(F32)
        parts.append(jnp.pad(flat, (0, _small_rows(n) * LANES - flat.shape[0])))
    flat = jnp.concatenate(parts)
    return jnp.pad(flat, (0, SMALL_ROWS * LANES - flat.shape[0])).reshape(SMALL_ROWS, LANES)


def _unpack_small(packed):
    flat = packed.reshape(-1)
    out, at = {}, 0
    for n in SMALL:
        size = int(np.prod(SMALL_SHAPES[n]))
        out[n] = flat[at:at + size].reshape(SMALL_SHAPES[n])
        at += _small_rows(n) * LANES
    return out


def _pairs_cols(a):
    t = a.shape[0]
    return jnp.transpose(a.reshape(t, FOX_HEADS // 2, 2), (1, 0, 2))


def _pairs_rows(a):
    return a.reshape(FOX_HEADS // 2, 2, a.shape[1])


def _layer_fwd(h0, mem, p, w_in, other_weights):
    s = {'h0': h0}
    s['xn1'], z, f = _norm_matmul(h0, p['g_mix'], w_in, tn=IN_COLS_PAD, tail=LANES, name="norm_in_proj_fwd")
    s['z'] = z
    s['qn'], s['kn'], s['vb'], q2 = _qkv_prep(z, p['g_q_fox'], p['g_k_fox'])
    s['ft'] = jnp.transpose(f[:, :FOX_HEADS])
    c = _forget_cumsum(s['ft'], p['b_forget'])
    s['ccol'], s['crow'] = _pairs_cols(jnp.transpose(c)), _pairs_rows(c)
    s['fox'], s['lse_row'] = _fox_fwd(jnp.transpose(q2), s['kn'], jnp.transpose(s['vb']), s['crow'] * LOG2E,
                                      s['ccol'] * LOG2E)
    pool, s['mixed'] = _pool_fwd(z, p['w_pool'].astype(BF16), p['pool_scale'])
    s['cat'] = jnp.concatenate([s['fox'], pool], axis=1)
    w = dict(other_weights(s['lse_row']), w_in=w_in)
    h1 = _matmul(s['cat'], w['w_out'], res=h0, name="out_proj_fwd")
    s['h1'] = h1

    s['hn2'], s['mq_raw'] = _norm_matmul(h1, p['g_mem_q'], w['w_mem_q'], name="norm_mem_q_fwd")
    s['mn'], s['mkv'], s['mk'], s['mv'] = _mem_kv_fwd(mem, p['g_mem_kv'], w['w_mem_kv'], p['g_k_mem'])
    s['mo'] = _cross_fwd(s['mq_raw'], p['g_q_mem'], s['mk'], s['mv'])
    h2 = _matmul(s['mo'], w['w_mem_out'], res=h1, name="mem_out_fwd")
    s['h2'] = h2

    s['hn3'], s['gu'], s['act'] = _norm_gate_up_swiglu(h2, p['g_ffn'], w['w_gate_up'])
    h3 = _matmul(s['act'], w['w_down'], res=h2, name="down_fwd")
    return h3, s, w


def _layer_bwd(dh, mem, p, w, s, after=None, at_mixer=None):
    g = {}
    g['w_down'] = _matmul(s['act'], dh, ta=True, out_dtype=BF16, tm=1408, tn=512, tk=1024, name="down_dw")
    dgu = _down_dx_swiglu_bwd(dh, w['w_down'], s['gu'], after=after)
    g['w_gate_up'] = _matmul(s['hn3'], dgu, ta=True, out_dtype=BF16, tm=1024, tn=1408, tk=1024, name="gate_up_dw")
    dh, g['g_ffn'] = _matmul_norm_bwd(dgu, w['w_gate_up'], s['h2'], p['g_ffn'], dh, name="gate_up_dx_norm_bwd")

    g['w_mem_out'] = _matmul(s['mo'], dh, ta=True, out_dtype=BF16, tm=512, tn=1024, tk=1024, name="mem_out_dw")
    dmo = _matmul(dh, w['w_mem_out'], tb=True, out_dtype=BF16, name="mem_out_dx")
    dmq, dmk, dmv, g['g_q_mem'] = _cross_bwd(s['mq_raw'], dmo, p['g_q_mem'], s['mk'], s['mv'])
    g['w_mem_kv'], g['g_mem_kv'], g['g_k_mem'] = _mem_kv_bwd(dmk, dmv, s['mkv'], s['mn'], mem, p['g_mem_kv'],
                                                               w['w_mem_kv'], p['g_k_mem'])
    g['w_mem_q'] = _matmul(s['hn2'], dmq, ta=True, out_dtype=BF16, tm=1024, tn=512, tk=1024, name="mem_q_dw")
    dh, g['g_mem_q'] = _matmul_norm_bwd(dmq, w['w_mem_q'], s['h1'], p['g_mem_q'], dh, name="mem_q_dx_norm_bwd")

    g['w_out'] = _matmul(s['cat'], dh, ta=True, out_dtype=BF16, tm=1024, tn=512, tk=1024, name="out_proj_dw")
    after = at_mixer(g) if at_mixer is not None else None
    dcat = _matmul(dh, w['w_out'], tb=True, after=after, tn=1024, name="out_proj_dx")
    dpin, g['w_pool'], dscale = _pool_bwd(dcat, s['mixed'], p['w_pool'].astype(BF16), p['pool_scale'])
    g['pool_scale'] = dscale.reshape(POOL_WIDTH)
    do, delta = _fox_delta(dcat, s['fox'])
    delta_row = _pairs_rows(jnp.transpose(delta[:, :FOX_HEADS]))
    dq, dk, dv, dc_keys, dc_queries = _fox_bwd(s['qn'], jnp.transpose(s['qn']), s['kn'], jnp.transpose(s['kn']), s['vb'],
                                               do, jnp.transpose(do), s['lse_row'], delta_row, s['crow'], s['ccol'])
    dft, db = _forget_cumsum_bwd(dc_keys, dc_queries, s['ft'], p['b_forget'])
    g['b_forget'] = db.reshape(FOX_HEADS)
    dfpad = jnp.pad(jnp.transpose(dft), ((0, 0), (0, LANES - FOX_HEADS)))
    dz, g['g_q_fox'], g['g_k_fox'] = _mixer_dz(s['z'], dq, dk, dv, dpin, dfpad, p['g_q_fox'], p['g_k_fox'])
    g['w_in'] = _matmul(s['xn1'], dz, ta=True, out_dtype=BF16, tm=512, tn=IN_COLS_PAD, tk=1024, name="in_proj_dw")
    dh, g['g_mix'] = _matmul_norm_bwd(dz, w['w_in'], s['h0'], p['g_mix'], dh, name="in_proj_dx_norm_bwd")
    return dh, g


def _local_step(x2, mem2, target2, small, w_in, other_weights, send_grads):
    h = x2
    saved, full = [], []
    for l in range(DEPTH):
        h, s, w = _layer_fwd(h, mem2, {k: v[l] for k, v in small.items()}, w_in[l], other_weights(l))
        saved.append(s)
        full.append(w)
    loss, dh = _loss_head(h, target2)
    after = None
    grads = [None] * DEPTH
    for l in reversed(range(1, DEPTH)):
        dh, grads[l] = _layer_bwd(dh, mem2, {k: v[l] for k, v in small.items()}, full[l], saved[l], after=after)
        after = send_grads(l, BIG, grads[l])
    dh, grads[0] = _layer_bwd(dh, mem2, {k: v[0] for k, v in small.items()}, full[0], saved[0], after=after,
                              at_mixer=lambda g: send_grads(0, SENT_AT_MIXER, g))
    return loss, dh, grads


def kernel(x, mem, g_mix, w_in, b_forget, g_q_fox, g_k_fox, w_pool, pool_scale, w_out, g_mem_q, g_mem_kv, w_mem_q, w_mem_kv, g_q_mem, g_k_mem, w_mem_out, g_ffn, w_gate_up, w_down, loss_target, m_g_mix, m_w_in, m_b_forget, m_g_q_fox, m_g_k_fox, m_w_pool, m_pool_scale, m_w_out, m_g_mem_q, m_g_mem_kv, m_w_mem_q, m_w_mem_kv, m_g_q_mem, m_g_k_mem, m_w_mem_out, m_g_ffn, m_w_gate_up, m_w_down, v_g_mix, v_w_in, v_b_forget, v_g_q_fox, v_g_k_fox, v_w_pool, v_pool_scale, v_w_out, v_g_mem_q, v_g_mem_kv, v_w_mem_q, v_w_mem_kv, v_g_q_mem, v_g_k_mem, v_w_mem_out, v_g_ffn, v_w_gate_up, v_w_down):
    weights = dict(g_mix=g_mix, w_in=w_in, b_forget=b_forget, g_q_fox=g_q_fox, g_k_fox=g_k_fox, w_pool=w_pool,
                   pool_scale=pool_scale, w_out=w_out, g_mem_q=g_mem_q, g_mem_kv=g_mem_kv, w_mem_q=w_mem_q,
                   w_mem_kv=w_mem_kv, g_q_mem=g_q_mem, g_k_mem=g_k_mem, w_mem_out=w_mem_out, g_ffn=g_ffn,
                   w_gate_up=w_gate_up, w_down=w_down)
    mom_m = dict(g_mix=m_g_mix, w_in=m_w_in, b_forget=m_b_forget, g_q_fox=m_g_q_fox, g_k_fox=m_g_k_fox, w_pool=m_w_pool,
                 pool_scale=m_pool_scale, w_out=m_w_out, g_mem_q=m_g_mem_q, g_mem_kv=m_g_mem_kv, w_mem_q=m_w_mem_q,
                 w_mem_kv=m_w_mem_kv, g_q_mem=m_g_q_mem, g_k_mem=m_g_k_mem, w_mem_out=m_w_mem_out, g_ffn=m_g_ffn,
                 w_gate_up=m_w_gate_up, w_down=m_w_down)
    mom_v = dict(g_mix=v_g_mix, w_in=v_w_in, b_forget=v_b_forget, g_q_fox=v_g_q_fox, g_k_fox=v_g_k_fox, w_pool=v_w_pool,
                 pool_scale=v_pool_scale, w_out=v_w_out, g_mem_q=v_g_mem_q, g_mem_kv=v_g_mem_kv, w_mem_q=v_w_mem_q,
                 w_mem_kv=v_w_mem_kv, g_q_mem=v_g_q_mem, g_k_mem=v_g_k_mem, w_mem_out=v_w_mem_out, g_ffn=v_g_ffn,
                 w_gate_up=v_w_gate_up, w_down=v_w_down)

    (w_in_all,) = _all_gather([weights['w_in'].astype(BF16)], name="w_in_all_gather")
    later = [(n, l) for l in range(DEPTH) for n in BIG if n != 'w_in']
    shards = [weights[n][l].astype(BF16) for n, l in later]
    weights_sent = _send_start(shards, _landing_zones(shards, False), blocks=False, name="weights_send_start",
                               after=w_in_all)
    small = {n: weights[n] for n in SMALL}
    small['g_mix'] = small['g_mix'] + weights_sent['token'][0, 0]
    gathered = {}

    def other_weights(l):
        def get(after):
            if not gathered:
                gathered.update(zip(later, _send_wait(weights_sent, after, name="weights_send_wait")))
            return {n: _full_weight(n, gathered[n, l]) for n in BIG if n != 'w_in'}
        return get

    grads_sent = []

    def send_grads(l, names, g):
        blocks = [_grad_blocks(n, g[n]) for n in names]
        sent = _send_start(blocks, _landing_zones(blocks, True), blocks=True,
                           name=f"grads{l}_send_start" if len(names) > 1 else "grads0_last_send_start")
        grads_sent.append((l, names, sent))
        return sent['token']

    w_in_full = [_full_weight('w_in', w_in_all[:, l]) for l in range(DEPTH)]
    loss_part, grad_x, grads = _local_step(x[0], mem[0], loss_target[0], small, w_in_full, other_weights, send_grads)
    loss = lax.psum(loss_part[0, 0], ("x", "y", "c"))

    last = [n for n in BIG if n not in SENT_AT_MIXER]
    last_token = send_grads(0, last, grads[0])
    landed = {}
    for l, names, sent in grads_sent[:-1]:
        landed.update({(l, n): a for n, a in zip(names, _send_wait(sent, grad_x, name=f"grads{l}_send_wait"))})

    def summed(n):
        return jnp.concatenate([_sum_slots(landed[l, n], name="grad_sum_" + n, after=last_token)
                                for l in range(DEPTH)], 0)

    grad, delta, new_m, new_v = {}, {}, {}, {}

    def update(n):
        shape = weights[n].shape
        two_d = lambda a: a.reshape(shape[0] * shape[1], shape[2])
        d, nm, nv = _adamw(two_d(weights[n]), grad[n], two_d(mom_m[n]), two_d(mom_v[n]), name="adamw_" + n)
        grad[n], delta[n], new_m[n], new_v[n] = (a.reshape(shape) for a in (grad[n], d, nm, nv))

    for n in SENT_AT_MIXER:
        grad[n] = summed(n)
        update(n)
    small_part = _pack_small({n: jnp.stack([grads[l][n] for l in range(DEPTH)], 0) for n in SMALL})
    (small_all,) = _all_gather([small_part], name="small_grads_all_gather")
    small_sum = _sum_slots(small_all, name="grad_sum_small")
    d, nm, nv = _adamw(_pack_small(weights), small_sum, _pack_small(mom_m), _pack_small(mom_v), name="adamw_small")
    grad.update(_unpack_small(small_sum))
    delta.update(_unpack_small(d))
    new_m.update(_unpack_small(nm))
    new_v.update(_unpack_small(nv))
    landed.update({(0, n): a for n, a in zip(last, _send_wait(grads_sent[-1][2], d, name="grads0_last_send_wait"))})
    for n in last:
        grad[n] = summed(n)
        update(n)

    return (loss, grad_x[None], *[grad[n] for n in WEIGHTS], *[delta[n] for n in WEIGHTS],
            *[new_m[n] for n in WEIGHTS], *[new_v[n] for n in WEIGHTS])
```

```python
import functools

import numpy as np
import jax
import jax.numpy as jnp
from jax import lax
from jax.experimental import pallas as pl
from jax.experimental.pallas import tpu as pltpu

F32 = jnp.float32
BF16 = jnp.bfloat16

N_DEV = 8
D_MODEL = 1024
DEPTH = 2
FOX_HEADS = 8
FOX_HEAD_DIM = 64
FOX_WIDTH = 512
POOL_WIDTH = 512
POOL_WINDOWS = (2, 4, 8, 16)
POOL_GROUP_DIM = 128
POOL_HALO = 16
IN_COLS = 2056
IN_COLS_PAD = 2176
MEM_HEADS = 4
MEM_HEAD_DIM = 128
MEM_WIDTH = 512
D_FF = 2816
EPS = 1e-6
FOX_SCALE = FOX_HEAD_DIM ** -0.5
LOG2E = 1.4426950408889634
LN2 = 0.6931471805599453
FOX_ACC_ROWS = FOX_HEAD_DIM + 16
MEM_SCALE = MEM_HEAD_DIM ** -0.5
LANES = 128

ADAM_LR = 0.001
ADAM_B1 = 0.9
ADAM_B2 = 0.999
ADAM_EPS = 1e-08
ADAM_WD = 0.01
ADAM_STEP = 10

VMEM_LIMIT = 56 * 1024 * 1024
MESH_ID = pl.DeviceIdType.MESH

WEIGHTS = ['g_mix', 'w_in', 'b_forget', 'g_q_fox', 'g_k_fox', 'w_pool', 'pool_scale', 'w_out', 'g_mem_q', 'g_mem_kv',
           'w_mem_q', 'w_mem_kv', 'g_q_mem', 'g_k_mem', 'w_mem_out', 'g_ffn', 'w_gate_up', 'w_down']
BIG = ['w_in', 'w_out', 'w_mem_q', 'w_mem_kv', 'w_mem_out', 'w_gate_up', 'w_down']
SMALL = [n for n in WEIGHTS if n not in BIG]
SENT_AT_MIXER = ['w_down', 'w_gate_up', 'w_mem_out', 'w_mem_q', 'w_mem_kv', 'w_out']


def _pcall(body, **kw):
    return pl.pallas_call(body, **kw)


def _params(*sem):
    return pltpu.CompilerParams(dimension_semantics=sem or None, vmem_limit_bytes=VMEM_LIMIT)


def _dot(a, b, dims=None):
    if dims is None:
        return jnp.dot(a, b, preferred_element_type=F32)
    return lax.dot_general(a, b, (dims, ((), ())), preferred_element_type=F32)


NT = ((1,), (1,))
TN = ((0,), (0,))


def _dot_exact(x, ones_bf16, terms=3):
    hi = x.astype(BF16)
    r1 = x - hi.astype(F32)
    mid = r1.astype(BF16)
    if terms == 2:
        return _dot(hi, ones_bf16) + _dot(mid, ones_bf16)
    lo = (r1 - mid.astype(F32)).astype(BF16)
    return _dot(hi, ones_bf16) + _dot(mid, ones_bf16) + _dot(lo, ones_bf16)


def _matmul(a, b, *, ta=False, tb=False, out_dtype=F32, res=None, after=None, tm=1024, tn=512, tk=None, name):
    planes = b.shape[0] if b.ndim == 3 else None
    bshape = b.shape[-2:]
    m, k = (a.shape[1], a.shape[0]) if ta else a.shape
    n = bshape[0] if tb else bshape[1]
    assert k == (bshape[1] if tb else bshape[0])
    tm, tn = min(tm, m), min(tn, n)
    tk = min(tk or k, k)
    assert m % tm == 0 and n % tn == 0 and k % tk == 0, (name, m, n, k, tm, tn, tk)
    nk = k // tk
    dims = ((0 if ta else 1,), (1 if tb else 0,))

    def body(*refs):
        a_ref, b_ref = refs[0], refs[1]
        r_ref = refs[2] if res is not None else None
        o_ref = refs[2 + (res is not None) + (after is not None)]
        part = _dot(a_ref[...].astype(BF16), b_ref[...].astype(BF16), dims)

        def finish(acc):
            if r_ref is not None:
                acc = acc + r_ref[...]
            o_ref[...] = acc.astype(o_ref.dtype)

        if nk == 1:
            finish(part)
        else:
            acc_ref = refs[-1]
            kk = pl.program_id(3)

            @pl.when(kk == 0)
            def _():
                acc_ref[...] = part

            @pl.when(kk > 0)
            def _():
                acc_ref[...] += part

            @pl.when(kk == nk - 1)
            def _():
                finish(acc_ref[...])

    lead = (lambda p: (p,)) if planes else (lambda p: ())
    sq = (None,) if planes else ()
    a_spec = pl.BlockSpec((tk, tm), lambda p, i, j, kk: (kk, i)) if ta else pl.BlockSpec((tm, tk), lambda p, i, j, kk: (i, kk))
    b_spec = (pl.BlockSpec(sq + (tn, tk), lambda p, i, j, kk: lead(p) + (j, kk)) if tb
              else pl.BlockSpec(sq + (tk, tn), lambda p, i, j, kk: lead(p) + (kk, j)))
    o_spec = pl.BlockSpec(sq + (tm, tn), lambda p, i, j, kk: lead(p) + (i, j))
    in_specs = ([a_spec, b_spec] + ([o_spec] if res is not None else [])
                + ([pl.BlockSpec(memory_space=pl.ANY)] if after is not None else []))
    args = (a, b) + ((res,) if res is not None else ()) + ((after,) if after is not None else ())
    return _pcall(
        body, name=name, grid=(planes or 1, m // tm, n // tn, nk), in_specs=in_specs, out_specs=o_spec,
        out_shape=jax.ShapeDtypeStruct(((planes,) if planes else ()) + (m, n), out_dtype),
        scratch_shapes=[pltpu.VMEM((tm, tn), F32)] if nk > 1 else [],
        compiler_params=_params("parallel", "parallel", "parallel", "arbitrary"),
    )(*args)


def _rms(x):
    return lax.rsqrt(jnp.mean(x * x, axis=-1, keepdims=True) + EPS)


def _norm_matmul(h, g, w, *, tm=512, tn=512, tail=0, name):
    t, d = h.shape
    n = w.shape[1]
    tn = min(tn, n)
    assert t % tm == 0 and n % tn == 0 and (not tail or tn == n)

    def body(h_ref, g_ref, w_ref, xn_ref, y_ref, *tail_ref):
        @pl.when(pl.program_id(1) == 0)
        def _():
            x = h_ref[...]
            xn_ref[...] = (x * _rms(x) * g_ref[...]).astype(BF16)

        y = _dot(xn_ref[...], w_ref[...])
        if tail:
            y_ref[...] = y[:, :n - tail]
            tail_ref[0][...] = y[:, n - tail:]
        else:
            y_ref[...] = y

    row = pl.BlockSpec((tm, d), lambda i, j: (i, 0))
    tails = ([pl.BlockSpec((tm, tail), lambda i, j: (i, 0))], [jax.ShapeDtypeStruct((t, tail), F32)]) if tail else ([], [])
    return _pcall(
        body, name=name, grid=(t // tm, n // tn),
        in_specs=[row, pl.BlockSpec((1, d), lambda i, j: (0, 0)), pl.BlockSpec((d, tn), lambda i, j: (0, j))],
        out_specs=[row, pl.BlockSpec((tm, tn - tail), lambda i, j: (i, j))] + tails[0],
        out_shape=[jax.ShapeDtypeStruct((t, d), BF16), jax.ShapeDtypeStruct((t, n - tail), F32)] + tails[1],
        compiler_params=_params("parallel", "arbitrary"),
    )(h, g.reshape(1, d), w)


def _matmul_norm_bwd(a, w, h, g, dres, *, tm=512, tk=None, name):
    stacked = a.ndim == 3
    t = a.shape[-2]
    d, k = w.shape
    tk = a.shape[-1] if stacked else min(tk or k, k)
    nk = k // tk
    assert t % tm == 0 and k % tk == 0 and (not stacked or a.shape[0] == nk)

    def body(a_ref, w_ref, h_ref, g_ref, r_ref, dx_ref, dg_ref, *acc):
        i, kk = pl.program_id(0), pl.program_id(1)
        part = _dot(a_ref[...], w_ref[...], NT)

        def finish(dy):
            x = h_ref[...]
            r = _rms(x)
            xhat = x * r
            gy = dy * g_ref[...]
            dx_ref[...] = r_ref[...] + r * (gy - xhat * jnp.mean(gy * xhat, axis=-1, keepdims=True))
            dg_part = jnp.sum(dy * xhat, axis=0, keepdims=True)

            @pl.when(i == 0)
            def _():
                dg_ref[...] = dg_part

            @pl.when(i > 0)
            def _():
                dg_ref[...] += dg_part

        if nk == 1:
            finish(part)
        else:
            acc_ref = acc[0]

            @pl.when(kk == 0)
            def _():
                acc_ref[...] = part

            @pl.when(kk > 0)
            def _():
                acc_ref[...] += part

            @pl.when(kk == nk - 1)
            def _():
                finish(acc_ref[...])

    a_spec = (pl.BlockSpec((None, tm, tk), lambda i, kk: (kk, i, 0)) if stacked
              else pl.BlockSpec((tm, tk), lambda i, kk: (i, kk)))
    row = pl.BlockSpec((tm, d), lambda i, kk: (i, 0))
    vec = pl.BlockSpec((1, d), lambda i, kk: (0, 0))
    dx, dg = _pcall(
        body, name=name, grid=(t // tm, nk),
        in_specs=[a_spec, pl.BlockSpec((d, tk), lambda i, kk: (0, kk)), row, vec, row], out_specs=[row, vec],
        out_shape=[jax.ShapeDtypeStruct((t, d), F32), jax.ShapeDtypeStruct((1, d), F32)],
        scratch_shapes=[pltpu.VMEM((tm, d), F32)] if nk > 1 else [],
        compiler_params=_params("arbitrary", "arbitrary"),
    )(a, w, h, g.reshape(1, d), dres)
    return dx, dg.reshape(d)


def _norm_gate_up_swiglu(h, g, w, *, tm=512, tn=1408):
    t, d = h.shape
    nj = D_FF // tn
    assert t % tm == 0 and D_FF % tn == 0

    def body(h_ref, g_ref, wg_ref, wu_ref, hn_ref, gu_ref, act_ref):
        @pl.when(pl.program_id(1) == 0)
        def _():
            x = h_ref[...]
            hn_ref[...] = (x * _rms(x) * g_ref[...]).astype(BF16)

        hn = hn_ref[...]
        gate = _dot(hn, wg_ref[...])
        up = _dot(hn, wu_ref[...])
        gu_ref[0] = gate
        gu_ref[1] = up
        act_ref[...] = (gate * jax.nn.sigmoid(gate) * up).astype(BF16)

    row = pl.BlockSpec((tm, d), lambda i, j: (i, 0))
    return _pcall(
        body, name="gate_up_swiglu_fwd", grid=(t // tm, nj),
        in_specs=[row, pl.BlockSpec((1, d), lambda i, j: (0, 0)), pl.BlockSpec((d, tn), lambda i, j: (0, j)),
                  pl.BlockSpec((d, tn), lambda i, j: (0, nj + j))],
        out_specs=[row, pl.BlockSpec((2, tm, tn), lambda i, j: (0, i, j)), pl.BlockSpec((tm, tn), lambda i, j: (i, j))],
        out_shape=[jax.ShapeDtypeStruct((t, d), BF16), jax.ShapeDtypeStruct((2, t, D_FF), F32),
                   jax.ShapeDtypeStruct((t, D_FF), BF16)],
        compiler_params=_params("parallel", "arbitrary"),
    )(h, g.reshape(1, d), w, w)


def _down_dx_swiglu_bwd(dh, w_down, gu, *, after=None, tm=512, tn=1408):
    t, d = dh.shape
    assert t % tm == 0 and D_FF % tn == 0

    def body(dh_ref, w_ref, gu_ref, *rest):
        dgu_ref = rest[-1]
        da = _dot(dh_ref[...].astype(BF16), w_ref[...], NT)
        gate, up = gu_ref[0], gu_ref[1]
        sg = jax.nn.sigmoid(gate)
        silu = gate * sg
        dgu_ref[0] = (da * up * (sg + silu * (1.0 - sg))).astype(BF16)
        dgu_ref[1] = (da * silu).astype(BF16)

    stack = pl.BlockSpec((2, tm, tn), lambda i, j: (0, i, j))
    return _pcall(
        body, name="down_dx_swiglu_bwd", grid=(t // tm, D_FF // tn),
        in_specs=[pl.BlockSpec((tm, d), lambda i, j: (i, 0)), pl.BlockSpec((tn, d), lambda i, j: (j, 0)), stack]
        + ([pl.BlockSpec(memory_space=pl.ANY)] if after is not None else []),
        out_specs=stack, out_shape=jax.ShapeDtypeStruct((2, t, D_FF), BF16),
        compiler_params=_params("parallel", "parallel"),
    )(dh, w_down, gu, *((after,) if after is not None else ()))


def _group_matrix(width, group):
    r = lax.broadcasted_iota(jnp.int32, (width, width), 0) // group
    c = lax.broadcasted_iota(jnp.int32, (width, width), 1) // group
    return (r == c).astype(BF16)


def _qkv_prep(z, gq, gk, *, bt=512):
    t = z.shape[0]
    w = FOX_WIDTH

    def body(q_ref, k_ref, v_ref, gq_ref, gk_ref, qo_ref, ko_ref, vo_ref, qt_ref, kt_ref, vt_ref, q2t_ref):
        gm = _group_matrix(w, FOX_HEAD_DIM)
        for x_ref, g_ref, o_ref, ot_ref, scale in ((q_ref, gq_ref, qo_ref, qt_ref, FOX_SCALE),
                                                   (k_ref, gk_ref, ko_ref, kt_ref, 1.0)):
            x = x_ref[...]
            ms = _dot_exact(x * x, gm, terms=2) * (1.0 / FOX_HEAD_DIM)
            y = x * lax.rsqrt(ms + EPS) * g_ref[...]
            o_ref[...] = (y * scale).astype(BF16)
            yt = jnp.transpose(y)
            ot_ref[...] = (yt * scale).astype(BF16)
            if o_ref is qo_ref:
                q2t_ref[...] = (yt * (scale * LOG2E)).astype(BF16)
        v = v_ref[...]
        vo_ref[...] = v.astype(BF16)
        vt_ref[...] = jnp.transpose(v).astype(BF16)

    col = lambda c: pl.BlockSpec((bt, w), lambda i, c=c: (i, c))
    vec = pl.BlockSpec((1, w), lambda i: (0, 0))
    out = pl.BlockSpec((bt, w), lambda i: (i, 0))
    out_t = pl.BlockSpec((w, bt), lambda i: (0, i))
    return _pcall(
        body, name="fox_qkv_prep", grid=(t // bt,), in_specs=[col(0), col(1), col(2), vec, vec],
        out_specs=[out] * 3 + [out_t] * 4,
        out_shape=[jax.ShapeDtypeStruct((t, w), BF16)] * 3 + [jax.ShapeDtypeStruct((w, t), BF16)] * 4,
        compiler_params=_params("parallel"),
    )(z, z, z, jnp.tile(gq, FOX_HEADS).reshape(1, w), jnp.tile(gk, FOX_HEADS).reshape(1, w))


def _log_sigmoid(f):
    return jnp.minimum(f, 0.0) - jnp.log1p(jnp.exp(-jnp.abs(f)))


def _forget_cumsum(ft, b, *, chunk=512):
    hh, t = ft.shape

    def body(f_ref, b_ref, c_ref):
        r = lax.broadcasted_iota(jnp.int32, (chunk, chunk), 0)
        c = lax.broadcasted_iota(jnp.int32, (chunk, chunk), 1)
        upper = (r <= c).astype(BF16)
        carry = jnp.zeros((hh, 1), F32)
        for ch in range(t // chunk):
            sl = slice(ch * chunk, (ch + 1) * chunk)
            cs = _dot_exact(_log_sigmoid(f_ref[:, sl] + b_ref[...]), upper) + carry
            c_ref[:, sl] = cs
            carry = cs[:, chunk - 1:chunk]

    return _pcall(body, name="fox_forget_cumsum", out_shape=jax.ShapeDtypeStruct((hh, t), F32),
                  compiler_params=_params())(ft, b.reshape(hh, 1))


def _forget_cumsum_bwd(dc_keys, dc_queries, ft, b, *, chunk=512):
    hh, t = ft.shape

    def body(dck_ref, dcq_ref, f_ref, b_ref, df_ref, db_ref):
        r = lax.broadcasted_iota(jnp.int32, (chunk, chunk), 0)
        c = lax.broadcasted_iota(jnp.int32, (chunk, chunk), 1)
        lower = (r >= c).astype(BF16)
        carry = jnp.zeros((hh, 1), F32)
        db = jnp.zeros((hh, 1), F32)
        for ch in reversed(range(t // chunk)):
            sl = slice(ch * chunk, (ch + 1) * chunk)
            dls = _dot_exact(dck_ref[:, sl] + dcq_ref[:, sl], lower) + carry
            carry = dls[:, 0:1]
            df = dls * jax.nn.sigmoid(-(f_ref[:, sl] + b_ref[...]))
            df_ref[:, sl] = df
            db = db + jnp.sum(df, axis=1, keepdims=True)
        db_ref[...] = db

    return _pcall(body, name="fox_forget_cumsum_bwd",
                  out_shape=[jax.ShapeDtypeStruct((hh, t), F32), jax.ShapeDtypeStruct((hh, 1), F32)],
                  compiler_params=_params())(dc_keys, dc_queries, ft, b.reshape(hh, 1))


def _lane_is_first_head():
    return lax.broadcasted_iota(jnp.int32, (1, LANES), 1) < FOX_HEAD_DIM


def _fox_fwd(q2t, kn, vt, crow2, ccol2, *, bq=512, bk=1024):
    t = kn.shape[0]
    nq = t // bq
    pairs = FOX_WIDTH // LANES
    assert t % bk == 0
    tiles = [(i, j) for i in range(nq) for j in range(i * bq // bk, -1, -1)]
    it = jnp.asarray(np.array([a for a, _ in tiles], np.int32))
    jt = jnp.asarray(np.array([b for _, b in tiles], np.int32))

    def body(it_ref, jt_ref, qt_ref, k_ref, vt_ref, cr_ref, cc_ref, o_ref, lse_ref, m_sc, acc_sc):
        s_id = pl.program_id(1)
        i, j = it_ref[s_id], jt_ref[s_id]
        first = _lane_is_first_head()
        holds_diagonal = j == (i * bq) // bk

        @pl.when(holds_diagonal)
        def _():
            m_sc[...] = jnp.full(m_sc.shape, -jnp.inf, F32)
            acc_sc[...] = jnp.zeros(acc_sc.shape, F32)

        def scores():
            k2, qt2 = k_ref[...], qt_ref[...]
            return [_dot(jnp.where(first if hh == 0 else jnp.logical_not(first), k2, jnp.zeros_like(k2)), qt2)
                    for hh in range(2)]

        def pv(hh, pt_bf16):
            v_ones = jnp.concatenate([vt_ref[hh * FOX_HEAD_DIM:(hh + 1) * FOX_HEAD_DIM, :],
                                      jnp.ones((FOX_ACC_ROWS - FOX_HEAD_DIM, bk), BF16)], axis=0)
            return _dot(v_ones, pt_bf16)

        def tile(diagonal):
            sc = scores()
            for hh in range(2):
                ut = sc[hh] - cc_ref[0, :, hh:hh + 1]
                if diagonal:
                    key = j * bk + lax.broadcasted_iota(jnp.int32, ut.shape, 0)
                    query = i * bq + lax.broadcasted_iota(jnp.int32, ut.shape, 1)
                    ut = jnp.where(key <= query, ut, -jnp.inf)
                c_t = cr_ref[0, hh:hh + 1, :]
                m_prev = m_sc[hh]
                m_new = jnp.maximum(m_prev, jnp.max(ut, axis=0, keepdims=True) + c_t)
                acc_sc[hh] = jnp.exp2(m_prev - m_new) * acc_sc[hh] + pv(hh, jnp.exp2(ut + (c_t - m_new)).astype(BF16))
                m_sc[hh] = m_new

        @pl.when(holds_diagonal)
        def _():
            tile(True)

        @pl.when(jnp.logical_not(holds_diagonal))
        def _():
            tile(False)

        @pl.when(j == 0)
        def _():
            sums = [acc_sc[hh, FOX_HEAD_DIM:FOX_HEAD_DIM + 1, :] for hh in range(2)]
            ot = jnp.concatenate([acc_sc[hh, :FOX_HEAD_DIM, :] / sums[hh] for hh in range(2)], axis=0)
            o_ref[...] = jnp.transpose(ot).astype(o_ref.dtype)
            for hh in range(2):
                lse_ref[0, hh:hh + 1, :] = m_sc[hh] * LN2 + jnp.log(sums[hh])

    qspec = pl.BlockSpec((bq, LANES), lambda p, s, it, jt: (it[s], p))
    kspec = pl.BlockSpec((bk, LANES), lambda p, s, it, jt: (jt[s], p))
    vtspec = pl.BlockSpec((LANES, bk), lambda p, s, it, jt: (p, jt[s]))
    qtspec = pl.BlockSpec((LANES, bq), lambda p, s, it, jt: (p, it[s]))
    rowq = pl.BlockSpec((1, 2, bq), lambda p, s, it, jt: (p, 0, it[s]))
    colk = pl.BlockSpec((1, bk, 2), lambda p, s, it, jt: (p, jt[s], 0))
    return _pcall(
        body, name="fox_attention_fwd",
        grid_spec=pltpu.PrefetchScalarGridSpec(
            num_scalar_prefetch=2, grid=(pairs, len(tiles)),
            in_specs=[qtspec, kspec, vtspec, rowq, colk], out_specs=[qspec, rowq],
            scratch_shapes=[pltpu.VMEM((2, 1, bq), F32), pltpu.VMEM((2, FOX_ACC_ROWS, bq), F32)]),
        out_shape=[jax.ShapeDtypeStruct((t, FOX_WIDTH), BF16), jax.ShapeDtypeStruct((pairs, 2, t), F32)],
        compiler_params=_params("parallel", "arbitrary"),
    )(it, jt, q2t, kn, vt, crow2, ccol2)


def _fox_bwd(qn, qnt, kn, knt, vb, do, dot, lse_row, delta_row, crow, ccol, *, bq=512, bk=512):
    t = qn.shape[0]
    nq, nk = t // bq, t // bk
    pairs = FOX_WIDTH // LANES
    tiles = [(i, j) for j in range(nk) for i in range(j * bk // bq, nq)]
    it = jnp.asarray(np.array([a for a, _ in tiles], np.int32))
    jt = jnp.asarray(np.array([b for _, b in tiles], np.int32))

    def body(it_ref, jt_ref, q_ref, qt_ref, k_ref, kt_ref, v_ref, do_ref, dot_ref, lse_ref, dl_ref, cr_ref, cc_ref,
             dqt_ref, dk_ref, dv_ref, dc_ref, dr_ref, dk_sc, dv_sc, dc_sc):
        s_id = pl.program_id(1)
        i, j = it_ref[s_id], jt_ref[s_id]
        first = _lane_is_first_head()
        holds_diagonal = i == (j * bk) // bq

        @pl.when(s_id == 0)
        def _():
            dqt_ref[...] = jnp.zeros(dqt_ref.shape, F32)
            dr_ref[...] = jnp.zeros(dr_ref.shape, F32)

        @pl.when(holds_diagonal)
        def _():
            dk_sc[...] = jnp.zeros(dk_sc.shape, F32)
            dv_sc[...] = jnp.zeros(dv_sc.shape, F32)
            dc_sc[...] = jnp.zeros(dc_sc.shape, F32)

        def tile(diagonal):
            q2, qt2, k2, kt2, v2, do2, dot2 = (q_ref[...], qt_ref[...], k_ref[...], kt_ref[...], v_ref[...], do_ref[...],
                                               dot_ref[...])
            dk_t, dv_t, dqt_t = [], [], []
            for hh in range(2):
                mine = first if hh == 0 else jnp.logical_not(first)
                st = (_dot(jnp.where(mine, k2, jnp.zeros_like(k2)), qt2)
                      + (cr_ref[0, hh:hh + 1, :] - cc_ref[0, :, hh:hh + 1]))
                if diagonal:
                    key = j * bk + lax.broadcasted_iota(jnp.int32, st.shape, 0)
                    query = i * bq + lax.broadcasted_iota(jnp.int32, st.shape, 1)
                    st = jnp.where(key <= query, st, -jnp.inf)
                pt = jnp.exp(st - lse_ref[0, hh:hh + 1, :])
                dv_t.append(_dot(pt.astype(BF16), do2))
                dpt = _dot(jnp.where(mine, v2, jnp.zeros_like(v2)), dot2)
                dst = pt * (dpt - dl_ref[0, hh:hh + 1, :])
                dc_sc[hh] += jnp.sum(dst, axis=1, keepdims=True)
                dr_ref[0, i, hh:hh + 1, :] += jnp.sum(dst, axis=0, keepdims=True)
                dsb = dst.astype(BF16)
                dk_t.append(_dot(dsb, q2))
                dqt_t.append(_dot(kt2[hh * FOX_HEAD_DIM:(hh + 1) * FOX_HEAD_DIM], dsb))
            dk_sc[...] += jnp.where(first, dk_t[0], dk_t[1])
            dv_sc[...] += jnp.where(first, dv_t[0], dv_t[1])
            dqt_ref[0, i] += jnp.concatenate(dqt_t, axis=0)

        @pl.when(jnp.logical_not(holds_diagonal))
        def _():
            tile(False)

        @pl.when(holds_diagonal)
        def _():
            tile(True)

        @pl.when(i == nq - 1)
        def _():
            dk_ref[...] = dk_sc[...]
            dv_ref[...] = dv_sc[...]
            for hh in range(2):
                dc_ref[0, :, hh:hh + 1] = -dc_sc[hh]

    qspec = pl.BlockSpec((bq, LANES), lambda p, s, it, jt: (it[s], p))
    qtspec = pl.BlockSpec((LANES, bq), lambda p, s, it, jt: (p, it[s]))
    kspec = pl.BlockSpec((bk, LANES), lambda p, s, it, jt: (jt[s], p))
    ktspec = pl.BlockSpec((LANES, bk), lambda p, s, it, jt: (p, jt[s]))
    rowq = pl.BlockSpec((1, 2, bq), lambda p, s, it, jt: (p, 0, it[s]))
    colk = pl.BlockSpec((1, bk, 2), lambda p, s, it, jt: (p, jt[s], 0))
    dqt_spec = pl.BlockSpec((1, nq, LANES, bq), lambda p, s, it, jt: (p, 0, 0, 0))
    dr_spec = pl.BlockSpec((1, nq, 2, bq), lambda p, s, it, jt: (p, 0, 0, 0))
    dqt, dk, dv, dc_keys, dc_queries = _pcall(
        body, name="fox_attention_bwd",
        grid_spec=pltpu.PrefetchScalarGridSpec(
            num_scalar_prefetch=2, grid=(pairs, len(tiles)),
            in_specs=[qspec, qtspec, kspec, ktspec, kspec, qspec, qtspec, rowq, rowq, rowq, colk],
            out_specs=[dqt_spec, kspec, kspec, colk, dr_spec],
            scratch_shapes=[pltpu.VMEM((bk, LANES), F32), pltpu.VMEM((bk, LANES), F32), pltpu.VMEM((2, bk, 1), F32)]),
        out_shape=[jax.ShapeDtypeStruct((pairs, nq, LANES, bq), F32), jax.ShapeDtypeStruct((t, FOX_WIDTH), F32),
                   jax.ShapeDtypeStruct((t, FOX_WIDTH), F32), jax.ShapeDtypeStruct((pairs, t, 2), F32),
                   jax.ShapeDtypeStruct((pairs, nq, 2, bq), F32)],
        compiler_params=_params("parallel", "arbitrary"),
    )(it, jt, qn, qnt, kn, knt, vb, do, dot, lse_row, delta_row, crow, ccol)
    dq = jnp.transpose(dqt, (1, 3, 0, 2)).reshape(t, FOX_WIDTH)
    dc_keys = jnp.transpose(dc_keys, (0, 2, 1)).reshape(FOX_HEADS, t)
    dc_queries = jnp.transpose(dc_queries, (0, 2, 1, 3)).reshape(FOX_HEADS, t)
    return dq, dk, dv, dc_keys, dc_queries


def _fox_delta(dcat, fox, *, bt=512):
    t = fox.shape[0]
    w = FOX_WIDTH

    def body(do_ref, o_ref, dob_ref, dl_ref, dot_ref):
        do = do_ref[...]
        dot_ref[...] = jnp.transpose(do).astype(BF16)
        dob = do.astype(BF16)
        r = lax.broadcasted_iota(jnp.int32, (w, LANES), 0) // FOX_HEAD_DIM
        c = lax.broadcasted_iota(jnp.int32, (w, LANES), 1)
        dl_ref[...] = _dot_exact(dob.astype(F32) * o_ref[...].astype(F32), (r == c).astype(BF16))
        dob_ref[...] = dob

    blk = pl.BlockSpec((bt, w), lambda i: (i, 0))
    return _pcall(
        body, name="fox_delta", grid=(t // bt,), in_specs=[blk, blk],
        out_specs=[blk, pl.BlockSpec((bt, LANES), lambda i: (i, 0)), pl.BlockSpec((w, bt), lambda i: (0, i))],
        out_shape=[jax.ShapeDtypeStruct((t, w), BF16), jax.ShapeDtypeStruct((t, LANES), F32),
                   jax.ShapeDtypeStruct((w, t), BF16)],
        compiler_params=_params("parallel"),
    )(dcat, fox)


def _mixer_dz(z, dq, dk, dv, dpin, dfpad, gq, gk, *, bt=256):
    t = z.shape[0]
    w = FOX_WIDTH

    def body(q_ref, k_ref, dq_ref, dk_ref, dv_ref, dp_ref, df_ref, gq_ref, gk_ref, dz_ref, dgq_ref, dgk_ref):
        gm = _group_matrix(w, FOX_HEAD_DIM)
        first_step = pl.program_id(0) == 0
        for n, (x_ref, dy_ref, g_ref, dg_ref, scale) in enumerate(
                ((q_ref, dq_ref, gq_ref, dgq_ref, FOX_SCALE), (k_ref, dk_ref, gk_ref, dgk_ref, 1.0))):
            x = x_ref[...]
            r = lax.rsqrt(_dot_exact(x * x, gm, terms=2) * (1.0 / FOX_HEAD_DIM) + EPS)
            xhat = x * r
            dy = dy_ref[...] * scale
            gy = dy * g_ref[...]
            dx = r * (gy - xhat * (_dot_exact(gy * xhat, gm, terms=2) * (1.0 / FOX_HEAD_DIM)))
            dz_ref[:, n * w:(n + 1) * w] = dx.astype(BF16)
            part = jnp.sum(dy * xhat, axis=0, keepdims=True)

            @pl.when(first_step)
            def _():
                dg_ref[...] = part

            @pl.when(jnp.logical_not(first_step))
            def _():
                dg_ref[...] += part

        dz_ref[:, 2 * w:3 * w] = dv_ref[...].astype(BF16)
        dz_ref[:, 3 * w:4 * w] = dp_ref[...].astype(BF16)
        dz_ref[:, 4 * w:] = df_ref[...].astype(BF16)

    col = lambda c: pl.BlockSpec((bt, w), lambda i, c=c: (i, c))
    blk = pl.BlockSpec((bt, w), lambda i: (i, 0))
    vec = pl.BlockSpec((1, w), lambda i: (0, 0))
    dz, dgq, dgk = _pcall(
        body, name="mixer_dz", grid=(t // bt,),
        in_specs=[col(0), col(1), blk, blk, blk, blk, pl.BlockSpec((bt, LANES), lambda i: (i, 0)), vec, vec],
        out_specs=[pl.BlockSpec((bt, IN_COLS_PAD), lambda i: (i, 0)), vec, vec],
        out_shape=[jax.ShapeDtypeStruct((t, IN_COLS_PAD), BF16), jax.ShapeDtypeStruct((1, w), F32),
                   jax.ShapeDtypeStruct((1, w), F32)],
        compiler_params=_params("arbitrary"),
    )(z, z, dq, dk, dv, dpin, dfpad, jnp.tile(gq, FOX_HEADS).reshape(1, w), jnp.tile(gk, FOX_HEADS).reshape(1, w))
    return dz, dgq.reshape(FOX_HEADS, FOX_HEAD_DIM).sum(0), dgk.reshape(FOX_HEADS, FOX_HEAD_DIM).sum(0)


def _pool_fwd(z, wp, scale, *, bt=512):
    t = z.shape[0]
    w = POOL_WIDTH
    hb = bt // POOL_HALO

    def body(p_ref, h_ref, wp_ref, sc_ref, y_ref, mx_ref):
        i = pl.program_id(0)
        cur = p_ref[...]
        halo = jnp.where(i > 0, h_ref[...], 0.0)
        ext = jnp.concatenate([halo, cur], axis=0)
        trow = i * bt + lax.broadcasted_iota(jnp.int32, (bt, 1), 0)
        for g, win in enumerate(POOL_WINDOWS):
            sl = slice(g * LANES, (g + 1) * LANES)
            e = ext[:, sl]
            acc = e[POOL_HALO:]
            for k in range(1, win):
                acc = acc + pltpu.roll(e, k, 0)[POOL_HALO:]
            cnt = jnp.minimum(trow + 1, win).astype(F32)
            mixed = (acc / cnt - cur[:, sl]).astype(BF16)
            mx_ref[:, sl] = mixed
            y_ref[:, sl] = (_dot(mixed, wp_ref[g]) * sc_ref[:, sl]).astype(BF16)

    blk = pl.BlockSpec((bt, w), lambda i: (i, 0))
    return _pcall(
        body, name="pool_fwd", grid=(t // bt,),
        in_specs=[pl.BlockSpec((bt, w), lambda i: (i, 3)),
                  pl.BlockSpec((POOL_HALO, w), lambda i: (jnp.maximum(i * hb - 1, 0), 3)),
                  pl.BlockSpec((len(POOL_WINDOWS), LANES, LANES), lambda i: (0, 0, 0)),
                  pl.BlockSpec((1, w), lambda i: (0, 0))],
        out_specs=[blk, blk], out_shape=[jax.ShapeDtypeStruct((t, w), BF16)] * 2,
        compiler_params=_params("parallel"),
    )(z, z, wp, scale.reshape(1, w))


def _pool_bwd(dcat, mixed, wp, scale, *, bt=512):
    t = mixed.shape[0]
    w = POOL_WIDTH
    hb = bt // POOL_HALO
    nb = t // bt
    n_ext = bt + POOL_HALO

    def body(d_ref, h_ref, mx_ref, wp_ref, sc_ref, dp_ref, dwp_ref, dsc_ref):
        i = pl.program_id(0)
        cur = d_ref[...]
        nxt = jnp.where(i < nb - 1, h_ref[...], 0.0)
        ext = jnp.concatenate([cur, nxt], axis=0)
        trow = i * bt + lax.broadcasted_iota(jnp.int32, (n_ext, 1), 0)

        @pl.when(i == 0)
        def _():
            dwp_ref[...] = jnp.zeros(dwp_ref.shape, F32)
            dsc_ref[...] = jnp.zeros(dsc_ref.shape, F32)

        for g, win in enumerate(POOL_WINDOWS):
            sl = slice(g * LANES, (g + 1) * LANES)
            dy = (ext[:, sl] * sc_ref[:, sl]).astype(BF16)
            dm = _dot(dy, wp_ref[g], NT)
            mixed_g = mx_ref[:, sl]
            dsc_ref[:, sl] += jnp.sum(cur[:, sl] * _dot(mixed_g, wp_ref[g]), axis=0, keepdims=True)
            dwp_ref[g] += _dot(mixed_g, dy[:bt], TN)
            r = dm / jnp.minimum(trow + 1, win).astype(F32)
            acc = r[:bt]
            for k in range(1, win):
                acc = acc + pltpu.roll(r, n_ext - k, 0)[:bt]
            dp_ref[:, sl] = acc - dm[:bt]

    return _pcall(
        body, name="pool_bwd", grid=(nb,),
        in_specs=[pl.BlockSpec((bt, w), lambda i: (i, 1)),
                  pl.BlockSpec((POOL_HALO, w), lambda i: (jnp.minimum((i + 1) * hb, t // POOL_HALO - 1), 1)),
                  pl.BlockSpec((bt, w), lambda i: (i, 0)),
                  pl.BlockSpec((len(POOL_WINDOWS), LANES, LANES), lambda i: (0, 0, 0)),
                  pl.BlockSpec((1, w), lambda i: (0, 0))],
        out_specs=[pl.BlockSpec((bt, w), lambda i: (i, 0)),
                   pl.BlockSpec((len(POOL_WINDOWS), LANES, LANES), lambda i: (0, 0, 0)),
                   pl.BlockSpec((1, w), lambda i: (0, 0))],
        out_shape=[jax.ShapeDtypeStruct((t, w), F32), jax.ShapeDtypeStruct((len(POOL_WINDOWS), LANES, LANES), F32),
                   jax.ShapeDtypeStruct((1, w), F32)],
        compiler_params=_params("arbitrary"),
    )(dcat, dcat, mixed, wp, scale.reshape(1, w))


def _head_rms(x):
    return lax.rsqrt(jnp.mean(x * x, axis=-1, keepdims=True) + EPS)


def _mem_kv_fwd(mem, g_kv, w_kv, g_k):
    mlen, d = mem.shape

    def body(m_ref, g_ref, w_ref, gk_ref, mn_ref, mkv_ref, mk_ref, mv_ref):
        x = m_ref[...]
        mn = (x * lax.rsqrt(jnp.mean(x * x, axis=-1, keepdims=True) + EPS) * g_ref[...]).astype(BF16)
        mn_ref[...] = mn
        mkv = _dot(mn, w_ref[...])
        mkv_ref[...] = mkv
        for h in range(MEM_HEADS):
            sl = slice(h * MEM_HEAD_DIM, (h + 1) * MEM_HEAD_DIM)
            kh = mkv[:, sl]
            mk_ref[:, sl] = (kh * _head_rms(kh) * gk_ref[...]).astype(BF16)
        mv_ref[...] = mkv[:, MEM_WIDTH:].astype(BF16)

    return _pcall(
        body, name="mem_kv_fwd",
        out_shape=[jax.ShapeDtypeStruct((mlen, d), BF16), jax.ShapeDtypeStruct((mlen, 2 * MEM_WIDTH), F32),
                   jax.ShapeDtypeStruct((mlen, MEM_WIDTH), BF16), jax.ShapeDtypeStruct((mlen, MEM_WIDTH), BF16)],
        compiler_params=_params(),
    )(mem, g_kv.reshape(1, d), w_kv, g_k.reshape(1, MEM_HEAD_DIM))


def _mem_kv_bwd(dmk, dmv, mkv, mn, mem, g_kv, w_kv, g_k):
    mlen, d = mem.shape

    def body(dmk_ref, dmv_ref, mkv_ref, mn_ref, m_ref, g_ref, w_ref, gk_ref, dw_ref, dg_ref, dgk_ref, dkv_sc):
        dgk = jnp.zeros((1, MEM_HEAD_DIM), F32)
        for h in range(MEM_HEADS):
            sl = slice(h * MEM_HEAD_DIM, (h + 1) * MEM_HEAD_DIM)
            x = mkv_ref[:, sl]
            r = _head_rms(x)
            xhat = x * r
            dy = dmk_ref[:, sl]
            gy = dy * gk_ref[...]
            dkv_sc[:, sl] = (r * (gy - xhat * jnp.mean(gy * xhat, axis=-1, keepdims=True))).astype(BF16)
            dgk = dgk + jnp.sum(dy * xhat, axis=0, keepdims=True)
        dgk_ref[...] = dgk
        dkv_sc[:, MEM_WIDTH:] = dmv_ref[...].astype(BF16)
        dkv = dkv_sc[...]
        dw_ref[...] = _dot(mn_ref[...], dkv, TN).astype(BF16)
        dmn = _dot(dkv, w_ref[...], NT)
        x = m_ref[...]
        xhat = x * lax.rsqrt(jnp.mean(x * x, axis=-1, keepdims=True) + EPS)
        dg_ref[...] = jnp.sum(dmn * xhat, axis=0, keepdims=True)

    dw, dg, dgk = _pcall(
        body, name="mem_kv_bwd",
        out_shape=[jax.ShapeDtypeStruct((d, 2 * MEM_WIDTH), BF16), jax.ShapeDtypeStruct((1, d), F32),
                   jax.ShapeDtypeStruct((1, MEM_HEAD_DIM), F32)],
        scratch_shapes=[pltpu.VMEM((mlen, 2 * MEM_WIDTH), BF16)],
        compiler_params=_params(),
    )(dmk, dmv, mkv, mn, mem, g_kv.reshape(1, d), w_kv, g_k.reshape(1, MEM_HEAD_DIM))
    return dw, dg.reshape(d), dgk.reshape(MEM_HEAD_DIM)


def _cross_probs(x, g, mk_h):
    r = _head_rms(x)
    xhat = x * r
    qn = (xhat * g).astype(BF16)
    s = _dot(qn, mk_h, NT) * MEM_SCALE
    e = jnp.exp(s - jnp.max(s, axis=-1, keepdims=True))
    return r, xhat, qn, e / jnp.sum(e, axis=-1, keepdims=True)


def _cross_fwd(mq_raw, g_q, mk, mv, *, bt=512):
    t = mq_raw.shape[0]
    mlen = mk.shape[0]

    def body(x_ref, g_ref, mk_ref, mv_ref, o_ref):
        for h in range(MEM_HEADS):
            sl = slice(h * MEM_HEAD_DIM, (h + 1) * MEM_HEAD_DIM)
            _, _, _, p = _cross_probs(x_ref[:, sl], g_ref[...], mk_ref[:, sl])
            o_ref[:, sl] = _dot(p.astype(BF16), mv_ref[:, sl]).astype(BF16)

    blk = pl.BlockSpec((bt, MEM_WIDTH), lambda i: (i, 0))
    kv = pl.BlockSpec((mlen, MEM_WIDTH), lambda i: (0, 0))
    return _pcall(
        body, name="cross_attention_fwd", grid=(t // bt,),
        in_specs=[blk, pl.BlockSpec((1, MEM_HEAD_DIM), lambda i: (0, 0)), kv, kv], out_specs=blk,
        out_shape=jax.ShapeDtypeStruct((t, MEM_WIDTH), BF16), compiler_params=_params("parallel"),
    )(mq_raw, g_q.reshape(1, MEM_HEAD_DIM), mk, mv)


def _cross_bwd(mq_raw, dmo, g_q, mk, mv, *, bt=512):
    t = mq_raw.shape[0]
    mlen = mk.shape[0]

    def body(x_ref, do_ref, g_ref, mk_ref, mv_ref, dx_ref, dmk_ref, dmv_ref, dg_ref):
        @pl.when(pl.program_id(0) == 0)
        def _():
            dmk_ref[...] = jnp.zeros(dmk_ref.shape, F32)
            dmv_ref[...] = jnp.zeros(dmv_ref.shape, F32)
            dg_ref[...] = jnp.zeros(dg_ref.shape, F32)

        for h in range(MEM_HEADS):
            sl = slice(h * MEM_HEAD_DIM, (h + 1) * MEM_HEAD_DIM)
            r, xhat, qn, p = _cross_probs(x_ref[:, sl], g_ref[...], mk_ref[:, sl])
            do = do_ref[:, sl]
            dp = _dot(do, mv_ref[:, sl], NT)
            ds = (p * (dp - jnp.sum(p * dp, axis=-1, keepdims=True)) * MEM_SCALE).astype(BF16)
            dmv_ref[:, sl] += _dot(p.astype(BF16), do, TN)
            dmk_ref[:, sl] += _dot(ds, qn, TN)
            dqn = _dot(ds, mk_ref[:, sl])
            gy = dqn * g_ref[...]
            dx_ref[:, sl] = (r * (gy - xhat * jnp.mean(gy * xhat, axis=-1, keepdims=True))).astype(BF16)
            dg_ref[...] += jnp.sum(dqn * xhat, axis=0, keepdims=True)

    blk = pl.BlockSpec((bt, MEM_WIDTH), lambda i: (i, 0))
    kv = pl.BlockSpec((mlen, MEM_WIDTH), lambda i: (0, 0))
    gs = pl.BlockSpec((1, MEM_HEAD_DIM), lambda i: (0, 0))
    dx, dmk, dmv, dg = _pcall(
        body, name="cross_attention_bwd", grid=(t // bt,),
        in_specs=[blk, blk, gs, kv, kv], out_specs=[blk, kv, kv, gs],
        out_shape=[jax.ShapeDtypeStruct((t, MEM_WIDTH), BF16), jax.ShapeDtypeStruct((mlen, MEM_WIDTH), F32),
                   jax.ShapeDtypeStruct((mlen, MEM_WIDTH), F32), jax.ShapeDtypeStruct((1, MEM_HEAD_DIM), F32)],
        compiler_params=_params("arbitrary"),
    )(mq_raw, dmo, g_q.reshape(1, MEM_HEAD_DIM), mk, mv)
    return dx, dmk, dmv, dg.reshape(MEM_HEAD_DIM)


def _loss_head(y, target, *, bt=512):
    t, d = y.shape

    def body(y_ref, t_ref, dy_ref, l_ref):
        e = y_ref[...] - t_ref[...]
        dy_ref[...] = e * (1.0 / d)
        part = (0.5 / d) * jnp.sum(jnp.sum(e * e, axis=1, keepdims=True), axis=0, keepdims=True)

        @pl.when(pl.program_id(0) == 0)
        def _():
            l_ref[...] = part

        @pl.when(pl.program_id(0) > 0)
        def _():
            l_ref[...] += part

    blk = pl.BlockSpec((bt, d), lambda i: (i, 0))
    dy, loss = _pcall(
        body, name="loss_head", grid=(t // bt,), in_specs=[blk, blk],
        out_specs=[blk, pl.BlockSpec((1, 1), lambda i: (0, 0))],
        out_shape=[jax.ShapeDtypeStruct((t, d), F32), jax.ShapeDtypeStruct((1, 1), F32)],
        compiler_params=_params("arbitrary"),
    )(y, target)
    return loss, dy


def _row_tile(rows, cols, budget=1 << 19):
    best = None
    for cand in range(8, rows + 1, 8):
        if rows % cand == 0 and cand * cols <= budget:
            best = cand
    return best or rows


def _adamw(w, g, m, v, *, name):
    rows, cols = w.shape
    bt = _row_tile(rows, cols)
    c1 = 1.0 - ADAM_B1 ** ADAM_STEP
    c2 = 1.0 - ADAM_B2 ** ADAM_STEP

    def body(w_ref, g_ref, m_ref, v_ref, d_ref, nm_ref, nv_ref):
        g_v = g_ref[...]
        nm = ADAM_B1 * m_ref[...] + (1.0 - ADAM_B1) * g_v
        nv = ADAM_B2 * v_ref[...] + (1.0 - ADAM_B2) * (g_v * g_v)
        nm_ref[...] = nm
        nv_ref[...] = nv
        d_ref[...] = -ADAM_LR * ((nm / c1) / (jnp.sqrt(nv / c2) + ADAM_EPS) + ADAM_WD * w_ref[...])

    blk = pl.BlockSpec((bt, cols), lambda i: (i, 0))
    return _pcall(
        body, name=name, grid=(rows // bt,), in_specs=[blk] * 4, out_specs=[blk] * 3,
        out_shape=[jax.ShapeDtypeStruct((rows, cols), F32)] * 3, compiler_params=_params("parallel"),
    )(w, g, m, v)


def _sum_slots(x, *, name, after=None):
    n, rows, cols = x.shape
    bt = _row_tile(rows, cols, budget=1 << 17)

    def body(x_ref, *rest):
        o_ref = rest[-1]
        acc = x_ref[0].astype(F32)
        for s in range(1, n):
            acc = acc + x_ref[s].astype(F32)
        o_ref[...] = acc

    return _pcall(
        body, name=name, grid=(rows // bt,),
        in_specs=[pl.BlockSpec((n, bt, cols), lambda i: (0, i, 0))]
        + ([pl.BlockSpec(memory_space=pl.ANY)] if after is not None else []),
        out_specs=pl.BlockSpec((bt, cols), lambda i: (i, 0)),
        out_shape=jax.ShapeDtypeStruct((rows, cols), F32), compiler_params=_params("parallel"),
    )(x, *((after,) if after is not None else ()))


def _any_spec():
    return pl.BlockSpec(memory_space=pl.ANY)


def _all_gather(xs, *, name):
    n = len(xs)

    def body(*refs):
        x_refs, out_refs = refs[:n], refs[n:2 * n]
        send_sems, recv_sems, local_sems = refs[2 * n:]
        x, y, c = lax.axis_index("x"), lax.axis_index("y"), lax.axis_index("c")
        me, sibling = (x, y, c), (x, y, 1 - c)
        chips = [(1 - x, y), (x, 1 - y), (1 - x, 1 - y)]

        def slot(a, px, py, pc):
            return out_refs[a].at[4 * px + 2 * py + pc]

        def copy(a, k, block, to, src=None):
            return pltpu.make_async_remote_copy(
                src_ref=slot(a, *block) if src is None else src, dst_ref=slot(a, *block),
                send_sem=send_sems.at[a, k], recv_sem=recv_sems.at[a, k], device_id=to, device_id_type=MESH_ID)

        mine = [pltpu.make_async_copy(x_refs[a], slot(a, *me), local_sems.at[a]) for a in range(n)]
        for cp in mine:
            cp.start()
        first = []
        for j, chip in enumerate(chips):
            first += [copy(a, 1 + j, me, (*chip, c), src=x_refs[a]) for a in range(n)]
        first += [copy(a, 0, me, sibling, src=x_refs[a]) for a in range(n)]
        for cp in first:
            cp.start()
        passed = []
        for j, chip in enumerate(chips):
            for a in range(n):
                copy(a, 1 + j, (*chip, c), me).wait_recv()
                cp = copy(a, 4 + j, (*chip, c), sibling)
                cp.start()
                passed.append(cp)
        for a in range(n):
            copy(a, 0, sibling, me).wait_recv()
        for j, chip in enumerate(chips):
            for a in range(n):
                copy(a, 4 + j, (*chip, 1 - c), me).wait_recv()
        for cp in first + passed:
            cp.wait_send()
        for cp in mine:
            cp.wait()

    return _pcall(
        body, name=name, in_specs=[_any_spec()] * n, out_specs=[_any_spec()] * n,
        out_shape=[jax.ShapeDtypeStruct((N_DEV,) + x.shape, x.dtype) for x in xs],
        scratch_shapes=[pltpu.SemaphoreType.DMA((n, 7)), pltpu.SemaphoreType.DMA((n, 7)), pltpu.SemaphoreType.DMA((n,))],
    )(*xs)


def _mesh_peer(k):
    px = lax.axis_index("x") ^ ((k >> 2) & 1)
    py = lax.axis_index("y") ^ ((k >> 1) & 1)
    pc = lax.axis_index("c") ^ (k & 1)
    return (px, py, pc), 4 * px + 2 * py + pc


def _my_index():
    return 4 * lax.axis_index("x") + 2 * lax.axis_index("y") + lax.axis_index("c")


def _landing_zones(xs, blocks):
    me = _my_index()
    lands = []
    for x in xs:
        own = lax.dynamic_index_in_dim(x, me, 0, keepdims=True) if blocks else x[None]
        zone = lax.empty((N_DEV,) + own.shape[1:], x.dtype)
        lands.append(lax.dynamic_update_slice(zone, own, (me,) + (0,) * (own.ndim - 1)))
    return lands


def _send_start(xs, lands, *, blocks, name, after=None):
    n = len(xs)
    peers = N_DEV - 1
    first_out = 2 * n + (after is not None)

    def body(*refs):
        x_refs, land_refs = refs[:n], refs[n:2 * n]
        send_sems, recv_sems = refs[first_out:first_out + peers], refs[first_out + peers:first_out + 2 * peers]
        token = refs[-1]
        me_idx = _my_index()
        for k in (2, 4, 6, 3, 5, 7, 1):
            to, to_idx = _mesh_peer(k)
            for a in range(n):
                pltpu.make_async_remote_copy(
                    src_ref=x_refs[a].at[to_idx] if blocks else x_refs[a], dst_ref=land_refs[a].at[me_idx],
                    send_sem=send_sems[k - 1], recv_sem=recv_sems[k - 1], device_id=to, device_id_type=MESH_ID).start()
        token[...] = jnp.zeros(token.shape, token.dtype)

    hbm = pl.BlockSpec(memory_space=pltpu.HBM)
    sem = pl.BlockSpec(memory_space=pltpu.SEMAPHORE)
    both = list(xs) + list(lands)
    out = _pcall(
        body, name=name,
        out_shape=(*[pltpu.SemaphoreType.DMA(())] * (2 * peers), *[pltpu.HBM(a.shape, a.dtype) for a in both],
                   jax.ShapeDtypeStruct((8, LANES), F32)),
        in_specs=[hbm] * (2 * n) + ([pl.BlockSpec(memory_space=pl.ANY)] if after is not None else []),
        out_specs=(*[sem] * (2 * peers), *[hbm] * (2 * n), pl.BlockSpec(memory_space=pltpu.VMEM)),
        input_output_aliases={i: 2 * peers + i for i in range(2 * n)},
        compiler_params=pltpu.CompilerParams(has_side_effects=pltpu.SideEffectType.DATAFLOW_SIDE_EFFECTING),
    )(*[pltpu.with_memory_space_constraint(a, pltpu.HBM) for a in both], *((after,) if after is not None else ()))
    return dict(sems=out[:2 * peers], xs=out[2 * peers:2 * peers + n], lands=out[2 * peers + n:2 * peers + 2 * n],
                token=out[-1], blocks=blocks)


def _send_wait(started, after, *, name):
    n = len(started['xs'])
    blocks = started['blocks']
    peers = N_DEV - 1

    def body(*refs):
        x_refs, land_refs = refs[:n], refs[n:2 * n]
        send_sems, recv_sems = refs[2 * n:2 * n + peers], refs[2 * n + peers:2 * n + 2 * peers]
        for k in range(1, N_DEV):
            frm, frm_idx = _mesh_peer(k)
            for a in range(n):
                copy = pltpu.make_async_remote_copy(
                    src_ref=x_refs[a].at[frm_idx] if blocks else x_refs[a], dst_ref=land_refs[a].at[frm_idx],
                    send_sem=send_sems[k - 1], recv_sem=recv_sems[k - 1], device_id=frm, device_id_type=MESH_ID)
                copy.wait_send()
                copy.wait_recv()

    hbm = pl.BlockSpec(memory_space=pltpu.HBM)
    sem = pl.BlockSpec(memory_space=pltpu.SEMAPHORE)
    both = list(started['xs']) + list(started['lands'])
    out = _pcall(
        body, name=name, out_shape=[pltpu.HBM(a.shape, a.dtype) for a in both],
        in_specs=[hbm] * (2 * n) + [sem] * (2 * peers) + [pl.BlockSpec(memory_space=pl.ANY)], out_specs=[hbm] * (2 * n),
        input_output_aliases={i: i for i in range(2 * n)},
        compiler_params=pltpu.CompilerParams(has_side_effects=pltpu.SideEffectType.DATAFLOW_SIDE_EFFECTING),
    )(*both, *started['sems'], after)
    return out[n:]


COLUMN_SHARDED = ('w_in', 'w_mem_out', 'w_gate_up')


def _full_weight(name, g):
    if name in COLUMN_SHARDED:
        g = jnp.transpose(g, (1, 0, 2))
        g = g.reshape(g.shape[0], -1)
    else:
        g = g.reshape(-1, g.shape[-1])
    return jnp.pad(g, ((0, 0), (0, IN_COLS_PAD - IN_COLS))) if name == 'w_in' else g


def _grad_blocks(name, g):
    if name == 'w_gate_up':
        rows, cols = g.shape[1], 2 * g.shape[2] // N_DEV
        g = jnp.transpose(g.reshape(2, rows, N_DEV // 2, cols), (0, 2, 1, 3))
        return g.reshape(N_DEV, rows, cols).astype(BF16)
    if name == 'w_in':
        g = g[:, :IN_COLS]
    if name in COLUMN_SHARDED:
        rows, cols = g.shape[0], g.shape[1] // N_DEV
        g = jnp.transpose(g.reshape(rows, N_DEV, cols), (1, 0, 2))
    else:
        rows, cols = g.shape[0] // N_DEV, g.shape[1]
        g = g.reshape(N_DEV, rows, cols)
    return g.astype(BF16)


SMALL_SHAPES = {'g_mix': (DEPTH, D_MODEL), 'b_forget': (DEPTH, FOX_HEADS), 'g_q_fox': (DEPTH, FOX_HEAD_DIM),
                'g_k_fox': (DEPTH, FOX_HEAD_DIM), 'w_pool': (DEPTH, 4, POOL_GROUP_DIM, POOL_GROUP_DIM),
                'pool_scale': (DEPTH, POOL_WIDTH), 'g_mem_q': (DEPTH, D_MODEL), 'g_mem_kv': (DEPTH, D_MODEL),
                'g_q_mem': (DEPTH, MEM_HEAD_DIM), 'g_k_mem': (DEPTH, MEM_HEAD_DIM), 'g_ffn': (DEPTH, D_MODEL)}


def _small_rows(name):
    return -(-int(np.prod(SMALL_SHAPES[name])) // LANES)


SMALL_ROWS = -(-sum(_small_rows(n) for n in SMALL) // 8) * 8


def _pack_small(tree):
    parts = []
    for n in SMALL:
        flat = tree[n].reshape(-1).astype(F32)
        parts.append(jnp.pad(flat, (0, _small_rows(n) * LANES - flat.shape[0])))
    flat = jnp.concatenate(parts)
    return jnp.pad(flat, (0, SMALL_ROWS * LANES - flat.shape[0])).reshape(SMALL_ROWS, LANES)


def _unpack_small(packed):
    flat = packed.reshape(-1)
    out, at = {}, 0
    for n in SMALL:
        size = int(np.prod(SMALL_SHAPES[n]))
        out[n] = flat[at:at + size].reshape(SMALL_SHAPES[n])
        at += _small_rows(n) * LANES
    return out


def _pairs_cols(a):
    t = a.shape[0]
    return jnp.transpose(a.reshape(t, FOX_HEADS // 2, 2), (1, 0, 2))


def _pairs_rows(a):
    return a.reshape(FOX_HEADS // 2, 2, a.shape[1])


def _layer_fwd(h0, mem, p, w_in, other_weights):
    s = {'h0': h0}
    s['xn1'], z, f = _norm_matmul(h0, p['g_mix'], w_in, tn=IN_COLS_PAD, tail=LANES, name="norm_in_proj_fwd")
    s['z'] = z
    s['qn'], s['kn'], s['vb'], s['qnt'], s['knt'], vt, q2t = _qkv_prep(z, p['g_q_fox'], p['g_k_fox'])
    s['ft'] = jnp.transpose(f[:, :FOX_HEADS])
    c = _forget_cumsum(s['ft'], p['b_forget'])
    s['ccol'], s['crow'] = _pairs_cols(jnp.transpose(c)), _pairs_rows(c)
    s['fox'], s['lse_row'] = _fox_fwd(q2t, s['kn'], vt, s['crow'] * LOG2E, s['ccol'] * LOG2E)
    pool, s['mixed'] = _pool_fwd(z, p['w_pool'].astype(BF16), p['pool_scale'])
    s['cat'] = jnp.concatenate([s['fox'], pool], axis=1)
    w = dict(other_weights(s['lse_row']), w_in=w_in)
    h1 = _matmul(s['cat'], w['w_out'], res=h0, name="out_proj_fwd")
    s['h1'] = h1

    s['hn2'], s['mq_raw'] = _norm_matmul(h1, p['g_mem_q'], w['w_mem_q'], name="norm_mem_q_fwd")
    s['mn'], s['mkv'], s['mk'], s['mv'] = _mem_kv_fwd(mem, p['g_mem_kv'], w['w_mem_kv'], p['g_k_mem'])
    s['mo'] = _cross_fwd(s['mq_raw'], p['g_q_mem'], s['mk'], s['mv'])
    h2 = _matmul(s['mo'], w['w_mem_out'], res=h1, name="mem_out_fwd")
    s['h2'] = h2

    s['hn3'], s['gu'], s['act'] = _norm_gate_up_swiglu(h2, p['g_ffn'], w['w_gate_up'])
    h3 = _matmul(s['act'], w['w_down'], res=h2, name="down_fwd")
    return h3, s, w


def _layer_bwd(dh, mem, p, w, s, after=None, at_mixer=None):
    g = {}
    g['w_down'] = _matmul(s['act'], dh, ta=True, out_dtype=BF16, tm=1408, tn=512, tk=1024, name="down_dw")
    dgu = _down_dx_swiglu_bwd(dh, w['w_down'], s['gu'], after=after)
    g['w_gate_up'] = _matmul(s['hn3'], dgu, ta=True, out_dtype=BF16, tm=1024, tn=1408, tk=1024, name="gate_up_dw")
    dh, g['g_ffn'] = _matmul_norm_bwd(dgu, w['w_gate_up'], s['h2'], p['g_ffn'], dh, name="gate_up_dx_norm_bwd")

    g['w_mem_out'] = _matmul(s['mo'], dh, ta=True, out_dtype=BF16, tm=512, tn=1024, tk=1024, name="mem_out_dw")
    dmo = _matmul(dh, w['w_mem_out'], tb=True, out_dtype=BF16, name="mem_out_dx")
    dmq, dmk, dmv, g['g_q_mem'] = _cross_bwd(s['mq_raw'], dmo, p['g_q_mem'], s['mk'], s['mv'])
    g['w_mem_kv'], g['g_mem_kv'], g['g_k_mem'] = _mem_kv_bwd(dmk, dmv, s['mkv'], s['mn'], mem, p['g_mem_kv'],
                                                               w['w_mem_kv'], p['g_k_mem'])
    g['w_mem_q'] = _matmul(s['hn2'], dmq, ta=True, out_dtype=BF16, tm=1024, tn=512, tk=1024, name="mem_q_dw")
    dh, g['g_mem_q'] = _matmul_norm_bwd(dmq, w['w_mem_q'], s['h1'], p['g_mem_q'], dh, name="mem_q_dx_norm_bwd")

    g['w_out'] = _matmul(s['cat'], dh, ta=True, out_dtype=BF16, tm=1024, tn=512, tk=1024, name="out_proj_dw")
    after = at_mixer(g) if at_mixer is not None else None
    dcat = _matmul(dh, w['w_out'], tb=True, after=after, tn=1024, name="out_proj_dx")
    dpin, g['w_pool'], dscale = _pool_bwd(dcat, s['mixed'], p['w_pool'].astype(BF16), p['pool_scale'])
    g['pool_scale'] = dscale.reshape(POOL_WIDTH)
    do, delta, dot = _fox_delta(dcat, s['fox'])
    delta_row = _pairs_rows(jnp.transpose(delta[:, :FOX_HEADS]))
    dq, dk, dv, dc_keys, dc_queries = _fox_bwd(s['qn'], s['qnt'], s['kn'], s['knt'], s['vb'], do, dot, s['lse_row'],
                                               delta_row, s['crow'], s['ccol'])
    dft, db = _forget_cumsum_bwd(dc_keys, dc_queries, s['ft'], p['b_forget'])
    g['b_forget'] = db.reshape(FOX_HEADS)
    dfpad = jnp.pad(jnp.transpose(dft), ((0, 0), (0, LANES - FOX_HEADS)))
    dz, g['g_q_fox'], g['g_k_fox'] = _mixer_dz(s['z'], dq, dk, dv, dpin, dfpad, p['g_q_fox'], p['g_k_fox'])
    g['w_in'] = _matmul(s['xn1'], dz, ta=True, out_dtype=BF16, tm=512, tn=IN_COLS_PAD, tk=1024, name="in_proj_dw")
    dh, g['g_mix'] = _matmul_norm_bwd(dz, w['w_in'], s['h0'], p['g_mix'], dh, name="in_proj_dx_norm_bwd")
    return dh, g


def _local_step(x2, mem2, target2, small, w_in, other_weights, send_grads):
    h = x2
    saved, full = [], []
    for l in range(DEPTH):
        h, s, w = _layer_fwd(h, mem2, {k: v[l] for k, v in small.items()}, w_in[l], other_weights(l))
        saved.append(s)
        full.append(w)
    loss, dh = _loss_head(h, target2)
    after = None
    grads = [None] * DEPTH
    for l in reversed(range(1, DEPTH)):
        dh, grads[l] = _layer_bwd(dh, mem2, {k: v[l] for k, v in small.items()}, full[l], saved[l], after=after)
        after = send_grads(l, BIG, grads[l])
    dh, grads[0] = _layer_bwd(dh, mem2, {k: v[0] for k, v in small.items()}, full[0], saved[0], after=after,
                              at_mixer=lambda g: send_grads(0, SENT_AT_MIXER, g))
    return loss, dh, grads


def kernel(x, mem, g_mix, w_in, b_forget, g_q_fox, g_k_fox, w_pool, pool_scale, w_out, g_mem_q, g_mem_kv, w_mem_q, w_mem_kv, g_q_mem, g_k_mem, w_mem_out, g_ffn, w_gate_up, w_down, loss_target, m_g_mix, m_w_in, m_b_forget, m_g_q_fox, m_g_k_fox, m_w_pool, m_pool_scale, m_w_out, m_g_mem_q, m_g_mem_kv, m_w_mem_q, m_w_mem_kv, m_g_q_mem, m_g_k_mem, m_w_mem_out, m_g_ffn, m_w_gate_up, m_w_down, v_g_mix, v_w_in, v_b_forget, v_g_q_fox, v_g_k_fox, v_w_pool, v_pool_scale, v_w_out, v_g_mem_q, v_g_mem_kv, v_w_mem_q, v_w_mem_kv, v_g_q_mem, v_g_k_mem, v_w_mem_out, v_g_ffn, v_w_gate_up, v_w_down):
    weights = dict(g_mix=g_mix, w_in=w_in, b_forget=b_forget, g_q_fox=g_q_fox, g_k_fox=g_k_fox, w_pool=w_pool,
                   pool_scale=pool_scale, w_out=w_out, g_mem_q=g_mem_q, g_mem_kv=g_mem_kv, w_mem_q=w_mem_q,
                   w_mem_kv=w_mem_kv, g_q_mem=g_q_mem, g_k_mem=g_k_mem, w_mem_out=w_mem_out, g_ffn=g_ffn,
                   w_gate_up=w_gate_up, w_down=w_down)
    mom_m = dict(g_mix=m_g_mix, w_in=m_w_in, b_forget=m_b_forget, g_q_fox=m_g_q_fox, g_k_fox=m_g_k_fox, w_pool=m_w_pool,
                 pool_scale=m_pool_scale, w_out=m_w_out, g_mem_q=m_g_mem_q, g_mem_kv=m_g_mem_kv, w_mem_q=m_w_mem_q,
                 w_mem_kv=m_w_mem_kv, g_q_mem=m_g_q_mem, g_k_mem=m_g_k_mem, w_mem_out=m_w_mem_out, g_ffn=m_g_ffn,
                 w_gate_up=m_w_gate_up, w_down=m_w_down)
    mom_v = dict(g_mix=v_g_mix, w_in=v_w_in, b_forget=v_b_forget, g_q_fox=v_g_q_fox, g_k_fox=v_g_k_fox, w_pool=v_w_pool,
                 pool_scale=v_pool_scale, w_out=v_w_out, g_mem_q=v_g_mem_q, g_mem_kv=v_g_mem_kv, w_mem_q=v_w_mem_q,
                 w_mem_kv=v_w_mem_kv, g_q_mem=v_g_q_mem, g_k_mem=v_g_k_mem, w_mem_out=v_w_mem_out, g_ffn=v_g_ffn,
                 w_gate_up=v_w_gate_up, w_down=v_w_down)

    (w_in_all,) = _all_gather([weights['w_in'].astype(BF16)], name="w_in_all_gather")
    later = [(n, l) for l in range(DEPTH) for n in BIG if n != 'w_in']
    shards = [weights[n][l].astype(BF16) for n, l in later]
    weights_sent = _send_start(shards, _landing_zones(shards, False), blocks=False, name="weights_send_start",
                               after=w_in_all)
    small = {n: weights[n] for n in SMALL}
    small['g_mix'] = small['g_mix'] + weights_sent['token'][0, 0]
    gathered = {}

    def other_weights(l):
        def get(after):
            if not gathered:
                gathered.update(zip(later, _send_wait(weights_sent, after, name="weights_send_wait")))
            return {n: _full_weight(n, gathered[n, l]) for n in BIG if n != 'w_in'}
        return get

    grads_sent = []

    def send_grads(l, names, g):
        blocks = [_grad_blocks(n, g[n]) for n in names]
        sent = _send_start(blocks, _landing_zones(blocks, True), blocks=True,
                           name=f"grads{l}_send_start" if len(names) > 1 else "grads0_last_send_start")
        grads_sent.append((l, names, sent))
        return sent['token']

    w_in_full = [_full_weight('w_in', w_in_all[:, l]) for l in range(DEPTH)]
    loss_part, grad_x, grads = _local_step(x[0], mem[0], loss_target[0], small, w_in_full, other_weights, send_grads)
    loss = lax.psum(loss_part[0, 0], ("x", "y", "c"))

    last = [n for n in BIG if n not in SENT_AT_MIXER]
    last_token = send_grads(0, last, grads[0])
    landed = {}
    for l, names, sent in grads_sent[:-1]:
        landed.update({(l, n): a for n, a in zip(names, _send_wait(sent, grad_x, name=f"grads{l}_send_wait"))})

    def summed(n):
        return jnp.concatenate([_sum_slots(landed[l, n], name="grad_sum_" + n, after=last_token)
                                for l in range(DEPTH)], 0)

    grad, delta, new_m, new_v = {}, {}, {}, {}

    def update(n):
        shape = weights[n].shape
        two_d = lambda a: a.reshape(shape[0] * shape[1], shape[2])
        d, nm, nv = _adamw(two_d(weights[n]), grad[n], two_d(mom_m[n]), two_d(mom_v[n]), name="adamw_" + n)
        grad[n], delta[n], new_m[n], new_v[n] = (a.reshape(shape) for a in (grad[n], d, nm, nv))

    for n in SENT_AT_MIXER:
        grad[n] = summed(n)
        update(n)
    small_part = _pack_small({n: jnp.stack([grads[l][n] for l in range(DEPTH)], 0) for n in SMALL})
    (small_all,) = _all_gather([small_part], name="small_grads_all_gather")
    small_sum = _sum_slots(small_all, name="grad_sum_small")
    d, nm, nv = _adamw(_pack_small(weights), small_sum, _pack_small(mom_m), _pack_small(mom_v), name="adamw_small")
    grad.update(_unpack_small(small_sum))
    delta.update(_unpack_small(d))
    new_m.update(_unpack_small(nm))
    new_v.update(_unpack_small(nv))
    landed.update({(0, n): a for n, a in zip(last, _send_wait(grads_sent[-1][2], d, name="grads0_last_send_wait"))})
    for n in last:
        grad[n] = summed(n)
        update(n)

    return (loss, grad_x[None], *[grad[n] for n in WEIGHTS], *[delta[n] for n in WEIGHTS],
            *[new_m[n] for n in WEIGHTS], *[new_v[n] for n in WEIGHTS])
```

```python
import functools

import numpy as np
import jax
import jax.numpy as jnp
from jax import lax
from jax.experimental import pallas as pl
from jax.experimental.pallas import tpu as pltpu

F32 = jnp.float32
BF16 = jnp.bfloat16

N_DEV = 8
D_MODEL = 1024
DEPTH = 2
FOX_HEADS = 8
FOX_HEAD_DIM = 64
FOX_WIDTH = 512
POOL_WIDTH = 512
POOL_WINDOWS = (2, 4, 8, 16)
POOL_GROUP_DIM = 128
POOL_HALO = 16
IN_COLS = 2056
IN_COLS_PAD = 2176
MEM_HEADS = 4
MEM_HEAD_DIM = 128
MEM_WIDTH = 512
D_FF = 2816
EPS = 1e-6
FOX_SCALE = FOX_HEAD_DIM ** -0.5
LOG2E = 1.4426950408889634
LN2 = 0.6931471805599453
FOX_ACC_ROWS = FOX_HEAD_DIM + 16
MEM_SCALE = MEM_HEAD_DIM ** -0.5
LANES = 128

ADAM_LR = 0.001
ADAM_B1 = 0.9
ADAM_B2 = 0.999
ADAM_EPS = 1e-08
ADAM_WD = 0.01
ADAM_STEP = 10

VMEM_LIMIT = 56 * 1024 * 1024
MESH_ID = pl.DeviceIdType.MESH

WEIGHTS = ['g_mix', 'w_in', 'b_forget', 'g_q_fox', 'g_k_fox', 'w_pool', 'pool_scale', 'w_out', 'g_mem_q', 'g_mem_kv',
           'w_mem_q', 'w_mem_kv', 'g_q_mem', 'g_k_mem', 'w_mem_out', 'g_ffn', 'w_gate_up', 'w_down']
BIG = ['w_in', 'w_out', 'w_mem_q', 'w_mem_kv', 'w_mem_out', 'w_gate_up', 'w_down']
SMALL = [n for n in WEIGHTS if n not in BIG]
SENT_AT_MIXER = ['w_down', 'w_gate_up', 'w_mem_out', 'w_mem_q', 'w_mem_kv', 'w_out']


def _pcall(body, **kw):
    return pl.pallas_call(body, **kw)


def _params(*sem):
    return pltpu.CompilerParams(dimension_semantics=sem or None, vmem_limit_bytes=VMEM_LIMIT)


def _dot(a, b, dims=None):
    if dims is None:
        return jnp.dot(a, b, preferred_element_type=F32)
    return lax.dot_general(a, b, (dims, ((), ())), preferred_element_type=F32)


NT = ((1,), (1,))
TN = ((0,), (0,))


def _dot_exact(x, ones_bf16, terms=3):
    hi = x.astype(BF16)
    r1 = x - hi.astype(F32)
    mid = r1.astype(BF16)
    if terms == 2:
        return _dot(hi, ones_bf16) + _dot(mid, ones_bf16)
    lo = (r1 - mid.astype(F32)).astype(BF16)
    return _dot(hi, ones_bf16) + _dot(mid, ones_bf16) + _dot(lo, ones_bf16)


def _matmul(a, b, *, ta=False, tb=False, out_dtype=F32, res=None, after=None, tm=1024, tn=512, tk=None, name):
    planes = b.shape[0] if b.ndim == 3 else None
    bshape = b.shape[-2:]
    m, k = (a.shape[1], a.shape[0]) if ta else a.shape
    n = bshape[0] if tb else bshape[1]
    assert k == (bshape[1] if tb else bshape[0])
    tm, tn = min(tm, m), min(tn, n)
    tk = min(tk or k, k)
    assert m % tm == 0 and n % tn == 0 and k % tk == 0, (name, m, n, k, tm, tn, tk)
    nk = k // tk
    dims = ((0 if ta else 1,), (1 if tb else 0,))

    def body(*refs):
        a_ref, b_ref = refs[0], refs[1]
        r_ref = refs[2] if res is not None else None
        o_ref = refs[2 + (res is not None) + (after is not None)]
        part = _dot(a_ref[...].astype(BF16), b_ref[...].astype(BF16), dims)

        def finish(acc):
            if r_ref is not None:
                acc = acc + r_ref[...]
            o_ref[...] = acc.astype(o_ref.dtype)

        if nk == 1:
            finish(part)
        else:
            acc_ref = refs[-1]
            kk = pl.program_id(3)

            @pl.when(kk == 0)
            def _():
                acc_ref[...] = part

            @pl.when(kk > 0)
            def _():
                acc_ref[...] += part

            @pl.when(kk == nk - 1)
            def _():
                finish(acc_ref[...])

    lead = (lambda p: (p,)) if planes else (lambda p: ())
    sq = (None,) if planes else ()
    a_spec = pl.BlockSpec((tk, tm), lambda p, i, j, kk: (kk, i)) if ta else pl.BlockSpec((tm, tk), lambda p, i, j, kk: (i, kk))
    b_spec = (pl.BlockSpec(sq + (tn, tk), lambda p, i, j, kk: lead(p) + (j, kk)) if tb
              else pl.BlockSpec(sq + (tk, tn), lambda p, i, j, kk: lead(p) + (kk, j)))
    o_spec = pl.BlockSpec(sq + (tm, tn), lambda p, i, j, kk: lead(p) + (i, j))
    in_specs = ([a_spec, b_spec] + ([o_spec] if res is not None else [])
                + ([pl.BlockSpec(memory_space=pl.ANY)] if after is not None else []))
    args = (a, b) + ((res,) if res is not None else ()) + ((after,) if after is not None else ())
    return _pcall(
        body, name=name, grid=(planes or 1, m // tm, n // tn, nk), in_specs=in_specs, out_specs=o_spec,
        out_shape=jax.ShapeDtypeStruct(((planes,) if planes else ()) + (m, n), out_dtype),
        scratch_shapes=[pltpu.VMEM((tm, tn), F32)] if nk > 1 else [],
        compiler_params=_params("parallel", "parallel", "parallel", "arbitrary"),
    )(*args)


def _rms(x):
    return lax.rsqrt(jnp.mean(x * x, axis=-1, keepdims=True) + EPS)


def _norm_matmul(h, g, w, *, tm=512, tn=512, tail=0, name):
    t, d = h.shape
    n = w.shape[1]
    tn = min(tn, n)
    assert t % tm == 0 and n % tn == 0 and (not tail or tn == n)

    def body(h_ref, g_ref, w_ref, xn_ref, y_ref, *tail_ref):
        @pl.when(pl.program_id(1) == 0)
        def _():
            x = h_ref[...]
            xn_ref[...] = (x * _rms(x) * g_ref[...]).astype(BF16)

        y = _dot(xn_ref[...], w_ref[...])
        if tail:
            y_ref[...] = y[:, :n - tail]
            tail_ref[0][...] = y[:, n - tail:]
        else:
            y_ref[...] = y

    row = pl.BlockSpec((tm, d), lambda i, j: (i, 0))
    tails = ([pl.BlockSpec((tm, tail), lambda i, j: (i, 0))], [jax.ShapeDtypeStruct((t, tail), F32)]) if tail else ([], [])
    return _pcall(
        body, name=name, grid=(t // tm, n // tn),
        in_specs=[row, pl.BlockSpec((1, d), lambda i, j: (0, 0)), pl.BlockSpec((d, tn), lambda i, j: (0, j))],
        out_specs=[row, pl.BlockSpec((tm, tn - tail), lambda i, j: (i, j))] + tails[0],
        out_shape=[jax.ShapeDtypeStruct((t, d), BF16), jax.ShapeDtypeStruct((t, n - tail), F32)] + tails[1],
        compiler_params=_params("parallel", "arbitrary"),
    )(h, g.reshape(1, d), w)


def _matmul_norm_bwd(a, w, h, g, dres, *, tm=512, tk=None, name):
    stacked = a.ndim == 3
    t = a.shape[-2]
    d, k = w.shape
    tk = a.shape[-1] if stacked else min(tk or k, k)
    nk = k // tk
    assert t % tm == 0 and k % tk == 0 and (not stacked or a.shape[0] == nk)

    def body(a_ref, w_ref, h_ref, g_ref, r_ref, dx_ref, dg_ref, *acc):
        i, kk = pl.program_id(0), pl.program_id(1)
        part = _dot(a_ref[...], w_ref[...], NT)

        def finish(dy):
            x = h_ref[...]
            r = _rms(x)
            xhat = x * r
            gy = dy * g_ref[...]
            dx_ref[...] = r_ref[...] + r * (gy - xhat * jnp.mean(gy * xhat, axis=-1, keepdims=True))
            dg_part = jnp.sum(dy * xhat, axis=0, keepdims=True)

            @pl.when(i == 0)
            def _():
                dg_ref[...] = dg_part

            @pl.when(i > 0)
            def _():
                dg_ref[...] += dg_part

        if nk == 1:
            finish(part)
        else:
            acc_ref = acc[0]

            @pl.when(kk == 0)
            def _():
                acc_ref[...] = part

            @pl.when(kk > 0)
            def _():
                acc_ref[...] += part

            @pl.when(kk == nk - 1)
            def _():
                finish(acc_ref[...])

    a_spec = (pl.BlockSpec((None, tm, tk), lambda i, kk: (kk, i, 0)) if stacked
              else pl.BlockSpec((tm, tk), lambda i, kk: (i, kk)))
    row = pl.BlockSpec((tm, d), lambda i, kk: (i, 0))
    vec = pl.BlockSpec((1, d), lambda i, kk: (0, 0))
    dx, dg = _pcall(
        body, name=name, grid=(t // tm, nk),
        in_specs=[a_spec, pl.BlockSpec((d, tk), lambda i, kk: (0, kk)), row, vec, row], out_specs=[row, vec],
        out_shape=[jax.ShapeDtypeStruct((t, d), F32), jax.ShapeDtypeStruct((1, d), F32)],
        scratch_shapes=[pltpu.VMEM((tm, d), F32)] if nk > 1 else [],
        compiler_params=_params("arbitrary", "arbitrary"),
    )(a, w, h, g.reshape(1, d), dres)
    return dx, dg.reshape(d)


def _norm_gate_up_swiglu(h, g, w, *, tm=512, tn=1408):
    t, d = h.shape
    nj = D_FF // tn
    assert t % tm == 0 and D_FF % tn == 0

    def body(h_ref, g_ref, wg_ref, wu_ref, hn_ref, gu_ref, act_ref):
        @pl.when(pl.program_id(1) == 0)
        def _():
            x = h_ref[...]
            hn_ref[...] = (x * _rms(x) * g_ref[...]).astype(BF16)

        hn = hn_ref[...]
        gate = _dot(hn, wg_ref[...])
        up = _dot(hn, wu_ref[...])
        gu_ref[0] = gate.astype(BF16)
        gu_ref[1] = up.astype(BF16)
        act_ref[...] = (gate * jax.nn.sigmoid(gate) * up).astype(BF16)

    row = pl.BlockSpec((tm, d), lambda i, j: (i, 0))
    return _pcall(
        body, name="gate_up_swiglu_fwd", grid=(t // tm, nj),
        in_specs=[row, pl.BlockSpec((1, d), lambda i, j: (0, 0)), pl.BlockSpec((d, tn), lambda i, j: (0, j)),
                  pl.BlockSpec((d, tn), lambda i, j: (0, nj + j))],
        out_specs=[row, pl.BlockSpec((2, tm, tn), lambda i, j: (0, i, j)), pl.BlockSpec((tm, tn), lambda i, j: (i, j))],
        out_shape=[jax.ShapeDtypeStruct((t, d), BF16), jax.ShapeDtypeStruct((2, t, D_FF), BF16),
                   jax.ShapeDtypeStruct((t, D_FF), BF16)],
        compiler_params=_params("parallel", "arbitrary"),
    )(h, g.reshape(1, d), w, w)


def _down_dx_swiglu_bwd(dh, w_down, gu, *, after=None, tm=512, tn=1408):
    t, d = dh.shape
    assert t % tm == 0 and D_FF % tn == 0

    def body(dh_ref, w_ref, gu_ref, *rest):
        dgu_ref = rest[-1]
        da = _dot(dh_ref[...].astype(BF16), w_ref[...], NT)
        gate, up = gu_ref[0].astype(F32), gu_ref[1].astype(F32)
        sg = jax.nn.sigmoid(gate)
        silu = gate * sg
        dgu_ref[0] = (da * up * (sg + silu * (1.0 - sg))).astype(BF16)
        dgu_ref[1] = (da * silu).astype(BF16)

    stack = pl.BlockSpec((2, tm, tn), lambda i, j: (0, i, j))
    return _pcall(
        body, name="down_dx_swiglu_bwd", grid=(t // tm, D_FF // tn),
        in_specs=[pl.BlockSpec((tm, d), lambda i, j: (i, 0)), pl.BlockSpec((tn, d), lambda i, j: (j, 0)), stack]
        + ([pl.BlockSpec(memory_space=pl.ANY)] if after is not None else []),
        out_specs=stack, out_shape=jax.ShapeDtypeStruct((2, t, D_FF), BF16),
        compiler_params=_params("parallel", "parallel"),
    )(dh, w_down, gu, *((after,) if after is not None else ()))


def _group_matrix(width, group):
    r = lax.broadcasted_iota(jnp.int32, (width, width), 0) // group
    c = lax.broadcasted_iota(jnp.int32, (width, width), 1) // group
    return (r == c).astype(BF16)


def _qkv_prep(z, gq, gk, *, bt=512):
    t = z.shape[0]
    w = FOX_WIDTH

    def body(q_ref, k_ref, v_ref, gq_ref, gk_ref, qo_ref, ko_ref, vo_ref, qt_ref, kt_ref, vt_ref, q2t_ref):
        gm = _group_matrix(w, FOX_HEAD_DIM)
        for x_ref, g_ref, o_ref, ot_ref, scale in ((q_ref, gq_ref, qo_ref, qt_ref, FOX_SCALE),
                                                   (k_ref, gk_ref, ko_ref, kt_ref, 1.0)):
            x = x_ref[...]
            ms = _dot_exact(x * x, gm, terms=2) * (1.0 / FOX_HEAD_DIM)
            y = x * lax.rsqrt(ms + EPS) * g_ref[...]
            o_ref[...] = (y * scale).astype(BF16)
            yt = jnp.transpose(y)
            ot_ref[...] = (yt * scale).astype(BF16)
            if o_ref is qo_ref:
                q2t_ref[...] = (yt * (scale * LOG2E)).astype(BF16)
        v = v_ref[...]
        vo_ref[...] = v.astype(BF16)
        vt_ref[...] = jnp.transpose(v).astype(BF16)

    col = lambda c: pl.BlockSpec((bt, w), lambda i, c=c: (i, c))
    vec = pl.BlockSpec((1, w), lambda i: (0, 0))
    out = pl.BlockSpec((bt, w), lambda i: (i, 0))
    out_t = pl.BlockSpec((w, bt), lambda i: (0, i))
    return _pcall(
        body, name="fox_qkv_prep", grid=(t // bt,), in_specs=[col(0), col(1), col(2), vec, vec],
        out_specs=[out] * 3 + [out_t] * 4,
        out_shape=[jax.ShapeDtypeStruct((t, w), BF16)] * 3 + [jax.ShapeDtypeStruct((w, t), BF16)] * 4,
        compiler_params=_params("parallel"),
    )(z, z, z, jnp.tile(gq, FOX_HEADS).reshape(1, w), jnp.tile(gk, FOX_HEADS).reshape(1, w))


def _log_sigmoid(f):
    return jnp.minimum(f, 0.0) - jnp.log1p(jnp.exp(-jnp.abs(f)))


def _forget_cumsum(ft, b, *, chunk=512):
    hh, t = ft.shape

    def body(f_ref, b_ref, c_ref):
        r = lax.broadcasted_iota(jnp.int32, (chunk, chunk), 0)
        c = lax.broadcasted_iota(jnp.int32, (chunk, chunk), 1)
        upper = (r <= c).astype(BF16)
        carry = jnp.zeros((hh, 1), F32)
        for ch in range(t // chunk):
            sl = slice(ch * chunk, (ch + 1) * chunk)
            cs = _dot_exact(_log_sigmoid(f_ref[:, sl] + b_ref[...]), upper) + carry
            c_ref[:, sl] = cs
            carry = cs[:, chunk - 1:chunk]

    return _pcall(body, name="fox_forget_cumsum", out_shape=jax.ShapeDtypeStruct((hh, t), F32),
                  compiler_params=_params())(ft, b.reshape(hh, 1))


def _forget_cumsum_bwd(dc_keys, dc_queries, ft, b, *, chunk=512):
    hh, t = ft.shape

    def body(dck_ref, dcq_ref, f_ref, b_ref, df_ref, db_ref):
        r = lax.broadcasted_iota(jnp.int32, (chunk, chunk), 0)
        c = lax.broadcasted_iota(jnp.int32, (chunk, chunk), 1)
        lower = (r >= c).astype(BF16)
        carry = jnp.zeros((hh, 1), F32)
        db = jnp.zeros((hh, 1), F32)
        for ch in reversed(range(t // chunk)):
            sl = slice(ch * chunk, (ch + 1) * chunk)
            dls = _dot_exact(dck_ref[:, sl] + dcq_ref[:, sl], lower) + carry
            carry = dls[:, 0:1]
            df = dls * jax.nn.sigmoid(-(f_ref[:, sl] + b_ref[...]))
            df_ref[:, sl] = df
            db = db + jnp.sum(df, axis=1, keepdims=True)
        db_ref[...] = db

    return _pcall(body, name="fox_forget_cumsum_bwd",
                  out_shape=[jax.ShapeDtypeStruct((hh, t), F32), jax.ShapeDtypeStruct((hh, 1), F32)],
                  compiler_params=_params())(dc_keys, dc_queries, ft, b.reshape(hh, 1))


def _lane_is_first_head():
    return lax.broadcasted_iota(jnp.int32, (1, LANES), 1) < FOX_HEAD_DIM


def _fox_fwd(q2t, kn, vt, crow2, ccol2, *, bq=512, bk=1024):
    t = kn.shape[0]
    nq = t // bq
    pairs = FOX_WIDTH // LANES
    assert t % bk == 0
    tiles = [(i, j) for i in range(nq) for j in range(i * bq // bk, -1, -1)]
    it = jnp.asarray(np.array([a for a, _ in tiles], np.int32))
    jt = jnp.asarray(np.array([b for _, b in tiles], np.int32))

    def body(it_ref, jt_ref, qt_ref, k_ref, vt_ref, cr_ref, cc_ref, o_ref, lse_ref, m_sc, acc_sc):
        s_id = pl.program_id(1)
        i, j = it_ref[s_id], jt_ref[s_id]
        first = _lane_is_first_head()
        holds_diagonal = j == (i * bq) // bk

        @pl.when(holds_diagonal)
        def _():
            m_sc[...] = jnp.full(m_sc.shape, -jnp.inf, F32)
            acc_sc[...] = jnp.zeros(acc_sc.shape, F32)

        def scores():
            k2, qt2 = k_ref[...], qt_ref[...]
            return [_dot(jnp.where(first if hh == 0 else jnp.logical_not(first), k2, jnp.zeros_like(k2)), qt2)
                    for hh in range(2)]

        def pv(hh, pt_bf16):
            v_ones = jnp.concatenate([vt_ref[hh * FOX_HEAD_DIM:(hh + 1) * FOX_HEAD_DIM, :],
                                      jnp.ones((FOX_ACC_ROWS - FOX_HEAD_DIM, bk), BF16)], axis=0)
            return _dot(v_ones, pt_bf16)

        def tile(diagonal):
            sc = scores()
            for hh in range(2):
                ut = sc[hh] - cc_ref[0, :, hh:hh + 1]
                if diagonal:
                    key = j * bk + lax.broadcasted_iota(jnp.int32, ut.shape, 0)
                    query = i * bq + lax.broadcasted_iota(jnp.int32, ut.shape, 1)
                    ut = jnp.where(key <= query, ut, -jnp.inf)
                c_t = cr_ref[0, hh:hh + 1, :]
                m_prev = m_sc[hh]
                m_new = jnp.maximum(m_prev, jnp.max(ut, axis=0, keepdims=True) + c_t)
                acc_sc[hh] = jnp.exp2(m_prev - m_new) * acc_sc[hh] + pv(hh, jnp.exp2(ut + (c_t - m_new)).astype(BF16))
                m_sc[hh] = m_new

        @pl.when(holds_diagonal)
        def _():
            tile(True)

        @pl.when(jnp.logical_not(holds_diagonal))
        def _():
            tile(False)

        @pl.when(j == 0)
        def _():
            sums = [acc_sc[hh, FOX_HEAD_DIM:FOX_HEAD_DIM + 1, :] for hh in range(2)]
            ot = jnp.concatenate([acc_sc[hh, :FOX_HEAD_DIM, :] / sums[hh] for hh in range(2)], axis=0)
            o_ref[...] = jnp.transpose(ot).astype(o_ref.dtype)
            for hh in range(2):
                lse_ref[0, hh:hh + 1, :] = m_sc[hh] * LN2 + jnp.log(sums[hh])

    qspec = pl.BlockSpec((bq, LANES), lambda p, s, it, jt: (it[s], p))
    kspec = pl.BlockSpec((bk, LANES), lambda p, s, it, jt: (jt[s], p))
    vtspec = pl.BlockSpec((LANES, bk), lambda p, s, it, jt: (p, jt[s]))
    qtspec = pl.BlockSpec((LANES, bq), lambda p, s, it, jt: (p, it[s]))
    rowq = pl.BlockSpec((1, 2, bq), lambda p, s, it, jt: (p, 0, it[s]))
    colk = pl.BlockSpec((1, bk, 2), lambda p, s, it, jt: (p, jt[s], 0))
    return _pcall(
        body, name="fox_attention_fwd",
        grid_spec=pltpu.PrefetchScalarGridSpec(
            num_scalar_prefetch=2, grid=(pairs, len(tiles)),
            in_specs=[qtspec, kspec, vtspec, rowq, colk], out_specs=[qspec, rowq],
            scratch_shapes=[pltpu.VMEM((2, 1, bq), F32), pltpu.VMEM((2, FOX_ACC_ROWS, bq), F32)]),
        out_shape=[jax.ShapeDtypeStruct((t, FOX_WIDTH), BF16), jax.ShapeDtypeStruct((pairs, 2, t), F32)],
        compiler_params=_params("parallel", "arbitrary"),
    )(it, jt, q2t, kn, vt, crow2, ccol2)


def _fox_bwd(qn, qnt, kn, knt, vb, do, dot, lse_row, delta_row, crow, ccol, *, bq=512, bk=512):
    t = qn.shape[0]
    nq, nk = t // bq, t // bk
    pairs = FOX_WIDTH // LANES
    tiles = [(i, j) for j in range(nk) for i in range(j * bk // bq, nq)]
    it = jnp.asarray(np.array([a for a, _ in tiles], np.int32))
    jt = jnp.asarray(np.array([b for _, b in tiles], np.int32))

    def body(it_ref, jt_ref, q_ref, qt_ref, k_ref, kt_ref, v_ref, do_ref, dot_ref, lse_ref, dl_ref, cr_ref, cc_ref,
             dqt_ref, dk_ref, dv_ref, dc_ref, dr_ref, dk_sc, dv_sc, dc_sc):
        s_id = pl.program_id(1)
        i, j = it_ref[s_id], jt_ref[s_id]
        first = _lane_is_first_head()
        holds_diagonal = i == (j * bk) // bq

        @pl.when(s_id == 0)
        def _():
            dqt_ref[...] = jnp.zeros(dqt_ref.shape, F32)
            dr_ref[...] = jnp.zeros(dr_ref.shape, F32)

        @pl.when(holds_diagonal)
        def _():
            dk_sc[...] = jnp.zeros(dk_sc.shape, F32)
            dv_sc[...] = jnp.zeros(dv_sc.shape, F32)
            dc_sc[...] = jnp.zeros(dc_sc.shape, F32)

        def tile(diagonal):
            q2, qt2, k2, kt2, v2, do2, dot2 = (q_ref[...], qt_ref[...], k_ref[...], kt_ref[...], v_ref[...], do_ref[...],
                                               dot_ref[...])
            dk_t, dv_t, dqt_t = [], [], []
            for hh in range(2):
                mine = first if hh == 0 else jnp.logical_not(first)
                st = (_dot(jnp.where(mine, k2, jnp.zeros_like(k2)), qt2)
                      + (cr_ref[0, hh:hh + 1, :] - cc_ref[0, :, hh:hh + 1]))
                if diagonal:
                    key = j * bk + lax.broadcasted_iota(jnp.int32, st.shape, 0)
                    query = i * bq + lax.broadcasted_iota(jnp.int32, st.shape, 1)
                    st = jnp.where(key <= query, st, -jnp.inf)
                pt = jnp.exp(st - lse_ref[0, hh:hh + 1, :])
                dv_t.append(_dot(pt.astype(BF16), do2))
                dpt = _dot(jnp.where(mine, v2, jnp.zeros_like(v2)), dot2)
                dst = pt * (dpt - dl_ref[0, hh:hh + 1, :])
                dc_sc[hh] += jnp.sum(dst, axis=1, keepdims=True)
                dr_ref[0, i, hh:hh + 1, :] += jnp.sum(dst, axis=0, keepdims=True)
                dsb = dst.astype(BF16)
                dk_t.append(_dot(dsb, q2))
                dqt_t.append(_dot(kt2[hh * FOX_HEAD_DIM:(hh + 1) * FOX_HEAD_DIM], dsb))
            dk_sc[...] += jnp.where(first, dk_t[0], dk_t[1])
            dv_sc[...] += jnp.where(first, dv_t[0], dv_t[1])
            dqt_ref[0, i] += jnp.concatenate(dqt_t, axis=0)

        @pl.when(jnp.logical_not(holds_diagonal))
        def _():
            tile(False)

        @pl.when(holds_diagonal)
        def _():
            tile(True)

        @pl.when(i == nq - 1)
        def _():
            dk_ref[...] = dk_sc[...]
            dv_ref[...] = dv_sc[...]
            for hh in range(2):
                dc_ref[0, :, hh:hh + 1] = -dc_sc[hh]

    qspec = pl.BlockSpec((bq, LANES), lambda p, s, it, jt: (it[s], p))
    qtspec = pl.BlockSpec((LANES, bq), lambda p, s, it, jt: (p, it[s]))
    kspec = pl.BlockSpec((bk, LANES), lambda p, s, it, jt: (jt[s], p))
    ktspec = pl.BlockSpec((LANES, bk), lambda p, s, it, jt: (p, jt[s]))
    rowq = pl.BlockSpec((1, 2, bq), lambda p, s, it, jt: (p, 0, it[s]))
    colk = pl.BlockSpec((1, bk, 2), lambda p, s, it, jt: (p, jt[s], 0))
    dqt_spec = pl.BlockSpec((1, nq, LANES, bq), lambda p, s, it, jt: (p, 0, 0, 0))
    dr_spec = pl.BlockSpec((1, nq, 2, bq), lambda p, s, it, jt: (p, 0, 0, 0))
    dqt, dk, dv, dc_keys, dc_queries = _pcall(
        body, name="fox_attention_bwd",
        grid_spec=pltpu.PrefetchScalarGridSpec(
            num_scalar_prefetch=2, grid=(pairs, len(tiles)),
            in_specs=[qspec, qtspec, kspec, ktspec, kspec, qspec, qtspec, rowq, rowq, rowq, colk],
            out_specs=[dqt_spec, kspec, kspec, colk, dr_spec],
            scratch_shapes=[pltpu.VMEM((bk, LANES), F32), pltpu.VMEM((bk, LANES), F32), pltpu.VMEM((2, bk, 1), F32)]),
        out_shape=[jax.ShapeDtypeStruct((pairs, nq, LANES, bq), F32), jax.ShapeDtypeStruct((t, FOX_WIDTH), F32),
                   jax.ShapeDtypeStruct((t, FOX_WIDTH), F32), jax.ShapeDtypeStruct((pairs, t, 2), F32),
                   jax.ShapeDtypeStruct((pairs, nq, 2, bq), F32)],
        compiler_params=_params("parallel", "arbitrary"),
    )(it, jt, qn, qnt, kn, knt, vb, do, dot, lse_row, delta_row, crow, ccol)
    dq = jnp.transpose(dqt, (1, 3, 0, 2)).reshape(t, FOX_WIDTH)
    dc_keys = jnp.transpose(dc_keys, (0, 2, 1)).reshape(FOX_HEADS, t)
    dc_queries = jnp.transpose(dc_queries, (0, 2, 1, 3)).reshape(FOX_HEADS, t)
    return dq, dk, dv, dc_keys, dc_queries


def _fox_delta(dcat, fox, *, bt=512):
    t = fox.shape[0]
    w = FOX_WIDTH

    def body(do_ref, o_ref, dob_ref, dl_ref, dot_ref):
        do = do_ref[...]
        dot_ref[...] = jnp.transpose(do).astype(BF16)
        dob = do.astype(BF16)
        r = lax.broadcasted_iota(jnp.int32, (w, LANES), 0) // FOX_HEAD_DIM
        c = lax.broadcasted_iota(jnp.int32, (w, LANES), 1)
        dl_ref[...] = _dot_exact(dob.astype(F32) * o_ref[...].astype(F32), (r == c).astype(BF16))
        dob_ref[...] = dob

    blk = pl.BlockSpec((bt, w), lambda i: (i, 0))
    return _pcall(
        body, name="fox_delta", grid=(t // bt,), in_specs=[blk, blk],
        out_specs=[blk, pl.BlockSpec((bt, LANES), lambda i: (i, 0)), pl.BlockSpec((w, bt), lambda i: (0, i))],
        out_shape=[jax.ShapeDtypeStruct((t, w), BF16), jax.ShapeDtypeStruct((t, LANES), F32),
                   jax.ShapeDtypeStruct((w, t), BF16)],
        compiler_params=_params("parallel"),
    )(dcat, fox)


def _mixer_dz(z, dq, dk, dv, dpin, dfpad, gq, gk, *, bt=256):
    t = z.shape[0]
    w = FOX_WIDTH

    def body(q_ref, k_ref, dq_ref, dk_ref, dv_ref, dp_ref, df_ref, gq_ref, gk_ref, dz_ref, dgq_ref, dgk_ref):
        gm = _group_matrix(w, FOX_HEAD_DIM)
        first_step = pl.program_id(0) == 0
        for n, (x_ref, dy_ref, g_ref, dg_ref, scale) in enumerate(
                ((q_ref, dq_ref, gq_ref, dgq_ref, FOX_SCALE), (k_ref, dk_ref, gk_ref, dgk_ref, 1.0))):
            x = x_ref[...]
            r = lax.rsqrt(_dot_exact(x * x, gm, terms=2) * (1.0 / FOX_HEAD_DIM) + EPS)
            xhat = x * r
            dy = dy_ref[...] * scale
            gy = dy * g_ref[...]
            dx = r * (gy - xhat * (_dot_exact(gy * xhat, gm, terms=2) * (1.0 / FOX_HEAD_DIM)))
            dz_ref[:, n * w:(n + 1) * w] = dx.astype(BF16)
            part = jnp.sum(dy * xhat, axis=0, keepdims=True)

            @pl.when(first_step)
            def _():
                dg_ref[...] = part

            @pl.when(jnp.logical_not(first_step))
            def _():
                dg_ref[...] += part

        dz_ref[:, 2 * w:3 * w] = dv_ref[...].astype(BF16)
        dz_ref[:, 3 * w:4 * w] = dp_ref[...].astype(BF16)
        dz_ref[:, 4 * w:] = df_ref[...].astype(BF16)

    col = lambda c: pl.BlockSpec((bt, w), lambda i, c=c: (i, c))
    blk = pl.BlockSpec((bt, w), lambda i: (i, 0))
    vec = pl.BlockSpec((1, w), lambda i: (0, 0))
    dz, dgq, dgk = _pcall(
        body, name="mixer_dz", grid=(t // bt,),
        in_specs=[col(0), col(1), blk, blk, blk, blk, pl.BlockSpec((bt, LANES), lambda i: (i, 0)), vec, vec],
        out_specs=[pl.BlockSpec((bt, IN_COLS_PAD), lambda i: (i, 0)), vec, vec],
        out_shape=[jax.ShapeDtypeStruct((t, IN_COLS_PAD), BF16), jax.ShapeDtypeStruct((1, w), F32),
                   jax.ShapeDtypeStruct((1, w), F32)],
        compiler_params=_params("arbitrary"),
    )(z, z, dq, dk, dv, dpin, dfpad, jnp.tile(gq, FOX_HEADS).reshape(1, w), jnp.tile(gk, FOX_HEADS).reshape(1, w))
    return dz, dgq.reshape(FOX_HEADS, FOX_HEAD_DIM).sum(0), dgk.reshape(FOX_HEADS, FOX_HEAD_DIM).sum(0)


def _pool_fwd(z, wp, scale, *, bt=512):
    t = z.shape[0]
    w = POOL_WIDTH
    hb = bt // POOL_HALO

    def body(p_ref, h_ref, wp_ref, sc_ref, y_ref, mx_ref):
        i = pl.program_id(0)
        cur = p_ref[...]
        halo = jnp.where(i > 0, h_ref[...], 0.0)
        ext = jnp.concatenate([halo, cur], axis=0)
        trow = i * bt + lax.broadcasted_iota(jnp.int32, (bt, 1), 0)
        for g, win in enumerate(POOL_WINDOWS):
            sl = slice(g * LANES, (g + 1) * LANES)
            e = ext[:, sl]
            acc = e[POOL_HALO:]
            for k in range(1, win):
                acc = acc + pltpu.roll(e, k, 0)[POOL_HALO:]
            cnt = jnp.minimum(trow + 1, win).astype(F32)
            mixed = (acc / cnt - cur[:, sl]).astype(BF16)
            mx_ref[:, sl] = mixed
            y_ref[:, sl] = (_dot(mixed, wp_ref[g]) * sc_ref[:, sl]).astype(BF16)

    blk = pl.BlockSpec((bt, w), lambda i: (i, 0))
    return _pcall(
        body, name="pool_fwd", grid=(t // bt,),
        in_specs=[pl.BlockSpec((bt, w), lambda i: (i, 3)),
                  pl.BlockSpec((POOL_HALO, w), lambda i: (jnp.maximum(i * hb - 1, 0), 3)),
                  pl.BlockSpec((len(POOL_WINDOWS), LANES, LANES), lambda i: (0, 0, 0)),
                  pl.BlockSpec((1, w), lambda i: (0, 0))],
        out_specs=[blk, blk], out_shape=[jax.ShapeDtypeStruct((t, w), BF16)] * 2,
        compiler_params=_params("parallel"),
    )(z, z, wp, scale.reshape(1, w))


def _pool_bwd(dcat, mixed, wp, scale, *, bt=512):
    t = mixed.shape[0]
    w = POOL_WIDTH
    hb = bt // POOL_HALO
    nb = t // bt
    n_ext = bt + POOL_HALO

    def body(d_ref, h_ref, mx_ref, wp_ref, sc_ref, dp_ref, dwp_ref, dsc_ref):
        i = pl.program_id(0)
        cur = d_ref[...]
        nxt = jnp.where(i < nb - 1, h_ref[...], 0.0)
        ext = jnp.concatenate([cur, nxt], axis=0)
        trow = i * bt + lax.broadcasted_iota(jnp.int32, (n_ext, 1), 0)

        @pl.when(i == 0)
        def _():
            dwp_ref[...] = jnp.zeros(dwp_ref.shape, F32)
            dsc_ref[...] = jnp.zeros(dsc_ref.shape, F32)

        for g, win in enumerate(POOL_WINDOWS):
            sl = slice(g * LANES, (g + 1) * LANES)
            dy = (ext[:, sl] * sc_ref[:, sl]).astype(BF16)
            dm = _dot(dy, wp_ref[g], NT)
            mixed_g = mx_ref[:, sl]
            dsc_ref[:, sl] += jnp.sum(cur[:, sl] * _dot(mixed_g, wp_ref[g]), axis=0, keepdims=True)
            dwp_ref[g] += _dot(mixed_g, dy[:bt], TN)
            r = dm / jnp.minimum(trow + 1, win).astype(F32)
            acc = r[:bt]
            for k in range(1, win):
                acc = acc + pltpu.roll(r, n_ext - k, 0)[:bt]
            dp_ref[:, sl] = acc - dm[:bt]

    return _pcall(
        body, name="pool_bwd", grid=(nb,),
        in_specs=[pl.BlockSpec((bt, w), lambda i: (i, 1)),
                  pl.BlockSpec((POOL_HALO, w), lambda i: (jnp.minimum((i + 1) * hb, t // POOL_HALO - 1), 1)),
                  pl.BlockSpec((bt, w), lambda i: (i, 0)),
                  pl.BlockSpec((len(POOL_WINDOWS), LANES, LANES), lambda i: (0, 0, 0)),
                  pl.BlockSpec((1, w), lambda i: (0, 0))],
        out_specs=[pl.BlockSpec((bt, w), lambda i: (i, 0)),
                   pl.BlockSpec((len(POOL_WINDOWS), LANES, LANES), lambda i: (0, 0, 0)),
                   pl.BlockSpec((1, w), lambda i: (0, 0))],
        out_shape=[jax.ShapeDtypeStruct((t, w), F32), jax.ShapeDtypeStruct((len(POOL_WINDOWS), LANES, LANES), F32),
                   jax.ShapeDtypeStruct((1, w), F32)],
        compiler_params=_params("arbitrary"),
    )(dcat, dcat, mixed, wp, scale.reshape(1, w))


def _head_rms(x):
    return lax.rsqrt(jnp.mean(x * x, axis=-1, keepdims=True) + EPS)


def _mem_kv_fwd(mem, g_kv, w_kv, g_k):
    mlen, d = mem.shape

    def body(m_ref, g_ref, w_ref, gk_ref, mn_ref, mkv_ref, mk_ref, mv_ref):
        x = m_ref[...]
        mn = (x * lax.rsqrt(jnp.mean(x * x, axis=-1, keepdims=True) + EPS) * g_ref[...]).astype(BF16)
        mn_ref[...] = mn
        mkv = _dot(mn, w_ref[...])
        mkv_ref[...] = mkv
        for h in range(MEM_HEADS):
            sl = slice(h * MEM_HEAD_DIM, (h + 1) * MEM_HEAD_DIM)
            kh = mkv[:, sl]
            mk_ref[:, sl] = (kh * _head_rms(kh) * gk_ref[...]).astype(BF16)
        mv_ref[...] = mkv[:, MEM_WIDTH:].astype(BF16)

    return _pcall(
        body, name="mem_kv_fwd",
        out_shape=[jax.ShapeDtypeStruct((mlen, d), BF16), jax.ShapeDtypeStruct((mlen, 2 * MEM_WIDTH), F32),
                   jax.ShapeDtypeStruct((mlen, MEM_WIDTH), BF16), jax.ShapeDtypeStruct((mlen, MEM_WIDTH), BF16)],
        compiler_params=_params(),
    )(mem, g_kv.reshape(1, d), w_kv, g_k.reshape(1, MEM_HEAD_DIM))


def _mem_kv_bwd(dmk, dmv, mkv, mn, mem, g_kv, w_kv, g_k):
    mlen, d = mem.shape

    def body(dmk_ref, dmv_ref, mkv_ref, mn_ref, m_ref, g_ref, w_ref, gk_ref, dw_ref, dg_ref, dgk_ref, dkv_sc):
        dgk = jnp.zeros((1, MEM_HEAD_DIM), F32)
        for h in range(MEM_HEADS):
            sl = slice(h * MEM_HEAD_DIM, (h + 1) * MEM_HEAD_DIM)
            x = mkv_ref[:, sl]
            r = _head_rms(x)
            xhat = x * r
            dy = dmk_ref[:, sl]
            gy = dy * gk_ref[...]
            dkv_sc[:, sl] = (r * (gy - xhat * jnp.mean(gy * xhat, axis=-1, keepdims=True))).astype(BF16)
            dgk = dgk + jnp.sum(dy * xhat, axis=0, keepdims=True)
        dgk_ref[...] = dgk
        dkv_sc[:, MEM_WIDTH:] = dmv_ref[...].astype(BF16)
        dkv = dkv_sc[...]
        dw_ref[...] = _dot(mn_ref[...], dkv, TN).astype(BF16)
        dmn = _dot(dkv, w_ref[...], NT)
        x = m_ref[...]
        xhat = x * lax.rsqrt(jnp.mean(x * x, axis=-1, keepdims=True) + EPS)
        dg_ref[...] = jnp.sum(dmn * xhat, axis=0, keepdims=True)

    dw, dg, dgk = _pcall(
        body, name="mem_kv_bwd",
        out_shape=[jax.ShapeDtypeStruct((d, 2 * MEM_WIDTH), BF16), jax.ShapeDtypeStruct((1, d), F32),
                   jax.ShapeDtypeStruct((1, MEM_HEAD_DIM), F32)],
        scratch_shapes=[pltpu.VMEM((mlen, 2 * MEM_WIDTH), BF16)],
        compiler_params=_params(),
    )(dmk, dmv, mkv, mn, mem, g_kv.reshape(1, d), w_kv, g_k.reshape(1, MEM_HEAD_DIM))
    return dw, dg.reshape(d), dgk.reshape(MEM_HEAD_DIM)


def _cross_probs(x, g, mk_h):
    r = _head_rms(x)
    xhat = x * r
    qn = (xhat * g).astype(BF16)
    s = _dot(qn, mk_h, NT) * MEM_SCALE
    e = jnp.exp(s - jnp.max(s, axis=-1, keepdims=True))
    return r, xhat, qn, e / jnp.sum(e, axis=-1, keepdims=True)


def _cross_fwd(mq_raw, g_q, mk, mv, *, bt=512):
    t = mq_raw.shape[0]
    mlen = mk.shape[0]

    def body(x_ref, g_ref, mk_ref, mv_ref, o_ref):
        for h in range(MEM_HEADS):
            sl = slice(h * MEM_HEAD_DIM, (h + 1) * MEM_HEAD_DIM)
            _, _, _, p = _cross_probs(x_ref[:, sl], g_ref[...], mk_ref[:, sl])
            o_ref[:, sl] = _dot(p.astype(BF16), mv_ref[:, sl]).astype(BF16)

    blk = pl.BlockSpec((bt, MEM_WIDTH), lambda i: (i, 0))
    kv = pl.BlockSpec((mlen, MEM_WIDTH), lambda i: (0, 0))
    return _pcall(
        body, name="cross_attention_fwd", grid=(t // bt,),
        in_specs=[blk, pl.BlockSpec((1, MEM_HEAD_DIM), lambda i: (0, 0)), kv, kv], out_specs=blk,
        out_shape=jax.ShapeDtypeStruct((t, MEM_WIDTH), BF16), compiler_params=_params("parallel"),
    )(mq_raw, g_q.reshape(1, MEM_HEAD_DIM), mk, mv)


def _cross_bwd(mq_raw, dmo, g_q, mk, mv, *, bt=512):
    t = mq_raw.shape[0]
    mlen = mk.shape[0]

    def body(x_ref, do_ref, g_ref, mk_ref, mv_ref, dx_ref, dmk_ref, dmv_ref, dg_ref):
        @pl.when(pl.program_id(0) == 0)
        def _():
            dmk_ref[...] = jnp.zeros(dmk_ref.shape, F32)
            dmv_ref[...] = jnp.zeros(dmv_ref.shape, F32)
            dg_ref[...] = jnp.zeros(dg_ref.shape, F32)

        for h in range(MEM_HEADS):
            sl = slice(h * MEM_HEAD_DIM, (h + 1) * MEM_HEAD_DIM)
            r, xhat, qn, p = _cross_probs(x_ref[:, sl], g_ref[...], mk_ref[:, sl])
            do = do_ref[:, sl]
            dp = _dot(do, mv_ref[:, sl], NT)
            ds = (p * (dp - jnp.sum(p * dp, axis=-1, keepdims=True)) * MEM_SCALE).astype(BF16)
            dmv_ref[:, sl] += _dot(p.astype(BF16), do, TN)
            dmk_ref[:, sl] += _dot(ds, qn, TN)
            dqn = _dot(ds, mk_ref[:, sl])
            gy = dqn * g_ref[...]
            dx_ref[:, sl] = (r * (gy - xhat * jnp.mean(gy * xhat, axis=-1, keepdims=True))).astype(BF16)
            dg_ref[...] += jnp.sum(dqn * xhat, axis=0, keepdims=True)

    blk = pl.BlockSpec((bt, MEM_WIDTH), lambda i: (i, 0))
    kv = pl.BlockSpec((mlen, MEM_WIDTH), lambda i: (0, 0))
    gs = pl.BlockSpec((1, MEM_HEAD_DIM), lambda i: (0, 0))
    dx, dmk, dmv, dg = _pcall(
        body, name="cross_attention_bwd", grid=(t // bt,),
        in_specs=[blk, blk, gs, kv, kv], out_specs=[blk, kv, kv, gs],
        out_shape=[jax.ShapeDtypeStruct((t, MEM_WIDTH), BF16), jax.ShapeDtypeStruct((mlen, MEM_WIDTH), F32),
                   jax.ShapeDtypeStruct((mlen, MEM_WIDTH), F32), jax.ShapeDtypeStruct((1, MEM_HEAD_DIM), F32)],
        compiler_params=_params("arbitrary"),
    )(mq_raw, dmo, g_q.reshape(1, MEM_HEAD_DIM), mk, mv)
    return dx, dmk, dmv, dg.reshape(MEM_HEAD_DIM)


def _loss_head(y, target, *, bt=512):
    t, d = y.shape

    def body(y_ref, t_ref, dy_ref, l_ref):
        e = y_ref[...] - t_ref[...]
        dy_ref[...] = e * (1.0 / d)
        part = (0.5 / d) * jnp.sum(jnp.sum(e * e, axis=1, keepdims=True), axis=0, keepdims=True)

        @pl.when(pl.program_id(0) == 0)
        def _():
            l_ref[...] = part

        @pl.when(pl.program_id(0) > 0)
        def _():
            l_ref[...] += part

    blk = pl.BlockSpec((bt, d), lambda i: (i, 0))
    dy, loss = _pcall(
        body, name="loss_head", grid=(t // bt,), in_specs=[blk, blk],
        out_specs=[blk, pl.BlockSpec((1, 1), lambda i: (0, 0))],
        out_shape=[jax.ShapeDtypeStruct((t, d), F32), jax.ShapeDtypeStruct((1, 1), F32)],
        compiler_params=_params("arbitrary"),
    )(y, target)
    return loss, dy


def _row_tile(rows, cols, budget=1 << 19):
    best = None
    for cand in range(8, rows + 1, 8):
        if rows % cand == 0 and cand * cols <= budget:
            best = cand
    return best or rows


def _adamw(w, g, m, v, *, name):
    rows, cols = w.shape
    bt = _row_tile(rows, cols)
    c1 = 1.0 - ADAM_B1 ** ADAM_STEP
    c2 = 1.0 - ADAM_B2 ** ADAM_STEP

    def body(w_ref, g_ref, m_ref, v_ref, d_ref, nm_ref, nv_ref):
        g_v = g_ref[...]
        nm = ADAM_B1 * m_ref[...] + (1.0 - ADAM_B1) * g_v
        nv = ADAM_B2 * v_ref[...] + (1.0 - ADAM_B2) * (g_v * g_v)
        nm_ref[...] = nm
        nv_ref[...] = nv
        d_ref[...] = -ADAM_LR * ((nm / c1) / (jnp.sqrt(nv / c2) + ADAM_EPS) + ADAM_WD * w_ref[...])

    blk = pl.BlockSpec((bt, cols), lambda i: (i, 0))
    return _pcall(
        body, name=name, grid=(rows // bt,), in_specs=[blk] * 4, out_specs=[blk] * 3,
        out_shape=[jax.ShapeDtypeStruct((rows, cols), F32)] * 3, compiler_params=_params("parallel"),
    )(w, g, m, v)


def _sum_slots(x, *, name, after=None):
    n, rows, cols = x.shape
    bt = _row_tile(rows, cols, budget=1 << 17)

    def body(x_ref, *rest):
        o_ref = rest[-1]
        acc = x_ref[0].astype(F32)
        for s in range(1, n):
            acc = acc + x_ref[s].astype(F32)
        o_ref[...] = acc

    return _pcall(
        body, name=name, grid=(rows // bt,),
        in_specs=[pl.BlockSpec((n, bt, cols), lambda i: (0, i, 0))]
        + ([pl.BlockSpec(memory_space=pl.ANY)] if after is not None else []),
        out_specs=pl.BlockSpec((bt, cols), lambda i: (i, 0)),
        out_shape=jax.ShapeDtypeStruct((rows, cols), F32), compiler_params=_params("parallel"),
    )(x, *((after,) if after is not None else ()))


def _any_spec():
    return pl.BlockSpec(memory_space=pl.ANY)


def _all_gather(xs, *, name):
    n = len(xs)

    def body(*refs):
        x_refs, out_refs = refs[:n], refs[n:2 * n]
        send_sems, recv_sems, local_sems = refs[2 * n:]
        x, y, c = lax.axis_index("x"), lax.axis_index("y"), lax.axis_index("c")
        me, sibling = (x, y, c), (x, y, 1 - c)
        chips = [(1 - x, y), (x, 1 - y), (1 - x, 1 - y)]

        def slot(a, px, py, pc):
            return out_refs[a].at[4 * px + 2 * py + pc]

        def copy(a, k, block, to, src=None):
            return pltpu.make_async_remote_copy(
                src_ref=slot(a, *block) if src is None else src, dst_ref=slot(a, *block),
                send_sem=send_sems.at[a, k], recv_sem=recv_sems.at[a, k], device_id=to, device_id_type=MESH_ID)

        mine = [pltpu.make_async_copy(x_refs[a], slot(a, *me), local_sems.at[a]) for a in range(n)]
        for cp in mine:
            cp.start()
        first = []
        for j, chip in enumerate(chips):
            first += [copy(a, 1 + j, me, (*chip, c), src=x_refs[a]) for a in range(n)]
        first += [copy(a, 0, me, sibling, src=x_refs[a]) for a in range(n)]
        for cp in first:
            cp.start()
        passed = []
        for j, chip in enumerate(chips):
            for a in range(n):
                copy(a, 1 + j, (*chip, c), me).wait_recv()
                cp = copy(a, 4 + j, (*chip, c), sibling)
                cp.start()
                passed.append(cp)
        for a in range(n):
            copy(a, 0, sibling, me).wait_recv()
        for j, chip in enumerate(chips):
            for a in range(n):
                copy(a, 4 + j, (*chip, 1 - c), me).wait_recv()
        for cp in first + passed:
            cp.wait_send()
        for cp in mine:
            cp.wait()

    return _pcall(
        body, name=name, in_specs=[_any_spec()] * n, out_specs=[_any_spec()] * n,
        out_shape=[jax.ShapeDtypeStruct((N_DEV,) + x.shape, x.dtype) for x in xs],
        scratch_shapes=[pltpu.SemaphoreType.DMA((n, 7)), pltpu.SemaphoreType.DMA((n, 7)), pltpu.SemaphoreType.DMA((n,))],
    )(*xs)


def _mesh_peer(k):
    px = lax.axis_index("x") ^ ((k >> 2) & 1)
    py = lax.axis_index("y") ^ ((k >> 1) & 1)
    pc = lax.axis_index("c") ^ (k & 1)
    return (px, py, pc), 4 * px + 2 * py + pc


def _my_index():
    return 4 * lax.axis_index("x") + 2 * lax.axis_index("y") + lax.axis_index("c")


def _landing_zones(xs, blocks):
    me = _my_index()
    lands = []
    for x in xs:
        own = lax.dynamic_index_in_dim(x, me, 0, keepdims=True) if blocks else x[None]
        zone = lax.empty((N_DEV,) + own.shape[1:], x.dtype)
        lands.append(lax.dynamic_update_slice(zone, own, (me,) + (0,) * (own.ndim - 1)))
    return lands


def _send_start(xs, lands, *, blocks, name, after=None):
    n = len(xs)
    peers = N_DEV - 1
    first_out = 2 * n + (after is not None)

    def body(*refs):
        x_refs, land_refs = refs[:n], refs[n:2 * n]
        send_sems, recv_sems = refs[first_out:first_out + peers], refs[first_out + peers:first_out + 2 * peers]
        token = refs[-1]
        me_idx = _my_index()
        for k in (2, 4, 6, 3, 5, 7, 1):
            to, to_idx = _mesh_peer(k)
            for a in range(n):
                pltpu.make_async_remote_copy(
                    src_ref=x_refs[a].at[to_idx] if blocks else x_refs[a], dst_ref=land_refs[a].at[me_idx],
                    send_sem=send_sems[k - 1], recv_sem=recv_sems[k - 1], device_id=to, device_id_type=MESH_ID).start()
        token[...] = jnp.zeros(token.shape, token.dtype)

    hbm = pl.BlockSpec(memory_space=pltpu.HBM)
    sem = pl.BlockSpec(memory_space=pltpu.SEMAPHORE)
    both = list(xs) + list(lands)
    out = _pcall(
        body, name=name,
        out_shape=(*[pltpu.SemaphoreType.DMA(())] * (2 * peers), *[pltpu.HBM(a.shape, a.dtype) for a in both],
                   jax.ShapeDtypeStruct((8, LANES), F32)),
        in_specs=[hbm] * (2 * n) + ([pl.BlockSpec(memory_space=pl.ANY)] if after is not None else []),
        out_specs=(*[sem] * (2 * peers), *[hbm] * (2 * n), pl.BlockSpec(memory_space=pltpu.VMEM)),
        input_output_aliases={i: 2 * peers + i for i in range(2 * n)},
        compiler_params=pltpu.CompilerParams(has_side_effects=pltpu.SideEffectType.DATAFLOW_SIDE_EFFECTING),
    )(*[pltpu.with_memory_space_constraint(a, pltpu.HBM) for a in both], *((after,) if after is not None else ()))
    return dict(sems=out[:2 * peers], xs=out[2 * peers:2 * peers + n], lands=out[2 * peers + n:2 * peers + 2 * n],
                token=out[-1], blocks=blocks)


def _send_wait(started, after, *, name):
    n = len(started['xs'])
    blocks = started['blocks']
    peers = N_DEV - 1

    def body(*refs):
        x_refs, land_refs = refs[:n], refs[n:2 * n]
        send_sems, recv_sems = refs[2 * n:2 * n + peers], refs[2 * n + peers:2 * n + 2 * peers]
        for k in range(1, N_DEV):
            frm, frm_idx = _mesh_peer(k)
            for a in range(n):
                copy = pltpu.make_async_remote_copy(
                    src_ref=x_refs[a].at[frm_idx] if blocks else x_refs[a], dst_ref=land_refs[a].at[frm_idx],
                    send_sem=send_sems[k - 1], recv_sem=recv_sems[k - 1], device_id=frm, device_id_type=MESH_ID)
                copy.wait_send()
                copy.wait_recv()

    hbm = pl.BlockSpec(memory_space=pltpu.HBM)
    sem = pl.BlockSpec(memory_space=pltpu.SEMAPHORE)
    both = list(started['xs']) + list(started['lands'])
    out = _pcall(
        body, name=name, out_shape=[pltpu.HBM(a.shape, a.dtype) for a in both],
        in_specs=[hbm] * (2 * n) + [sem] * (2 * peers) + [pl.BlockSpec(memory_space=pl.ANY)], out_specs=[hbm] * (2 * n),
        input_output_aliases={i: i for i in range(2 * n)},
        compiler_params=pltpu.CompilerParams(has_side_effects=pltpu.SideEffectType.DATAFLOW_SIDE_EFFECTING),
    )(*both, *started['sems'], after)
    return out[n:]


COLUMN_SHARDED = ('w_in', 'w_mem_out', 'w_gate_up')


def _full_weight(name, g):
    if name in COLUMN_SHARDED:
        g = jnp.transpose(g, (1, 0, 2))
        g = g.reshape(g.shape[0], -1)
    else:
        g = g.reshape(-1, g.shape[-1])
    return jnp.pad(g, ((0, 0), (0, IN_COLS_PAD - IN_COLS))) if name == 'w_in' else g


def _grad_blocks(name, g):
    if name == 'w_gate_up':
        rows, cols = g.shape[1], 2 * g.shape[2] // N_DEV
        g = jnp.transpose(g.reshape(2, rows, N_DEV // 2, cols), (0, 2, 1, 3))
        return g.reshape(N_DEV, rows, cols).astype(BF16)
    if name == 'w_in':
        g = g[:, :IN_COLS]
    if name in COLUMN_SHARDED:
        rows, cols = g.shape[0], g.shape[1] // N_DEV
        g = jnp.transpose(g.reshape(rows, N_DEV, cols), (1, 0, 2))
    else:
        rows, cols = g.shape[0] // N_DEV, g.shape[1]
        g = g.reshape(N_DEV, rows, cols)
    return g.astype(BF16)


SMALL_SHAPES = {'g_mix': (DEPTH, D_MODEL), 'b_forget': (DEPTH, FOX_HEADS), 'g_q_fox': (DEPTH, FOX_HEAD_DIM),
                'g_k_fox': (DEPTH, FOX_HEAD_DIM), 'w_pool': (DEPTH, 4, POOL_GROUP_DIM, POOL_GROUP_DIM),
                'pool_scale': (DEPTH, POOL_WIDTH), 'g_mem_q': (DEPTH, D_MODEL), 'g_mem_kv': (DEPTH, D_MODEL),
                'g_q_mem': (DEPTH, MEM_HEAD_DIM), 'g_k_mem': (DEPTH, MEM_HEAD_DIM), 'g_ffn': (DEPTH, D_MODEL)}


def _small_rows(name):
    return -(-int(np.prod(SMALL_SHAPES[name])) // LANES)


SMALL_ROWS = -(-sum(_small_rows(n) for n in SMALL) // 8) * 8


def _pack_small(tree):
    parts = []
    for n in SMALL:
        flat = tree[n].reshape(-1).astype(F32)
        parts.append(jnp.pad(flat, (0, _small_rows(n) * LANES - flat.shape[0])))
    flat = jnp.concatenate(parts)
    return jnp.pad(flat, (0, SMALL_ROWS * LANES - flat.shape[0])).reshape(SMALL_ROWS, LANES)


def _unpack_small(packed):
    flat = packed.reshape(-1)
    out, at = {}, 0
    for n in SMALL:
        size = int(np.prod(SMALL_SHAPES[n]))
        out[n] = flat[at:at + size].reshape(SMALL_SHAPES[n])
        at += _small_rows(n) * LANES
    return out


def _pairs_cols(a):
    t = a.shape[0]
    return jnp.transpose(a.reshape(t, FOX_HEADS // 2, 2), (1, 0, 2))


def _pairs_rows(a):
    return a.reshape(FOX_HEADS // 2, 2, a.shape[1])


def _layer_fwd(h0, mem, p, w_in, other_weights):
    s = {'h0': h0}
    s['xn1'], z, f = _norm_matmul(h0, p['g_mix'], w_in, tn=IN_COLS_PAD, tail=LANES, name="norm_in_proj_fwd")
    s['z'] = z
    s['qn'], s['kn'], s['vb'], s['qnt'], s['knt'], vt, q2t = _qkv_prep(z, p['g_q_fox'], p['g_k_fox'])
    s['ft'] = jnp.transpose(f[:, :FOX_HEADS])
    c = _forget_cumsum(s['ft'], p['b_forget'])
    s['ccol'], s['crow'] = _pairs_cols(jnp.transpose(c)), _pairs_rows(c)
    s['fox'], s['lse_row'] = _fox_fwd(q2t, s['kn'], vt, s['crow'] * LOG2E, s['ccol'] * LOG2E)
    pool, s['mixed'] = _pool_fwd(z, p['w_pool'].astype(BF16), p['pool_scale'])
    s['cat'] = jnp.concatenate([s['fox'], pool], axis=1)
    w = dict(other_weights(s['lse_row']), w_in=w_in)
    h1 = _matmul(s['cat'], w['w_out'], res=h0, name="out_proj_fwd")
    s['h1'] = h1

    s['hn2'], s['mq_raw'] = _norm_matmul(h1, p['g_mem_q'], w['w_mem_q'], name="norm_mem_q_fwd")
    s['mn'], s['mkv'], s['mk'], s['mv'] = _mem_kv_fwd(mem, p['g_mem_kv'], w['w_mem_kv'], p['g_k_mem'])
    s['mo'] = _cross_fwd(s['mq_raw'], p['g_q_mem'], s['mk'], s['mv'])
    h2 = _matmul(s['mo'], w['w_mem_out'], res=h1, name="mem_out_fwd")
    s['h2'] = h2

    s['hn3'], s['gu'], s['act'] = _norm_gate_up_swiglu(h2, p['g_ffn'], w['w_gate_up'])
    h3 = _matmul(s['act'], w['w_down'], res=h2, name="down_fwd")
    return h3, s, w


def _layer_bwd(dh, mem, p, w, s, after=None, at_mixer=None):
    g = {}
    g['w_down'] = _matmul(s['act'], dh, ta=True, out_dtype=BF16, tm=1408, tn=512, tk=1024, name="down_dw")
    dgu = _down_dx_swiglu_bwd(dh, w['w_down'], s['gu'], after=after)
    g['w_gate_up'] = _matmul(s['hn3'], dgu, ta=True, out_dtype=BF16, tm=1024, tn=1408, tk=1024, name="gate_up_dw")
    dh, g['g_ffn'] = _matmul_norm_bwd(dgu, w['w_gate_up'], s['h2'], p['g_ffn'], dh, name="gate_up_dx_norm_bwd")

    g['w_mem_out'] = _matmul(s['mo'], dh, ta=True, out_dtype=BF16, tm=512, tn=1024, tk=1024, name="mem_out_dw")
    dmo = _matmul(dh, w['w_mem_out'], tb=True, out_dtype=BF16, name="mem_out_dx")
    dmq, dmk, dmv, g['g_q_mem'] = _cross_bwd(s['mq_raw'], dmo, p['g_q_mem'], s['mk'], s['mv'])
    g['w_mem_kv'], g['g_mem_kv'], g['g_k_mem'] = _mem_kv_bwd(dmk, dmv, s['mkv'], s['mn'], mem, p['g_mem_kv'],
                                                               w['w_mem_kv'], p['g_k_mem'])
    g['w_mem_q'] = _matmul(s['hn2'], dmq, ta=True, out_dtype=BF16, tm=1024, tn=512, tk=1024, name="mem_q_dw")
    dh, g['g_mem_q'] = _matmul_norm_bwd(dmq, w['w_mem_q'], s['h1'], p['g_mem_q'], dh, name="mem_q_dx_norm_bwd")

    g['w_out'] = _matmul(s['cat'], dh, ta=True, out_dtype=BF16, tm=1024, tn=512, tk=1024, name="out_proj_dw")
    after = at_mixer(g) if at_mixer is not None else None
    dcat = _matmul(dh, w['w_out'], tb=True, after=after, tn=1024, name="out_proj_dx")
    dpin, g['w_pool'], dscale = _pool_bwd(dcat, s['mixed'], p['w_pool'].astype(BF16), p['pool_scale'])
    g['pool_scale'] = dscale.reshape(POOL_WIDTH)
    do, delta, dot = _fox_delta(dcat, s['fox'])
    delta_row = _pairs_rows(jnp.transpose(delta[:, :FOX_HEADS]))
    dq, dk, dv, dc_keys, dc_queries = _fox_bwd(s['qn'], s['qnt'], s['kn'], s['knt'], s['vb'], do, dot, s['lse_row'],
                                               delta_row, s['crow'], s['ccol'])
    dft, db = _forget_cumsum_bwd(dc_keys, dc_queries, s['ft'], p['b_forget'])
    g['b_forget'] = db.reshape(FOX_HEADS)
    dfpad = jnp.pad(jnp.transpose(dft), ((0, 0), (0, LANES - FOX_HEADS)))
    dz, g['g_q_fox'], g['g_k_fox'] = _mixer_dz(s['z'], dq, dk, dv, dpin, dfpad, p['g_q_fox'], p['g_k_fox'])
    g['w_in'] = _matmul(s['xn1'], dz, ta=True, out_dtype=BF16, tm=512, tn=IN_COLS_PAD, tk=1024, name="in_proj_dw")
    dh, g['g_mix'] = _matmul_norm_bwd(dz, w['w_in'], s['h0'], p['g_mix'], dh, name="in_proj_dx_norm_bwd")
    return dh, g


def _local_step(x2, mem2, target2, small, w_in, other_weights, send_grads):
    h = x2
    saved, full = [], []
    for l in range(DEPTH):
        h, s, w = _layer_fwd(h, mem2, {k: v[l] for k, v in small.items()}, w_in[l], other_weights(l))
        saved.append(s)
        full.append(w)
    loss, dh = _loss_head(h, target2)
    after = None
    grads = [None] * DEPTH
    for l in reversed(range(1, DEPTH)):
        dh, grads[l] = _layer_bwd(dh, mem2, {k: v[l] for k, v in small.items()}, full[l], saved[l], after=after)
        after = send_grads(l, BIG, grads[l])
    dh, grads[0] = _layer_bwd(dh, mem2, {k: v[0] for k, v in small.items()}, full[0], saved[0], after=after,
                              at_mixer=lambda g: send_grads(0, SENT_AT_MIXER, g))
    return loss, dh, grads


def kernel(x, mem, g_mix, w_in, b_forget, g_q_fox, g_k_fox, w_pool, pool_scale, w_out, g_mem_q, g_mem_kv, w_mem_q, w_mem_kv, g_q_mem, g_k_mem, w_mem_out, g_ffn, w_gate_up, w_down, loss_target, m_g_mix, m_w_in, m_b_forget, m_g_q_fox, m_g_k_fox, m_w_pool, m_pool_scale, m_w_out, m_g_mem_q, m_g_mem_kv, m_w_mem_q, m_w_mem_kv, m_g_q_mem, m_g_k_mem, m_w_mem_out, m_g_ffn, m_w_gate_up, m_w_down, v_g_mix, v_w_in, v_b_forget, v_g_q_fox, v_g_k_fox, v_w_pool, v_pool_scale, v_w_out, v_g_mem_q, v_g_mem_kv, v_w_mem_q, v_w_mem_kv, v_g_q_mem, v_g_k_mem, v_w_mem_out, v_g_ffn, v_w_gate_up, v_w_down):
    weights = dict(g_mix=g_mix, w_in=w_in, b_forget=b_forget, g_q_fox=g_q_fox, g_k_fox=g_k_fox, w_pool=w_pool,
                   pool_scale=pool_scale, w_out=w_out, g_mem_q=g_mem_q, g_mem_kv=g_mem_kv, w_mem_q=w_mem_q,
                   w_mem_kv=w_mem_kv, g_q_mem=g_q_mem, g_k_mem=g_k_mem, w_mem_out=w_mem_out, g_ffn=g_ffn,
                   w_gate_up=w_gate_up, w_down=w_down)
    mom_m = dict(g_mix=m_g_mix, w_in=m_w_in, b_forget=m_b_forget, g_q_fox=m_g_q_fox, g_k_fox=m_g_k_fox, w_pool=m_w_pool,
                 pool_scale=m_pool_scale, w_out=m_w_out, g_mem_q=m_g_mem_q, g_mem_kv=m_g_mem_kv, w_mem_q=m_w_mem_q,
                 w_mem_kv=m_w_mem_kv, g_q_mem=m_g_q_mem, g_k_mem=m_g_k_mem, w_mem_out=m_w_mem_out, g_ffn=m_g_ffn,
                 w_gate_up=m_w_gate_up, w_down=m_w_down)
    mom_v = dict(g_mix=v_g_mix, w_in=v_w_in, b_forget=v_b_forget, g_q_fox=v_g_q_fox, g_k_fox=v_g_k_fox, w_pool=v_w_pool,
                 pool_scale=v_pool_scale, w_out=v_w_out, g_mem_q=v_g_mem_q, g_mem_kv=v_g_mem_kv, w_mem_q=v_w_mem_q,
                 w_mem_kv=v_w_mem_kv, g_q_mem=v_g_q_mem, g_k_mem=v_g_k_mem, w_mem_out=v_w_mem_out, g_ffn=v_g_ffn,
                 w_gate_up=v_w_gate_up, w_down=v_w_down)

    (w_in_all,) = _all_gather([weights['w_in'].astype(BF16)], name="w_in_all_gather")
    later = [(n, l) for l in range(DEPTH) for n in BIG if n != 'w_in']
    shards = [weights[n][l].astype(BF16) for n, l in later]
    weights_sent = _send_start(shards, _landing_zones(shards, False), blocks=False, name="weights_send_start",
                               after=w_in_all)
    small = {n: weights[n] for n in SMALL}
    small['g_mix'] = small['g_mix'] + weights_sent['token'][0, 0]
    gathered = {}

    def other_weights(l):
        def get(after):
            if not gathered:
                gathered.update(zip(later, _send_wait(weights_sent, after, name="weights_send_wait")))
            return {n: _full_weight(n, gathered[n, l]) for n in BIG if n != 'w_in'}
        return get

    grads_sent = []

    def send_grads(l, names, g):
        blocks = [_grad_blocks(n, g[n]) for n in names]
        sent = _send_start(blocks, _landing_zones(blocks, True), blocks=True,
                           name=f"grads{l}_send_start" if len(names) > 1 else "grads0_last_send_start")
        grads_sent.append((l, names, sent))
        return sent['token']

    w_in_full = [_full_weight('w_in', w_in_all[:, l]) for l in range(DEPTH)]
    loss_part, grad_x, grads = _local_step(x[0], mem[0], loss_target[0], small, w_in_full, other_weights, send_grads)
    loss = lax.psum(loss_part[0, 0], ("x", "y", "c"))

    last = [n for n in BIG if n not in SENT_AT_MIXER]
    last_token = send_grads(0, last, grads[0])
    landed = {}
    for l, names, sent in grads_sent[:-1]:
        landed.update({(l, n): a for n, a in zip(names, _send_wait(sent, grad_x, name=f"grads{l}_send_wait"))})

    def summed(n):
        return jnp.concatenate([_sum_slots(landed[l, n], name="grad_sum_" + n, after=last_token)
                                for l in range(DEPTH)], 0)

    grad, delta, new_m, new_v = {}, {}, {}, {}

    def update(n):
        shape = weights[n].shape
        two_d = lambda a: a.reshape(shape[0] * shape[1], shape[2])
        d, nm, nv = _adamw(two_d(weights[n]), grad[n], two_d(mom_m[n]), two_d(mom_v[n]), name="adamw_" + n)
        grad[n], delta[n], new_m[n], new_v[n] = (a.reshape(shape) for a in (grad[n], d, nm, nv))

    for n in SENT_AT_MIXER:
        grad[n] = summed(n)
        update(n)
    small_part = _pack_small({n: jnp.stack([grads[l][n] for l in range(DEPTH)], 0) for n in SMALL})
    (small_all,) = _all_gather([small_part], name="small_grads_all_gather")
    small_sum = _sum_slots(small_all, name="grad_sum_small")
    d, nm, nv = _adamw(_pack_small(weights), small_sum, _pack_small(mom_m), _pack_small(mom_v), name="adamw_small")
    grad.update(_unpack_small(small_sum))
    delta.update(_unpack_small(d))
    new_m.update(_unpack_small(nm))
    new_v.update(_unpack_small(nv))
    landed.update({(0, n): a for n, a in zip(last, _send_wait(grads_sent[-1][2], d, name="grads0_last_send_wait"))})
    for n in last:
        grad[n] = summed(n)
        update(n)

    return (loss, grad_x[None], *[grad[n] for n in WEIGHTS], *[delta[n] for n in WEIGHTS],
            *[new_m[n] for n in WEIGHTS], *[new_v[n] for n in WEIGHTS])
```

```python
import functools

import numpy as np
import jax
import jax.numpy as jnp
from jax import lax
from jax.experimental import pallas as pl
from jax.experimental.pallas import tpu as pltpu

F32 = jnp.float32
BF16 = jnp.bfloat16

N_DEV = 8
D_MODEL = 1024
DEPTH = 2
FOX_HEADS = 8
FOX_HEAD_DIM = 64
FOX_WIDTH = 512
POOL_WIDTH = 512
POOL_WINDOWS = (2, 4, 8, 16)
POOL_GROUP_DIM = 128
POOL_HALO = 16
IN_COLS = 2056
IN_COLS_PAD = 2176
MEM_HEADS = 4
MEM_HEAD_DIM = 128
MEM_WIDTH = 512
D_FF = 2816
EPS = 1e-6
FOX_SCALE = FOX_HEAD_DIM ** -0.5
LOG2E = 1.4426950408889634
LN2 = 0.6931471805599453
FOX_ACC_ROWS = FOX_HEAD_DIM + 16
MEM_SCALE = MEM_HEAD_DIM ** -0.5
LANES = 128

ADAM_LR = 0.001
ADAM_B1 = 0.9
ADAM_B2 = 0.999
ADAM_EPS = 1e-08
ADAM_WD = 0.01
ADAM_STEP = 10

VMEM_LIMIT = 56 * 1024 * 1024
MESH_ID = pl.DeviceIdType.MESH

WEIGHTS = ['g_mix', 'w_in', 'b_forget', 'g_q_fox', 'g_k_fox', 'w_pool', 'pool_scale', 'w_out', 'g_mem_q', 'g_mem_kv',
           'w_mem_q', 'w_mem_kv', 'g_q_mem', 'g_k_mem', 'w_mem_out', 'g_ffn', 'w_gate_up', 'w_down']
BIG = ['w_in', 'w_out', 'w_mem_q', 'w_mem_kv', 'w_mem_out', 'w_gate_up', 'w_down']
SMALL = [n for n in WEIGHTS if n not in BIG]
SENT_AT_MIXER = ['w_down', 'w_gate_up', 'w_mem_out', 'w_mem_q', 'w_mem_kv', 'w_out']


def _pcall(body, **kw):
    return pl.pallas_call(body, **kw)


def _params(*sem):
    return pltpu.CompilerParams(dimension_semantics=sem or None, vmem_limit_bytes=VMEM_LIMIT)


def _dot(a, b, dims=None):
    if dims is None:
        return jnp.dot(a, b, preferred_element_type=F32)
    return lax.dot_general(a, b, (dims, ((), ())), preferred_element_type=F32)


NT = ((1,), (1,))
TN = ((0,), (0,))


def _dot_exact(x, ones_bf16, terms=3):
    hi = x.astype(BF16)
    r1 = x - hi.astype(F32)
    mid = r1.astype(BF16)
    if terms == 2:
        return _dot(hi, ones_bf16) + _dot(mid, ones_bf16)
    lo = (r1 - mid.astype(F32)).astype(BF16)
    return _dot(hi, ones_bf16) + _dot(mid, ones_bf16) + _dot(lo, ones_bf16)


def _matmul(a, b, *, ta=False, tb=False, out_dtype=F32, res=None, after=None, tm=1024, tn=512, tk=None, name):
    planes = b.shape[0] if b.ndim == 3 else None
    bshape = b.shape[-2:]
    m, k = (a.shape[1], a.shape[0]) if ta else a.shape
    n = bshape[0] if tb else bshape[1]
    assert k == (bshape[1] if tb else bshape[0])
    tm, tn = min(tm, m), min(tn, n)
    tk = min(tk or k, k)
    assert m % tm == 0 and n % tn == 0 and k % tk == 0, (name, m, n, k, tm, tn, tk)
    nk = k // tk
    dims = ((0 if ta else 1,), (1 if tb else 0,))

    def body(*refs):
        a_ref, b_ref = refs[0], refs[1]
        r_ref = refs[2] if res is not None else None
        o_ref = refs[2 + (res is not None) + (after is not None)]
        part = _dot(a_ref[...].astype(BF16), b_ref[...].astype(BF16), dims)

        def finish(acc):
            if r_ref is not None:
                acc = acc + r_ref[...]
            o_ref[...] = acc.astype(o_ref.dtype)

        if nk == 1:
            finish(part)
        else:
            acc_ref = refs[-1]
            kk = pl.program_id(3)

            @pl.when(kk == 0)
            def _():
                acc_ref[...] = part

            @pl.when(kk > 0)
            def _():
                acc_ref[...] += part

            @pl.when(kk == nk - 1)
            def _():
                finish(acc_ref[...])

    lead = (lambda p: (p,)) if planes else (lambda p: ())
    sq = (None,) if planes else ()
    a_spec = pl.BlockSpec((tk, tm), lambda p, i, j, kk: (kk, i)) if ta else pl.BlockSpec((tm, tk), lambda p, i, j, kk: (i, kk))
    b_spec = (pl.BlockSpec(sq + (tn, tk), lambda p, i, j, kk: lead(p) + (j, kk)) if tb
              else pl.BlockSpec(sq + (tk, tn), lambda p, i, j, kk: lead(p) + (kk, j)))
    o_spec = pl.BlockSpec(sq + (tm, tn), lambda p, i, j, kk: lead(p) + (i, j))
    in_specs = ([a_spec, b_spec] + ([o_spec] if res is not None else [])
                + ([pl.BlockSpec(memory_space=pl.ANY)] if after is not None else []))
    args = (a, b) + ((res,) if res is not None else ()) + ((after,) if after is not None else ())
    return _pcall(
        body, name=name, grid=(planes or 1, m // tm, n // tn, nk), in_specs=in_specs, out_specs=o_spec,
        out_shape=jax.ShapeDtypeStruct(((planes,) if planes else ()) + (m, n), out_dtype),
        scratch_shapes=[pltpu.VMEM((tm, tn), F32)] if nk > 1 else [],
        compiler_params=_params("parallel", "parallel", "parallel", "arbitrary"),
    )(*args)


def _rms(x):
    return lax.rsqrt(jnp.mean(x * x, axis=-1, keepdims=True) + EPS)


def _norm_matmul(h, g, w, *, tm=512, tn=512, tail=0, name):
    t, d = h.shape
    n = w.shape[1]
    tn = min(tn, n)
    assert t % tm == 0 and n % tn == 0 and (not tail or tn == n)

    def body(h_ref, g_ref, w_ref, xn_ref, y_ref, *tail_ref):
        @pl.when(pl.program_id(1) == 0)
        def _():
            x = h_ref[...]
            xn_ref[...] = (x * _rms(x) * g_ref[...]).astype(BF16)

        y = _dot(xn_ref[...], w_ref[...])
        if tail:
            y_ref[...] = y[:, :n - tail]
            tail_ref[0][...] = y[:, n - tail:]
        else:
            y_ref[...] = y

    row = pl.BlockSpec((tm, d), lambda i, j: (i, 0))
    tails = ([pl.BlockSpec((tm, tail), lambda i, j: (i, 0))], [jax.ShapeDtypeStruct((t, tail), F32)]) if tail else ([], [])
    return _pcall(
        body, name=name, grid=(t // tm, n // tn),
        in_specs=[row, pl.BlockSpec((1, d), lambda i, j: (0, 0)), pl.BlockSpec((d, tn), lambda i, j: (0, j))],
        out_specs=[row, pl.BlockSpec((tm, tn - tail), lambda i, j: (i, j))] + tails[0],
        out_shape=[jax.ShapeDtypeStruct((t, d), BF16), jax.ShapeDtypeStruct((t, n - tail), F32)] + tails[1],
        compiler_params=_params("parallel", "arbitrary"),
    )(h, g.reshape(1, d), w)


def _matmul_norm_bwd(a, w, h, g, dres, *, tm=512, tk=None, name):
    stacked = a.ndim == 3
    t = a.shape[-2]
    d, k = w.shape
    tk = a.shape[-1] if stacked else min(tk or k, k)
    nk = k // tk
    assert t % tm == 0 and k % tk == 0 and (not stacked or a.shape[0] == nk)

    def body(a_ref, w_ref, h_ref, g_ref, r_ref, dx_ref, dg_ref, *acc):
        i, kk = pl.program_id(0), pl.program_id(1)
        part = _dot(a_ref[...], w_ref[...], NT)

        def finish(dy):
            x = h_ref[...]
            r = _rms(x)
            xhat = x * r
            gy = dy * g_ref[...]
            dx_ref[...] = r_ref[...] + r * (gy - xhat * jnp.mean(gy * xhat, axis=-1, keepdims=True))
            dg_part = jnp.sum(dy * xhat, axis=0, keepdims=True)

            @pl.when(i == 0)
            def _():
                dg_ref[...] = dg_part

            @pl.when(i > 0)
            def _():
                dg_ref[...] += dg_part

        if nk == 1:
            finish(part)
        else:
            acc_ref = acc[0]

            @pl.when(kk == 0)
            def _():
                acc_ref[...] = part

            @pl.when(kk > 0)
            def _():
                acc_ref[...] += part

            @pl.when(kk == nk - 1)
            def _():
                finish(acc_ref[...])

    a_spec = (pl.BlockSpec((None, tm, tk), lambda i, kk: (kk, i, 0)) if stacked
              else pl.BlockSpec((tm, tk), lambda i, kk: (i, kk)))
    row = pl.BlockSpec((tm, d), lambda i, kk: (i, 0))
    vec = pl.BlockSpec((1, d), lambda i, kk: (0, 0))
    dx, dg = _pcall(
        body, name=name, grid=(t // tm, nk),
        in_specs=[a_spec, pl.BlockSpec((d, tk), lambda i, kk: (0, kk)), row, vec, row], out_specs=[row, vec],
        out_shape=[jax.ShapeDtypeStruct((t, d), F32), jax.ShapeDtypeStruct((1, d), F32)],
        scratch_shapes=[pltpu.VMEM((tm, d), F32)] if nk > 1 else [],
        compiler_params=_params("arbitrary", "arbitrary"),
    )(a, w, h, g.reshape(1, d), dres)
    return dx, dg.reshape(d)


def _norm_gate_up_swiglu(h, g, w, *, tm=512, tn=1408):
    t, d = h.shape
    nj = D_FF // tn
    assert t % tm == 0 and D_FF % tn == 0

    def body(h_ref, g_ref, wg_ref, wu_ref, hn_ref, gu_ref, act_ref):
        @pl.when(pl.program_id(1) == 0)
        def _():
            x = h_ref[...]
            hn_ref[...] = (x * _rms(x) * g_ref[...]).astype(BF16)

        hn = hn_ref[...]
        gate = _dot(hn, wg_ref[...])
        up = _dot(hn, wu_ref[...])
        gu_ref[0] = gate.astype(BF16)
        gu_ref[1] = up.astype(BF16)
        act_ref[...] = (gate * jax.nn.sigmoid(gate) * up).astype(BF16)

    row = pl.BlockSpec((tm, d), lambda i, j: (i, 0))
    return _pcall(
        body, name="gate_up_swiglu_fwd", grid=(t // tm, nj),
        in_specs=[row, pl.BlockSpec((1, d), lambda i, j: (0, 0)), pl.BlockSpec((d, tn), lambda i, j: (0, j)),
                  pl.BlockSpec((d, tn), lambda i, j: (0, nj + j))],
        out_specs=[row, pl.BlockSpec((2, tm, tn), lambda i, j: (0, i, j)), pl.BlockSpec((tm, tn), lambda i, j: (i, j))],
        out_shape=[jax.ShapeDtypeStruct((t, d), BF16), jax.ShapeDtypeStruct((2, t, D_FF), BF16),
                   jax.ShapeDtypeStruct((t, D_FF), BF16)],
        compiler_params=_params("parallel", "arbitrary"),
    )(h, g.reshape(1, d), w, w)


def _down_dx_swiglu_bwd(dh, w_down, gu, *, after=None, tm=1024, tn=1408):
    t, d = dh.shape
    assert t % tm == 0 and D_FF % tn == 0

    def body(dh_ref, w_ref, gu_ref, *rest):
        dgu_ref = rest[-1]
        da = _dot(dh_ref[...].astype(BF16), w_ref[...], NT)
        gate, up = gu_ref[0].astype(F32), gu_ref[1].astype(F32)
        sg = jax.nn.sigmoid(gate)
        silu = gate * sg
        dgu_ref[0] = (da * up * (sg + silu * (1.0 - sg))).astype(BF16)
        dgu_ref[1] = (da * silu).astype(BF16)

    stack = pl.BlockSpec((2, tm, tn), lambda i, j: (0, i, j))
    return _pcall(
        body, name="down_dx_swiglu_bwd", grid=(t // tm, D_FF // tn),
        in_specs=[pl.BlockSpec((tm, d), lambda i, j: (i, 0)), pl.BlockSpec((tn, d), lambda i, j: (j, 0)), stack]
        + ([pl.BlockSpec(memory_space=pl.ANY)] if after is not None else []),
        out_specs=stack, out_shape=jax.ShapeDtypeStruct((2, t, D_FF), BF16),
        compiler_params=_params("parallel", "parallel"),
    )(dh, w_down, gu, *((after,) if after is not None else ()))


def _group_matrix(width, group):
    r = lax.broadcasted_iota(jnp.int32, (width, width), 0) // group
    c = lax.broadcasted_iota(jnp.int32, (width, width), 1) // group
    return (r == c).astype(BF16)


def _qkv_prep(z, gq, gk, *, bt=512):
    t = z.shape[0]
    w = FOX_WIDTH

    def body(q_ref, k_ref, v_ref, gq_ref, gk_ref, qo_ref, ko_ref, vo_ref, qt_ref, kt_ref, vt_ref, q2t_ref):
        gm = _group_matrix(w, FOX_HEAD_DIM)
        for x_ref, g_ref, o_ref, ot_ref, scale in ((q_ref, gq_ref, qo_ref, qt_ref, FOX_SCALE),
                                                   (k_ref, gk_ref, ko_ref, kt_ref, 1.0)):
            x = x_ref[...]
            ms = _dot_exact(x * x, gm, terms=2) * (1.0 / FOX_HEAD_DIM)
            y = x * lax.rsqrt(ms + EPS) * g_ref[...]
            o_ref[...] = (y * scale).astype(BF16)
            yt = jnp.transpose(y)
            ot_ref[...] = (yt * scale).astype(BF16)
            if o_ref is qo_ref:
                q2t_ref[...] = (yt * (scale * LOG2E)).astype(BF16)
        v = v_ref[...]
        vo_ref[...] = v.astype(BF16)
        vt_ref[...] = jnp.transpose(v).astype(BF16)

    col = lambda c: pl.BlockSpec((bt, w), lambda i, c=c: (i, c))
    vec = pl.BlockSpec((1, w), lambda i: (0, 0))
    out = pl.BlockSpec((bt, w), lambda i: (i, 0))
    out_t = pl.BlockSpec((w, bt), lambda i: (0, i))
    return _pcall(
        body, name="fox_qkv_prep", grid=(t // bt,), in_specs=[col(0), col(1), col(2), vec, vec],
        out_specs=[out] * 3 + [out_t] * 4,
        out_shape=[jax.ShapeDtypeStruct((t, w), BF16)] * 3 + [jax.ShapeDtypeStruct((w, t), BF16)] * 4,
        compiler_params=_params("parallel"),
    )(z, z, z, jnp.tile(gq, FOX_HEADS).reshape(1, w), jnp.tile(gk, FOX_HEADS).reshape(1, w))


def _log_sigmoid(f):
    return jnp.minimum(f, 0.0) - jnp.log1p(jnp.exp(-jnp.abs(f)))


def _forget_cumsum(ft, b, *, chunk=512):
    hh, t = ft.shape

    def body(f_ref, b_ref, c_ref):
        r = lax.broadcasted_iota(jnp.int32, (chunk, chunk), 0)
        c = lax.broadcasted_iota(jnp.int32, (chunk, chunk), 1)
        upper = (r <= c).astype(BF16)
        carry = jnp.zeros((hh, 1), F32)
        for ch in range(t // chunk):
            sl = slice(ch * chunk, (ch + 1) * chunk)
            cs = _dot_exact(_log_sigmoid(f_ref[:, sl] + b_ref[...]), upper) + carry
            c_ref[:, sl] = cs
            carry = cs[:, chunk - 1:chunk]

    return _pcall(body, name="fox_forget_cumsum", out_shape=jax.ShapeDtypeStruct((hh, t), F32),
                  compiler_params=_params())(ft, b.reshape(hh, 1))


def _forget_cumsum_bwd(dc_keys, dc_queries, ft, b, *, chunk=512):
    hh, t = ft.shape

    def body(dck_ref, dcq_ref, f_ref, b_ref, df_ref, db_ref):
        r = lax.broadcasted_iota(jnp.int32, (chunk, chunk), 0)
        c = lax.broadcasted_iota(jnp.int32, (chunk, chunk), 1)
        lower = (r >= c).astype(BF16)
        carry = jnp.zeros((hh, 1), F32)
        db = jnp.zeros((hh, 1), F32)
        for ch in reversed(range(t // chunk)):
            sl = slice(ch * chunk, (ch + 1) * chunk)
            dls = _dot_exact(dck_ref[:, sl] + dcq_ref[:, sl], lower) + carry
            carry = dls[:, 0:1]
            df = dls * jax.nn.sigmoid(-(f_ref[:, sl] + b_ref[...]))
            df_ref[:, sl] = df
            db = db + jnp.sum(df, axis=1, keepdims=True)
        db_ref[...] = db

    return _pcall(body, name="fox_forget_cumsum_bwd",
                  out_shape=[jax.ShapeDtypeStruct((hh, t), F32), jax.ShapeDtypeStruct((hh, 1), F32)],
                  compiler_params=_params())(dc_keys, dc_queries, ft, b.reshape(hh, 1))


def _lane_is_first_head():
    return lax.broadcasted_iota(jnp.int32, (1, LANES), 1) < FOX_HEAD_DIM


def _fox_fwd(q2t, kn, vt, crow2, ccol2, *, bq=512, bk=1024):
    t = kn.shape[0]
    nq = t // bq
    pairs = FOX_WIDTH // LANES
    assert t % bk == 0
    tiles = [(i, j) for i in range(nq) for j in range(i * bq // bk, -1, -1)]
    it = jnp.asarray(np.array([a for a, _ in tiles], np.int32))
    jt = jnp.asarray(np.array([b for _, b in tiles], np.int32))

    def body(it_ref, jt_ref, qt_ref, k_ref, vt_ref, cr_ref, cc_ref, o_ref, lse_ref, m_sc, acc_sc):
        s_id = pl.program_id(1)
        i, j = it_ref[s_id], jt_ref[s_id]
        first = _lane_is_first_head()
        holds_diagonal = j == (i * bq) // bk

        @pl.when(holds_diagonal)
        def _():
            m_sc[...] = jnp.full(m_sc.shape, -jnp.inf, F32)
            acc_sc[...] = jnp.zeros(acc_sc.shape, F32)

        def scores():
            k2, qt2 = k_ref[...], qt_ref[...]
            return [_dot(jnp.where(first if hh == 0 else jnp.logical_not(first), k2, jnp.zeros_like(k2)), qt2)
                    for hh in range(2)]

        def pv(hh, pt_bf16):
            v_ones = jnp.concatenate([vt_ref[hh * FOX_HEAD_DIM:(hh + 1) * FOX_HEAD_DIM, :],
                                      jnp.ones((FOX_ACC_ROWS - FOX_HEAD_DIM, bk), BF16)], axis=0)
            return _dot(v_ones, pt_bf16)

        def tile(diagonal):
            sc = scores()
            for hh in range(2):
                ut = sc[hh] - cc_ref[0, :, hh:hh + 1]
                if diagonal:
                    key = j * bk + lax.broadcasted_iota(jnp.int32, ut.shape, 0)
                    query = i * bq + lax.broadcasted_iota(jnp.int32, ut.shape, 1)
                    ut = jnp.where(key <= query, ut, -jnp.inf)
                c_t = cr_ref[0, hh:hh + 1, :]
                m_prev = m_sc[hh]
                m_new = jnp.maximum(m_prev, jnp.max(ut, axis=0, keepdims=True) + c_t)
                acc_sc[hh] = jnp.exp2(m_prev - m_new) * acc_sc[hh] + pv(hh, jnp.exp2(ut + (c_t - m_new)).astype(BF16))
                m_sc[hh] = m_new

        @pl.when(holds_diagonal)
        def _():
            tile(True)

        @pl.when(jnp.logical_not(holds_diagonal))
        def _():
            tile(False)

        @pl.when(j == 0)
        def _():
            sums = [acc_sc[hh, FOX_HEAD_DIM:FOX_HEAD_DIM + 1, :] for hh in range(2)]
            ot = jnp.concatenate([acc_sc[hh, :FOX_HEAD_DIM, :] / sums[hh] for hh in range(2)], axis=0)
            o_ref[...] = jnp.transpose(ot).astype(o_ref.dtype)
            for hh in range(2):
                lse_ref[0, hh:hh + 1, :] = m_sc[hh] * LN2 + jnp.log(sums[hh])

    qspec = pl.BlockSpec((bq, LANES), lambda p, s, it, jt: (it[s], p))
    kspec = pl.BlockSpec((bk, LANES), lambda p, s, it, jt: (jt[s], p))
    vtspec = pl.BlockSpec((LANES, bk), lambda p, s, it, jt: (p, jt[s]))
    qtspec = pl.BlockSpec((LANES, bq), lambda p, s, it, jt: (p, it[s]))
    rowq = pl.BlockSpec((1, 2, bq), lambda p, s, it, jt: (p, 0, it[s]))
    colk = pl.BlockSpec((1, bk, 2), lambda p, s, it, jt: (p, jt[s], 0))
    return _pcall(
        body, name="fox_attention_fwd",
        grid_spec=pltpu.PrefetchScalarGridSpec(
            num_scalar_prefetch=2, grid=(pairs, len(tiles)),
            in_specs=[qtspec, kspec, vtspec, rowq, colk], out_specs=[qspec, rowq],
            scratch_shapes=[pltpu.VMEM((2, 1, bq), F32), pltpu.VMEM((2, FOX_ACC_ROWS, bq), F32)]),
        out_shape=[jax.ShapeDtypeStruct((t, FOX_WIDTH), BF16), jax.ShapeDtypeStruct((pairs, 2, t), F32)],
        compiler_params=_params("parallel", "arbitrary"),
    )(it, jt, q2t, kn, vt, crow2, ccol2)


def _fox_bwd(qn, qnt, kn, knt, vb, do, dot, lse_row, delta_row, crow, ccol, *, bq=512, bk=512):
    t = qn.shape[0]
    nq, nk = t // bq, t // bk
    pairs = FOX_WIDTH // LANES
    tiles = [(i, j) for j in range(nk) for i in range(j * bk // bq, nq)]
    it = jnp.asarray(np.array([a for a, _ in tiles], np.int32))
    jt = jnp.asarray(np.array([b for _, b in tiles], np.int32))

    def body(it_ref, jt_ref, q_ref, qt_ref, k_ref, kt_ref, v_ref, do_ref, dot_ref, lse_ref, dl_ref, cr_ref, cc_ref,
             dqt_ref, dk_ref, dv_ref, dc_ref, dr_ref, dk_sc, dv_sc, dc_sc):
        s_id = pl.program_id(1)
        i, j = it_ref[s_id], jt_ref[s_id]
        first = _lane_is_first_head()
        holds_diagonal = i == (j * bk) // bq

        @pl.when(s_id == 0)
        def _():
            dqt_ref[...] = jnp.zeros(dqt_ref.shape, F32)
            dr_ref[...] = jnp.zeros(dr_ref.shape, F32)

        @pl.when(holds_diagonal)
        def _():
            dk_sc[...] = jnp.zeros(dk_sc.shape, F32)
            dv_sc[...] = jnp.zeros(dv_sc.shape, F32)
            dc_sc[...] = jnp.zeros(dc_sc.shape, F32)

        def tile(diagonal):
            q2, qt2, k2, kt2, v2, do2, dot2 = (q_ref[...], qt_ref[...], k_ref[...], kt_ref[...], v_ref[...], do_ref[...],
                                               dot_ref[...])
            dk_t, dv_t, dqt_t = [], [], []
            for hh in range(2):
                mine = first if hh == 0 else jnp.logical_not(first)
                st = _dot(jnp.where(mine, k2, jnp.zeros_like(k2)), qt2) - cc_ref[0, :, hh:hh + 1]
                if diagonal:
                    key = j * bk + lax.broadcasted_iota(jnp.int32, st.shape, 0)
                    query = i * bq + lax.broadcasted_iota(jnp.int32, st.shape, 1)
                    st = jnp.where(key <= query, st, -jnp.inf)
                pt = jnp.exp(st + (cr_ref[0, hh:hh + 1, :] - lse_ref[0, hh:hh + 1, :]))
                dv_t.append(_dot(pt.astype(BF16), do2))
                dpt = _dot(jnp.where(mine, v2, jnp.zeros_like(v2)), dot2)
                dst = pt * (dpt - dl_ref[0, hh:hh + 1, :])
                dc_sc[hh] += jnp.sum(dst, axis=1, keepdims=True)
                dr_ref[0, i, hh:hh + 1, :] += jnp.sum(dst, axis=0, keepdims=True)
                dsb = dst.astype(BF16)
                dk_t.append(_dot(dsb, q2))
                dqt_t.append(_dot(kt2[hh * FOX_HEAD_DIM:(hh + 1) * FOX_HEAD_DIM], dsb))
            dk_sc[...] += jnp.where(first, dk_t[0], dk_t[1])
            dv_sc[...] += jnp.where(first, dv_t[0], dv_t[1])
            dqt_ref[0, i] += jnp.concatenate(dqt_t, axis=0)

        @pl.when(jnp.logical_not(holds_diagonal))
        def _():
            tile(False)

        @pl.when(holds_diagonal)
        def _():
            tile(True)

        @pl.when(i == nq - 1)
        def _():
            dk_ref[...] = dk_sc[...]
            dv_ref[...] = dv_sc[...].astype(BF16)
            for hh in range(2):
                dc_ref[0, :, hh:hh + 1] = -dc_sc[hh]

    qspec = pl.BlockSpec((bq, LANES), lambda p, s, it, jt: (it[s], p))
    qtspec = pl.BlockSpec((LANES, bq), lambda p, s, it, jt: (p, it[s]))
    kspec = pl.BlockSpec((bk, LANES), lambda p, s, it, jt: (jt[s], p))
    ktspec = pl.BlockSpec((LANES, bk), lambda p, s, it, jt: (p, jt[s]))
    rowq = pl.BlockSpec((1, 2, bq), lambda p, s, it, jt: (p, 0, it[s]))
    colk = pl.BlockSpec((1, bk, 2), lambda p, s, it, jt: (p, jt[s], 0))
    dqt_spec = pl.BlockSpec((1, nq, LANES, bq), lambda p, s, it, jt: (p, 0, 0, 0))
    dr_spec = pl.BlockSpec((1, nq, 2, bq), lambda p, s, it, jt: (p, 0, 0, 0))
    dqt, dk, dv, dc_keys, dc_queries = _pcall(
        body, name="fox_attention_bwd",
        grid_spec=pltpu.PrefetchScalarGridSpec(
            num_scalar_prefetch=2, grid=(pairs, len(tiles)),
            in_specs=[qspec, qtspec, kspec, ktspec, kspec, qspec, qtspec, rowq, rowq, rowq, colk],
            out_specs=[dqt_spec, kspec, kspec, colk, dr_spec],
            scratch_shapes=[pltpu.VMEM((bk, LANES), F32), pltpu.VMEM((bk, LANES), F32), pltpu.VMEM((2, bk, 1), F32)]),
        out_shape=[jax.ShapeDtypeStruct((pairs, nq, LANES, bq), F32), jax.ShapeDtypeStruct((t, FOX_WIDTH), F32),
                   jax.ShapeDtypeStruct((t, FOX_WIDTH), BF16), jax.ShapeDtypeStruct((pairs, t, 2), F32),
                   jax.ShapeDtypeStruct((pairs, nq, 2, bq), F32)],
        compiler_params=_params("parallel", "arbitrary"),
    )(it, jt, qn, qnt, kn, knt, vb, do, dot, lse_row, delta_row, crow, ccol)
    dq = jnp.transpose(dqt, (1, 3, 0, 2)).reshape(t, FOX_WIDTH)
    dc_keys = jnp.transpose(dc_keys, (0, 2, 1)).reshape(FOX_HEADS, t)
    dc_queries = jnp.transpose(dc_queries, (0, 2, 1, 3)).reshape(FOX_HEADS, t)
    return dq, dk, dv, dc_keys, dc_queries


def _fox_delta(dcat, fox, *, bt=512):
    t = fox.shape[0]
    w = FOX_WIDTH

    def body(do_ref, o_ref, dob_ref, dl_ref, dot_ref):
        do = do_ref[...]
        dot_ref[...] = jnp.transpose(do).astype(BF16)
        dob = do.astype(BF16)
        r = lax.broadcasted_iota(jnp.int32, (w, LANES), 0) // FOX_HEAD_DIM
        c = lax.broadcasted_iota(jnp.int32, (w, LANES), 1)
        dl_ref[...] = _dot_exact(dob.astype(F32) * o_ref[...].astype(F32), (r == c).astype(BF16))
        dob_ref[...] = dob

    blk = pl.BlockSpec((bt, w), lambda i: (i, 0))
    return _pcall(
        body, name="fox_delta", grid=(t // bt,), in_specs=[blk, blk],
        out_specs=[blk, pl.BlockSpec((bt, LANES), lambda i: (i, 0)), pl.BlockSpec((w, bt), lambda i: (0, i))],
        out_shape=[jax.ShapeDtypeStruct((t, w), BF16), jax.ShapeDtypeStruct((t, LANES), F32),
                   jax.ShapeDtypeStruct((w, t), BF16)],
        compiler_params=_params("parallel"),
    )(dcat, fox)


def _mixer_dz(z, dq, dk, dv, dpin, dfpad, gq, gk, *, bt=256):
    t = z.shape[0]
    w = FOX_WIDTH

    def body(q_ref, k_ref, dq_ref, dk_ref, dv_ref, dp_ref, df_ref, gq_ref, gk_ref, dz_ref, dgq_ref, dgk_ref):
        gm = _group_matrix(w, FOX_HEAD_DIM)
        first_step = pl.program_id(0) == 0
        for n, (x_ref, dy_ref, g_ref, dg_ref, scale) in enumerate(
                ((q_ref, dq_ref, gq_ref, dgq_ref, FOX_SCALE), (k_ref, dk_ref, gk_ref, dgk_ref, 1.0))):
            x = x_ref[...]
            r = lax.rsqrt(_dot_exact(x * x, gm, terms=2) * (1.0 / FOX_HEAD_DIM) + EPS)
            xhat = x * r
            dy = dy_ref[...] * scale
            gy = dy * g_ref[...]
            dx = r * (gy - xhat * (_dot_exact(gy * xhat, gm, terms=2) * (1.0 / FOX_HEAD_DIM)))
            dz_ref[:, n * w:(n + 1) * w] = dx.astype(BF16)
            part = jnp.sum(dy * xhat, axis=0, keepdims=True)

            @pl.when(first_step)
            def _():
                dg_ref[...] = part

            @pl.when(jnp.logical_not(first_step))
            def _():
                dg_ref[...] += part

        dz_ref[:, 2 * w:3 * w] = dv_ref[...].astype(BF16)
        dz_ref[:, 3 * w:4 * w] = dp_ref[...].astype(BF16)
        dz_ref[:, 4 * w:] = df_ref[...].astype(BF16)

    col = lambda c: pl.BlockSpec((bt, w), lambda i, c=c: (i, c))
    blk = pl.BlockSpec((bt, w), lambda i: (i, 0))
    vec = pl.BlockSpec((1, w), lambda i: (0, 0))
    dz, dgq, dgk = _pcall(
        body, name="mixer_dz", grid=(t // bt,),
        in_specs=[col(0), col(1), blk, blk, blk, blk, pl.BlockSpec((bt, LANES), lambda i: (i, 0)), vec, vec],
        out_specs=[pl.BlockSpec((bt, IN_COLS_PAD), lambda i: (i, 0)), vec, vec],
        out_shape=[jax.ShapeDtypeStruct((t, IN_COLS_PAD), BF16), jax.ShapeDtypeStruct((1, w), F32),
                   jax.ShapeDtypeStruct((1, w), F32)],
        compiler_params=_params("arbitrary"),
    )(z, z, dq, dk, dv, dpin, dfpad, jnp.tile(gq, FOX_HEADS).reshape(1, w), jnp.tile(gk, FOX_HEADS).reshape(1, w))
    return dz, dgq.reshape(FOX_HEADS, FOX_HEAD_DIM).sum(0), dgk.reshape(FOX_HEADS, FOX_HEAD_DIM).sum(0)


def _pool_fwd(z, wp, scale, *, bt=512):
    t = z.shape[0]
    w = POOL_WIDTH
    hb = bt // POOL_HALO

    def body(p_ref, h_ref, wp_ref, sc_ref, y_ref, mx_ref):
        i = pl.program_id(0)
        cur = p_ref[...]
        halo = jnp.where(i > 0, h_ref[...], 0.0)
        ext = jnp.concatenate([halo, cur], axis=0)
        trow = i * bt + lax.broadcasted_iota(jnp.int32, (bt, 1), 0)
        for g, win in enumerate(POOL_WINDOWS):
            sl = slice(g * LANES, (g + 1) * LANES)
            e = ext[:, sl]
            acc = e[POOL_HALO:]
            for k in range(1, win):
                acc = acc + pltpu.roll(e, k, 0)[POOL_HALO:]
            cnt = jnp.minimum(trow + 1, win).astype(F32)
            mixed = (acc / cnt - cur[:, sl]).astype(BF16)
            mx_ref[:, sl] = mixed
            y_ref[:, sl] = (_dot(mixed, wp_ref[g]) * sc_ref[:, sl]).astype(BF16)

    blk = pl.BlockSpec((bt, w), lambda i: (i, 0))
    return _pcall(
        body, name="pool_fwd", grid=(t // bt,),
        in_specs=[pl.BlockSpec((bt, w), lambda i: (i, 3)),
                  pl.BlockSpec((POOL_HALO, w), lambda i: (jnp.maximum(i * hb - 1, 0), 3)),
                  pl.BlockSpec((len(POOL_WINDOWS), LANES, LANES), lambda i: (0, 0, 0)),
                  pl.BlockSpec((1, w), lambda i: (0, 0))],
        out_specs=[blk, blk], out_shape=[jax.ShapeDtypeStruct((t, w), BF16)] * 2,
        compiler_params=_params("parallel"),
    )(z, z, wp, scale.reshape(1, w))


def _pool_bwd(dcat, mixed, wp, scale, *, bt=512):
    t = mixed.shape[0]
    w = POOL_WIDTH
    hb = bt // POOL_HALO
    nb = t // bt
    n_ext = bt + POOL_HALO

    def body(d_ref, h_ref, mx_ref, wp_ref, sc_ref, dp_ref, dwp_ref, dsc_ref):
        i = pl.program_id(0)
        cur = d_ref[...]
        nxt = jnp.where(i < nb - 1, h_ref[...], 0.0)
        ext = jnp.concatenate([cur, nxt], axis=0)
        trow = i * bt + lax.broadcasted_iota(jnp.int32, (n_ext, 1), 0)

        @pl.when(i == 0)
        def _():
            dwp_ref[...] = jnp.zeros(dwp_ref.shape, F32)
            dsc_ref[...] = jnp.zeros(dsc_ref.shape, F32)

        for g, win in enumerate(POOL_WINDOWS):
            sl = slice(g * LANES, (g + 1) * LANES)
            dy = (ext[:, sl] * sc_ref[:, sl]).astype(BF16)
            dm = _dot(dy, wp_ref[g], NT)
            mixed_g = mx_ref[:, sl]
            dsc_ref[:, sl] += jnp.sum(cur[:, sl] * _dot(mixed_g, wp_ref[g]), axis=0, keepdims=True)
            dwp_ref[g] += _dot(mixed_g, dy[:bt], TN)
            r = dm / jnp.minimum(trow + 1, win).astype(F32)
            acc = r[:bt]
            for k in range(1, win):
                acc = acc + pltpu.roll(r, n_ext - k, 0)[:bt]
            dp_ref[:, sl] = (acc - dm[:bt]).astype(BF16)

    return _pcall(
        body, name="pool_bwd", grid=(nb,),
        in_specs=[pl.BlockSpec((bt, w), lambda i: (i, 1)),
                  pl.BlockSpec((POOL_HALO, w), lambda i: (jnp.minimum((i + 1) * hb, t // POOL_HALO - 1), 1)),
                  pl.BlockSpec((bt, w), lambda i: (i, 0)),
                  pl.BlockSpec((len(POOL_WINDOWS), LANES, LANES), lambda i: (0, 0, 0)),
                  pl.BlockSpec((1, w), lambda i: (0, 0))],
        out_specs=[pl.BlockSpec((bt, w), lambda i: (i, 0)),
                   pl.BlockSpec((len(POOL_WINDOWS), LANES, LANES), lambda i: (0, 0, 0)),
                   pl.BlockSpec((1, w), lambda i: (0, 0))],
        out_shape=[jax.ShapeDtypeStruct((t, w), BF16), jax.ShapeDtypeStruct((len(POOL_WINDOWS), LANES, LANES), F32),
                   jax.ShapeDtypeStruct((1, w), F32)],
        compiler_params=_params("arbitrary"),
    )(dcat, dcat, mixed, wp, scale.reshape(1, w))


def _head_rms(x):
    return lax.rsqrt(jnp.mean(x * x, axis=-1, keepdims=True) + EPS)


def _mem_kv_fwd(mem, g_kv, w_kv, g_k):
    mlen, d = mem.shape

    def body(m_ref, g_ref, w_ref, gk_ref, mn_ref, mkv_ref, mk_ref, mv_ref):
        x = m_ref[...]
        mn = (x * lax.rsqrt(jnp.mean(x * x, axis=-1, keepdims=True) + EPS) * g_ref[...]).astype(BF16)
        mn_ref[...] = mn
        mkv = _dot(mn, w_ref[...])
        mkv_ref[...] = mkv
        for h in range(MEM_HEADS):
            sl = slice(h * MEM_HEAD_DIM, (h + 1) * MEM_HEAD_DIM)
            kh = mkv[:, sl]
            mk_ref[:, sl] = (kh * _head_rms(kh) * gk_ref[...]).astype(BF16)
        mv_ref[...] = mkv[:, MEM_WIDTH:].astype(BF16)

    return _pcall(
        body, name="mem_kv_fwd",
        out_shape=[jax.ShapeDtypeStruct((mlen, d), BF16), jax.ShapeDtypeStruct((mlen, 2 * MEM_WIDTH), F32),
                   jax.ShapeDtypeStruct((mlen, MEM_WIDTH), BF16), jax.ShapeDtypeStruct((mlen, MEM_WIDTH), BF16)],
        compiler_params=_params(),
    )(mem, g_kv.reshape(1, d), w_kv, g_k.reshape(1, MEM_HEAD_DIM))


def _mem_kv_bwd(dmk, dmv, mkv, mn, mem, g_kv, w_kv, g_k):
    mlen, d = mem.shape

    def body(dmk_ref, dmv_ref, mkv_ref, mn_ref, m_ref, g_ref, w_ref, gk_ref, dw_ref, dg_ref, dgk_ref, dkv_sc):
        dgk = jnp.zeros((1, MEM_HEAD_DIM), F32)
        for h in range(MEM_HEADS):
            sl = slice(h * MEM_HEAD_DIM, (h + 1) * MEM_HEAD_DIM)
            x = mkv_ref[:, sl]
            r = _head_rms(x)
            xhat = x * r
            dy = dmk_ref[:, sl]
            gy = dy * gk_ref[...]
            dkv_sc[:, sl] = (r * (gy - xhat * jnp.mean(gy * xhat, axis=-1, keepdims=True))).astype(BF16)
            dgk = dgk + jnp.sum(dy * xhat, axis=0, keepdims=True)
        dgk_ref[...] = dgk
        dkv_sc[:, MEM_WIDTH:] = dmv_ref[...].astype(BF16)
        dkv = dkv_sc[...]
        dw_ref[...] = _dot(mn_ref[...], dkv, TN).astype(BF16)
        dmn = _dot(dkv, w_ref[...], NT)
        x = m_ref[...]
        xhat = x * lax.rsqrt(jnp.mean(x * x, axis=-1, keepdims=True) + EPS)
        dg_ref[...] = jnp.sum(dmn * xhat, axis=0, keepdims=True)

    dw, dg, dgk = _pcall(
        body, name="mem_kv_bwd",
        out_shape=[jax.ShapeDtypeStruct((d, 2 * MEM_WIDTH), BF16), jax.ShapeDtypeStruct((1, d), F32),
                   jax.ShapeDtypeStruct((1, MEM_HEAD_DIM), F32)],
        scratch_shapes=[pltpu.VMEM((mlen, 2 * MEM_WIDTH), BF16)],
        compiler_params=_params(),
    )(dmk, dmv, mkv, mn, mem, g_kv.reshape(1, d), w_kv, g_k.reshape(1, MEM_HEAD_DIM))
    return dw, dg.reshape(d), dgk.reshape(MEM_HEAD_DIM)


def _cross_probs(x, g, mk_h):
    r = _head_rms(x)
    xhat = x * r
    qn = (xhat * g).astype(BF16)
    s = _dot(qn, mk_h, NT) * MEM_SCALE
    e = jnp.exp(s - jnp.max(s, axis=-1, keepdims=True))
    return r, xhat, qn, e / jnp.sum(e, axis=-1, keepdims=True)


def _cross_fwd(mq_raw, g_q, mk, mv, *, bt=512):
    t = mq_raw.shape[0]
    mlen = mk.shape[0]

    def body(x_ref, g_ref, mk_ref, mv_ref, o_ref):
        for h in range(MEM_HEADS):
            sl = slice(h * MEM_HEAD_DIM, (h + 1) * MEM_HEAD_DIM)
            _, _, _, p = _cross_probs(x_ref[:, sl], g_ref[...], mk_ref[:, sl])
            o_ref[:, sl] = _dot(p.astype(BF16), mv_ref[:, sl]).astype(BF16)

    blk = pl.BlockSpec((bt, MEM_WIDTH), lambda i: (i, 0))
    kv = pl.BlockSpec((mlen, MEM_WIDTH), lambda i: (0, 0))
    return _pcall(
        body, name="cross_attention_fwd", grid=(t // bt,),
        in_specs=[blk, pl.BlockSpec((1, MEM_HEAD_DIM), lambda i: (0, 0)), kv, kv], out_specs=blk,
        out_shape=jax.ShapeDtypeStruct((t, MEM_WIDTH), BF16), compiler_params=_params("parallel"),
    )(mq_raw, g_q.reshape(1, MEM_HEAD_DIM), mk, mv)


def _cross_bwd(mq_raw, dmo, g_q, mk, mv, *, bt=512):
    t = mq_raw.shape[0]
    mlen = mk.shape[0]

    def body(x_ref, do_ref, g_ref, mk_ref, mv_ref, dx_ref, dmk_ref, dmv_ref, dg_ref):
        @pl.when(pl.program_id(0) == 0)
        def _():
            dmk_ref[...] = jnp.zeros(dmk_ref.shape, F32)
            dmv_ref[...] = jnp.zeros(dmv_ref.shape, F32)
            dg_ref[...] = jnp.zeros(dg_ref.shape, F32)

        for h in range(MEM_HEADS):
            sl = slice(h * MEM_HEAD_DIM, (h + 1) * MEM_HEAD_DIM)
            r, xhat, qn, p = _cross_probs(x_ref[:, sl], g_ref[...], mk_ref[:, sl])
            do = do_ref[:, sl]
            dp = _dot(do, mv_ref[:, sl], NT)
            ds = (p * (dp - jnp.sum(p * dp, axis=-1, keepdims=True)) * MEM_SCALE).astype(BF16)
            dmv_ref[:, sl] += _dot(p.astype(BF16), do, TN)
            dmk_ref[:, sl] += _dot(ds, qn, TN)
            dqn = _dot(ds, mk_ref[:, sl])
            gy = dqn * g_ref[...]
            dx_ref[:, sl] = (r * (gy - xhat * jnp.mean(gy * xhat, axis=-1, keepdims=True))).astype(BF16)
            dg_ref[...] += jnp.sum(dqn * xhat, axis=0, keepdims=True)

    blk = pl.BlockSpec((bt, MEM_WIDTH), lambda i: (i, 0))
    kv = pl.BlockSpec((mlen, MEM_WIDTH), lambda i: (0, 0))
    gs = pl.BlockSpec((1, MEM_HEAD_DIM), lambda i: (0, 0))
    dx, dmk, dmv, dg = _pcall(
        body, name="cross_attention_bwd", grid=(t // bt,),
        in_specs=[blk, blk, gs, kv, kv], out_specs=[blk, kv, kv, gs],
        out_shape=[jax.ShapeDtypeStruct((t, MEM_WIDTH), BF16), jax.ShapeDtypeStruct((mlen, MEM_WIDTH), F32),
                   jax.ShapeDtypeStruct((mlen, MEM_WIDTH), F32), jax.ShapeDtypeStruct((1, MEM_HEAD_DIM), F32)],
        compiler_params=_params("arbitrary"),
    )(mq_raw, dmo, g_q.reshape(1, MEM_HEAD_DIM), mk, mv)
    return dx, dmk, dmv, dg.reshape(MEM_HEAD_DIM)


def _loss_head(y, target, *, bt=512):
    t, d = y.shape

    def body(y_ref, t_ref, dy_ref, l_ref):
        e = y_ref[...] - t_ref[...]
        dy_ref[...] = e * (1.0 / d)
        part = (0.5 / d) * jnp.sum(jnp.sum(e * e, axis=1, keepdims=True), axis=0, keepdims=True)

        @pl.when(pl.program_id(0) == 0)
        def _():
            l_ref[...] = part

        @pl.when(pl.program_id(0) > 0)
        def _():
            l_ref[...] += part

    blk = pl.BlockSpec((bt, d), lambda i: (i, 0))
    dy, loss = _pcall(
        body, name="loss_head", grid=(t // bt,), in_specs=[blk, blk],
        out_specs=[blk, pl.BlockSpec((1, 1), lambda i: (0, 0))],
        out_shape=[jax.ShapeDtypeStruct((t, d), F32), jax.ShapeDtypeStruct((1, 1), F32)],
        compiler_params=_params("arbitrary"),
    )(y, target)
    return loss, dy


def _row_tile(rows, cols, budget=1 << 19):
    best = None
    for cand in range(8, rows + 1, 8):
        if rows % cand == 0 and cand * cols <= budget:
            best = cand
    return best or rows


def _adamw(w, g, m, v, *, name):
    rows, cols = w.shape
    bt = _row_tile(rows, cols)
    c1 = 1.0 - ADAM_B1 ** ADAM_STEP
    c2 = 1.0 - ADAM_B2 ** ADAM_STEP

    def body(w_ref, g_ref, m_ref, v_ref, d_ref, nm_ref, nv_ref):
        g_v = g_ref[...]
        nm = ADAM_B1 * m_ref[...] + (1.0 - ADAM_B1) * g_v
        nv = ADAM_B2 * v_ref[...] + (1.0 - ADAM_B2) * (g_v * g_v)
        nm_ref[...] = nm
        nv_ref[...] = nv
        d_ref[...] = -ADAM_LR * ((nm / c1) / (jnp.sqrt(nv / c2) + ADAM_EPS) + ADAM_WD * w_ref[...])

    blk = pl.BlockSpec((bt, cols), lambda i: (i, 0))
    return _pcall(
        body, name=name, grid=(rows // bt,), in_specs=[blk] * 4, out_specs=[blk] * 3,
        out_shape=[jax.ShapeDtypeStruct((rows, cols), F32)] * 3, compiler_params=_params("parallel"),
    )(w, g, m, v)


def _sum_slots(x, *, name, after=None):
    n, rows, cols = x.shape
    bt = _row_tile(rows, cols, budget=1 << 17)

    def body(x_ref, *rest):
        o_ref = rest[-1]
        acc = x_ref[0].astype(F32)
        for s in range(1, n):
            acc = acc + x_ref[s].astype(F32)
        o_ref[...] = acc

    return _pcall(
        body, name=name, grid=(rows // bt,),
        in_specs=[pl.BlockSpec((n, bt, cols), lambda i: (0, i, 0))]
        + ([pl.BlockSpec(memory_space=pl.ANY)] if after is not None else []),
        out_specs=pl.BlockSpec((bt, cols), lambda i: (i, 0)),
        out_shape=jax.ShapeDtypeStruct((rows, cols), F32), compiler_params=_params("parallel"),
    )(x, *((after,) if after is not None else ()))


def _any_spec():
    return pl.BlockSpec(memory_space=pl.ANY)


def _all_gather(xs, *, name):
    n = len(xs)

    def body(*refs):
        x_refs, out_refs = refs[:n], refs[n:2 * n]
        send_sems, recv_sems, local_sems = refs[2 * n:]
        x, y, c = lax.axis_index("x"), lax.axis_index("y"), lax.axis_index("c")
        me, sibling = (x, y, c), (x, y, 1 - c)
        chips = [(1 - x, y), (x, 1 - y), (1 - x, 1 - y)]

        def slot(a, px, py, pc):
            return out_refs[a].at[4 * px + 2 * py + pc]

        def copy(a, k, block, to, src=None):
            return pltpu.make_async_remote_copy(
                src_ref=slot(a, *block) if src is None else src, dst_ref=slot(a, *block),
                send_sem=send_sems.at[a, k], recv_sem=recv_sems.at[a, k], device_id=to, device_id_type=MESH_ID)

        mine = [pltpu.make_async_copy(x_refs[a], slot(a, *me), local_sems.at[a]) for a in range(n)]
        for cp in mine:
            cp.start()
        first = []
        for j, chip in enumerate(chips):
            first += [copy(a, 1 + j, me, (*chip, c), src=x_refs[a]) for a in range(n)]
        first += [copy(a, 0, me, sibling, src=x_refs[a]) for a in range(n)]
        for cp in first:
            cp.start()
        passed = []
        for j, chip in enumerate(chips):
            for a in range(n):
                copy(a, 1 + j, (*chip, c), me).wait_recv()
                cp = copy(a, 4 + j, (*chip, c), sibling)
                cp.start()
                passed.append(cp)
        for a in range(n):
            copy(a, 0, sibling, me).wait_recv()
        for j, chip in enumerate(chips):
            for a in range(n):
                copy(a, 4 + j, (*chip, 1 - c), me).wait_recv()
        for cp in first + passed:
            cp.wait_send()
        for cp in mine:
            cp.wait()

    return _pcall(
        body, name=name, in_specs=[_any_spec()] * n, out_specs=[_any_spec()] * n,
        out_shape=[jax.ShapeDtypeStruct((N_DEV,) + x.shape, x.dtype) for x in xs],
        scratch_shapes=[pltpu.SemaphoreType.DMA((n, 7)), pltpu.SemaphoreType.DMA((n, 7)), pltpu.SemaphoreType.DMA((n,))],
    )(*xs)


def _mesh_peer(k):
    px = lax.axis_index("x") ^ ((k >> 2) & 1)
    py = lax.axis_index("y") ^ ((k >> 1) & 1)
    pc = lax.axis_index("c") ^ (k & 1)
    return (px, py, pc), 4 * px + 2 * py + pc


def _my_index():
    return 4 * lax.axis_index("x") + 2 * lax.axis_index("y") + lax.axis_index("c")


def _landing_zones(xs, blocks):
    me = _my_index()
    lands = []
    for x in xs:
        own = lax.dynamic_index_in_dim(x, me, 0, keepdims=True) if blocks else x[None]
        zone = lax.empty((N_DEV,) + own.shape[1:], x.dtype)
        lands.append(lax.dynamic_update_slice(zone, own, (me,) + (0,) * (own.ndim - 1)))
    return lands


def _send_start(xs, lands, *, blocks, name, after=None):
    n = len(xs)
    peers = N_DEV - 1
    first_out = 2 * n + (after is not None)

    def body(*refs):
        x_refs, land_refs = refs[:n], refs[n:2 * n]
        send_sems, recv_sems = refs[first_out:first_out + peers], refs[first_out + peers:first_out + 2 * peers]
        token = refs[-1]
        me_idx = _my_index()
        for k in (2, 4, 6, 3, 5, 7, 1):
            to, to_idx = _mesh_peer(k)
            for a in range(n):
                pltpu.make_async_remote_copy(
                    src_ref=x_refs[a].at[to_idx] if blocks else x_refs[a], dst_ref=land_refs[a].at[me_idx],
                    send_sem=send_sems[k - 1], recv_sem=recv_sems[k - 1], device_id=to, device_id_type=MESH_ID).start()
        token[...] = jnp.zeros(token.shape, token.dtype)

    hbm = pl.BlockSpec(memory_space=pltpu.HBM)
    sem = pl.BlockSpec(memory_space=pltpu.SEMAPHORE)
    both = list(xs) + list(lands)
    out = _pcall(
        body, name=name,
        out_shape=(*[pltpu.SemaphoreType.DMA(())] * (2 * peers), *[pltpu.HBM(a.shape, a.dtype) for a in both],
                   jax.ShapeDtypeStruct((8, LANES), F32)),
        in_specs=[hbm] * (2 * n) + ([pl.BlockSpec(memory_space=pl.ANY)] if after is not None else []),
        out_specs=(*[sem] * (2 * peers), *[hbm] * (2 * n), pl.BlockSpec(memory_space=pltpu.VMEM)),
        input_output_aliases={i: 2 * peers + i for i in range(2 * n)},
        compiler_params=pltpu.CompilerParams(has_side_effects=pltpu.SideEffectType.DATAFLOW_SIDE_EFFECTING),
    )(*[pltpu.with_memory_space_constraint(a, pltpu.HBM) for a in both], *((after,) if after is not None else ()))
    return dict(sems=out[:2 * peers], xs=out[2 * peers:2 * peers + n], lands=out[2 * peers + n:2 * peers + 2 * n],
                token=out[-1], blocks=blocks)


def _send_wait(started, after, *, name):
    n = len(started['xs'])
    blocks = started['blocks']
    peers = N_DEV - 1

    def body(*refs):
        x_refs, land_refs = refs[:n], refs[n:2 * n]
        send_sems, recv_sems = refs[2 * n:2 * n + peers], refs[2 * n + peers:2 * n + 2 * peers]
        for k in range(1, N_DEV):
            frm, frm_idx = _mesh_peer(k)
            for a in range(n):
                copy = pltpu.make_async_remote_copy(
                    src_ref=x_refs[a].at[frm_idx] if blocks else x_refs[a], dst_ref=land_refs[a].at[frm_idx],
                    send_sem=send_sems[k - 1], recv_sem=recv_sems[k - 1], device_id=frm, device_id_type=MESH_ID)
                copy.wait_send()
                copy.wait_recv()

    hbm = pl.BlockSpec(memory_space=pltpu.HBM)
    sem = pl.BlockSpec(memory_space=pltpu.SEMAPHORE)
    both = list(started['xs']) + list(started['lands'])
    out = _pcall(
        body, name=name, out_shape=[pltpu.HBM(a.shape, a.dtype) for a in both],
        in_specs=[hbm] * (2 * n) + [sem] * (2 * peers) + [pl.BlockSpec(memory_space=pl.ANY)], out_specs=[hbm] * (2 * n),
        input_output_aliases={i: i for i in range(2 * n)},
        compiler_params=pltpu.CompilerParams(has_side_effects=pltpu.SideEffectType.DATAFLOW_SIDE_EFFECTING),
    )(*both, *started['sems'], after)
    return out[n:]


COLUMN_SHARDED = ('w_in', 'w_mem_out', 'w_gate_up')


def _full_weight(name, g):
    if name in COLUMN_SHARDED:
        g = jnp.transpose(g, (1, 0, 2))
        g = g.reshape(g.shape[0], -1)
    else:
        g = g.reshape(-1, g.shape[-1])
    return jnp.pad(g, ((0, 0), (0, IN_COLS_PAD - IN_COLS))) if name == 'w_in' else g


def _grad_blocks(name, g):
    if name == 'w_gate_up':
        rows, cols = g.shape[1], 2 * g.shape[2] // N_DEV
        g = jnp.transpose(g.reshape(2, rows, N_DEV // 2, cols), (0, 2, 1, 3))
        return g.reshape(N_DEV, rows, cols).astype(BF16)
    if name == 'w_in':
        g = g[:, :IN_COLS]
    if name in COLUMN_SHARDED:
        rows, cols = g.shape[0], g.shape[1] // N_DEV
        g = jnp.transpose(g.reshape(rows, N_DEV, cols), (1, 0, 2))
    else:
        rows, cols = g.shape[0] // N_DEV, g.shape[1]
        g = g.reshape(N_DEV, rows, cols)
    return g.astype(BF16)


SMALL_SHAPES = {'g_mix': (DEPTH, D_MODEL), 'b_forget': (DEPTH, FOX_HEADS), 'g_q_fox': (DEPTH, FOX_HEAD_DIM),
                'g_k_fox': (DEPTH, FOX_HEAD_DIM), 'w_pool': (DEPTH, 4, POOL_GROUP_DIM, POOL_GROUP_DIM),
                'pool_scale': (DEPTH, POOL_WIDTH), 'g_mem_q': (DEPTH, D_MODEL), 'g_mem_kv': (DEPTH, D_MODEL),
                'g_q_mem': (DEPTH, MEM_HEAD_DIM), 'g_k_mem': (DEPTH, MEM_HEAD_DIM), 'g_ffn': (DEPTH, D_MODEL)}


def _small_rows(name):
    return -(-int(np.prod(SMALL_SHAPES[name])) // LANES)


SMALL_ROWS = -(-sum(_small_rows(n) for n in SMALL) // 8) * 8


def _pack_small(tree):
    parts = []
    for n in SMALL:
        flat = tree[n].reshape(-1).astype(F32)
        parts.append(jnp.pad(flat, (0, _small_rows(n) * LANES - flat.shape[0])))
    flat = jnp.concatenate(parts)
    return jnp.pad(flat, (0, SMALL_ROWS * LANES - flat.shape[0])).reshape(SMALL_ROWS, LANES)


def _unpack_small(packed):
    flat = packed.reshape(-1)
    out, at = {}, 0
    for n in SMALL:
        size = int(np.prod(SMALL_SHAPES[n]))
        out[n] = flat[at:at + size].reshape(SMALL_SHAPES[n])
        at += _small_rows(n) * LANES
    return out


def _pairs_cols(a):
    t = a.shape[0]
    return jnp.transpose(a.reshape(t, FOX_HEADS // 2, 2), (1, 0, 2))


def _pairs_rows(a):
    return a.reshape(FOX_HEADS // 2, 2, a.shape[1])


def _layer_fwd(h0, mem, p, w_in, other_weights):
    s = {'h0': h0}
    s['xn1'], z, f = _norm_matmul(h0, p['g_mix'], w_in, tn=IN_COLS_PAD, tail=LANES, name="norm_in_proj_fwd")
    s['z'] = z
    s['qn'], s['kn'], s['vb'], s['qnt'], s['knt'], vt, q2t = _qkv_prep(z, p['g_q_fox'], p['g_k_fox'])
    s['ft'] = jnp.transpose(f[:, :FOX_HEADS])
    c = _forget_cumsum(s['ft'], p['b_forget'])
    s['ccol'], s['crow'] = _pairs_cols(jnp.transpose(c)), _pairs_rows(c)
    s['fox'], s['lse_row'] = _fox_fwd(q2t, s['kn'], vt, s['crow'] * LOG2E, s['ccol'] * LOG2E)
    pool, s['mixed'] = _pool_fwd(z, p['w_pool'].astype(BF16), p['pool_scale'])
    s['cat'] = jnp.concatenate([s['fox'], pool], axis=1)
    w = dict(other_weights(s['lse_row']), w_in=w_in)
    h1 = _matmul(s['cat'], w['w_out'], res=h0, name="out_proj_fwd")
    s['h1'] = h1

    s['hn2'], s['mq_raw'] = _norm_matmul(h1, p['g_mem_q'], w['w_mem_q'], name="norm_mem_q_fwd")
    s['mn'], s['mkv'], s['mk'], s['mv'] = _mem_kv_fwd(mem, p['g_mem_kv'], w['w_mem_kv'], p['g_k_mem'])
    s['mo'] = _cross_fwd(s['mq_raw'], p['g_q_mem'], s['mk'], s['mv'])
    h2 = _matmul(s['mo'], w['w_mem_out'], res=h1, name="mem_out_fwd")
    s['h2'] = h2

    s['hn3'], s['gu'], s['act'] = _norm_gate_up_swiglu(h2, p['g_ffn'], w['w_gate_up'])
    h3 = _matmul(s['act'], w['w_down'], res=h2, name="down_fwd")
    return h3, s, w


def _layer_bwd(dh, mem, p, w, s, after=None, at_mixer=None):
    g = {}
    g['w_down'] = _matmul(s['act'], dh, ta=True, out_dtype=BF16, tm=1408, tn=512, tk=1024, name="down_dw")
    dgu = _down_dx_swiglu_bwd(dh, w['w_down'], s['gu'], after=after)
    g['w_gate_up'] = _matmul(s['hn3'], dgu, ta=True, out_dtype=BF16, tm=1024, tn=1408, tk=1024, name="gate_up_dw")
    dh, g['g_ffn'] = _matmul_norm_bwd(dgu, w['w_gate_up'], s['h2'], p['g_ffn'], dh, name="gate_up_dx_norm_bwd")

    g['w_mem_out'] = _matmul(s['mo'], dh, ta=True, out_dtype=BF16, tm=512, tn=1024, tk=1024, name="mem_out_dw")
    dmo = _matmul(dh, w['w_mem_out'], tb=True, out_dtype=BF16, name="mem_out_dx")
    dmq, dmk, dmv, g['g_q_mem'] = _cross_bwd(s['mq_raw'], dmo, p['g_q_mem'], s['mk'], s['mv'])
    g['w_mem_kv'], g['g_mem_kv'], g['g_k_mem'] = _mem_kv_bwd(dmk, dmv, s['mkv'], s['mn'], mem, p['g_mem_kv'],
                                                               w['w_mem_kv'], p['g_k_mem'])
    g['w_mem_q'] = _matmul(s['hn2'], dmq, ta=True, out_dtype=BF16, tm=1024, tn=512, tk=1024, name="mem_q_dw")
    dh, g['g_mem_q'] = _matmul_norm_bwd(dmq, w['w_mem_q'], s['h1'], p['g_mem_q'], dh, name="mem_q_dx_norm_bwd")

    g['w_out'] = _matmul(s['cat'], dh, ta=True, out_dtype=BF16, tm=1024, tn=512, tk=1024, name="out_proj_dw")
    after = at_mixer(g) if at_mixer is not None else None
    dcat = _matmul(dh, w['w_out'], tb=True, after=after, tn=1024, name="out_proj_dx")
    dpin, g['w_pool'], dscale = _pool_bwd(dcat, s['mixed'], p['w_pool'].astype(BF16), p['pool_scale'])
    g['pool_scale'] = dscale.reshape(POOL_WIDTH)
    do, delta, dot = _fox_delta(dcat, s['fox'])
    delta_row = _pairs_rows(jnp.transpose(delta[:, :FOX_HEADS]))
    dq, dk, dv, dc_keys, dc_queries = _fox_bwd(s['qn'], s['qnt'], s['kn'], s['knt'], s['vb'], do, dot, s['lse_row'],
                                               delta_row, s['crow'], s['ccol'])
    dft, db = _forget_cumsum_bwd(dc_keys, dc_queries, s['ft'], p['b_forget'])
    g['b_forget'] = db.reshape(FOX_HEADS)
    dfpad = jnp.pad(jnp.transpose(dft), ((0, 0), (0, LANES - FOX_HEADS)))
    dz, g['g_q_fox'], g['g_k_fox'] = _mixer_dz(s['z'], dq, dk, dv, dpin, dfpad, p['g_q_fox'], p['g_k_fox'])
    g['w_in'] = _matmul(s['xn1'], dz, ta=True, out_dtype=BF16, tm=512, tn=IN_COLS_PAD, tk=1024, name="in_proj_dw")
    dh, g['g_mix'] = _matmul_norm_bwd(dz, w['w_in'], s['h0'], p['g_mix'], dh, name="in_proj_dx_norm_bwd")
    return dh, g


def _local_step(x2, mem2, target2, small, w_in, other_weights, send_grads):
    h = x2
    saved, full = [], []
    for l in range(DEPTH):
        h, s, w = _layer_fwd(h, mem2, {k: v[l] for k, v in small.items()}, w_in[l], other_weights(l))
        saved.append(s)
        full.append(w)
    loss, dh = _loss_head(h, target2)
    after = None
    grads = [None] * DEPTH
    for l in reversed(range(1, DEPTH)):
        dh, grads[l] = _layer_bwd(dh, mem2, {k: v[l] for k, v in small.items()}, full[l], saved[l], after=after)
        after = send_grads(l, BIG, grads[l])
    dh, grads[0] = _layer_bwd(dh, mem2, {k: v[0] for k, v in small.items()}, full[0], saved[0], after=after,
                              at_mixer=lambda g: send_grads(0, SENT_AT_MIXER, g))
    return loss, dh, grads


def kernel(x, mem, g_mix, w_in, b_forget, g_q_fox, g_k_fox, w_pool, pool_scale, w_out, g_mem_q, g_mem_kv, w_mem_q, w_mem_kv, g_q_mem, g_k_mem, w_mem_out, g_ffn, w_gate_up, w_down, loss_target, m_g_mix, m_w_in, m_b_forget, m_g_q_fox, m_g_k_fox, m_w_pool, m_pool_scale, m_w_out, m_g_mem_q, m_g_mem_kv, m_w_mem_q, m_w_mem_kv, m_g_q_mem, m_g_k_mem, m_w_mem_out, m_g_ffn, m_w_gate_up, m_w_down, v_g_mix, v_w_in, v_b_forget, v_g_q_fox, v_g_k_fox, v_w_pool, v_pool_scale, v_w_out, v_g_mem_q, v_g_mem_kv, v_w_mem_q, v_w_mem_kv, v_g_q_mem, v_g_k_mem, v_w_mem_out, v_g_ffn, v_w_gate_up, v_w_down):
    weights = dict(g_mix=g_mix, w_in=w_in, b_forget=b_forget, g_q_fox=g_q_fox, g_k_fox=g_k_fox, w_pool=w_pool,
                   pool_scale=pool_scale, w_out=w_out, g_mem_q=g_mem_q, g_mem_kv=g_mem_kv, w_mem_q=w_mem_q,
                   w_mem_kv=w_mem_kv, g_q_mem=g_q_mem, g_k_mem=g_k_mem, w_mem_out=w_mem_out, g_ffn=g_ffn,
                   w_gate_up=w_gate_up, w_down=w_down)
    mom_m = dict(g_mix=m_g_mix, w_in=m_w_in, b_forget=m_b_forget, g_q_fox=m_g_q_fox, g_k_fox=m_g_k_fox, w_pool=m_w_pool,
                 pool_scale=m_pool_scale, w_out=m_w_out, g_mem_q=m_g_mem_q, g_mem_kv=m_g_mem_kv, w_mem_q=m_w_mem_q,
                 w_mem_kv=m_w_mem_kv, g_q_mem=m_g_q_mem, g_k_mem=m_g_k_mem, w_mem_out=m_w_mem_out, g_ffn=m_g_ffn,
                 w_gate_up=m_w_gate_up, w_down=m_w_down)
    mom_v = dict(g_mix=v_g_mix, w_in=v_w_in, b_forget=v_b_forget, g_q_fox=v_g_q_fox, g_k_fox=v_g_k_fox, w_pool=v_w_pool,
                 pool_scale=v_pool_scale, w_out=v_w_out, g_mem_q=v_g_mem_q, g_mem_kv=v_g_mem_kv, w_mem_q=v_w_mem_q,
                 w_mem_kv=v_w_mem_kv, g_q_mem=v_g_q_mem, g_k_mem=v_g_k_mem, w_mem_out=v_w_mem_out, g_ffn=v_g_ffn,
                 w_gate_up=v_w_gate_up, w_down=v_w_down)

    (w_in_all,) = _all_gather([weights['w_in'].astype(BF16)], name="w_in_all_gather")
    later = [(n, l) for l in range(DEPTH) for n in BIG if n != 'w_in']
    shards = [weights[n][l].astype(BF16) for n, l in later]
    weights_sent = _send_start(shards, _landing_zones(shards, False), blocks=False, name="weights_send_start",
                               after=w_in_all)
    small = {n: weights[n] for n in SMALL}
    small['g_mix'] = small['g_mix'] + weights_sent['token'][0, 0]
    gathered = {}

    def other_weights(l):
        def get(after):
            if not gathered:
                gathered.update(zip(later, _send_wait(weights_sent, after, name="weights_send_wait")))
            return {n: _full_weight(n, gathered[n, l]) for n in BIG if n != 'w_in'}
        return get

    grads_sent = []

    def send_grads(l, names, g):
        blocks = [_grad_blocks(n, g[n]) for n in names]
        sent = _send_start(blocks, _landing_zones(blocks, True), blocks=True,
                           name=f"grads{l}_send_start" if len(names) > 1 else "grads0_last_send_start")
        grads_sent.append((l, names, sent))
        return sent['token']

    w_in_full = [_full_weight('w_in', w_in_all[:, l]) for l in range(DEPTH)]
    loss_part, grad_x, grads = _local_step(x[0], mem[0], loss_target[0], small, w_in_full, other_weights, send_grads)
    loss = lax.psum(loss_part[0, 0], ("x", "y", "c"))

    last = [n for n in BIG if n not in SENT_AT_MIXER]
    last_token = send_grads(0, last, grads[0])
    landed = {}
    for l, names, sent in grads_sent[:-1]:
        landed.update({(l, n): a for n, a in zip(names, _send_wait(sent, grad_x, name=f"grads{l}_send_wait"))})

    def summed(n):
        return jnp.concatenate([_sum_slots(landed[l, n], name="grad_sum_" + n, after=last_token)
                                for l in range(DEPTH)], 0)

    grad, delta, new_m, new_v = {}, {}, {}, {}

    def update(n):
        shape = weights[n].shape
        two_d = lambda a: a.reshape(shape[0] * shape[1], shape[2])
        d, nm, nv = _adamw(two_d(weights[n]), grad[n], two_d(mom_m[n]), two_d(mom_v[n]), name="adamw_" + n)
        grad[n], delta[n], new_m[n], new_v[n] = (a.reshape(shape) for a in (grad[n], d, nm, nv))

    for n in SENT_AT_MIXER:
        grad[n] = summed(n)
        update(n)
    small_part = _pack_small({n: jnp.stack([grads[l][n] for l in range(DEPTH)], 0) for n in SMALL})
    (small_all,) = _all_gather([small_part], name="small_grads_all_gather")
    small_sum = _sum_slots(small_all, name="grad_sum_small")
    d, nm, nv = _adamw(_pack_small(weights), small_sum, _pack_small(mom_m), _pack_small(mom_v), name="adamw_small")
    grad.update(_unpack_small(small_sum))
    delta.update(_unpack_small(d))
    new_m.update(_unpack_small(nm))
    new_v.update(_unpack_small(nv))
    landed.update({(0, n): a for n, a in zip(last, _send_wait(grads_sent[-1][2], d, name="grads0_last_send_wait"))})
    for n in last:
        grad[n] = summed(n)
        update(n)

    return (loss, grad_x[None], *[grad[n] for n in WEIGHTS], *[delta[n] for n in WEIGHTS],
            *[new_m[n] for n in WEIGHTS], *[new_v[n] for n in WEIGHTS])
```

```python
import functools

import numpy as np
import jax
import jax.numpy as jnp
from jax import lax
from jax.experimental import pallas as pl
from jax.experimental.pallas import tpu as pltpu

F32 = jnp.float32
BF16 = jnp.bfloat16

N_DEV = 8
D_MODEL = 1024
DEPTH = 2
FOX_HEADS = 8
FOX_HEAD_DIM = 64
FOX_WIDTH = 512
POOL_WIDTH = 512
POOL_WINDOWS = (2, 4, 8, 16)
POOL_GROUP_DIM = 128
POOL_HALO = 16
IN_COLS = 2056
IN_COLS_PAD = 2176
MEM_HEADS = 4
MEM_HEAD_DIM = 128
MEM_WIDTH = 512
D_FF = 2816
EPS = 1e-6
FOX_SCALE = FOX_HEAD_DIM ** -0.5
LOG2E = 1.4426950408889634
LN2 = 0.6931471805599453
FOX_ACC_ROWS = FOX_HEAD_DIM + 16
MEM_SCALE = MEM_HEAD_DIM ** -0.5
LANES = 128

ADAM_LR = 0.001
ADAM_B1 = 0.9
ADAM_B2 = 0.999
ADAM_EPS = 1e-08
ADAM_WD = 0.01
ADAM_STEP = 10

VMEM_LIMIT = 56 * 1024 * 1024
MESH_ID = pl.DeviceIdType.MESH

WEIGHTS = ['g_mix', 'w_in', 'b_forget', 'g_q_fox', 'g_k_fox', 'w_pool', 'pool_scale', 'w_out', 'g_mem_q', 'g_mem_kv',
           'w_mem_q', 'w_mem_kv', 'g_q_mem', 'g_k_mem', 'w_mem_out', 'g_ffn', 'w_gate_up', 'w_down']
BIG = ['w_in', 'w_out', 'w_mem_q', 'w_mem_kv', 'w_mem_out', 'w_gate_up', 'w_down']
SMALL = [n for n in WEIGHTS if n not in BIG]
SENT_AT_MIXER = ['w_down', 'w_gate_up', 'w_mem_out', 'w_mem_q', 'w_mem_kv', 'w_out']


def _pcall(body, **kw):
    return pl.pallas_call(body, **kw)


def _params(*sem):
    return pltpu.CompilerParams(dimension_semantics=sem or None, vmem_limit_bytes=VMEM_LIMIT)


def _dot(a, b, dims=None):
    if dims is None:
        return jnp.dot(a, b, preferred_element_type=F32)
    return lax.dot_general(a, b, (dims, ((), ())), preferred_element_type=F32)


NT = ((1,), (1,))
TN = ((0,), (0,))


def _dot_exact(x, ones_bf16, terms=3):
    hi = x.astype(BF16)
    r1 = x - hi.astype(F32)
    mid = r1.astype(BF16)
    if terms == 2:
        return _dot(hi, ones_bf16) + _dot(mid, ones_bf16)
    lo = (r1 - mid.astype(F32)).astype(BF16)
    return _dot(hi, ones_bf16) + _dot(mid, ones_bf16) + _dot(lo, ones_bf16)


def _matmul(a, b, *, ta=False, tb=False, out_dtype=F32, res=None, after=None, tm=1024, tn=512, tk=None, name):
    planes = b.shape[0] if b.ndim == 3 else None
    bshape = b.shape[-2:]
    m, k = (a.shape[1], a.shape[0]) if ta else a.shape
    n = bshape[0] if tb else bshape[1]
    assert k == (bshape[1] if tb else bshape[0])
    tm, tn = min(tm, m), min(tn, n)
    tk = min(tk or k, k)
    assert m % tm == 0 and n % tn == 0 and k % tk == 0, (name, m, n, k, tm, tn, tk)
    nk = k // tk
    dims = ((0 if ta else 1,), (1 if tb else 0,))

    def body(*refs):
        a_ref, b_ref = refs[0], refs[1]
        r_ref = refs[2] if res is not None else None
        o_ref = refs[2 + (res is not None) + (after is not None)]
        part = _dot(a_ref[...].astype(BF16), b_ref[...].astype(BF16), dims)

        def finish(acc):
            if r_ref is not None:
                acc = acc + r_ref[...]
            o_ref[...] = acc.astype(o_ref.dtype)

        if nk == 1:
            finish(part)
        else:
            acc_ref = refs[-1]
            kk = pl.program_id(3)

            @pl.when(kk == 0)
            def _():
                acc_ref[...] = part

            @pl.when(kk > 0)
            def _():
                acc_ref[...] += part

            @pl.when(kk == nk - 1)
            def _():
                finish(acc_ref[...])

    lead = (lambda p: (p,)) if planes else (lambda p: ())
    sq = (None,) if planes else ()
    a_spec = pl.BlockSpec((tk, tm), lambda p, i, j, kk: (kk, i)) if ta else pl.BlockSpec((tm, tk), lambda p, i, j, kk: (i, kk))
    b_spec = (pl.BlockSpec(sq + (tn, tk), lambda p, i, j, kk: lead(p) + (j, kk)) if tb
              else pl.BlockSpec(sq + (tk, tn), lambda p, i, j, kk: lead(p) + (kk, j)))
    o_spec = pl.BlockSpec(sq + (tm, tn), lambda p, i, j, kk: lead(p) + (i, j))
    in_specs = ([a_spec, b_spec] + ([o_spec] if res is not None else [])
                + ([pl.BlockSpec(memory_space=pl.ANY)] if after is not None else []))
    args = (a, b) + ((res,) if res is not None else ()) + ((after,) if after is not None else ())
    return _pcall(
        body, name=name, grid=(planes or 1, m // tm, n // tn, nk), in_specs=in_specs, out_specs=o_spec,
        out_shape=jax.ShapeDtypeStruct(((planes,) if planes else ()) + (m, n), out_dtype),
        scratch_shapes=[pltpu.VMEM((tm, tn), F32)] if nk > 1 else [],
        compiler_params=_params("parallel", "parallel", "parallel", "arbitrary"),
    )(*args)


def _rms(x):
    return lax.rsqrt(jnp.mean(x * x, axis=-1, keepdims=True) + EPS)


def _norm_matmul(h, g, w, *, tm=512, tn=512, tail=0, name):
    t, d = h.shape
    n = w.shape[1]
    tn = min(tn, n)
    assert t % tm == 0 and n % tn == 0 and (not tail or tn == n)

    def body(h_ref, g_ref, w_ref, xn_ref, y_ref, *tail_ref):
        @pl.when(pl.program_id(1) == 0)
        def _():
            x = h_ref[...]
            xn_ref[...] = (x * _rms(x) * g_ref[...]).astype(BF16)

        y = _dot(xn_ref[...], w_ref[...])
        if tail:
            y_ref[...] = y[:, :n - tail]
            tail_ref[0][...] = y[:, n - tail:]
        else:
            y_ref[...] = y

    row = pl.BlockSpec((tm, d), lambda i, j: (i, 0))
    tails = ([pl.BlockSpec((tm, tail), lambda i, j: (i, 0))], [jax.ShapeDtypeStruct((t, tail), F32)]) if tail else ([], [])
    return _pcall(
        body, name=name, grid=(t // tm, n // tn),
        in_specs=[row, pl.BlockSpec((1, d), lambda i, j: (0, 0)), pl.BlockSpec((d, tn), lambda i, j: (0, j))],
        out_specs=[row, pl.BlockSpec((tm, tn - tail), lambda i, j: (i, j))] + tails[0],
        out_shape=[jax.ShapeDtypeStruct((t, d), BF16), jax.ShapeDtypeStruct((t, n - tail), F32)] + tails[1],
        compiler_params=_params("parallel", "arbitrary"),
    )(h, g.reshape(1, d), w)


def _matmul_norm_bwd(a, w, h, g, dres, *, tm=512, tk=None, name):
    stacked = a.ndim == 3
    t = a.shape[-2]
    d, k = w.shape
    tk = a.shape[-1] if stacked else min(tk or k, k)
    nk = k // tk
    assert t % tm == 0 and k % tk == 0 and (not stacked or a.shape[0] == nk)

    def body(a_ref, w_ref, h_ref, g_ref, r_ref, dx_ref, dg_ref, *acc):
        i, kk = pl.program_id(0), pl.program_id(1)
        part = _dot(a_ref[...], w_ref[...], NT)

        def finish(dy):
            x = h_ref[...]
            r = _rms(x)
            xhat = x * r
            gy = dy * g_ref[...]
            dx_ref[...] = r_ref[...] + r * (gy - xhat * jnp.mean(gy * xhat, axis=-1, keepdims=True))
            dg_part = jnp.sum(dy * xhat, axis=0, keepdims=True)

            @pl.when(i == 0)
            def _():
                dg_ref[...] = dg_part

            @pl.when(i > 0)
            def _():
                dg_ref[...] += dg_part

        if nk == 1:
            finish(part)
        else:
            acc_ref = acc[0]

            @pl.when(kk == 0)
            def _():
                acc_ref[...] = part

            @pl.when(kk > 0)
            def _():
                acc_ref[...] += part

            @pl.when(kk == nk - 1)
            def _():
                finish(acc_ref[...])

    a_spec = (pl.BlockSpec((None, tm, tk), lambda i, kk: (kk, i, 0)) if stacked
              else pl.BlockSpec((tm, tk), lambda i, kk: (i, kk)))
    row = pl.BlockSpec((tm, d), lambda i, kk: (i, 0))
    vec = pl.BlockSpec((1, d), lambda i, kk: (0, 0))
    dx, dg = _pcall(
        body, name=name, grid=(t // tm, nk),
        in_specs=[a_spec, pl.BlockSpec((d, tk), lambda i, kk: (0, kk)), row, vec, row], out_specs=[row, vec],
        out_shape=[jax.ShapeDtypeStruct((t, d), F32), jax.ShapeDtypeStruct((1, d), F32)],
        scratch_shapes=[pltpu.VMEM((tm, d), F32)] if nk > 1 else [],
        compiler_params=_params("arbitrary", "arbitrary"),
    )(a, w, h, g.reshape(1, d), dres)
    return dx, dg.reshape(d)


def _norm_gate_up_swiglu(h, g, w, *, tm=512, tn=1408):
    t, d = h.shape
    nj = D_FF // tn
    assert t % tm == 0 and D_FF % tn == 0

    def body(h_ref, g_ref, wg_ref, wu_ref, hn_ref, gu_ref, act_ref):
        @pl.when(pl.program_id(1) == 0)
        def _():
            x = h_ref[...]
            hn_ref[...] = (x * _rms(x) * g_ref[...]).astype(BF16)

        hn = hn_ref[...]
        gate = _dot(hn, wg_ref[...])
        up = _dot(hn, wu_ref[...])
        gu_ref[0] = gate.astype(BF16)
        gu_ref[1] = up.astype(BF16)
        act_ref[...] = (gate * jax.nn.sigmoid(gate) * up).astype(BF16)

    row = pl.BlockSpec((tm, d), lambda i, j: (i, 0))
    return _pcall(
        body, name="gate_up_swiglu_fwd", grid=(t // tm, nj),
        in_specs=[row, pl.BlockSpec((1, d), lambda i, j: (0, 0)), pl.BlockSpec((d, tn), lambda i, j: (0, j)),
                  pl.BlockSpec((d, tn), lambda i, j: (0, nj + j))],
        out_specs=[row, pl.BlockSpec((2, tm, tn), lambda i, j: (0, i, j)), pl.BlockSpec((tm, tn), lambda i, j: (i, j))],
        out_shape=[jax.ShapeDtypeStruct((t, d), BF16), jax.ShapeDtypeStruct((2, t, D_FF), BF16),
                   jax.ShapeDtypeStruct((t, D_FF), BF16)],
        compiler_params=_params("parallel", "arbitrary"),
    )(h, g.reshape(1, d), w, w)


def _down_dx_swiglu_bwd(dh, w_down, gu, *, after=None, tm=1024, tn=1408):
    t, d = dh.shape
    assert t % tm == 0 and D_FF % tn == 0

    def body(dh_ref, w_ref, gu_ref, *rest):
        dgu_ref = rest[-1]
        da = _dot(dh_ref[...].astype(BF16), w_ref[...], NT)
        gate, up = gu_ref[0].astype(F32), gu_ref[1].astype(F32)
        sg = jax.nn.sigmoid(gate)
        silu = gate * sg
        dgu_ref[0] = (da * up * (sg + silu * (1.0 - sg))).astype(BF16)
        dgu_ref[1] = (da * silu).astype(BF16)

    stack = pl.BlockSpec((2, tm, tn), lambda i, j: (0, i, j))
    return _pcall(
        body, name="down_dx_swiglu_bwd", grid=(t // tm, D_FF // tn),
        in_specs=[pl.BlockSpec((tm, d), lambda i, j: (i, 0)), pl.BlockSpec((tn, d), lambda i, j: (j, 0)), stack]
        + ([pl.BlockSpec(memory_space=pl.ANY)] if after is not None else []),
        out_specs=stack, out_shape=jax.ShapeDtypeStruct((2, t, D_FF), BF16),
        compiler_params=_params("parallel", "parallel"),
    )(dh, w_down, gu, *((after,) if after is not None else ()))


def _group_matrix(width, group):
    r = lax.broadcasted_iota(jnp.int32, (width, width), 0) // group
    c = lax.broadcasted_iota(jnp.int32, (width, width), 1) // group
    return (r == c).astype(BF16)


def _qkv_prep(z, gq, gk, *, bt=512):
    t = z.shape[0]
    w = FOX_WIDTH

    def body(q_ref, k_ref, v_ref, gq_ref, gk_ref, qo_ref, ko_ref, vo_ref, qt_ref, kt_ref, vt_ref, q2t_ref):
        gm = _group_matrix(w, FOX_HEAD_DIM)
        for x_ref, g_ref, o_ref, ot_ref, scale in ((q_ref, gq_ref, qo_ref, qt_ref, FOX_SCALE),
                                                   (k_ref, gk_ref, ko_ref, kt_ref, 1.0)):
            x = x_ref[...]
            ms = _dot_exact(x * x, gm, terms=2) * (1.0 / FOX_HEAD_DIM)
            y = x * lax.rsqrt(ms + EPS) * g_ref[...]
            o_ref[...] = (y * scale).astype(BF16)
            yt = jnp.transpose(y)
            ot_ref[...] = (yt * scale).astype(BF16)
            if o_ref is qo_ref:
                q2t_ref[...] = (yt * (scale * LOG2E)).astype(BF16)
        v = v_ref[...]
        vo_ref[...] = v.astype(BF16)
        vt_ref[...] = jnp.transpose(v).astype(BF16)

    col = lambda c: pl.BlockSpec((bt, w), lambda i, c=c: (i, c))
    vec = pl.BlockSpec((1, w), lambda i: (0, 0))
    out = pl.BlockSpec((bt, w), lambda i: (i, 0))
    out_t = pl.BlockSpec((w, bt), lambda i: (0, i))
    return _pcall(
        body, name="fox_qkv_prep", grid=(t // bt,), in_specs=[col(0), col(1), col(2), vec, vec],
        out_specs=[out] * 3 + [out_t] * 4,
        out_shape=[jax.ShapeDtypeStruct((t, w), BF16)] * 3 + [jax.ShapeDtypeStruct((w, t), BF16)] * 4,
        compiler_params=_params("parallel"),
    )(z, z, z, jnp.tile(gq, FOX_HEADS).reshape(1, w), jnp.tile(gk, FOX_HEADS).reshape(1, w))


def _log_sigmoid(f):
    return jnp.minimum(f, 0.0) - jnp.log1p(jnp.exp(-jnp.abs(f)))


def _forget_cumsum(ft, b, *, chunk=512):
    hh, t = ft.shape

    def body(f_ref, b_ref, c_ref):
        r = lax.broadcasted_iota(jnp.int32, (chunk, chunk), 0)
        c = lax.broadcasted_iota(jnp.int32, (chunk, chunk), 1)
        upper = (r <= c).astype(BF16)
        carry = jnp.zeros((hh, 1), F32)
        for ch in range(t // chunk):
            sl = slice(ch * chunk, (ch + 1) * chunk)
            cs = _dot_exact(_log_sigmoid(f_ref[:, sl] + b_ref[...]), upper) + carry
            c_ref[:, sl] = cs
            carry = cs[:, chunk - 1:chunk]

    return _pcall(body, name="fox_forget_cumsum", out_shape=jax.ShapeDtypeStruct((hh, t), F32),
                  compiler_params=_params())(ft, b.reshape(hh, 1))


def _forget_cumsum_bwd(dc_keys, dc_queries, ft, b, *, chunk=512):
    hh, t = ft.shape

    def body(dck_ref, dcq_ref, f_ref, b_ref, df_ref, db_ref):
        r = lax.broadcasted_iota(jnp.int32, (chunk, chunk), 0)
        c = lax.broadcasted_iota(jnp.int32, (chunk, chunk), 1)
        lower = (r >= c).astype(BF16)
        carry = jnp.zeros((hh, 1), F32)
        db = jnp.zeros((hh, 1), F32)
        for ch in reversed(range(t // chunk)):
            sl = slice(ch * chunk, (ch + 1) * chunk)
            dls = _dot_exact(dck_ref[:, sl] + dcq_ref[:, sl], lower) + carry
            carry = dls[:, 0:1]
            df = dls * jax.nn.sigmoid(-(f_ref[:, sl] + b_ref[...]))
            df_ref[:, sl] = df
            db = db + jnp.sum(df, axis=1, keepdims=True)
        db_ref[...] = db

    return _pcall(body, name="fox_forget_cumsum_bwd",
                  out_shape=[jax.ShapeDtypeStruct((hh, t), F32), jax.ShapeDtypeStruct((hh, 1), F32)],
                  compiler_params=_params())(dc_keys, dc_queries, ft, b.reshape(hh, 1))


def _lane_is_first_head():
    return lax.broadcasted_iota(jnp.int32, (1, LANES), 1) < FOX_HEAD_DIM


def _fox_fwd(q2t, kn, vt, crow2, ccol2, *, bq=512, bk=1024):
    t = kn.shape[0]
    nq = t // bq
    pairs = FOX_WIDTH // LANES
    assert t % bk == 0
    tiles = [(i, j) for i in range(nq) for j in range(i * bq // bk, -1, -1)]
    it = jnp.asarray(np.array([a for a, _ in tiles], np.int32))
    jt = jnp.asarray(np.array([b for _, b in tiles], np.int32))

    def body(it_ref, jt_ref, qt_ref, k_ref, vt_ref, cr_ref, cc_ref, o_ref, lse_ref, m_sc, acc_sc):
        s_id = pl.program_id(1)
        i, j = it_ref[s_id], jt_ref[s_id]
        first = _lane_is_first_head()
        holds_diagonal = j == (i * bq) // bk

        @pl.when(holds_diagonal)
        def _():
            m_sc[...] = jnp.full(m_sc.shape, -jnp.inf, F32)
            acc_sc[...] = jnp.zeros(acc_sc.shape, F32)

        def scores():
            k2, qt2 = k_ref[...], qt_ref[...]
            return [_dot(jnp.where(first if hh == 0 else jnp.logical_not(first), k2, jnp.zeros_like(k2)), qt2)
                    for hh in range(2)]

        def pv(hh, pt_bf16):
            v_ones = jnp.concatenate([vt_ref[hh * FOX_HEAD_DIM:(hh + 1) * FOX_HEAD_DIM, :],
                                      jnp.ones((FOX_ACC_ROWS - FOX_HEAD_DIM, bk), BF16)], axis=0)
            return _dot(v_ones, pt_bf16)

        def tile(diagonal):
            sc = scores()
            for hh in range(2):
                ut = sc[hh] - cc_ref[0, :, hh:hh + 1]
                if diagonal:
                    key = j * bk + lax.broadcasted_iota(jnp.int32, ut.shape, 0)
                    query = i * bq + lax.broadcasted_iota(jnp.int32, ut.shape, 1)
                    ut = jnp.where(key <= query, ut, -jnp.inf)
                c_t = cr_ref[0, hh:hh + 1, :]
                m_prev = m_sc[hh]
                m_new = jnp.maximum(m_prev, jnp.max(ut, axis=0, keepdims=True) + c_t)
                acc_sc[hh] = jnp.exp2(m_prev - m_new) * acc_sc[hh] + pv(hh, jnp.exp2(ut + (c_t - m_new)).astype(BF16))
                m_sc[hh] = m_new

        @pl.when(holds_diagonal)
        def _():
            tile(True)

        @pl.when(jnp.logical_not(holds_diagonal))
        def _():
            tile(False)

        @pl.when(j == 0)
        def _():
            sums = [acc_sc[hh, FOX_HEAD_DIM:FOX_HEAD_DIM + 1, :] for hh in range(2)]
            ot = jnp.concatenate([acc_sc[hh, :FOX_HEAD_DIM, :] / sums[hh] for hh in range(2)], axis=0)
            o_ref[...] = jnp.transpose(ot).astype(o_ref.dtype)
            for hh in range(2):
                lse_ref[0, hh:hh + 1, :] = m_sc[hh] * LN2 + jnp.log(sums[hh])

    qspec = pl.BlockSpec((bq, LANES), lambda p, s, it, jt: (it[s], p))
    kspec = pl.BlockSpec((bk, LANES), lambda p, s, it, jt: (jt[s], p))
    vtspec = pl.BlockSpec((LANES, bk), lambda p, s, it, jt: (p, jt[s]))
    qtspec = pl.BlockSpec((LANES, bq), lambda p, s, it, jt: (p, it[s]))
    rowq = pl.BlockSpec((1, 2, bq), lambda p, s, it, jt: (p, 0, it[s]))
    colk = pl.BlockSpec((1, bk, 2), lambda p, s, it, jt: (p, jt[s], 0))
    return _pcall(
        body, name="fox_attention_fwd",
        grid_spec=pltpu.PrefetchScalarGridSpec(
            num_scalar_prefetch=2, grid=(pairs, len(tiles)),
            in_specs=[qtspec, kspec, vtspec, rowq, colk], out_specs=[qspec, rowq],
            scratch_shapes=[pltpu.VMEM((2, 1, bq), F32), pltpu.VMEM((2, FOX_ACC_ROWS, bq), F32)]),
        out_shape=[jax.ShapeDtypeStruct((t, FOX_WIDTH), BF16), jax.ShapeDtypeStruct((pairs, 2, t), F32)],
        compiler_params=_params("parallel", "arbitrary"),
    )(it, jt, q2t, kn, vt, crow2, ccol2)


def _fox_bwd(qn, qnt, kn, knt, vb, do, dot, lse_row, delta_row, crow, ccol, *, bq=512, bk=512):
    t = qn.shape[0]
    nq, nk = t // bq, t // bk
    pairs = FOX_WIDTH // LANES
    tiles = [(i, j) for j in range(nk) for i in range(j * bk // bq, nq)]
    it = jnp.asarray(np.array([a for a, _ in tiles], np.int32))
    jt = jnp.asarray(np.array([b for _, b in tiles], np.int32))

    def body(it_ref, jt_ref, q_ref, qt_ref, k_ref, kt_ref, v_ref, do_ref, dot_ref, lse_ref, dl_ref, cr_ref, cc_ref,
             dqt_ref, dk_ref, dv_ref, dc_ref, dr_ref, dk_sc, dv_sc, dc_sc):
        s_id = pl.program_id(1)
        i, j = it_ref[s_id], jt_ref[s_id]
        first = _lane_is_first_head()
        holds_diagonal = i == (j * bk) // bq

        @pl.when(s_id == 0)
        def _():
            dqt_ref[...] = jnp.zeros(dqt_ref.shape, F32)
            dr_ref[...] = jnp.zeros(dr_ref.shape, F32)

        @pl.when(holds_diagonal)
        def _():
            dk_sc[...] = jnp.zeros(dk_sc.shape, F32)
            dv_sc[...] = jnp.zeros(dv_sc.shape, F32)
            dc_sc[...] = jnp.zeros(dc_sc.shape, F32)

        def tile(diagonal):
            q2, qt2, k2, kt2, v2, do2, dot2 = (q_ref[...], qt_ref[...], k_ref[...], kt_ref[...], v_ref[...], do_ref[...],
                                               dot_ref[...])
            dk_t, dv_t, dqt_t = [], [], []
            for hh in range(2):
                mine = first if hh == 0 else jnp.logical_not(first)
                st = _dot(jnp.where(mine, k2, jnp.zeros_like(k2)), qt2) - cc_ref[0, :, hh:hh + 1]
                if diagonal:
                    key = j * bk + lax.broadcasted_iota(jnp.int32, st.shape, 0)
                    query = i * bq + lax.broadcasted_iota(jnp.int32, st.shape, 1)
                    st = jnp.where(key <= query, st, -jnp.inf)
                pt = jnp.exp(st + (cr_ref[0, hh:hh + 1, :] - lse_ref[0, hh:hh + 1, :]))
                dv_t.append(_dot(pt.astype(BF16), do2))
                dpt = _dot(jnp.where(mine, v2, jnp.zeros_like(v2)), dot2)
                dst = pt * (dpt - dl_ref[0, hh:hh + 1, :])
                dc_sc[hh] += jnp.sum(dst, axis=1, keepdims=True)
                dr_ref[0, i, hh:hh + 1, :] += jnp.sum(dst, axis=0, keepdims=True)
                dsb = dst.astype(BF16)
                dk_t.append(_dot(dsb, q2))
                dqt_t.append(_dot(kt2[hh * FOX_HEAD_DIM:(hh + 1) * FOX_HEAD_DIM], dsb))
            dk_sc[...] += jnp.where(first, dk_t[0], dk_t[1])
            dv_sc[...] += jnp.where(first, dv_t[0], dv_t[1])
            dqt_ref[0, i] += jnp.concatenate(dqt_t, axis=0)

        @pl.when(jnp.logical_not(holds_diagonal))
        def _():
            tile(False)

        @pl.when(holds_diagonal)
        def _():
            tile(True)

        @pl.when(i == nq - 1)
        def _():
            dk_ref[...] = dk_sc[...]
            dv_ref[...] = dv_sc[...].astype(BF16)
            for hh in range(2):
                dc_ref[0, :, hh:hh + 1] = -dc_sc[hh]

    qspec = pl.BlockSpec((bq, LANES), lambda p, s, it, jt: (it[s], p))
    qtspec = pl.BlockSpec((LANES, bq), lambda p, s, it, jt: (p, it[s]))
    kspec = pl.BlockSpec((bk, LANES), lambda p, s, it, jt: (jt[s], p))
    ktspec = pl.BlockSpec((LANES, bk), lambda p, s, it, jt: (p, jt[s]))
    rowq = pl.BlockSpec((1, 2, bq), lambda p, s, it, jt: (p, 0, it[s]))
    colk = pl.BlockSpec((1, bk, 2), lambda p, s, it, jt: (p, jt[s], 0))
    dqt_spec = pl.BlockSpec((1, nq, LANES, bq), lambda p, s, it, jt: (p, 0, 0, 0))
    dr_spec = pl.BlockSpec((1, nq, 2, bq), lambda p, s, it, jt: (p, 0, 0, 0))
    dqt, dk, dv, dc_keys, dc_queries = _pcall(
        body, name="fox_attention_bwd",
        grid_spec=pltpu.PrefetchScalarGridSpec(
            num_scalar_prefetch=2, grid=(pairs, len(tiles)),
            in_specs=[qspec, qtspec, kspec, ktspec, kspec, qspec, qtspec, rowq, rowq, rowq, colk],
            out_specs=[dqt_spec, kspec, kspec, colk, dr_spec],
            scratch_shapes=[pltpu.VMEM((bk, LANES), F32), pltpu.VMEM((bk, LANES), F32), pltpu.VMEM((2, bk, 1), F32)]),
        out_shape=[jax.ShapeDtypeStruct((pairs, nq, LANES, bq), F32), jax.ShapeDtypeStruct((t, FOX_WIDTH), F32),
                   jax.ShapeDtypeStruct((t, FOX_WIDTH), BF16), jax.ShapeDtypeStruct((pairs, t, 2), F32),
                   jax.ShapeDtypeStruct((pairs, nq, 2, bq), F32)],
        compiler_params=_params("parallel", "arbitrary"),
    )(it, jt, qn, qnt, kn, knt, vb, do, dot, lse_row, delta_row, crow, ccol)
    dq = jnp.transpose(dqt, (1, 3, 0, 2)).reshape(t, FOX_WIDTH)
    dc_keys = jnp.transpose(dc_keys, (0, 2, 1)).reshape(FOX_HEADS, t)
    dc_queries = jnp.transpose(dc_queries, (0, 2, 1, 3)).reshape(FOX_HEADS, t)
    return dq, dk, dv, dc_keys, dc_queries


def _fox_delta(dcat, fox, *, bt=512):
    t = fox.shape[0]
    w = FOX_WIDTH

    def body(do_ref, o_ref, dob_ref, dl_ref, dot_ref):
        do = do_ref[...]
        dot_ref[...] = jnp.transpose(do).astype(BF16)
        dob = do.astype(BF16)
        r = lax.broadcasted_iota(jnp.int32, (w, LANES), 0) // FOX_HEAD_DIM
        c = lax.broadcasted_iota(jnp.int32, (w, LANES), 1)
        dl_ref[...] = _dot_exact(dob.astype(F32) * o_ref[...].astype(F32), (r == c).astype(BF16))
        dob_ref[...] = dob

    blk = pl.BlockSpec((bt, w), lambda i: (i, 0))
    return _pcall(
        body, name="fox_delta", grid=(t // bt,), in_specs=[blk, blk],
        out_specs=[blk, pl.BlockSpec((bt, LANES), lambda i: (i, 0)), pl.BlockSpec((w, bt), lambda i: (0, i))],
        out_shape=[jax.ShapeDtypeStruct((t, w), BF16), jax.ShapeDtypeStruct((t, LANES), F32),
                   jax.ShapeDtypeStruct((w, t), BF16)],
        compiler_params=_params("parallel"),
    )(dcat, fox)


def _mixer_dz(z, dq, dk, dv, dpin, dfpad, gq, gk, *, bt=256):
    t = z.shape[0]
    w = FOX_WIDTH

    def body(q_ref, k_ref, dq_ref, dk_ref, dv_ref, dp_ref, df_ref, gq_ref, gk_ref, dz_ref, dgq_ref, dgk_ref):
        gm = _group_matrix(w, FOX_HEAD_DIM)
        first_step = pl.program_id(0) == 0
        for n, (x_ref, dy_ref, g_ref, dg_ref, scale) in enumerate(
                ((q_ref, dq_ref, gq_ref, dgq_ref, FOX_SCALE), (k_ref, dk_ref, gk_ref, dgk_ref, 1.0))):
            x = x_ref[...]
            r = lax.rsqrt(_dot_exact(x * x, gm, terms=2) * (1.0 / FOX_HEAD_DIM) + EPS)
            xhat = x * r
            dy = dy_ref[...] * scale
            gy = dy * g_ref[...]
            dx = r * (gy - xhat * (_dot_exact(gy * xhat, gm, terms=2) * (1.0 / FOX_HEAD_DIM)))
            dz_ref[:, n * w:(n + 1) * w] = dx.astype(BF16)
            part = jnp.sum(dy * xhat, axis=0, keepdims=True)

            @pl.when(first_step)
            def _():
                dg_ref[...] = part

            @pl.when(jnp.logical_not(first_step))
            def _():
                dg_ref[...] += part

        dz_ref[:, 2 * w:3 * w] = dv_ref[...].astype(BF16)
        dz_ref[:, 3 * w:4 * w] = dp_ref[...].astype(BF16)
        dz_ref[:, 4 * w:] = df_ref[...].astype(BF16)

    col = lambda c: pl.BlockSpec((bt, w), lambda i, c=c: (i, c))
    blk = pl.BlockSpec((bt, w), lambda i: (i, 0))
    vec = pl.BlockSpec((1, w), lambda i: (0, 0))
    dz, dgq, dgk = _pcall(
        body, name="mixer_dz", grid=(t // bt,),
        in_specs=[col(0), col(1), blk, blk, blk, blk, pl.BlockSpec((bt, LANES), lambda i: (i, 0)), vec, vec],
        out_specs=[pl.BlockSpec((bt, IN_COLS_PAD), lambda i: (i, 0)), vec, vec],
        out_shape=[jax.ShapeDtypeStruct((t, IN_COLS_PAD), BF16), jax.ShapeDtypeStruct((1, w), F32),
                   jax.ShapeDtypeStruct((1, w), F32)],
        compiler_params=_params("arbitrary"),
    )(z, z, dq, dk, dv, dpin, dfpad, jnp.tile(gq, FOX_HEADS).reshape(1, w), jnp.tile(gk, FOX_HEADS).reshape(1, w))
    return dz, dgq.reshape(FOX_HEADS, FOX_HEAD_DIM).sum(0), dgk.reshape(FOX_HEADS, FOX_HEAD_DIM).sum(0)


def _pool_fwd(z, wp, scale, *, bt=512):
    t = z.shape[0]
    w = POOL_WIDTH
    hb = bt // POOL_HALO

    def body(p_ref, h_ref, wp_ref, sc_ref, y_ref, mx_ref):
        i = pl.program_id(0)
        cur = p_ref[...]
        halo = jnp.where(i > 0, h_ref[...], 0.0)
        ext = jnp.concatenate([halo, cur], axis=0)
        trow = i * bt + lax.broadcasted_iota(jnp.int32, (bt, 1), 0)
        for g, win in enumerate(POOL_WINDOWS):
            sl = slice(g * LANES, (g + 1) * LANES)
            e = ext[:, sl]
            acc = e[POOL_HALO:]
            for k in range(1, win):
                acc = acc + pltpu.roll(e, k, 0)[POOL_HALO:]
            cnt = jnp.minimum(trow + 1, win).astype(F32)
            mixed = (acc / cnt - cur[:, sl]).astype(BF16)
            mx_ref[:, sl] = mixed
            y_ref[:, sl] = (_dot(mixed, wp_ref[g]) * sc_ref[:, sl]).astype(BF16)

    blk = pl.BlockSpec((bt, w), lambda i: (i, 0))
    return _pcall(
        body, name="pool_fwd", grid=(t // bt,),
        in_specs=[pl.BlockSpec((bt, w), lambda i: (i, 3)),
                  pl.BlockSpec((POOL_HALO, w), lambda i: (jnp.maximum(i * hb - 1, 0), 3)),
                  pl.BlockSpec((len(POOL_WINDOWS), LANES, LANES), lambda i: (0, 0, 0)),
                  pl.BlockSpec((1, w), lambda i: (0, 0))],
        out_specs=[blk, blk], out_shape=[jax.ShapeDtypeStruct((t, w), BF16)] * 2,
        compiler_params=_params("parallel"),
    )(z, z, wp, scale.reshape(1, w))


def _pool_bwd(dcat, mixed, wp, scale, *, bt=512):
    t = mixed.shape[0]
    w = POOL_WIDTH
    hb = bt // POOL_HALO
    nb = t // bt
    n_ext = bt + POOL_HALO

    def body(d_ref, h_ref, mx_ref, wp_ref, sc_ref, dp_ref, dwp_ref, dsc_ref):
        i = pl.program_id(0)
        cur = d_ref[...]
        nxt = jnp.where(i < nb - 1, h_ref[...], 0.0)
        ext = jnp.concatenate([cur, nxt], axis=0)
        trow = i * bt + lax.broadcasted_iota(jnp.int32, (n_ext, 1), 0)

        @pl.when(i == 0)
        def _():
            dwp_ref[...] = jnp.zeros(dwp_ref.shape, F32)
            dsc_ref[...] = jnp.zeros(dsc_ref.shape, F32)

        for g, win in enumerate(POOL_WINDOWS):
            sl = slice(g * LANES, (g + 1) * LANES)
            dy = (ext[:, sl] * sc_ref[:, sl]).astype(BF16)
            dm = _dot(dy, wp_ref[g], NT)
            mixed_g = mx_ref[:, sl]
            dsc_ref[:, sl] += jnp.sum(cur[:, sl] * _dot(mixed_g, wp_ref[g]), axis=0, keepdims=True)
            dwp_ref[g] += _dot(mixed_g, dy[:bt], TN)
            r = dm / jnp.minimum(trow + 1, win).astype(F32)
            acc = r[:bt]
            for k in range(1, win):
                acc = acc + pltpu.roll(r, n_ext - k, 0)[:bt]
            dp_ref[:, sl] = (acc - dm[:bt]).astype(BF16)

    return _pcall(
        body, name="pool_bwd", grid=(nb,),
        in_specs=[pl.BlockSpec((bt, w), lambda i: (i, 1)),
                  pl.BlockSpec((POOL_HALO, w), lambda i: (jnp.minimum((i + 1) * hb, t // POOL_HALO - 1), 1)),
                  pl.BlockSpec((bt, w), lambda i: (i, 0)),
                  pl.BlockSpec((len(POOL_WINDOWS), LANES, LANES), lambda i: (0, 0, 0)),
                  pl.BlockSpec((1, w), lambda i: (0, 0))],
        out_specs=[pl.BlockSpec((bt, w), lambda i: (i, 0)),
                   pl.BlockSpec((len(POOL_WINDOWS), LANES, LANES), lambda i: (0, 0, 0)),
                   pl.BlockSpec((1, w), lambda i: (0, 0))],
        out_shape=[jax.ShapeDtypeStruct((t, w), BF16), jax.ShapeDtypeStruct((len(POOL_WINDOWS), LANES, LANES), F32),
                   jax.ShapeDtypeStruct((1, w), F32)],
        compiler_params=_params("arbitrary"),
    )(dcat, dcat, mixed, wp, scale.reshape(1, w))


def _head_rms(x):
    return lax.rsqrt(jnp.mean(x * x, axis=-1, keepdims=True) + EPS)


def _mem_kv_fwd(mem, g_kv, w_kv, g_k):
    mlen, d = mem.shape

    def body(m_ref, g_ref, w_ref, gk_ref, mn_ref, mkv_ref, mk_ref, mv_ref):
        x = m_ref[...]
        mn = (x * lax.rsqrt(jnp.mean(x * x, axis=-1, keepdims=True) + EPS) * g_ref[...]).astype(BF16)
        mn_ref[...] = mn
        mkv = _dot(mn, w_ref[...])
        mkv_ref[...] = mkv
        for h in range(MEM_HEADS):
            sl = slice(h * MEM_HEAD_DIM, (h + 1) * MEM_HEAD_DIM)
            kh = mkv[:, sl]
            mk_ref[:, sl] = (kh * _head_rms(kh) * gk_ref[...]).astype(BF16)
        mv_ref[...] = mkv[:, MEM_WIDTH:].astype(BF16)

    return _pcall(
        body, name="mem_kv_fwd",
        out_shape=[jax.ShapeDtypeStruct((mlen, d), BF16), jax.ShapeDtypeStruct((mlen, 2 * MEM_WIDTH), F32),
                   jax.ShapeDtypeStruct((mlen, MEM_WIDTH), BF16), jax.ShapeDtypeStruct((mlen, MEM_WIDTH), BF16)],
        compiler_params=_params(),
    )(mem, g_kv.reshape(1, d), w_kv, g_k.reshape(1, MEM_HEAD_DIM))


def _mem_kv_bwd(dmk, dmv, mkv, mn, mem, g_kv, w_kv, g_k):
    mlen, d = mem.shape

    def body(dmk_ref, dmv_ref, mkv_ref, mn_ref, m_ref, g_ref, w_ref, gk_ref, dw_ref, dg_ref, dgk_ref, dkv_sc):
        dgk = jnp.zeros((1, MEM_HEAD_DIM), F32)
        for h in range(MEM_HEADS):
            sl = slice(h * MEM_HEAD_DIM, (h + 1) * MEM_HEAD_DIM)
            x = mkv_ref[:, sl]
            r = _head_rms(x)
            xhat = x * r
            dy = dmk_ref[:, sl]
            gy = dy * gk_ref[...]
            dkv_sc[:, sl] = (r * (gy - xhat * jnp.mean(gy * xhat, axis=-1, keepdims=True))).astype(BF16)
            dgk = dgk + jnp.sum(dy * xhat, axis=0, keepdims=True)
        dgk_ref[...] = dgk
        dkv_sc[:, MEM_WIDTH:] = dmv_ref[...].astype(BF16)
        dkv = dkv_sc[...]
        dw_ref[...] = _dot(mn_ref[...], dkv, TN).astype(BF16)
        dmn = _dot(dkv, w_ref[...], NT)
        x = m_ref[...]
        xhat = x * lax.rsqrt(jnp.mean(x * x, axis=-1, keepdims=True) + EPS)
        dg_ref[...] = jnp.sum(dmn * xhat, axis=0, keepdims=True)

    dw, dg, dgk = _pcall(
        body, name="mem_kv_bwd",
        out_shape=[jax.ShapeDtypeStruct((d, 2 * MEM_WIDTH), BF16), jax.ShapeDtypeStruct((1, d), F32),
                   jax.ShapeDtypeStruct((1, MEM_HEAD_DIM), F32)],
        scratch_shapes=[pltpu.VMEM((mlen, 2 * MEM_WIDTH), BF16)],
        compiler_params=_params(),
    )(dmk, dmv, mkv, mn, mem, g_kv.reshape(1, d), w_kv, g_k.reshape(1, MEM_HEAD_DIM))
    return dw, dg.reshape(d), dgk.reshape(MEM_HEAD_DIM)


def _cross_probs(x, g, mk_h):
    r = _head_rms(x)
    xhat = x * r
    qn = (xhat * g).astype(BF16)
    s = _dot(qn, mk_h, NT) * MEM_SCALE
    e = jnp.exp(s - jnp.max(s, axis=-1, keepdims=True))
    return r, xhat, qn, e / jnp.sum(e, axis=-1, keepdims=True)


def _cross_fwd(mq_raw, g_q, mk, mv, *, bt=512):
    t = mq_raw.shape[0]
    mlen = mk.shape[0]

    def body(x_ref, g_ref, mk_ref, mv_ref, o_ref):
        for h in range(MEM_HEADS):
            sl = slice(h * MEM_HEAD_DIM, (h + 1) * MEM_HEAD_DIM)
            _, _, _, p = _cross_probs(x_ref[:, sl], g_ref[...], mk_ref[:, sl])
            o_ref[:, sl] = _dot(p.astype(BF16), mv_ref[:, sl]).astype(BF16)

    blk = pl.BlockSpec((bt, MEM_WIDTH), lambda i: (i, 0))
    kv = pl.BlockSpec((mlen, MEM_WIDTH), lambda i: (0, 0))
    return _pcall(
        body, name="cross_attention_fwd", grid=(t // bt,),
        in_specs=[blk, pl.BlockSpec((1, MEM_HEAD_DIM), lambda i: (0, 0)), kv, kv], out_specs=blk,
        out_shape=jax.ShapeDtypeStruct((t, MEM_WIDTH), BF16), compiler_params=_params("parallel"),
    )(mq_raw, g_q.reshape(1, MEM_HEAD_DIM), mk, mv)


def _cross_bwd(mq_raw, dmo, g_q, mk, mv, *, bt=512):
    t = mq_raw.shape[0]
    mlen = mk.shape[0]

    def body(x_ref, do_ref, g_ref, mk_ref, mv_ref, dx_ref, dmk_ref, dmv_ref, dg_ref):
        @pl.when(pl.program_id(0) == 0)
        def _():
            dmk_ref[...] = jnp.zeros(dmk_ref.shape, F32)
            dmv_ref[...] = jnp.zeros(dmv_ref.shape, F32)
            dg_ref[...] = jnp.zeros(dg_ref.shape, F32)

        for h in range(MEM_HEADS):
            sl = slice(h * MEM_HEAD_DIM, (h + 1) * MEM_HEAD_DIM)
            r, xhat, qn, p = _cross_probs(x_ref[:, sl], g_ref[...], mk_ref[:, sl])
            do = do_ref[:, sl]
            dp = _dot(do, mv_ref[:, sl], NT)
            ds = (p * (dp - jnp.sum(p * dp, axis=-1, keepdims=True)) * MEM_SCALE).astype(BF16)
            dmv_ref[:, sl] += _dot(p.astype(BF16), do, TN)
            dmk_ref[:, sl] += _dot(ds, qn, TN)
            dqn = _dot(ds, mk_ref[:, sl])
            gy = dqn * g_ref[...]
            dx_ref[:, sl] = (r * (gy - xhat * jnp.mean(gy * xhat, axis=-1, keepdims=True))).astype(BF16)
            dg_ref[...] += jnp.sum(dqn * xhat, axis=0, keepdims=True)

    blk = pl.BlockSpec((bt, MEM_WIDTH), lambda i: (i, 0))
    kv = pl.BlockSpec((mlen, MEM_WIDTH), lambda i: (0, 0))
    gs = pl.BlockSpec((1, MEM_HEAD_DIM), lambda i: (0, 0))
    dx, dmk, dmv, dg = _pcall(
        body, name="cross_attention_bwd", grid=(t // bt,),
        in_specs=[blk, blk, gs, kv, kv], out_specs=[blk, kv, kv, gs],
        out_shape=[jax.ShapeDtypeStruct((t, MEM_WIDTH), BF16), jax.ShapeDtypeStruct((mlen, MEM_WIDTH), F32),
                   jax.ShapeDtypeStruct((mlen, MEM_WIDTH), F32), jax.ShapeDtypeStruct((1, MEM_HEAD_DIM), F32)],
        compiler_params=_params("arbitrary"),
    )(mq_raw, dmo, g_q.reshape(1, MEM_HEAD_DIM), mk, mv)
    return dx, dmk, dmv, dg.reshape(MEM_HEAD_DIM)


def _loss_head(y, target, *, bt=512):
    t, d = y.shape

    def body(y_ref, t_ref, dy_ref, l_ref):
        e = y_ref[...] - t_ref[...]
        dy_ref[...] = e * (1.0 / d)
        part = (0.5 / d) * jnp.sum(jnp.sum(e * e, axis=1, keepdims=True), axis=0, keepdims=True)

        @pl.when(pl.program_id(0) == 0)
        def _():
            l_ref[...] = part

        @pl.when(pl.program_id(0) > 0)
        def _():
            l_ref[...] += part

    blk = pl.BlockSpec((bt, d), lambda i: (i, 0))
    dy, loss = _pcall(
        body, name="loss_head", grid=(t // bt,), in_specs=[blk, blk],
        out_specs=[blk, pl.BlockSpec((1, 1), lambda i: (0, 0))],
        out_shape=[jax.ShapeDtypeStruct((t, d), F32), jax.ShapeDtypeStruct((1, 1), F32)],
        compiler_params=_params("arbitrary"),
    )(y, target)
    return loss, dy


def _row_tile(rows, cols, budget=1 << 19):
    best = None
    for cand in range(8, rows + 1, 8):
        if rows % cand == 0 and cand * cols <= budget:
            best = cand
    return best or rows


def _adamw(w, g, m, v, *, name):
    rows, cols = w.shape
    bt = _row_tile(rows, cols)
    c1 = 1.0 - ADAM_B1 ** ADAM_STEP
    c2 = 1.0 - ADAM_B2 ** ADAM_STEP

    def body(w_ref, g_ref, m_ref, v_ref, d_ref, nm_ref, nv_ref):
        g_v = g_ref[...]
        nm = ADAM_B1 * m_ref[...] + (1.0 - ADAM_B1) * g_v
        nv = ADAM_B2 * v_ref[...] + (1.0 - ADAM_B2) * (g_v * g_v)
        nm_ref[...] = nm
        nv_ref[...] = nv
        d_ref[...] = -ADAM_LR * ((nm / c1) / (jnp.sqrt(nv / c2) + ADAM_EPS) + ADAM_WD * w_ref[...])

    blk = pl.BlockSpec((bt, cols), lambda i: (i, 0))
    return _pcall(
        body, name=name, grid=(rows // bt,), in_specs=[blk] * 4, out_specs=[blk] * 3,
        out_shape=[jax.ShapeDtypeStruct((rows, cols), F32)] * 3, compiler_params=_params("parallel"),
    )(w, g, m, v)


def _sum_slots(x, *, name, after=None):
    n, rows, cols = x.shape
    bt = _row_tile(rows, cols, budget=1 << 17)

    def body(x_ref, *rest):
        o_ref = rest[-1]
        acc = x_ref[0].astype(F32)
        for s in range(1, n):
            acc = acc + x_ref[s].astype(F32)
        o_ref[...] = acc

    return _pcall(
        body, name=name, grid=(rows // bt,),
        in_specs=[pl.BlockSpec((n, bt, cols), lambda i: (0, i, 0))]
        + ([pl.BlockSpec(memory_space=pl.ANY)] if after is not None else []),
        out_specs=pl.BlockSpec((bt, cols), lambda i: (i, 0)),
        out_shape=jax.ShapeDtypeStruct((rows, cols), F32), compiler_params=_params("parallel"),
    )(x, *((after,) if after is not None else ()))


def _any_spec():
    return pl.BlockSpec(memory_space=pl.ANY)


def _all_gather(xs, *, name):
    n = len(xs)

    def body(*refs):
        x_refs, out_refs = refs[:n], refs[n:2 * n]
        send_sems, recv_sems, local_sems = refs[2 * n:]
        x, y, c = lax.axis_index("x"), lax.axis_index("y"), lax.axis_index("c")
        me, sibling = (x, y, c), (x, y, 1 - c)
        chips = [(1 - x, y), (x, 1 - y), (1 - x, 1 - y)]

        def slot(a, px, py, pc):
            return out_refs[a].at[4 * px + 2 * py + pc]

        def copy(a, k, block, to, src=None):
            return pltpu.make_async_remote_copy(
                src_ref=slot(a, *block) if src is None else src, dst_ref=slot(a, *block),
                send_sem=send_sems.at[a, k], recv_sem=recv_sems.at[a, k], device_id=to, device_id_type=MESH_ID)

        mine = [pltpu.make_async_copy(x_refs[a], slot(a, *me), local_sems.at[a]) for a in range(n)]
        for cp in mine:
            cp.start()
        first = []
        for j, chip in enumerate(chips):
            first += [copy(a, 1 + j, me, (*chip, c), src=x_refs[a]) for a in range(n)]
        first += [copy(a, 0, me, sibling, src=x_refs[a]) for a in range(n)]
        for cp in first:
            cp.start()
        passed = []
        for j, chip in enumerate(chips):
            for a in range(n):
                copy(a, 1 + j, (*chip, c), me).wait_recv()
                cp = copy(a, 4 + j, (*chip, c), sibling)
                cp.start()
                passed.append(cp)
        for a in range(n):
            copy(a, 0, sibling, me).wait_recv()
        for j, chip in enumerate(chips):
            for a in range(n):
                copy(a, 4 + j, (*chip, 1 - c), me).wait_recv()
        for cp in first + passed:
            cp.wait_send()
        for cp in mine:
            cp.wait()

    return _pcall(
        body, name=name, in_specs=[_any_spec()] * n, out_specs=[_any_spec()] * n,
        out_shape=[jax.ShapeDtypeStruct((N_DEV,) + x.shape, x.dtype) for x in xs],
        scratch_shapes=[pltpu.SemaphoreType.DMA((n, 7)), pltpu.SemaphoreType.DMA((n, 7)), pltpu.SemaphoreType.DMA((n,))],
    )(*xs)


def _mesh_peer(k):
    px = lax.axis_index("x") ^ ((k >> 2) & 1)
    py = lax.axis_index("y") ^ ((k >> 1) & 1)
    pc = lax.axis_index("c") ^ (k & 1)
    return (px, py, pc), 4 * px + 2 * py + pc


def _my_index():
    return 4 * lax.axis_index("x") + 2 * lax.axis_index("y") + lax.axis_index("c")


def _landing_zones(xs, blocks):
    me = _my_index()
    lands = []
    for x in xs:
        own = lax.dynamic_index_in_dim(x, me, 0, keepdims=True) if blocks else x[None]
        zone = lax.empty((N_DEV,) + own.shape[1:], x.dtype)
        lands.append(lax.dynamic_update_slice(zone, own, (me,) + (0,) * (own.ndim - 1)))
    return lands


def _send_start(xs, lands, *, blocks, name, after=None):
    n = len(xs)
    peers = N_DEV - 1
    first_out = 2 * n + (after is not None)

    def body(*refs):
        x_refs, land_refs = refs[:n], refs[n:2 * n]
        send_sems, recv_sems = refs[first_out:first_out + peers], refs[first_out + peers:first_out + 2 * peers]
        token = refs[-1]
        me_idx = _my_index()
        for k in (2, 4, 6, 3, 5, 7, 1):
            to, to_idx = _mesh_peer(k)
            for a in range(n):
                pltpu.make_async_remote_copy(
                    src_ref=x_refs[a].at[to_idx] if blocks else x_refs[a], dst_ref=land_refs[a].at[me_idx],
                    send_sem=send_sems[k - 1], recv_sem=recv_sems[k - 1], device_id=to, device_id_type=MESH_ID).start()
        token[...] = jnp.zeros(token.shape, token.dtype)

    hbm = pl.BlockSpec(memory_space=pltpu.HBM)
    sem = pl.BlockSpec(memory_space=pltpu.SEMAPHORE)
    both = list(xs) + list(lands)
    out = _pcall(
        body, name=name,
        out_shape=(*[pltpu.SemaphoreType.DMA(())] * (2 * peers), *[pltpu.HBM(a.shape, a.dtype) for a in both],
                   jax.ShapeDtypeStruct((8, LANES), F32)),
        in_specs=[hbm] * (2 * n) + ([pl.BlockSpec(memory_space=pl.ANY)] if after is not None else []),
        out_specs=(*[sem] * (2 * peers), *[hbm] * (2 * n), pl.BlockSpec(memory_space=pltpu.VMEM)),
        input_output_aliases={i: 2 * peers + i for i in range(2 * n)},
        compiler_params=pltpu.CompilerParams(has_side_effects=pltpu.SideEffectType.DATAFLOW_SIDE_EFFECTING),
    )(*[pltpu.with_memory_space_constraint(a, pltpu.HBM) for a in both], *((after,) if after is not None else ()))
    return dict(sems=out[:2 * peers], xs=out[2 * peers:2 * peers + n], lands=out[2 * peers + n:2 * peers + 2 * n],
                token=out[-1], blocks=blocks)


def _send_wait(started, after, *, name):
    n = len(started['xs'])
    blocks = started['blocks']
    peers = N_DEV - 1

    def body(*refs):
        x_refs, land_refs = refs[:n], refs[n:2 * n]
        send_sems, recv_sems = refs[2 * n:2 * n + peers], refs[2 * n + peers:2 * n + 2 * peers]
        for k in range(1, N_DEV):
            frm, frm_idx = _mesh_peer(k)
            for a in range(n):
                copy = pltpu.make_async_remote_copy(
                    src_ref=x_refs[a].at[frm_idx] if blocks else x_refs[a], dst_ref=land_refs[a].at[frm_idx],
                    send_sem=send_sems[k - 1], recv_sem=recv_sems[k - 1], device_id=frm, device_id_type=MESH_ID)
                copy.wait_send()
                copy.wait_recv()

    hbm = pl.BlockSpec(memory_space=pltpu.HBM)
    sem = pl.BlockSpec(memory_space=pltpu.SEMAPHORE)
    both = list(started['xs']) + list(started['lands'])
    out = _pcall(
        body, name=name, out_shape=[pltpu.HBM(a.shape, a.dtype) for a in both],
        in_specs=[hbm] * (2 * n) + [sem] * (2 * peers) + [pl.BlockSpec(memory_space=pl.ANY)], out_specs=[hbm] * (2 * n),
        input_output_aliases={i: i for i in range(2 * n)},
        compiler_params=pltpu.CompilerParams(has_side_effects=pltpu.SideEffectType.DATAFLOW_SIDE_EFFECTING),
    )(*both, *started['sems'], after)
    return out[n:]


COLUMN_SHARDED = ('w_in', 'w_mem_out', 'w_gate_up')


def _full_weight(name, g):
    if name in COLUMN_SHARDED:
        g = jnp.transpose(g, (1, 0, 2))
        g = g.reshape(g.shape[0], -1)
    else:
        g = g.reshape(-1, g.shape[-1])
    return jnp.pad(g, ((0, 0), (0, IN_COLS_PAD - IN_COLS))) if name == 'w_in' else g


def _grad_blocks(name, g):
    if name == 'w_gate_up':
        rows, cols = g.shape[1], 2 * g.shape[2] // N_DEV
        g = jnp.transpose(g.reshape(2, rows, N_DEV // 2, cols), (0, 2, 1, 3))
        return g.reshape(N_DEV, rows, cols).astype(BF16)
    if name == 'w_in':
        g = g[:, :IN_COLS]
    if name in COLUMN_SHARDED:
        rows, cols = g.shape[0], g.shape[1] // N_DEV
        g = jnp.transpose(g.reshape(rows, N_DEV, cols), (1, 0, 2))
    else:
        rows, cols = g.shape[0] // N_DEV, g.shape[1]
        g = g.reshape(N_DEV, rows, cols)
    return g.astype(BF16)


SMALL_SHAPES = {'g_mix': (DEPTH, D_MODEL), 'b_forget': (DEPTH, FOX_HEADS), 'g_q_fox': (DEPTH, FOX_HEAD_DIM),
                'g_k_fox': (DEPTH, FOX_HEAD_DIM), 'w_pool': (DEPTH, 4, POOL_GROUP_DIM, POOL_GROUP_DIM),
                'pool_scale': (DEPTH, POOL_WIDTH), 'g_mem_q': (DEPTH, D_MODEL), 'g_mem_kv': (DEPTH, D_MODEL),
                'g_q_mem': (DEPTH, MEM_HEAD_DIM), 'g_k_mem': (DEPTH, MEM_HEAD_DIM), 'g_ffn': (DEPTH, D_MODEL)}


def _small_rows(name):
    return -(-int(np.prod(SMALL_SHAPES[name])) // LANES)


SMALL_ROWS = -(-sum(_small_rows(n) for n in SMALL) // 8) * 8


def _pack_small(tree):
    parts = []
    for n in SMALL:
        flat = tree[n].reshape(-1).astype(F32)
        parts.append(jnp.pad(flat, (0, _small_rows(n) * LANES - flat.shape[0])))
    flat = jnp.concatenate(parts)
    return jnp.pad(flat, (0, SMALL_ROWS * LANES - flat.shape[0])).reshape(SMALL_ROWS, LANES)


def _unpack_small(packed):
    flat = packed.reshape(-1)
    out, at = {}, 0
    for n in SMALL:
        size = int(np.prod(SMALL_SHAPES[n]))
        out[n] = flat[at:at + size].reshape(SMALL_SHAPES[n])
        at += _small_rows(n) * LANES
    return out


def _pairs_cols(a):
    t = a.shape[0]
    return jnp.transpose(a.reshape(t, FOX_HEADS // 2, 2), (1, 0, 2))


def _pairs_rows(a):
    return a.reshape(FOX_HEADS // 2, 2, a.shape[1])


def _layer_fwd(h0, mem, p, w_in, other_weights):
    s = {'h0': h0}
    s['xn1'], z, f = _norm_matmul(h0, p['g_mix'], w_in, tm=1024, tn=IN_COLS_PAD, tail=LANES, name="norm_in_proj_fwd")
    s['z'] = z
    s['qn'], s['kn'], s['vb'], s['qnt'], s['knt'], vt, q2t = _qkv_prep(z, p['g_q_fox'], p['g_k_fox'])
    s['ft'] = jnp.transpose(f[:, :FOX_HEADS])
    c = _forget_cumsum(s['ft'], p['b_forget'])
    s['ccol'], s['crow'] = _pairs_cols(jnp.transpose(c)), _pairs_rows(c)
    s['fox'], s['lse_row'] = _fox_fwd(q2t, s['kn'], vt, s['crow'] * LOG2E, s['ccol'] * LOG2E)
    pool, s['mixed'] = _pool_fwd(z, p['w_pool'].astype(BF16), p['pool_scale'])
    s['cat'] = jnp.concatenate([s['fox'], pool], axis=1)
    w = dict(other_weights(s['lse_row']), w_in=w_in)
    h1 = _matmul(s['cat'], w['w_out'], res=h0, name="out_proj_fwd")
    s['h1'] = h1

    s['hn2'], s['mq_raw'] = _norm_matmul(h1, p['g_mem_q'], w['w_mem_q'], name="norm_mem_q_fwd")
    s['mn'], s['mkv'], s['mk'], s['mv'] = _mem_kv_fwd(mem, p['g_mem_kv'], w['w_mem_kv'], p['g_k_mem'])
    s['mo'] = _cross_fwd(s['mq_raw'], p['g_q_mem'], s['mk'], s['mv'])
    h2 = _matmul(s['mo'], w['w_mem_out'], res=h1, name="mem_out_fwd")
    s['h2'] = h2

    s['hn3'], s['gu'], s['act'] = _norm_gate_up_swiglu(h2, p['g_ffn'], w['w_gate_up'])
    h3 = _matmul(s['act'], w['w_down'], res=h2, name="down_fwd")
    return h3, s, w


def _layer_bwd(dh, mem, p, w, s, after=None, at_mixer=None):
    g = {}
    g['w_down'] = _matmul(s['act'], dh, ta=True, out_dtype=BF16, tm=1408, tn=512, tk=1024, name="down_dw")
    dgu = _down_dx_swiglu_bwd(dh, w['w_down'], s['gu'], after=after)
    g['w_gate_up'] = _matmul(s['hn3'], dgu, ta=True, out_dtype=BF16, tm=1024, tn=1408, tk=1024, name="gate_up_dw")
    dh, g['g_ffn'] = _matmul_norm_bwd(dgu, w['w_gate_up'], s['h2'], p['g_ffn'], dh, name="gate_up_dx_norm_bwd")

    g['w_mem_out'] = _matmul(s['mo'], dh, ta=True, out_dtype=BF16, tm=512, tn=1024, tk=1024, name="mem_out_dw")
    dmo = _matmul(dh, w['w_mem_out'], tb=True, out_dtype=BF16, name="mem_out_dx")
    dmq, dmk, dmv, g['g_q_mem'] = _cross_bwd(s['mq_raw'], dmo, p['g_q_mem'], s['mk'], s['mv'])
    g['w_mem_kv'], g['g_mem_kv'], g['g_k_mem'] = _mem_kv_bwd(dmk, dmv, s['mkv'], s['mn'], mem, p['g_mem_kv'],
                                                               w['w_mem_kv'], p['g_k_mem'])
    g['w_mem_q'] = _matmul(s['hn2'], dmq, ta=True, out_dtype=BF16, tm=1024, tn=512, tk=1024, name="mem_q_dw")
    dh, g['g_mem_q'] = _matmul_norm_bwd(dmq, w['w_mem_q'], s['h1'], p['g_mem_q'], dh, name="mem_q_dx_norm_bwd")

    g['w_out'] = _matmul(s['cat'], dh, ta=True, out_dtype=BF16, tm=1024, tn=512, tk=1024, name="out_proj_dw")
    after = at_mixer(g) if at_mixer is not None else None
    dcat = _matmul(dh, w['w_out'], tb=True, after=after, tn=1024, name="out_proj_dx")
    dpin, g['w_pool'], dscale = _pool_bwd(dcat, s['mixed'], p['w_pool'].astype(BF16), p['pool_scale'])
    g['pool_scale'] = dscale.reshape(POOL_WIDTH)
    do, delta, dot = _fox_delta(dcat, s['fox'])
    delta_row = _pairs_rows(jnp.transpose(delta[:, :FOX_HEADS]))
    dq, dk, dv, dc_keys, dc_queries = _fox_bwd(s['qn'], s['qnt'], s['kn'], s['knt'], s['vb'], do, dot, s['lse_row'],
                                               delta_row, s['crow'], s['ccol'])
    dft, db = _forget_cumsum_bwd(dc_keys, dc_queries, s['ft'], p['b_forget'])
    g['b_forget'] = db.reshape(FOX_HEADS)
    dfpad = jnp.pad(jnp.transpose(dft), ((0, 0), (0, LANES - FOX_HEADS)))
    dz, g['g_q_fox'], g['g_k_fox'] = _mixer_dz(s['z'], dq, dk, dv, dpin, dfpad, p['g_q_fox'], p['g_k_fox'])
    g['w_in'] = _matmul(s['xn1'], dz, ta=True, out_dtype=BF16, tm=512, tn=IN_COLS_PAD, tk=1024, name="in_proj_dw")
    dh, g['g_mix'] = _matmul_norm_bwd(dz, w['w_in'], s['h0'], p['g_mix'], dh, name="in_proj_dx_norm_bwd")
    return dh, g


def _local_step(x2, mem2, target2, small, w_in, other_weights, send_grads):
    h = x2
    saved, full = [], []
    for l in range(DEPTH):
        h, s, w = _layer_fwd(h, mem2, {k: v[l] for k, v in small.items()}, w_in[l], other_weights(l))
        saved.append(s)
        full.append(w)
    loss, dh = _loss_head(h, target2)
    after = None
    grads = [None] * DEPTH
    for l in reversed(range(1, DEPTH)):
        dh, grads[l] = _layer_bwd(dh, mem2, {k: v[l] for k, v in small.items()}, full[l], saved[l], after=after)
        after = send_grads(l, BIG, grads[l])
    dh, grads[0] = _layer_bwd(dh, mem2, {k: v[0] for k, v in small.items()}, full[0], saved[0], after=after,
                              at_mixer=lambda g: send_grads(0, SENT_AT_MIXER, g))
    return loss, dh, grads


def kernel(x, mem, g_mix, w_in, b_forget, g_q_fox, g_k_fox, w_pool, pool_scale, w_out, g_mem_q, g_mem_kv, w_mem_q, w_mem_kv, g_q_mem, g_k_mem, w_mem_out, g_ffn, w_gate_up, w_down, loss_target, m_g_mix, m_w_in, m_b_forget, m_g_q_fox, m_g_k_fox, m_w_pool, m_pool_scale, m_w_out, m_g_mem_q, m_g_mem_kv, m_w_mem_q, m_w_mem_kv, m_g_q_mem, m_g_k_mem, m_w_mem_out, m_g_ffn, m_w_gate_up, m_w_down, v_g_mix, v_w_in, v_b_forget, v_g_q_fox, v_g_k_fox, v_w_pool, v_pool_scale, v_w_out, v_g_mem_q, v_g_mem_kv, v_w_mem_q, v_w_mem_kv, v_g_q_mem, v_g_k_mem, v_w_mem_out, v_g_ffn, v_w_gate_up, v_w_down):
    weights = dict(g_mix=g_mix, w_in=w_in, b_forget=b_forget, g_q_fox=g_q_fox, g_k_fox=g_k_fox, w_pool=w_pool,
                   pool_scale=pool_scale, w_out=w_out, g_mem_q=g_mem_q, g_mem_kv=g_mem_kv, w_mem_q=w_mem_q,
                   w_mem_kv=w_mem_kv, g_q_mem=g_q_mem, g_k_mem=g_k_mem, w_mem_out=w_mem_out, g_ffn=g_ffn,
                   w_gate_up=w_gate_up, w_down=w_down)
    mom_m = dict(g_mix=m_g_mix, w_in=m_w_in, b_forget=m_b_forget, g_q_fox=m_g_q_fox, g_k_fox=m_g_k_fox, w_pool=m_w_pool,
                 pool_scale=m_pool_scale, w_out=m_w_out, g_mem_q=m_g_mem_q, g_mem_kv=m_g_mem_kv, w_mem_q=m_w_mem_q,
                 w_mem_kv=m_w_mem_kv, g_q_mem=m_g_q_mem, g_k_mem=m_g_k_mem, w_mem_out=m_w_mem_out, g_ffn=m_g_ffn,
                 w_gate_up=m_w_gate_up, w_down=m_w_down)
    mom_v = dict(g_mix=v_g_mix, w_in=v_w_in, b_forget=v_b_forget, g_q_fox=v_g_q_fox, g_k_fox=v_g_k_fox, w_pool=v_w_pool,
                 pool_scale=v_pool_scale, w_out=v_w_out, g_mem_q=v_g_mem_q, g_mem_kv=v_g_mem_kv, w_mem_q=v_w_mem_q,
                 w_mem_kv=v_w_mem_kv, g_q_mem=v_g_q_mem, g_k_mem=v_g_k_mem, w_mem_out=v_w_mem_out, g_ffn=v_g_ffn,
                 w_gate_up=v_w_gate_up, w_down=v_w_down)

    (w_in_all,) = _all_gather([weights['w_in'].astype(BF16)], name="w_in_all_gather")
    later = [(n, l) for l in range(DEPTH) for n in BIG if n != 'w_in']
    shards = [weights[n][l].astype(BF16) for n, l in later]
    weights_sent = _send_start(shards, _landing_zones(shards, False), blocks=False, name="weights_send_start",
                               after=w_in_all)
    small = {n: weights[n] for n in SMALL}
    small['g_mix'] = small['g_mix'] + weights_sent['token'][0, 0]
    gathered = {}

    def other_weights(l):
        def get(after):
            if not gathered:
                gathered.update(zip(later, _send_wait(weights_sent, after, name="weights_send_wait")))
            return {n: _full_weight(n, gathered[n, l]) for n in BIG if n != 'w_in'}
        return get

    grads_sent = []

    def send_grads(l, names, g):
        blocks = [_grad_blocks(n, g[n]) for n in names]
        sent = _send_start(blocks, _landing_zones(blocks, True), blocks=True,
                           name=f"grads{l}_send_start" if len(names) > 1 else "grads0_last_send_start")
        grads_sent.append((l, names, sent))
        return sent['token']

    w_in_full = [_full_weight('w_in', w_in_all[:, l]) for l in range(DEPTH)]
    loss_part, grad_x, grads = _local_step(x[0], mem[0], loss_target[0], small, w_in_full, other_weights, send_grads)
    loss = lax.psum(loss_part[0, 0], ("x", "y", "c"))

    last = [n for n in BIG if n not in SENT_AT_MIXER]
    last_token = send_grads(0, last, grads[0])
    landed = {}
    for l, names, sent in grads_sent[:-1]:
        landed.update({(l, n): a for n, a in zip(names, _send_wait(sent, grad_x, name=f"grads{l}_send_wait"))})

    def summed(n):
        return jnp.concatenate([_sum_slots(landed[l, n], name="grad_sum_" + n, after=last_token)
                                for l in range(DEPTH)], 0)

    grad, delta, new_m, new_v = {}, {}, {}, {}

    def update(n):
        shape = weights[n].shape
        two_d = lambda a: a.reshape(shape[0] * shape[1], shape[2])
        d, nm, nv = _adamw(two_d(weights[n]), grad[n], two_d(mom_m[n]), two_d(mom_v[n]), name="adamw_" + n)
        grad[n], delta[n], new_m[n], new_v[n] = (a.reshape(shape) for a in (grad[n], d, nm, nv))

    for n in SENT_AT_MIXER:
        grad[n] = summed(n)
        update(n)
    small_part = _pack_small({n: jnp.stack([grads[l][n] for l in range(DEPTH)], 0) for n in SMALL})
    (small_all,) = _all_gather([small_part], name="small_grads_all_gather")
    small_sum = _sum_slots(small_all, name="grad_sum_small")
    d, nm, nv = _adamw(_pack_small(weights), small_sum, _pack_small(mom_m), _pack_small(mom_v), name="adamw_small")
    grad.update(_unpack_small(small_sum))
    delta.update(_unpack_small(d))
    new_m.update(_unpack_small(nm))
    new_v.update(_unpack_small(nv))
    landed.update({(0, n): a for n, a in zip(last, _send_wait(grads_sent[-1][2], d, name="grads0_last_send_wait"))})
    for n in last:
        grad[n] = summed(n)
        update(n)

    return (loss, grad_x[None], *[grad[n] for n in WEIGHTS], *[delta[n] for n in WEIGHTS],
            *[new_m[n] for n in WEIGHTS], *[new_v[n] for n in WEIGHTS])
```
